```python
import jax, jax.numpy as jnp
from jax import lax
import numpy as np

D_MODEL = 2048
BATCH = 2
SEQ = 4096
DEPTH = 1

PLE_DIM = 256
GRID_W = 64
ATT_HEADS = 8
ATT_KV_HEADS = 2
HEAD_DIM = 128
ROPE_BASE = 10000.0
Q_BLOCK = 128
GLA_HEADS = 4
GLA_DK = 128
GLA_DV = 256
GLA_RANK = 16
GLA_TAU = 16.0
GLA_CHUNK = 64
N_EXPERTS = 32
TOP_K = 4
D_FF = D_MODEL
SWIGLU_LIMIT = 7.0
SWIGLU_ALPHA = 1.702
MOE_BLOCK = 128
ATT_Q = ATT_HEADS * HEAD_DIM
ATT_KV = ATT_KV_HEADS * HEAD_DIM
GLA_K = GLA_HEADS * GLA_DK
GLA_V = GLA_HEADS * GLA_DV
IN_SPLITS = (ATT_Q, ATT_KV, ATT_KV, GLA_K, GLA_K, GLA_V, GLA_V, GLA_RANK, GLA_RANK, D_MODEL, D_MODEL)
IN_WIDTH = ATT_Q + 2 * ATT_KV + 2 * GLA_K + 2 * GLA_V + 2 * GLA_RANK + 2 * D_MODEL
DN_ALPHA = (2 * DEPTH) ** 0.25
DN_BETA = (8 * DEPTH) ** -0.25
LN_EPS = 1e-5
RMS_EPS = 1e-6

kernel_name = "hybrid_gqa_gla_moe_deepnorm_encoder"


def layer_norm(x, g, b):
    xf = x.astype(jnp.float32)
    mu = jnp.mean(xf, -1, keepdims=True)
    var = jnp.mean(jnp.square(xf - mu), -1, keepdims=True)
    return ((xf - mu) * lax.rsqrt(var + LN_EPS) * g + b).astype(x.dtype)


def rms_norm(x, g):
    xf = x.astype(jnp.float32)
    return (xf * lax.rsqrt(jnp.mean(xf * xf, -1, keepdims=True) + RMS_EPS) * g).astype(x.dtype)


def axial_rope(seq_len):
    rows = seq_len // GRID_W
    row = jnp.repeat(jnp.arange(rows), GRID_W)
    col = jnp.tile(jnp.arange(GRID_W), rows)
    n_pairs = HEAD_DIM // 4
    inv_freq = ROPE_BASE ** (-jnp.arange(n_pairs, dtype=jnp.float32) / n_pairs)
    ang = jnp.concatenate([row[:, None] * inv_freq, col[:, None] * inv_freq], -1)
    return jnp.cos(ang), jnp.sin(ang)


def apply_rope(x, cos, sin):
    xf = x.astype(jnp.float32).reshape(*x.shape[:-1], HEAD_DIM // 2, 2)
    x0, x1 = xf[..., 0], xf[..., 1]
    c, s = cos[None, :, None, :], sin[None, :, None, :]
    out = jnp.stack([x0 * c - x1 * s, x0 * s + x1 * c], -1)
    return out.reshape(x.shape).astype(x.dtype)


def axial_gqa(q, k, v, q_scale, k_scale):
    B, S = q.shape[:2]
    cos, sin = axial_rope(S)
    q = apply_rope(rms_norm(q, q_scale), cos, sin)
    k = apply_rope(rms_norm(k, k_scale), cos, sin)
    G = ATT_HEADS // ATT_KV_HEADS
    qblk = jnp.moveaxis(q.reshape(B, S // Q_BLOCK, Q_BLOCK, ATT_KV_HEADS, G, HEAD_DIM), 1, 0)
    scale = HEAD_DIM ** -0.5

    def block(qb):
        s = jnp.einsum('bqhgd,bkhd->bhgqk', qb, k, preferred_element_type=jnp.float32) * scale
        pr = jax.nn.softmax(s, axis=-1)
        return jnp.einsum('bhgqk,bkhd->bqhgd', pr.astype(v.dtype), v)

    o = lax.map(block, qblk)
    return jnp.moveaxis(o, 0, 1).reshape(B, S, ATT_Q)


def gla_scan(q, k, v, log_a):
    B, S, H, dk = q.shape
    dv = v.shape[-1]
    C = GLA_CHUNK
    n = S // C

    def chunks(t):
        return jnp.moveaxis(t.astype(jnp.float32).reshape(B, n, C, H, t.shape[-1]), 1, 0)

    mask = jnp.tril(jnp.ones((C, C), bool))[None, :, :, None, None]

    def step(state, inp):
        qc, kc, vc, ac = inp
        b = jnp.cumsum(ac, axis=1)
        o_inter = jnp.einsum('bihd,bhde->bihe', qc * jnp.exp(b), state)
        rel = jnp.exp(jnp.where(mask, b[:, :, None] - b[:, None], -jnp.inf))
        attn = jnp.sum(qc[:, :, None] * kc[:, None] * rel, -1)
        o_intra = jnp.einsum('bijh,bjhe->bihe', attn, vc)
        b_last = b[:, -1]
        k_dec = kc * jnp.exp(b_last[:, None] - b)
        state = state * jnp.exp(b_last)[..., None] + jnp.einsum('bjhd,bjhe->bhde', k_dec, vc)
        return state, o_inter + o_intra

    state0 = jnp.zeros((B, H, dk, dv), jnp.float32)
    _, o = lax.scan(step, state0, (chunks(q), chunks(k), chunks(v), chunks(log_a)))
    return jnp.moveaxis(o, 0, 1).reshape(B, S, H, dv)


def token_mixer(h, w_in, q_norm, k_norm, w_lr_f, b_lr_f, w_lr_b, b_lr_b, gla_norm, w_br_a, w_br_b, w_o):
    B, S, _ = h.shape
    z = h @ w_in
    split_points = np.cumsum(IN_SPLITS)[:-1].tolist()
    qa, ka, va, qb, kb, vb, og, lrf, lrb, ga, gbm = jnp.split(z, split_points, axis=-1)
    ya = axial_gqa(qa.reshape(B, S, ATT_HEADS, HEAD_DIM), ka.reshape(B, S, ATT_KV_HEADS, HEAD_DIM),
                   va.reshape(B, S, ATT_KV_HEADS, HEAD_DIM), q_norm, k_norm) @ w_br_a
    qb = qb.reshape(B, S, GLA_HEADS, GLA_DK) * (GLA_DK ** -0.5)
    kb = kb.reshape(B, S, GLA_HEADS, GLA_DK)
    vb = vb.reshape(B, S, GLA_HEADS, GLA_DV)
    la_f = (jax.nn.log_sigmoid((lrf @ w_lr_f + b_lr_f).astype(jnp.float32)) / GLA_TAU).reshape(B, S, GLA_HEADS, GLA_DK)
    la_b = (jax.nn.log_sigmoid((lrb @ w_lr_b + b_lr_b).astype(jnp.float32)) / GLA_TAU).reshape(B, S, GLA_HEADS, GLA_DK)
    o_f = gla_scan(qb, kb, vb, la_f)
    o_b = gla_scan(qb[:, ::-1], kb[:, ::-1], vb[:, ::-1], la_b[:, ::-1])[:, ::-1]
    o = rms_norm(o_f + o_b, gla_norm) * jax.nn.silu(og.reshape(B, S, GLA_HEADS, GLA_DV).astype(jnp.float32))
    yb = o.reshape(B, S, GLA_V).astype(h.dtype) @ w_br_b
    m = jax.nn.sigmoid(ga) * ya + jax.nn.sigmoid(gbm) * yb
    return m @ w_o


def moe(h, w_router, b_router, w_up, b_up, w_down, b_down):
    B, S, D = h.shape
    x = h.reshape(-1, D)
    T = x.shape[0]
    logits = (x @ w_router + b_router).astype(jnp.float32)
    top_val, top_idx = lax.top_k(logits, TOP_K)
    gates = jax.nn.softmax(top_val, axis=-1)
    flat_e = top_idx.reshape(-1)
    order = jnp.argsort(flat_e)
    sorted_e = flat_e[order]
    counts = jnp.bincount(flat_e, length=N_EXPERTS)
    padded = (counts + MOE_BLOCK - 1) // MOE_BLOCK * MOE_BLOCK
    group_start = jnp.cumsum(counts) - counts
    padded_end = jnp.cumsum(padded)
    padded_start = padded_end - padded
    dest_sorted = padded_start[sorted_e] + jnp.arange(T * TOP_K) - group_start[sorted_e]
    dest = jnp.zeros_like(dest_sorted).at[order].set(dest_sorted)
    n_rows = T * TOP_K + N_EXPERTS * MOE_BLOCK
    token_of_row = jnp.full((n_rows,), T, jnp.int32).at[dest].set(
        jnp.arange(T * TOP_K, dtype=jnp.int32) // TOP_K)
    x_pad = jnp.concatenate([x, jnp.zeros((1, D), x.dtype)], 0)
    x_rows = x_pad[token_of_row].reshape(n_rows // MOE_BLOCK, MOE_BLOCK, D)
    block_start = jnp.arange(n_rows // MOE_BLOCK) * MOE_BLOCK
    block_e = jnp.clip(jnp.searchsorted(padded_end, block_start, side='right'), 0, N_EXPERTS - 1)

    def expert_block(args):
        xb, e = args
        hu = xb @ w_up[e] + b_up[e]
        g, lin = jnp.split(hu, 2, axis=-1)
        g = jnp.minimum(g, SWIGLU_LIMIT)
        lin = jnp.clip(lin, -SWIGLU_LIMIT, SWIGLU_LIMIT)
        act = g * jax.nn.sigmoid(SWIGLU_ALPHA * g) * (lin + 1.0)
        return act @ w_down[e] + b_down[e]

    y_rows = lax.map(expert_block, (x_rows, block_e)).reshape(n_rows, D)
    y = jnp.einsum('tkd,tk->td', y_rows[dest].reshape(T, TOP_K, D), gates.astype(y_rows.dtype))
    return y.reshape(B, S, D)


def setup_inputs(seed: int = 0) -> dict:
    key = jax.random.key(seed)
    ks = jax.random.split(key, 32)

    def nrm(k, shape, scale):
        return jax.random.normal(k, shape, jnp.float32) * scale

    def gain(k, shape):
        return 1.0 + nrm(k, shape, 0.02)

    L, D = DEPTH, D_MODEL
    return {
        "x": nrm(ks[0], (BATCH, SEQ, D), 1.0),
        "p": nrm(ks[1], (DEPTH, BATCH, SEQ, PLE_DIM), 1.0),
        "in_ln_g": gain(ks[2], (D,)),
        "in_ln_b": nrm(ks[3], (D,), 0.02),
        "w_in": nrm(ks[4], (L, D, IN_WIDTH), D ** -0.5),
        "q_norm": gain(ks[5], (L, HEAD_DIM)),
        "k_norm": gain(ks[6], (L, HEAD_DIM)),
        "w_lr_f": nrm(ks[7], (L, GLA_RANK, GLA_K), GLA_RANK ** -0.5),
        "b_lr_f": nrm(ks[8], (L, GLA_K), 0.1),
        "w_lr_b": nrm(ks[9], (L, GLA_RANK, GLA_K), GLA_RANK ** -0.5),
        "b_lr_b": nrm(ks[10], (L, GLA_K), 0.1),
        "gla_norm": gain(ks[11], (L, GLA_DV)),
        "w_br_a": nrm(ks[12], (L, ATT_Q, D), ATT_Q ** -0.5),
        "w_br_b": nrm(ks[13], (L, GLA_V, D), GLA_V ** -0.5),
        "w_o": nrm(ks[14], (L, D, D), D ** -0.5 * DN_BETA),
        "ln1_g": gain(ks[15], (L, D)),
        "ln1_b": nrm(ks[16], (L, D), 0.02),
        "w_router": nrm(ks[17], (L, D, N_EXPERTS), D ** -0.5),
        "b_router": nrm(ks[18], (L, N_EXPERTS), 0.01),
        "w_up": nrm(ks[19], (L, N_EXPERTS, D, 2 * D_FF), D ** -0.5),
        "b_up": nrm(ks[20], (L, N_EXPERTS, 2 * D_FF), 0.02),
        "w_down": nrm(ks[21], (L, N_EXPERTS, D_FF, D), D_FF ** -0.5 * DN_BETA),
        "b_down": nrm(ks[22], (L, N_EXPERTS, D), 0.02),
        "ln2_g": gain(ks[23], (L, D)),
        "ln2_b": nrm(ks[24], (L, D), 0.02),
        "w_ple_gate": nrm(ks[25], (L, D, D), D ** -0.5),
        "w_ple_proj": nrm(ks[26], (L, PLE_DIM, D), PLE_DIM ** -0.5 * DN_BETA),
        "ln3_g": gain(ks[27], (L, D)),
        "ln3_b": nrm(ks[28], (L, D), 0.02),
    }


def reference(x, p, in_ln_g, in_ln_b, w_in, q_norm, k_norm, w_lr_f, b_lr_f, w_lr_b, b_lr_b, gla_norm,
              w_br_a, w_br_b, w_o, ln1_g, ln1_b, w_router, b_router, w_up, b_up, w_down, b_down,
              ln2_g, ln2_b, w_ple_gate, w_ple_proj, ln3_g, ln3_b):
    h = layer_norm(x, in_ln_g, in_ln_b)
    for i in range(DEPTH):
        mix = token_mixer(h, w_in[i], q_norm[i], k_norm[i], w_lr_f[i], b_lr_f[i], w_lr_b[i], b_lr_b[i],
                          gla_norm[i], w_br_a[i], w_br_b[i], w_o[i])
        h = layer_norm(DN_ALPHA * h + mix, ln1_g[i], ln1_b[i])
        h = layer_norm(DN_ALPHA * h + moe(h, w_router[i], b_router[i], w_up[i], b_up[i], w_down[i], b_down[i]),
                       ln2_g[i], ln2_b[i])
        ple = jax.nn.sigmoid(h @ w_ple_gate[i]) * (p[i] @ w_ple_proj[i])
        h = layer_norm(DN_ALPHA * h + ple, ln3_g[i], ln3_b[i])
    return h
```

```python
import functools

import jax
import jax.numpy as jnp
from jax import lax
from jax.experimental import pallas as pl
from jax.experimental.pallas import tpu as pltpu

F32 = jnp.float32
BF16 = jnp.bfloat16

D_MODEL = 2048
BATCH = 2
SEQ = 4096
TOK = BATCH * SEQ
PLE_DIM = 256
GRID_W = 64
ATT_HEADS = 8
ATT_KV_HEADS = 2
ATT_GROUP = ATT_HEADS // ATT_KV_HEADS
HEAD_DIM = 128
ROPE_BASE = 10000.0
GLA_HEADS = 4
GLA_DK = 128
GLA_DV = 256
GLA_RANK = 16
GLA_TAU = 16.0
GLA_CHUNK = 64
GLA_SUB = 16
N_EXPERTS = 32
TOP_K = 4
D_FF = D_MODEL
SWIGLU_LIMIT = 7.0
SWIGLU_ALPHA = 1.702
MOE_BLOCK = 128
ATT_Q = ATT_HEADS * HEAD_DIM
ATT_KV = ATT_KV_HEADS * HEAD_DIM
GLA_K = GLA_HEADS * GLA_DK
GLA_V = GLA_HEADS * GLA_DV
MAIN_W = ATT_Q + 2 * ATT_KV + 2 * GLA_K + 2 * GLA_V
LR_W = 2 * GLA_RANK
DEPTH = 1
DN_ALPHA = (2 * DEPTH) ** 0.25
LN_EPS = 1e-5
RMS_EPS = 1e-6

Z_TILE_OF_W_TILE = (0, 1, 6, 7, 8, 2, 3, 4, 5)
Z_Q, Z_VB, Z_OG, Z_KA, Z_VA, Z_QB, Z_KB = 0, 1024, 2048, 3072, 3328, 3584, 4096

VMEM_LIMIT = 56 * 1024 * 1024

N_ROWS = TOK * TOP_K + N_EXPERTS * MOE_BLOCK
N_BLOCKS = N_ROWS // MOE_BLOCK
SB_BLOCKS = 12
SB_ROWS = SB_BLOCKS * MOE_BLOCK
N_SB = N_EXPERTS + N_BLOCKS // SB_BLOCKS
FF_TILE = 256
N_FF_TILES = D_FF // FF_TILE


def _params(n_axes, vmem=None):
    return pltpu.CompilerParams(dimension_semantics=("arbitrary",) * n_axes,
                                vmem_limit_bytes=vmem or VMEM_LIMIT)


def _sigmoid(x):
    return 1.0 / (1.0 + jnp.exp(-x))


def _layer_norm(y, g, b):
    mu = jnp.mean(y, axis=-1, keepdims=True)
    yc = y - mu
    var = jnp.mean(yc * yc, axis=-1, keepdims=True)
    return yc * lax.rsqrt(var + LN_EPS) * g + b


def _dot(a, b):
    return jnp.dot(a, b, preferred_element_type=F32)


def _dot_nt(a, b):
    return lax.dot_general(a, b, (((1,), (1,)), ((), ())), preferred_element_type=F32)


def _dot_tn(a, b):
    return lax.dot_general(a, b, (((0,), (0,)), ((), ())), preferred_element_type=F32)


def _ln0_body(x_ref, g_ref, b_ref, hf_ref, hb_ref):
    y = _layer_norm(x_ref[...], g_ref[...], b_ref[...])
    hf_ref[...] = y
    hb_ref[...] = y.astype(BF16)


def _ln0(x2, g, b):
    tm = 256
    row = pl.BlockSpec((tm, D_MODEL), lambda i: (i, 0))
    vec = pl.BlockSpec((1, D_MODEL), lambda i: (0, 0))
    return pl.pallas_call(
        _ln0_body, grid=(TOK // tm,), in_specs=[row, vec, vec], out_specs=[row, row],
        out_shape=[jax.ShapeDtypeStruct((TOK, D_MODEL), F32), jax.ShapeDtypeStruct((TOK, D_MODEL), BF16)],
        compiler_params=_params(1), name="ln0")(x2, g.reshape(1, -1), b.reshape(1, -1))


def _inproj_body(perm_ref, a_ref, w_ref, o_ref, wb_ref, *, gate):
    del perm_ref

    @pl.when(pl.program_id(1) == 0)
    def _():
        wb_ref[...] = w_ref[...].astype(BF16)

    acc = _dot(a_ref[...], wb_ref[...])
    if gate:
        acc = _sigmoid(acc)
    o_ref[...] = acc.astype(o_ref.dtype)


def _inproj(hb, w, n_tiles, tile_perm, gate, name):
    tm, tn = 512, 512
    grid_spec = pltpu.PrefetchScalarGridSpec(
        num_scalar_prefetch=1, grid=(n_tiles, TOK // tm),
        in_specs=[pl.BlockSpec((tm, D_MODEL), lambda n, m, p: (m, 0)),
                  pl.BlockSpec((D_MODEL, tn), lambda n, m, p: (0, n))],
        out_specs=pl.BlockSpec((tm, tn), lambda n, m, p: (m, p[n])),
        scratch_shapes=[pltpu.VMEM((D_MODEL, tn), BF16)])
    return pl.pallas_call(
        functools.partial(_inproj_body, gate=gate), grid_spec=grid_spec,
        out_shape=jax.ShapeDtypeStruct((TOK, n_tiles * tn), BF16),
        compiler_params=_params(2), name=name)(jnp.asarray(tile_perm, jnp.int32), hb, w)


def _decay_body(h_ref, wlr_ref, w2_ref, b2_ref, trif_ref, trib_ref, bf_ref, bb_ref):
    zlr = _dot(h_ref[...], wlr_ref[...])
    pre = jnp.dot(zlr, w2_ref[...], preferred_element_type=F32,
                  precision=lax.Precision.HIGHEST) + b2_ref[...]
    la = (jnp.minimum(pre, 0.0) - jnp.log1p(jnp.exp(-jnp.abs(pre)))) * (1.0 / GLA_TAU)
    hi = la.astype(BF16)
    lo = (la - hi.astype(F32)).astype(BF16)
    bf_ref[...] = _dot(trif_ref[...], hi[:, :GLA_K]) + _dot(trif_ref[...], lo[:, :GLA_K])
    bb_ref[...] = _dot(trib_ref[...], hi[:, GLA_K:]) + _dot(trib_ref[...], lo[:, GLA_K:])


def _decay(hb, wlr, w2, b2):
    tm = 512
    r = jnp.arange(tm)
    same = (r[:, None] // GLA_CHUNK) == (r[None, :] // GLA_CHUNK)
    trif = (same & (r[None, :] <= r[:, None])).astype(BF16)
    trib = (same & (r[None, :] >= r[:, None])).astype(BF16)
    full = lambda shape: pl.BlockSpec(shape, lambda i: (0, 0))
    out = pl.BlockSpec((tm, GLA_K), lambda i: (i, 0))
    return pl.pallas_call(
        _decay_body, grid=(TOK // tm,),
        in_specs=[pl.BlockSpec((tm, D_MODEL), lambda i: (i, 0)), full((D_MODEL, LR_W)),
                  full((LR_W, 2 * GLA_K)), full((1, 2 * GLA_K)), full((tm, tm)), full((tm, tm))],
        out_specs=[out, out],
        out_shape=[jax.ShapeDtypeStruct((TOK, GLA_K), F32)] * 2,
        compiler_params=_params(1), name="decay")(hb, wlr, w2, b2, trif, trib)


def _qkprep_body(q_ref, k_ref, cos_ref, sin_ref, qn_ref, kn_ref, qo_ref, ko_ref):
    cos = cos_ref[...]
    sin = sin_ref[...]
    lane = lax.broadcasted_iota(jnp.int32, cos.shape, 1)
    even = (lane % 2) == 0

    def one(x, gain, scale):
        x = x.astype(F32)
        x = x * lax.rsqrt(jnp.mean(x * x, axis=-1, keepdims=True) + RMS_EPS) * gain
        partner = jnp.where(even, pltpu.roll(x, HEAD_DIM - 1, 1), pltpu.roll(x, 1, 1))
        return ((x * cos + partner * sin) * scale).astype(BF16)

    for hh in range(ATT_HEADS):
        sl = slice(hh * HEAD_DIM, (hh + 1) * HEAD_DIM)
        qo_ref[:, sl] = one(q_ref[:, sl], qn_ref[...], HEAD_DIM ** -0.5)
    for hh in range(ATT_KV_HEADS):
        sl = slice(hh * HEAD_DIM, (hh + 1) * HEAD_DIM)
        ko_ref[:, sl] = one(k_ref[:, sl], kn_ref[...], 1.0)


def _qkprep(zm, cos_full, sin_signed, q_norm, k_norm):
    tm = 256
    nrow = SEQ // tm
    tab = pl.BlockSpec((tm, HEAD_DIM), lambda i: (i % nrow, 0))
    vec = pl.BlockSpec((1, HEAD_DIM), lambda i: (0, 0))
    return pl.pallas_call(
        _qkprep_body, grid=(TOK // tm,),
        in_specs=[pl.BlockSpec((tm, ATT_Q), lambda i: (i, Z_Q // ATT_Q)),
                  pl.BlockSpec((tm, ATT_KV), lambda i: (i, Z_KA // ATT_KV)), tab, tab, vec, vec],
        out_specs=[pl.BlockSpec((tm, ATT_Q), lambda i: (i, 0)), pl.BlockSpec((tm, ATT_KV), lambda i: (i, 0))],
        out_shape=[jax.ShapeDtypeStruct((TOK, ATT_Q), BF16), jax.ShapeDtypeStruct((TOK, ATT_KV), BF16)],
        compiler_params=_params(1), name="qkprep")(zm, zm, cos_full, sin_signed,
                                                   q_norm.reshape(1, -1), k_norm.reshape(1, -1))


ATT_TQ = 256
ATT_TK = 512


def _attn_body(q_ref, k_ref, v_ref, o_ref):
    q = jnp.concatenate([q_ref[:, g * HEAD_DIM:(g + 1) * HEAD_DIM] for g in range(ATT_GROUP)], axis=0)
    rows = ATT_GROUP * ATT_TQ

    def step(c, carry):
        m, l, acc = carry
        r0 = pl.multiple_of(c * ATT_TK, ATT_TK)
        s = _dot_nt(q, k_ref[pl.ds(r0, ATT_TK), :])
        m_new = jnp.maximum(m, jnp.max(s, axis=-1, keepdims=True))
        alpha = jnp.exp(m - m_new)
        p = jnp.exp(s - m_new)
        l = alpha * l + jnp.sum(p, axis=-1, keepdims=True)
        acc = alpha * acc + _dot(p.astype(BF16), v_ref[pl.ds(r0, ATT_TK), :])
        return m_new, l, acc

    m0 = jnp.full((rows, 1), -jnp.inf, F32)
    l0 = jnp.zeros((rows, 1), F32)
    a0 = jnp.zeros((rows, HEAD_DIM), F32)
    _, l, acc = lax.fori_loop(0, SEQ // ATT_TK, step, (m0, l0, a0))
    out = (acc / l).astype(BF16)
    for g in range(ATT_GROUP):
        o_ref[:, g * HEAD_DIM:(g + 1) * HEAD_DIM] = out[g * ATT_TQ:(g + 1) * ATT_TQ]


def _attention(qr, kr, zm):
    nq = SEQ // ATT_TQ
    gw = ATT_GROUP * HEAD_DIM
    qspec = pl.BlockSpec((ATT_TQ, gw), lambda b, j, i: (b * nq + i, j))
    return pl.pallas_call(
        _attn_body, grid=(BATCH, ATT_KV_HEADS, nq),
        in_specs=[qspec,
                  pl.BlockSpec((SEQ, HEAD_DIM), lambda b, j, i: (b, j)),
                  pl.BlockSpec((SEQ, HEAD_DIM), lambda b, j, i: (b, Z_VA // HEAD_DIM + j))],
        out_specs=qspec,
        out_shape=jax.ShapeDtypeStruct((TOK, ATT_Q), BF16),
        compiler_params=_params(3), name="attn")(qr, kr, zm)


GLA_CB = 4
GLA_RB = GLA_CB * GLA_CHUNK
N_SUB = GLA_CHUNK // GLA_SUB


def _gla_body(q_ref, k_ref, v_ref, bc_ref, o_ref, st_ref, kf_ref, bs_ref, *, rev):
    @pl.when(pl.program_id(1) == 0)
    def _():
        st_ref[...] = jnp.zeros_like(st_ref)

    C, SUB = GLA_CHUNK, GLA_SUB
    rowc = lax.broadcasted_iota(jnp.int32, (C, GLA_DK), 0)
    rows_s = lax.broadcasted_iota(jnp.int32, (SUB, 128), 0)
    lane_s = lax.broadcasted_iota(jnp.int32, (SUB, 128), 1)

    def chunk(ci, carry):
        c = (GLA_CB - 1 - ci) if rev else ci
        r0 = pl.multiple_of(c * C, C)
        for hh in range(GLA_HEADS):
            ksl = slice(hh * GLA_DK, (hh + 1) * GLA_DK)
            vsl = slice(hh * GLA_DV, (hh + 1) * GLA_DV)
            q = q_ref[pl.ds(r0, C), ksl].astype(F32) * (GLA_DK ** -0.5)
            k = k_ref[pl.ds(r0, C), ksl].astype(F32)
            v = v_ref[pl.ds(r0, C), vsl]
            bc = bc_ref[pl.ds(r0, C), ksl]
            kf_ref[hh] = k
            bs_ref[hh] = bc
            st = st_ref[hh]
            blast = bc[0:1] if rev else bc[C - 1:C]
            o_inter = _dot_nt((q * jnp.exp(bc)).astype(BF16), st.astype(BF16))
            kdec = k * jnp.exp(blast - bc)
            st_ref[hh] = st * jnp.exp(blast) + _dot_tn(v, kdec.astype(BF16))

            a_rows = []
            for si in range(N_SUB):
                lo, hi = si * SUB, (si + 1) * SUB
                q_s, b_s = q[lo:hi], bc[lo:hi]
                has_earlier = (si < N_SUB - 1) if rev else (si > 0)
                if has_earlier:
                    ref_row = bc[hi:hi + 1] if rev else bc[lo - 1:lo]
                    earlier = (rowc >= hi) if rev else (rowc < lo)
                    qt = q_s * jnp.exp(b_s - ref_row)
                    kt = k * jnp.exp(jnp.where(earlier, ref_row - bc, -jnp.inf))
                    a = _dot_nt(qt.astype(BF16), kt.astype(BF16))
                else:
                    a = jnp.zeros((SUB, C), F32)
                diag = jnp.zeros((SUB, 128), F32)
                for jl in range(SUB):
                    j = lo + jl
                    d = jnp.minimum(b_s - bs_ref[hh, j:j + 1, :], 0.0)
                    col = jnp.sum(q_s * kf_ref[hh, j:j + 1, :] * jnp.exp(d), axis=-1, keepdims=True)
                    diag = jnp.where(lane_s == j, col, diag)
                keep = (lane_s >= rows_s + lo) if rev else (lane_s <= rows_s + lo)
                diag = jnp.where(keep & (lane_s >= lo) & (lane_s < hi), diag, 0.0)
                a_rows.append(a + diag[:, :C])
            a_full = jnp.concatenate(a_rows, axis=0)
            o_ref[pl.ds(r0, C), vsl] = o_inter + _dot(a_full.astype(BF16), v)
        return carry

    lax.fori_loop(0, GLA_CB, chunk, 0)


def _gla(zm, bcum, rev):
    ncb = SEQ // GLA_RB
    if rev:
        row = lambda b, c: b * ncb + (ncb - 1 - c)
    else:
        row = lambda b, c: b * ncb + c
    return pl.pallas_call(
        functools.partial(_gla_body, rev=rev), grid=(BATCH, ncb),
        in_specs=[pl.BlockSpec((GLA_RB, GLA_K), lambda b, c: (row(b, c), Z_QB // GLA_K)),
                  pl.BlockSpec((GLA_RB, GLA_K), lambda b, c: (row(b, c), Z_KB // GLA_K)),
                  pl.BlockSpec((GLA_RB, GLA_V), lambda b, c: (row(b, c), Z_VB // GLA_V)),
                  pl.BlockSpec((GLA_RB, GLA_K), lambda b, c: (row(b, c), 0))],
        out_specs=pl.BlockSpec((GLA_RB, GLA_V), lambda b, c: (row(b, c), 0)),
        out_shape=jax.ShapeDtypeStruct((TOK, GLA_V), F32),
        scratch_shapes=[pltpu.VMEM((GLA_HEADS, GLA_DV, GLA_DK), F32),
                        pltpu.VMEM((GLA_HEADS, GLA_CHUNK, GLA_DK), F32),
                        pltpu.VMEM((GLA_HEADS, GLA_CHUNK, GLA_DK), F32)],
        compiler_params=_params(2), name="gla_bwd" if rev else "gla_fwd")(zm, zm, zm, bcum)


def _mixer_body(attn_ref, of_ref, ob_ref, og_ref, ga_ref, gb_ref, h_ref,
                wa_ref, wb_ref, wo_ref, gn_ref, lg_ref, lb_ref, wr_ref, br_ref,
                h1_ref, idx_ref, gate_ref):
    osum = of_ref[...] + ob_ref[...]
    og = og_ref[...].astype(F32)
    gn = gn_ref[...]
    parts = []
    for hh in range(GLA_HEADS):
        sl = slice(hh * GLA_DV, (hh + 1) * GLA_DV)
        x = osum[:, sl]
        g = og[:, sl]
        xn = x * lax.rsqrt(jnp.mean(x * x, axis=-1, keepdims=True) + RMS_EPS) * gn
        parts.append((xn * (g * _sigmoid(g))).astype(BF16))
    onorm = jnp.concatenate(parts, axis=1)
    ya = _dot(attn_ref[...], wa_ref[...])
    yb = _dot(onorm, wb_ref[...])
    m = ga_ref[...].astype(F32) * ya + gb_ref[...].astype(F32) * yb
    mix = _dot(m.astype(BF16), wo_ref[...])
    h1 = _layer_norm(DN_ALPHA * h_ref[...] + mix, lg_ref[...], lb_ref[...])
    h1_ref[...] = h1

    logits = jnp.dot(h1, wr_ref[...], preferred_element_type=F32,
                     precision=lax.Precision.HIGHEST) + br_ref[...]
    lane = lax.broadcasted_iota(jnp.int32, logits.shape, 1)
    x = logits
    vals, idxs = [], []
    for _ in range(TOP_K):
        mx = jnp.max(x, axis=-1, keepdims=True)
        ix = jnp.min(jnp.where(x == mx, lane, 128), axis=-1, keepdims=True)
        vals.append(mx)
        idxs.append(ix)
        x = jnp.where(lane == ix, -jnp.inf, x)
    es = [jnp.exp(vv - vals[0]) for vv in vals]
    den = es[0] + es[1] + es[2] + es[3]
    idx_out = jnp.zeros(logits.shape, jnp.int32)
    gate_out = jnp.zeros(logits.shape, F32)
    for kk in range(TOP_K):
        idx_out = jnp.where(lane == kk, idxs[kk], idx_out)
        gate_out = jnp.where(lane == kk, es[kk] / den, gate_out)
    idx_ref[...] = idx_out
    gate_ref[...] = gate_out


def _mixer(attn, o_f, o_b, zm, zg, hf, wa, wb, wo, gn, lg, lb, wr, br):
    tm = 128
    const = lambda shape: pl.BlockSpec(shape, lambda i: (0,) * len(shape), pipeline_mode=pl.Buffered(1))
    row = lambda w, cb=0: pl.BlockSpec((tm, w), lambda i: (i, cb))
    return pl.pallas_call(
        _mixer_body, grid=(TOK // tm,),
        in_specs=[row(ATT_Q), row(GLA_V), row(GLA_V), row(GLA_V, Z_OG // GLA_V),
                  row(D_MODEL, 0), row(D_MODEL, 1), row(D_MODEL),
                  const((ATT_Q, D_MODEL)), const((GLA_V, D_MODEL)), const((D_MODEL, D_MODEL)),
                  const((1, GLA_DV)), const((1, D_MODEL)), const((1, D_MODEL)),
                  const((D_MODEL, 128)), const((1, 128))],
        out_specs=[row(D_MODEL), row(128), row(128)],
        out_shape=[jax.ShapeDtypeStruct((TOK, D_MODEL), F32),
                   jax.ShapeDtypeStruct((TOK, 128), jnp.int32),
                   jax.ShapeDtypeStruct((TOK, 128), F32)],
        compiler_params=_params(1), name="mixer")(attn, o_f, o_b, zm, zg, zg, hf, wa, wb, wo, gn, lg, lb, wr, br)


def _gather_body(tok_ref, h_hbm, o_ref, sem):
    def row_copy(r, t):
        return pltpu.make_async_copy(h_hbm.at[pl.ds(t, 1), :], o_ref.at[pl.ds(r, 1), :], sem)

    def start(r, c):
        row_copy(r, tok_ref[0, 0, r]).start()
        return c

    def wait(r, c):
        row_copy(r, 0).wait()
        return c

    lax.fori_loop(0, MOE_BLOCK, start, 0)
    lax.fori_loop(0, MOE_BLOCK, wait, 0)


def _gather_rows(tok_of_row, h1):
    tok3 = tok_of_row.reshape(N_BLOCKS, 1, MOE_BLOCK)
    return pl.pallas_call(
        _gather_body, grid=(N_BLOCKS,),
        in_specs=[pl.BlockSpec((1, 1, MOE_BLOCK), lambda i: (i, 0, 0), memory_space=pltpu.SMEM),
                  pl.BlockSpec(memory_space=pl.ANY)],
        out_specs=pl.BlockSpec((MOE_BLOCK, D_MODEL), lambda i: (i, 0)),
        out_shape=jax.ShapeDtypeStruct((N_ROWS, D_MODEL), F32),
        scratch_shapes=[pltpu.SemaphoreType.DMA(())],
        compiler_params=_params(1), name="gather")(tok3, h1)


def _expert_body(e_ref, st_ref, nb_ref, tail_ref, x_hbm, wg_ref, wl_ref, wd_ref, bg_ref, bl_ref, bd_ref, y_hbm,
                 xb_ref, acc_ref, stage_ref, wgb_ref, wlb_ref, wdb_ref, sem_in, sem_out):
    del e_ref
    s = pl.program_id(0)
    j = pl.program_id(1)
    nblk = nb_ref[s]
    row0 = st_ref[s]

    def rows(i):
        return pl.ds(pl.multiple_of(i * MOE_BLOCK, MOE_BLOCK), MOE_BLOCK)

    def hbm_rows(i):
        return pl.ds(pl.multiple_of(row0 + i * MOE_BLOCK, MOE_BLOCK), MOE_BLOCK)

    @pl.when((s == N_SB - 1) & (j == N_FF_TILES - 1))
    def _():
        stage_ref[0] = jnp.zeros((MOE_BLOCK, D_MODEL), F32)

        def tail_copy(i):
            dst = pl.ds(pl.multiple_of(tail_ref[0] + i * MOE_BLOCK, MOE_BLOCK), MOE_BLOCK)
            return pltpu.make_async_copy(stage_ref.at[0], y_hbm.at[dst, :], sem_out)

        def start(i, c):
            tail_copy(i).start()
            return c

        def wait(i, c):
            tail_copy(i).wait()
            return c

        lax.fori_loop(0, tail_ref[1], start, 0)
        lax.fori_loop(0, tail_ref[1], wait, 0)

    @pl.when(nblk > 0)
    def _():
        @pl.when(j == 0)
        def _():
            def in_copy(i, slot):
                return pltpu.make_async_copy(x_hbm.at[hbm_rows(i), :], stage_ref.at[slot], sem_in.at[slot])

            in_copy(0, 0).start()

            def load(i, c):
                slot = i & 1
                in_copy(i, slot).wait()

                @pl.when(i + 1 < nblk)
                def _():
                    in_copy(i + 1, 1 - slot).start()

                xb_ref[rows(i), :] = stage_ref[slot].astype(BF16)
                acc_ref[rows(i), :] = jnp.broadcast_to(bd_ref[0], (MOE_BLOCK, D_MODEL))
                return c

            lax.fori_loop(0, nblk, load, 0)

        wgb_ref[...] = wg_ref[0].astype(BF16)
        wlb_ref[...] = wl_ref[0].astype(BF16)
        wdb_ref[...] = wd_ref[0].astype(BF16)

        def block(i, c):
            x = xb_ref[rows(i), :]
            g = jnp.minimum(_dot(x, wgb_ref[...]) + bg_ref[0], SWIGLU_LIMIT)
            lin = jnp.clip(_dot(x, wlb_ref[...]) + bl_ref[0], -SWIGLU_LIMIT, SWIGLU_LIMIT)
            act = g * _sigmoid(SWIGLU_ALPHA * g) * (lin + 1.0)
            acc_ref[rows(i), :] += _dot(act.astype(BF16), wdb_ref[...])
            return c

        lax.fori_loop(0, nblk, block, 0)

        @pl.when(j == N_FF_TILES - 1)
        def _():
            def out_copy(i):
                return pltpu.make_async_copy(acc_ref.at[rows(i), :], y_hbm.at[hbm_rows(i), :], sem_out)

            def start(i, c):
                out_copy(i).start()
                return c

            def wait(i, c):
                out_copy(i).wait()
                return c

            lax.fori_loop(0, nblk, start, 0)
            lax.fori_loop(0, nblk, wait, 0)


def _experts(sb_e, sb_row, sb_nblk, tail, x_rows, w_up, b_up, w_down, b_down):
    def jj(j, nb, s):
        return jnp.where(nb[s] > 0, j, N_FF_TILES - 1)

    grid_spec = pltpu.PrefetchScalarGridSpec(
        num_scalar_prefetch=4, grid=(N_SB, N_FF_TILES),
        in_specs=[pl.BlockSpec(memory_space=pl.ANY),
                  pl.BlockSpec((1, D_MODEL, FF_TILE), lambda s, j, e, st, nb, tl: (e[s], 0, jj(j, nb, s))),
                  pl.BlockSpec((1, D_MODEL, FF_TILE), lambda s, j, e, st, nb, tl: (e[s], 0, N_FF_TILES + jj(j, nb, s))),
                  pl.BlockSpec((1, FF_TILE, D_MODEL), lambda s, j, e, st, nb, tl: (e[s], jj(j, nb, s), 0)),
                  pl.BlockSpec((1, 1, FF_TILE), lambda s, j, e, st, nb, tl: (e[s], 0, jj(j, nb, s))),
                  pl.BlockSpec((1, 1, FF_TILE), lambda s, j, e, st, nb, tl: (e[s], 0, N_FF_TILES + jj(j, nb, s))),
                  pl.BlockSpec((1, 1, D_MODEL), lambda s, j, e, st, nb, tl: (e[s], 0, 0))],
        out_specs=pl.BlockSpec(memory_space=pl.ANY),
        scratch_shapes=[pltpu.VMEM((SB_ROWS, D_MODEL), BF16),
                        pltpu.VMEM((SB_ROWS, D_MODEL), F32),
                        pltpu.VMEM((2, MOE_BLOCK, D_MODEL), F32),
                        pltpu.VMEM((D_MODEL, FF_TILE), BF16),
                        pltpu.VMEM((D_MODEL, FF_TILE), BF16),
                        pltpu.VMEM((FF_TILE, D_MODEL), BF16),
                        pltpu.SemaphoreType.DMA((2,)),
                        pltpu.SemaphoreType.DMA(())])
    return pl.pallas_call(
        _expert_body, grid_spec=grid_spec,
        out_shape=jax.ShapeDtypeStruct((N_ROWS, D_MODEL), F32),
        compiler_params=_params(2), name="experts")(
            sb_e, sb_row, sb_nblk, tail, x_rows, w_up, w_up, w_down,
            b_up.reshape(N_EXPERTS, 1, 2 * D_FF), b_up.reshape(N_EXPERTS, 1, 2 * D_FF),
            b_down.reshape(N_EXPERTS, 1, D_MODEL))


CMB_T = 64


def _combine_body(dest_ref, gate_ref, h1_ref, lg_ref, lb_ref, y_hbm, hf_ref, hb_ref, ybuf, sem):
    def row_copy(a, r):
        return pltpu.make_async_copy(y_hbm.at[pl.ds(r, 1), :], ybuf.at[a % TOP_K, pl.ds(a // TOP_K, 1), :], sem)

    def start(a, c):
        row_copy(a, dest_ref[0, 0, a]).start()
        return c

    def wait(a, c):
        row_copy(a, 0).wait()
        return c

    lax.fori_loop(0, CMB_T * TOP_K, start, 0)
    lax.fori_loop(0, CMB_T * TOP_K, wait, 0)
    gate = gate_ref[...]
    y = gate[:, 0:1] * ybuf[0]
    for kk in range(1, TOP_K):
        y = y + gate[:, kk:kk + 1] * ybuf[kk]
    h2 = _layer_norm(DN_ALPHA * h1_ref[...] + y, lg_ref[...], lb_ref[...])
    hf_ref[...] = h2
    hb_ref[...] = h2.astype(BF16)


def _combine(dest, gates, h1, lg, lb, y_rows):
    nt = TOK // CMB_T
    dest3 = dest.reshape(nt, 1, CMB_T * TOP_K)
    row = lambda w: pl.BlockSpec((CMB_T, w), lambda i: (i, 0))
    vec = pl.BlockSpec((1, D_MODEL), lambda i: (0, 0))
    return pl.pallas_call(
        _combine_body, grid=(nt,),
        in_specs=[pl.BlockSpec((1, 1, CMB_T * TOP_K), lambda i: (i, 0, 0), memory_space=pltpu.SMEM),
                  row(128), row(D_MODEL), vec, vec, pl.BlockSpec(memory_space=pl.ANY)],
        out_specs=[row(D_MODEL), row(D_MODEL)],
        out_shape=[jax.ShapeDtypeStruct((TOK, D_MODEL), F32), jax.ShapeDtypeStruct((TOK, D_MODEL), BF16)],
        scratch_shapes=[pltpu.VMEM((TOP_K, CMB_T, D_MODEL), F32), pltpu.SemaphoreType.DMA(())],
        compiler_params=_params(1), name="combine")(dest3, gates, h1, lg, lb, y_rows)


def _ple_body(hf_ref, hb_ref, p_ref, wg_ref, wp_ref, lg_ref, lb_ref, o_ref):
    gate = _sigmoid(_dot(hb_ref[...], wg_ref[...]))
    proj = _dot(p_ref[...].astype(BF16), wp_ref[...])
    o_ref[...] = _layer_norm(DN_ALPHA * hf_ref[...] + gate * proj, lg_ref[...], lb_ref[...])


def _ple(hf, hb, p2, wg, wp, lg, lb):
    tm = 256
    const = lambda shape: pl.BlockSpec(shape, lambda i: (0,) * len(shape), pipeline_mode=pl.Buffered(1))
    row = lambda w: pl.BlockSpec((tm, w), lambda i: (i, 0))
    return pl.pallas_call(
        _ple_body, grid=(TOK // tm,),
        in_specs=[row(D_MODEL), row(D_MODEL), row(PLE_DIM), const((D_MODEL, D_MODEL)), const((PLE_DIM, D_MODEL)),
                  const((1, D_MODEL)), const((1, D_MODEL))],
        out_specs=row(D_MODEL),
        out_shape=jax.ShapeDtypeStruct((TOK, D_MODEL), F32),
        compiler_params=_params(1), name="ple")(hf, hb, p2, wg, wp, lg, lb)


def _routing(top_idx):
    flat_e = top_idx.reshape(-1)
    onehot = (flat_e[:, None] == jnp.arange(N_EXPERTS, dtype=jnp.int32)[None, :]).astype(jnp.int32)
    csum = jnp.cumsum(onehot, axis=0)
    rank = jnp.sum(onehot * csum, axis=1) - 1
    counts = csum[-1]
    padded = (counts + MOE_BLOCK - 1) // MOE_BLOCK * MOE_BLOCK
    padded_end = jnp.cumsum(padded)
    padded_start = padded_end - padded
    dest = (padded_start[flat_e] + rank).astype(jnp.int32)
    tok_of_row = jnp.zeros((N_ROWS,), jnp.int32).at[dest].set(
        jnp.arange(TOK * TOP_K, dtype=jnp.int32) // TOP_K)
    nb = padded // MOE_BLOCK
    n_sb = (nb + SB_BLOCKS - 1) // SB_BLOCKS
    sb_end = jnp.cumsum(n_sb)
    total = sb_end[-1]
    s = jnp.arange(N_SB, dtype=jnp.int32)
    s_eff = jnp.minimum(s, total - 1)
    e = jnp.clip(jnp.searchsorted(sb_end, s_eff, side="right"), 0, N_EXPERTS - 1).astype(jnp.int32)
    local = s_eff - (sb_end[e] - n_sb[e])
    valid = s < total
    sb_nblk = jnp.where(valid, jnp.clip(nb[e] - local * SB_BLOCKS, 0, SB_BLOCKS), 0).astype(jnp.int32)
    sb_row = (padded_start[e] + local * SB_ROWS).astype(jnp.int32)
    tail = jnp.stack([padded_end[-1], (N_ROWS - padded_end[-1]) // MOE_BLOCK]).astype(jnp.int32)
    return dest, tok_of_row, e, sb_row, sb_nblk, tail


def _rope_tables():
    rows = SEQ // GRID_W
    row = jnp.repeat(jnp.arange(rows), GRID_W)
    col = jnp.tile(jnp.arange(GRID_W), rows)
    n_pairs = HEAD_DIM // 4
    inv_freq = ROPE_BASE ** (-jnp.arange(n_pairs, dtype=F32) / n_pairs)
    ang = jnp.concatenate([row[:, None] * inv_freq, col[:, None] * inv_freq], -1)
    cos_full = jnp.repeat(jnp.cos(ang), 2, axis=-1)
    sin = jnp.sin(ang)
    sin_signed = jnp.stack([-sin, sin], axis=-1).reshape(SEQ, HEAD_DIM)
    return cos_full, sin_signed


def kernel(x, p, in_ln_g, in_ln_b, w_in, q_norm, k_norm, w_lr_f, b_lr_f, w_lr_b, b_lr_b, gla_norm, w_br_a, w_br_b, w_o, ln1_g, ln1_b, w_router, b_router, w_up, b_up, w_down, b_down, ln2_g, ln2_b, w_ple_gate, w_ple_proj, ln3_g, ln3_b):
    assert x.shape == (BATCH, SEQ, D_MODEL) and w_in.shape[0] == DEPTH == 1
    win = w_in[0]
    hf, hb = _ln0(x.reshape(TOK, D_MODEL), in_ln_g, in_ln_b)

    zm = _inproj(hb, win, MAIN_W // 512, Z_TILE_OF_W_TILE, False, "inproj_main")
    zg = _inproj(hb, win[:, MAIN_W + LR_W:], 2 * D_MODEL // 512, tuple(range(8)), True, "inproj_gates")

    w2 = jnp.zeros((LR_W, 2 * GLA_K), F32)
    w2 = w2.at[:GLA_RANK, :GLA_K].set(w_lr_f[0]).at[GLA_RANK:, GLA_K:].set(w_lr_b[0])
    b2 = jnp.concatenate([b_lr_f[0], b_lr_b[0]]).reshape(1, -1)
    bc_f, bc_b = _decay(hb, win[:, MAIN_W:MAIN_W + LR_W].astype(BF16), w2, b2)

    cos_full, sin_signed = _rope_tables()
    qr, kr = _qkprep(zm, cos_full, sin_signed, q_norm[0], k_norm[0])
    attn = _attention(qr, kr, zm)
    o_f = _gla(zm, bc_f, False)
    o_b = _gla(zm, bc_b, True)

    wr = jnp.zeros((D_MODEL, 128), F32).at[:, :N_EXPERTS].set(w_router[0])
    br = jnp.full((1, 128), -jnp.inf, F32).at[0, :N_EXPERTS].set(b_router[0])
    h1, idx_pad, gate_pad = _mixer(
        attn, o_f, o_b, zm, zg, hf, w_br_a[0].astype(BF16), w_br_b[0].astype(BF16), w_o[0].astype(BF16),
        gla_norm[0].reshape(1, -1), ln1_g[0].reshape(1, -1), ln1_b[0].reshape(1, -1), wr, br)

    dest, tok_of_row, sb_e, sb_row, sb_nblk, tail = _routing(idx_pad[:, :TOP_K])
    x_rows = _gather_rows(tok_of_row, h1)
    y_rows = _experts(sb_e, sb_row, sb_nblk, tail, x_rows, w_up[0], b_up[0], w_down[0], b_down[0])
    h2f, h2b = _combine(dest, gate_pad, h1, ln2_g[0].reshape(1, -1), ln2_b[0].reshape(1, -1), y_rows)

    out = _ple(h2f, h2b, p[0].reshape(TOK, PLE_DIM), w_ple_gate[0].astype(BF16), w_ple_proj[0].astype(BF16),
               ln3_g[0].reshape(1, -1), ln3_b[0].reshape(1, -1))
    return out.reshape(BATCH, SEQ, D_MODEL)
```

```python
import functools

import jax
import jax.numpy as jnp
from jax import lax
from jax.experimental import pallas as pl
from jax.experimental.pallas import tpu as pltpu

F32 = jnp.float32
BF16 = jnp.bfloat16

D_MODEL = 2048
BATCH = 2
SEQ = 4096
TOK = BATCH * SEQ
PLE_DIM = 256
GRID_W = 64
ATT_HEADS = 8
ATT_KV_HEADS = 2
ATT_GROUP = ATT_HEADS // ATT_KV_HEADS
HEAD_DIM = 128
ROPE_BASE = 10000.0
GLA_HEADS = 4
GLA_DK = 128
GLA_DV = 256
GLA_RANK = 16
GLA_TAU = 16.0
GLA_CHUNK = 64
GLA_SUB = 16
N_EXPERTS = 32
TOP_K = 4
D_FF = D_MODEL
SWIGLU_LIMIT = 7.0
SWIGLU_ALPHA = 1.702
MOE_BLOCK = 128
ATT_Q = ATT_HEADS * HEAD_DIM
ATT_KV = ATT_KV_HEADS * HEAD_DIM
GLA_K = GLA_HEADS * GLA_DK
GLA_V = GLA_HEADS * GLA_DV
MAIN_W = ATT_Q + 2 * ATT_KV + 2 * GLA_K + 2 * GLA_V
LR_W = 2 * GLA_RANK
DEPTH = 1
DN_ALPHA = (2 * DEPTH) ** 0.25
LN_EPS = 1e-5
RMS_EPS = 1e-6

Z_TILE_OF_W_TILE = (0, 1, 6, 7, 8, 2, 3, 4, 5)
Z_Q, Z_VB, Z_OG, Z_KA, Z_VA, Z_QB, Z_KB = 0, 1024, 2048, 3072, 3328, 3584, 4096

VMEM_LIMIT = 56 * 1024 * 1024

N_ROWS = TOK * TOP_K + N_EXPERTS * MOE_BLOCK
N_BLOCKS = N_ROWS // MOE_BLOCK
SB_BLOCKS = 10
SB_ROWS = SB_BLOCKS * MOE_BLOCK
N_SB = N_EXPERTS + -(-N_BLOCKS // SB_BLOCKS)
SUBLANES = 8
FF_TILE = 256
N_FF_TILES = D_FF // FF_TILE


def _params(n_axes, vmem=None):
    return pltpu.CompilerParams(dimension_semantics=("arbitrary",) * n_axes,
                                vmem_limit_bytes=vmem or VMEM_LIMIT)


def _sigmoid(x):
    return 1.0 / (1.0 + jnp.exp(-x))


def _layer_norm(y, g, b):
    mu = jnp.mean(y, axis=-1, keepdims=True)
    yc = y - mu
    var = jnp.mean(yc * yc, axis=-1, keepdims=True)
    return yc * lax.rsqrt(var + LN_EPS) * g + b


def _dot(a, b):
    return jnp.dot(a, b, preferred_element_type=F32)


def _dot_nt(a, b):
    return lax.dot_general(a, b, (((1,), (1,)), ((), ())), preferred_element_type=F32)


def _dot_tn(a, b):
    return lax.dot_general(a, b, (((0,), (0,)), ((), ())), preferred_element_type=F32)


def _ln0_body(x_ref, g_ref, b_ref, hf_ref, hb_ref):
    y = _layer_norm(x_ref[...], g_ref[...], b_ref[...])
    hf_ref[...] = y
    hb_ref[...] = y.astype(BF16)


def _ln0(x2, g, b):
    tm = 256
    row = pl.BlockSpec((tm, D_MODEL), lambda i: (i, 0))
    vec = pl.BlockSpec((1, D_MODEL), lambda i: (0, 0))
    return pl.pallas_call(
        _ln0_body, grid=(TOK // tm,), in_specs=[row, vec, vec], out_specs=[row, row],
        out_shape=[jax.ShapeDtypeStruct((TOK, D_MODEL), F32), jax.ShapeDtypeStruct((TOK, D_MODEL), BF16)],
        compiler_params=_params(1), name="ln0")(x2, g.reshape(1, -1), b.reshape(1, -1))


def _inproj_body(perm_ref, a_ref, w_ref, o_ref, wb_ref, *, gate):
    del perm_ref

    @pl.when(pl.program_id(1) == 0)
    def _():
        wb_ref[...] = w_ref[...].astype(BF16)

    acc = _dot(a_ref[...], wb_ref[...])
    if gate:
        acc = _sigmoid(acc)
    o_ref[...] = acc.astype(o_ref.dtype)


def _inproj(hb, w, n_tiles, tile_perm, gate, name):
    tm, tn = 512, 512
    grid_spec = pltpu.PrefetchScalarGridSpec(
        num_scalar_prefetch=1, grid=(n_tiles, TOK // tm),
        in_specs=[pl.BlockSpec((tm, D_MODEL), lambda n, m, p: (m, 0)),
                  pl.BlockSpec((D_MODEL, tn), lambda n, m, p: (0, n))],
        out_specs=pl.BlockSpec((tm, tn), lambda n, m, p: (m, p[n])),
        scratch_shapes=[pltpu.VMEM((D_MODEL, tn), BF16)])
    return pl.pallas_call(
        functools.partial(_inproj_body, gate=gate), grid_spec=grid_spec,
        out_shape=jax.ShapeDtypeStruct((TOK, n_tiles * tn), BF16),
        compiler_params=_params(2), name=name)(jnp.asarray(tile_perm, jnp.int32), hb, w)


def _decay_body(h_ref, wlr_ref, w2_ref, b2_ref, trif_ref, trib_ref, bf_ref, bb_ref):
    zlr = _dot(h_ref[...], wlr_ref[...])
    pre = jnp.dot(zlr, w2_ref[...], preferred_element_type=F32,
                  precision=lax.Precision.HIGHEST) + b2_ref[...]
    la = (jnp.minimum(pre, 0.0) - jnp.log1p(jnp.exp(-jnp.abs(pre)))) * (1.0 / GLA_TAU)
    hi = la.astype(BF16)
    lo = (la - hi.astype(F32)).astype(BF16)
    bf_ref[...] = _dot(trif_ref[...], hi[:, :GLA_K]) + _dot(trif_ref[...], lo[:, :GLA_K])
    bb_ref[...] = _dot(trib_ref[...], hi[:, GLA_K:]) + _dot(trib_ref[...], lo[:, GLA_K:])


def _decay(hb, wlr, w2, b2):
    tm = 512
    r = jnp.arange(tm)
    same = (r[:, None] // GLA_CHUNK) == (r[None, :] // GLA_CHUNK)
    trif = (same & (r[None, :] <= r[:, None])).astype(BF16)
    trib = (same & (r[None, :] >= r[:, None])).astype(BF16)
    full = lambda shape: pl.BlockSpec(shape, lambda i: (0, 0))
    out = pl.BlockSpec((tm, GLA_K), lambda i: (i, 0))
    return pl.pallas_call(
        _decay_body, grid=(TOK // tm,),
        in_specs=[pl.BlockSpec((tm, D_MODEL), lambda i: (i, 0)), full((D_MODEL, LR_W)),
                  full((LR_W, 2 * GLA_K)), full((1, 2 * GLA_K)), full((tm, tm)), full((tm, tm))],
        out_specs=[out, out],
        out_shape=[jax.ShapeDtypeStruct((TOK, GLA_K), F32)] * 2,
        compiler_params=_params(1), name="decay")(hb, wlr, w2, b2, trif, trib)


def _qkprep_body(q_ref, k_ref, cos_ref, sin_ref, qn_ref, kn_ref, qo_ref, ko_ref):
    cos = cos_ref[...]
    sin = sin_ref[...]
    lane = lax.broadcasted_iota(jnp.int32, cos.shape, 1)
    even = (lane % 2) == 0

    def one(x, gain, scale):
        x = x.astype(F32)
        x = x * lax.rsqrt(jnp.mean(x * x, axis=-1, keepdims=True) + RMS_EPS) * gain
        partner = jnp.where(even, pltpu.roll(x, HEAD_DIM - 1, 1), pltpu.roll(x, 1, 1))
        return ((x * cos + partner * sin) * scale).astype(BF16)

    for hh in range(ATT_HEADS):
        sl = slice(hh * HEAD_DIM, (hh + 1) * HEAD_DIM)
        qo_ref[:, sl] = one(q_ref[:, sl], qn_ref[...], HEAD_DIM ** -0.5)
    for hh in range(ATT_KV_HEADS):
        sl = slice(hh * HEAD_DIM, (hh + 1) * HEAD_DIM)
        ko_ref[:, sl] = one(k_ref[:, sl], kn_ref[...], 1.0)


def _qkprep(zm, cos_full, sin_signed, q_norm, k_norm):
    tm = 256
    nrow = SEQ // tm
    tab = pl.BlockSpec((tm, HEAD_DIM), lambda i: (i % nrow, 0))
    vec = pl.BlockSpec((1, HEAD_DIM), lambda i: (0, 0))
    return pl.pallas_call(
        _qkprep_body, grid=(TOK // tm,),
        in_specs=[pl.BlockSpec((tm, ATT_Q), lambda i: (i, Z_Q // ATT_Q)),
                  pl.BlockSpec((tm, ATT_KV), lambda i: (i, Z_KA // ATT_KV)), tab, tab, vec, vec],
        out_specs=[pl.BlockSpec((tm, ATT_Q), lambda i: (i, 0)), pl.BlockSpec((tm, ATT_KV), lambda i: (i, 0))],
        out_shape=[jax.ShapeDtypeStruct((TOK, ATT_Q), BF16), jax.ShapeDtypeStruct((TOK, ATT_KV), BF16)],
        compiler_params=_params(1), name="qkprep")(zm, zm, cos_full, sin_signed,
                                                   q_norm.reshape(1, -1), k_norm.reshape(1, -1))


ATT_TQ = 256
ATT_TK = 512


def _attn_body(q_ref, k_ref, v_ref, o_ref):
    q = jnp.concatenate([q_ref[:, g * HEAD_DIM:(g + 1) * HEAD_DIM] for g in range(ATT_GROUP)], axis=0)
    rows = ATT_GROUP * ATT_TQ

    def step(c, carry):
        m, l, acc = carry
        r0 = pl.multiple_of(c * ATT_TK, ATT_TK)
        s = _dot_nt(q, k_ref[pl.ds(r0, ATT_TK), :])
        m_new = jnp.maximum(m, jnp.max(s, axis=-1, keepdims=True))
        alpha = jnp.exp(m - m_new)
        p = jnp.exp(s - m_new)
        l = alpha * l + jnp.sum(p, axis=-1, keepdims=True)
        acc = alpha * acc + _dot(p.astype(BF16), v_ref[pl.ds(r0, ATT_TK), :])
        return m_new, l, acc

    m0 = jnp.full((rows, 1), -jnp.inf, F32)
    l0 = jnp.zeros((rows, 1), F32)
    a0 = jnp.zeros((rows, HEAD_DIM), F32)
    _, l, acc = lax.fori_loop(0, SEQ // ATT_TK, step, (m0, l0, a0))
    out = (acc / l).astype(BF16)
    for g in range(ATT_GROUP):
        o_ref[:, g * HEAD_DIM:(g + 1) * HEAD_DIM] = out[g * ATT_TQ:(g + 1) * ATT_TQ]


def _attention(qr, kr, zm):
    nq = SEQ // ATT_TQ
    gw = ATT_GROUP * HEAD_DIM
    qspec = pl.BlockSpec((ATT_TQ, gw), lambda b, j, i: (b * nq + i, j))
    return pl.pallas_call(
        _attn_body, grid=(BATCH, ATT_KV_HEADS, nq),
        in_specs=[qspec,
                  pl.BlockSpec((SEQ, HEAD_DIM), lambda b, j, i: (b, j)),
                  pl.BlockSpec((SEQ, HEAD_DIM), lambda b, j, i: (b, Z_VA // HEAD_DIM + j))],
        out_specs=qspec,
        out_shape=jax.ShapeDtypeStruct((TOK, ATT_Q), BF16),
        compiler_params=_params(3), name="attn")(qr, kr, zm)


GLA_CB = 4
GLA_RB = GLA_CB * GLA_CHUNK
N_SUB = GLA_CHUNK // GLA_SUB


def _gla_body(q_ref, k_ref, v_ref, bc_ref, o_ref, st_ref, kf_ref, bs_ref, *, rev):
    @pl.when(pl.program_id(1) == 0)
    def _():
        st_ref[...] = jnp.zeros_like(st_ref)

    C, SUB = GLA_CHUNK, GLA_SUB
    rowc = lax.broadcasted_iota(jnp.int32, (C, GLA_DK), 0)
    rows_s = lax.broadcasted_iota(jnp.int32, (SUB, 128), 0)
    lane_s = lax.broadcasted_iota(jnp.int32, (SUB, 128), 1)

    def chunk(ci, carry):
        c = (GLA_CB - 1 - ci) if rev else ci
        r0 = pl.multiple_of(c * C, C)
        for hh in range(GLA_HEADS):
            ksl = slice(hh * GLA_DK, (hh + 1) * GLA_DK)
            vsl = slice(hh * GLA_DV, (hh + 1) * GLA_DV)
            q = q_ref[pl.ds(r0, C), ksl].astype(F32) * (GLA_DK ** -0.5)
            k = k_ref[pl.ds(r0, C), ksl].astype(F32)
            v = v_ref[pl.ds(r0, C), vsl]
            bc = bc_ref[pl.ds(r0, C), ksl]
            kf_ref[hh] = k
            bs_ref[hh] = bc
            st = st_ref[hh]
            blast = bc[0:1] if rev else bc[C - 1:C]
            o_inter = _dot_nt((q * jnp.exp(bc)).astype(BF16), st.astype(BF16))
            kdec = k * jnp.exp(blast - bc)
            st_ref[hh] = st * jnp.exp(blast) + _dot_tn(v, kdec.astype(BF16))

            a_rows = []
            for si in range(N_SUB):
                lo, hi = si * SUB, (si + 1) * SUB
                q_s, b_s = q[lo:hi], bc[lo:hi]
                has_earlier = (si < N_SUB - 1) if rev else (si > 0)
                if has_earlier:
                    ref_row = bc[hi:hi + 1] if rev else bc[lo - 1:lo]
                    earlier = (rowc >= hi) if rev else (rowc < lo)
                    qt = q_s * jnp.exp(b_s - ref_row)
                    kt = k * jnp.exp(jnp.where(earlier, ref_row - bc, -jnp.inf))
                    a = _dot_nt(qt.astype(BF16), kt.astype(BF16))
                else:
                    a = jnp.zeros((SUB, C), F32)
                diag = jnp.zeros((SUB, 128), F32)
                for jl in range(SUB):
                    j = lo + jl
                    d = jnp.minimum(b_s - bs_ref[hh, j:j + 1, :], 0.0)
                    col = jnp.sum(q_s * kf_ref[hh, j:j + 1, :] * jnp.exp(d), axis=-1, keepdims=True)
                    diag = jnp.where(lane_s == j, col, diag)
                keep = (lane_s >= rows_s + lo) if rev else (lane_s <= rows_s + lo)
                diag = jnp.where(keep & (lane_s >= lo) & (lane_s < hi), diag, 0.0)
                a_rows.append(a + diag[:, :C])
            a_full = jnp.concatenate(a_rows, axis=0)
            o_ref[pl.ds(r0, C), vsl] = o_inter + _dot(a_full.astype(BF16), v)
        return carry

    lax.fori_loop(0, GLA_CB, chunk, 0)


def _gla(zm, bcum, rev):
    ncb = SEQ // GLA_RB
    if rev:
        row = lambda b, c: b * ncb + (ncb - 1 - c)
    else:
        row = lambda b, c: b * ncb + c
    return pl.pallas_call(
        functools.partial(_gla_body, rev=rev), grid=(BATCH, ncb),
        in_specs=[pl.BlockSpec((GLA_RB, GLA_K), lambda b, c: (row(b, c), Z_QB // GLA_K)),
                  pl.BlockSpec((GLA_RB, GLA_K), lambda b, c: (row(b, c), Z_KB // GLA_K)),
                  pl.BlockSpec((GLA_RB, GLA_V), lambda b, c: (row(b, c), Z_VB // GLA_V)),
                  pl.BlockSpec((GLA_RB, GLA_K), lambda b, c: (row(b, c), 0))],
        out_specs=pl.BlockSpec((GLA_RB, GLA_V), lambda b, c: (row(b, c), 0)),
        out_shape=jax.ShapeDtypeStruct((TOK, GLA_V), F32),
        scratch_shapes=[pltpu.VMEM((GLA_HEADS, GLA_DV, GLA_DK), F32),
                        pltpu.VMEM((GLA_HEADS, GLA_CHUNK, GLA_DK), F32),
                        pltpu.VMEM((GLA_HEADS, GLA_CHUNK, GLA_DK), F32)],
        compiler_params=_params(2), name="gla_bwd" if rev else "gla_fwd")(zm, zm, zm, bcum)


def _mixer_body(attn_ref, of_ref, ob_ref, og_ref, ga_ref, gb_ref, h_ref,
                wa_ref, wb_ref, wo_ref, gn_ref, lg_ref, lb_ref, wr_ref, br_ref,
                h1_ref, idx_ref, gate_ref):
    osum = of_ref[...] + ob_ref[...]
    og = og_ref[...].astype(F32)
    gn = gn_ref[...]
    parts = []
    for hh in range(GLA_HEADS):
        sl = slice(hh * GLA_DV, (hh + 1) * GLA_DV)
        x = osum[:, sl]
        g = og[:, sl]
        xn = x * lax.rsqrt(jnp.mean(x * x, axis=-1, keepdims=True) + RMS_EPS) * gn
        parts.append((xn * (g * _sigmoid(g))).astype(BF16))
    onorm = jnp.concatenate(parts, axis=1)
    ya = _dot(attn_ref[...], wa_ref[...])
    yb = _dot(onorm, wb_ref[...])
    m = ga_ref[...].astype(F32) * ya + gb_ref[...].astype(F32) * yb
    mix = _dot(m.astype(BF16), wo_ref[...])
    h1 = _layer_norm(DN_ALPHA * h_ref[...] + mix, lg_ref[...], lb_ref[...])
    h1_ref[...] = h1

    logits = jnp.dot(h1, wr_ref[...], preferred_element_type=F32,
                     precision=lax.Precision.HIGHEST) + br_ref[...]
    lane = lax.broadcasted_iota(jnp.int32, logits.shape, 1)
    x = logits
    vals, idxs = [], []
    for _ in range(TOP_K):
        mx = jnp.max(x, axis=-1, keepdims=True)
        ix = jnp.min(jnp.where(x == mx, lane, 128), axis=-1, keepdims=True)
        vals.append(mx)
        idxs.append(ix)
        x = jnp.where(lane == ix, -jnp.inf, x)
    es = [jnp.exp(vv - vals[0]) for vv in vals]
    den = es[0] + es[1] + es[2] + es[3]
    idx_out = jnp.zeros(logits.shape, jnp.int32)
    gate_out = jnp.zeros(logits.shape, F32)
    for kk in range(TOP_K):
        idx_out = jnp.where(lane == kk, idxs[kk], idx_out)
        gate_out = jnp.where(lane == kk, es[kk] / den, gate_out)
    idx_ref[...] = idx_out
    gate_ref[...] = gate_out


def _mixer(attn, o_f, o_b, zm, zg, hf, wa, wb, wo, gn, lg, lb, wr, br):
    tm = 128
    const = lambda shape: pl.BlockSpec(shape, lambda i: (0,) * len(shape), pipeline_mode=pl.Buffered(1))
    row = lambda w, cb=0: pl.BlockSpec((tm, w), lambda i: (i, cb))
    return pl.pallas_call(
        _mixer_body, grid=(TOK // tm,),
        in_specs=[row(ATT_Q), row(GLA_V), row(GLA_V), row(GLA_V, Z_OG // GLA_V),
                  row(D_MODEL, 0), row(D_MODEL, 1), row(D_MODEL),
                  const((ATT_Q, D_MODEL)), const((GLA_V, D_MODEL)), const((D_MODEL, D_MODEL)),
                  const((1, GLA_DV)), const((1, D_MODEL)), const((1, D_MODEL)),
                  const((D_MODEL, 128)), const((1, 128))],
        out_specs=[row(D_MODEL), row(128), row(128)],
        out_shape=[jax.ShapeDtypeStruct((TOK, D_MODEL), F32),
                   jax.ShapeDtypeStruct((TOK, 128), jnp.int32),
                   jax.ShapeDtypeStruct((TOK, 128), F32)],
        compiler_params=_params(1), name="mixer")(attn, o_f, o_b, zm, zg, zg, hf, wa, wb, wo, gn, lg, lb, wr, br)


GROUP_SIZES = (512, 256, 128)


def _expert_body(e_ref, nb_ref, nv_ref, tab_hbm, h1_hbm, wg_ref, wl_ref, wd_ref, bg_ref, bl_ref, bd_ref,
                 y_hbm, gbuf, xb_ref, acc_ref, wgb_ref, wlb_ref, wdb_ref, ids, sem_ids, sem_g, sem_s):
    del e_ref
    s = pl.program_id(0)
    j = pl.program_id(1)
    nblk = nb_ref[s]
    slot = s & 1
    nxt = jnp.minimum(s + 1, N_SB - 1)
    nblk_next = jnp.where(s + 1 < N_SB, nb_ref[nxt], 0)
    tiles = MOE_BLOCK // SUBLANES

    def ids_copy(sb, sl):
        return pltpu.make_async_copy(tab_hbm.at[sb], ids.at[pl.ds(sl * 2 * SB_BLOCKS, 2 * SB_BLOCKS)],
                                     sem_ids.at[sl])

    def gather_block(sl, b):
        id_row = sl * 2 * SB_BLOCKS + b

        def issue(t, c):
            for u in range(SUBLANES):
                tok = ids[id_row, t * SUBLANES + u]
                pltpu.make_async_copy(h1_hbm.at[pl.ds(tok >> 3, 1), pl.ds(tok & 7, 1), :],
                                      gbuf.at[pl.ds(b * tiles + t, 1), pl.ds(u, 1), :], sem_g).start()
            return c

        lax.fori_loop(0, tiles, issue, 0)

    def gather_wait_block(b):
        pltpu.make_async_copy(h1_hbm.at[pl.ds(0, tiles)], gbuf.at[pl.ds(b * tiles, tiles)], sem_g).wait()

    def loop_blocks(n, fn):
        def body(b, c):
            fn(b)
            return c

        lax.fori_loop(0, n, body, 0)

    @pl.when((s == 0) & (j == 0))
    def _():
        ids_copy(0, 0).start()
        ids_copy(0, 0).wait()
        loop_blocks(nblk, lambda b: gather_block(0, b))

    @pl.when(nblk > 0)
    def _():
        @pl.when(j == 0)
        def _():
            @pl.when(nblk_next > 0)
            def _():
                ids_copy(nxt, 1 - slot).start()

            loop_blocks(nblk, gather_wait_block)

            def take_block(b):
                x = gbuf[pl.ds(b * tiles, tiles)].reshape(MOE_BLOCK, D_MODEL)
                xb_ref[pl.ds(pl.multiple_of(b * MOE_BLOCK, MOE_BLOCK), MOE_BLOCK), :] = x.astype(BF16)
                acc_ref[pl.ds(b * tiles, tiles)] = jnp.broadcast_to(bd_ref[0], (tiles, SUBLANES, D_MODEL))

            loop_blocks(nblk, take_block)

        @pl.when((j == 1) & (nblk_next > 0))
        def _():
            ids_copy(nxt, 1 - slot).wait()

        @pl.when(j >= 1)
        def _():
            for u in range(2):
                b = (j - 1) * 2 + u

                @pl.when(b < nblk_next)
                def _():
                    gather_block(1 - slot, b)

        wgb_ref[...] = wg_ref[0].astype(BF16)
        wlb_ref[...] = wl_ref[0].astype(BF16)
        wdb_ref[...] = wd_ref[0].astype(BF16)

        def group(r0, size):
            x = xb_ref[pl.ds(pl.multiple_of(r0, MOE_BLOCK), size), :]
            g = jnp.minimum(_dot(x, wgb_ref[...]) + bg_ref[0], SWIGLU_LIMIT)
            lin = jnp.clip(_dot(x, wlb_ref[...]) + bl_ref[0], -SWIGLU_LIMIT, SWIGLU_LIMIT)
            act = g * _sigmoid(SWIGLU_ALPHA * g) * (lin + 1.0)
            upd = _dot(act.astype(BF16), wdb_ref[...])
            acc_ref[pl.ds(r0 // SUBLANES, size // SUBLANES)] += upd.reshape(size // SUBLANES, SUBLANES, D_MODEL)

        big = GROUP_SIZES[0]
        n_big = nblk // (big // MOE_BLOCK)

        def big_group(i, c):
            group(i * big, big)
            return c

        lax.fori_loop(0, n_big, big_group, 0)
        done = n_big * (big // MOE_BLOCK)
        for size in GROUP_SIZES[1:]:
            take = ((nblk - done) // (size // MOE_BLOCK)) > 0

            @pl.when(take)
            def _(done=done, size=size):
                group(done * MOE_BLOCK, size)

            done = done + jnp.where(take, size // MOE_BLOCK, 0)

        @pl.when(j == N_FF_TILES - 1)
        def _():
            nvalid = nv_ref[s]

            dst_row = slot * 2 * SB_BLOCKS + SB_BLOCKS

            def row_copy(t, u, dst):
                return pltpu.make_async_copy(acc_ref.at[pl.ds(t, 1), pl.ds(u, 1), :],
                                             y_hbm.at[pl.ds(dst >> 3, 1), pl.ds(dst & 7, 1), :], sem_s)

            def issue_tile(t, c):
                id_row = dst_row + t // tiles
                col = (t % tiles) * SUBLANES
                for u in range(SUBLANES):
                    row_copy(t, u, ids[id_row, col + u]).start()
                return c

            def issue_row(r, c):
                row_copy(r >> 3, r & 7, ids[dst_row + r // MOE_BLOCK, r % MOE_BLOCK]).start()
                return c

            full_tiles = nvalid // SUBLANES
            lax.fori_loop(0, full_tiles, issue_tile, 0)
            lax.fori_loop(full_tiles * SUBLANES, nvalid, issue_row, 0)

            def wait_block(b, c):
                pltpu.make_async_copy(acc_ref.at[pl.ds(0, tiles)], y_hbm.at[pl.ds(0, tiles)], sem_s).wait()
                return c

            def wait_row(r, c):
                row_copy(0, 0, 0).wait()
                return c

            nfull = nvalid // MOE_BLOCK
            lax.fori_loop(0, nfull, wait_block, 0)
            lax.fori_loop(nfull * MOE_BLOCK, nvalid, wait_row, 0)


def _experts(sb_e, sb_nblk, sb_nvalid, tab, h1, w_up, b_up, w_down, b_down):
    def jj(j, nb, s):
        return jnp.where(nb[s] > 0, j, N_FF_TILES - 1)

    grid_spec = pltpu.PrefetchScalarGridSpec(
        num_scalar_prefetch=3, grid=(N_SB, N_FF_TILES),
        in_specs=[pl.BlockSpec(memory_space=pl.ANY),
                  pl.BlockSpec(memory_space=pl.ANY),
                  pl.BlockSpec((1, D_MODEL, FF_TILE), lambda s, j, e, nb, nv: (e[s], 0, jj(j, nb, s))),
                  pl.BlockSpec((1, D_MODEL, FF_TILE), lambda s, j, e, nb, nv: (e[s], 0, N_FF_TILES + jj(j, nb, s))),
                  pl.BlockSpec((1, FF_TILE, D_MODEL), lambda s, j, e, nb, nv: (e[s], jj(j, nb, s), 0)),
                  pl.BlockSpec((1, 1, FF_TILE), lambda s, j, e, nb, nv: (e[s], 0, jj(j, nb, s))),
                  pl.BlockSpec((1, 1, FF_TILE), lambda s, j, e, nb, nv: (e[s], 0, N_FF_TILES + jj(j, nb, s))),
                  pl.BlockSpec((1, 1, D_MODEL), lambda s, j, e, nb, nv: (e[s], 0, 0))],
        out_specs=pl.BlockSpec(memory_space=pl.ANY),
        scratch_shapes=[pltpu.VMEM((SB_ROWS // SUBLANES, SUBLANES, D_MODEL), F32),
                        pltpu.VMEM((SB_ROWS, D_MODEL), BF16),
                        pltpu.VMEM((SB_ROWS // SUBLANES, SUBLANES, D_MODEL), F32),
                        pltpu.VMEM((D_MODEL, FF_TILE), BF16),
                        pltpu.VMEM((D_MODEL, FF_TILE), BF16),
                        pltpu.VMEM((FF_TILE, D_MODEL), BF16),
                        pltpu.SMEM((2 * 2 * SB_BLOCKS, MOE_BLOCK), jnp.int32),
                        pltpu.SemaphoreType.DMA((2,)),
                        pltpu.SemaphoreType.DMA(()),
                        pltpu.SemaphoreType.DMA(())])
    y = pl.pallas_call(
        _expert_body, grid_spec=grid_spec,
        out_shape=jax.ShapeDtypeStruct((TOP_K * TOK // SUBLANES, SUBLANES, D_MODEL), F32),
        compiler_params=_params(2), name="experts")(
            sb_e, sb_nblk, sb_nvalid, tab, h1.reshape(TOK // SUBLANES, SUBLANES, D_MODEL), w_up, w_up, w_down,
            b_up.reshape(N_EXPERTS, 1, 2 * D_FF), b_up.reshape(N_EXPERTS, 1, 2 * D_FF),
            b_down.reshape(N_EXPERTS, 1, D_MODEL))
    return y.reshape(TOP_K * TOK, D_MODEL)


def _tail_body(y0_ref, y1_ref, y2_ref, y3_ref, gate_ref, h1_ref, p_ref, wg_ref, wp_ref,
               l2g_ref, l2b_ref, l3g_ref, l3b_ref, o_ref):
    gate = gate_ref[...]
    y = gate[:, 0:1] * y0_ref[...]
    for kk, y_ref in enumerate((y1_ref, y2_ref, y3_ref), start=1):
        y = y + gate[:, kk:kk + 1] * y_ref[...]
    h2 = _layer_norm(DN_ALPHA * h1_ref[...] + y, l2g_ref[...], l2b_ref[...])
    ple = _sigmoid(_dot(h2.astype(BF16), wg_ref[...])) * _dot(p_ref[...].astype(BF16), wp_ref[...])
    o_ref[...] = _layer_norm(DN_ALPHA * h2 + ple, l3g_ref[...], l3b_ref[...])


def _tail(y_slots, gates, h1, p2, wg, wp, l2g, l2b, l3g, l3b):
    tm = 256
    nt = TOK // tm
    const = lambda shape: pl.BlockSpec(shape, lambda i: (0,) * len(shape), pipeline_mode=pl.Buffered(1))
    row = lambda w: pl.BlockSpec((tm, w), lambda i: (i, 0))
    yspec = lambda kk: pl.BlockSpec((tm, D_MODEL), lambda i: (kk * nt + i, 0))
    vec = const((1, D_MODEL))
    return pl.pallas_call(
        _tail_body, grid=(nt,),
        in_specs=[yspec(0), yspec(1), yspec(2), yspec(3), row(128), row(D_MODEL), row(PLE_DIM),
                  const((D_MODEL, D_MODEL)), const((PLE_DIM, D_MODEL)), vec, vec, vec, vec],
        out_specs=row(D_MODEL),
        out_shape=jax.ShapeDtypeStruct((TOK, D_MODEL), F32),
        compiler_params=_params(1), name="tail")(y_slots, y_slots, y_slots, y_slots, gates, h1, p2, wg, wp,
                                                 l2g, l2b, l3g, l3b)


def _routing(top_idx):
    flat_e = top_idx.reshape(-1)
    experts = jnp.arange(N_EXPERTS, dtype=jnp.int32)
    onehot = (flat_e[:, None] == experts[None, :]).astype(jnp.int32)
    csum = jnp.cumsum(onehot, axis=0)
    rank = jnp.sum(onehot * csum, axis=1) - 1
    counts = csum[-1]
    padded = (counts + MOE_BLOCK - 1) // MOE_BLOCK * MOE_BLOCK
    padded_end = jnp.cumsum(padded)
    padded_start = padded_end - padded
    dest = (padded_start[flat_e] + rank).astype(jnp.int32)
    asg = jnp.arange(TOK * TOP_K, dtype=jnp.int32)
    asg_of_row = jnp.zeros((N_ROWS + SB_ROWS,), jnp.int32).at[dest].set(asg)
    tok = (asg_of_row // TOP_K).reshape(-1, MOE_BLOCK)
    dst = ((asg_of_row % TOP_K) * TOK + asg_of_row // TOP_K).reshape(-1, MOE_BLOCK)
    nb = padded // MOE_BLOCK
    n_sb = (nb + SB_BLOCKS - 1) // SB_BLOCKS
    sb_end = jnp.cumsum(n_sb)
    total = sb_end[-1]
    s = jnp.arange(N_SB, dtype=jnp.int32)
    s_eff = jnp.minimum(s, total - 1)
    e = jnp.minimum(jnp.sum((sb_end[None, :] <= s_eff[:, None]).astype(jnp.int32), axis=1), N_EXPERTS - 1)
    local = s_eff - (sb_end[e] - n_sb[e])
    valid = s < total
    sb_nblk = jnp.where(valid, jnp.clip(nb[e] - local * SB_BLOCKS, 0, SB_BLOCKS), 0).astype(jnp.int32)
    sb_blk = padded_start[e] // MOE_BLOCK + local * SB_BLOCKS
    sb_nvalid = jnp.where(valid, jnp.clip(counts[e] - local * SB_ROWS, 0, SB_ROWS), 0).astype(jnp.int32)
    blocks = sb_blk[:, None] + jnp.arange(SB_BLOCKS, dtype=jnp.int32)[None, :]
    tab = jnp.concatenate([tok[blocks], dst[blocks]], axis=1).astype(jnp.int32)
    return e.astype(jnp.int32), sb_nblk, sb_nvalid, tab


def _rope_tables():
    rows = SEQ // GRID_W
    row = jnp.repeat(jnp.arange(rows), GRID_W)
    col = jnp.tile(jnp.arange(GRID_W), rows)
    n_pairs = HEAD_DIM // 4
    inv_freq = ROPE_BASE ** (-jnp.arange(n_pairs, dtype=F32) / n_pairs)
    ang = jnp.concatenate([row[:, None] * inv_freq, col[:, None] * inv_freq], -1)
    cos_full = jnp.repeat(jnp.cos(ang), 2, axis=-1)
    sin = jnp.sin(ang)
    sin_signed = jnp.stack([-sin, sin], axis=-1).reshape(SEQ, HEAD_DIM)
    return cos_full, sin_signed


def kernel(x, p, in_ln_g, in_ln_b, w_in, q_norm, k_norm, w_lr_f, b_lr_f, w_lr_b, b_lr_b, gla_norm, w_br_a, w_br_b, w_o, ln1_g, ln1_b, w_router, b_router, w_up, b_up, w_down, b_down, ln2_g, ln2_b, w_ple_gate, w_ple_proj, ln3_g, ln3_b):
    assert x.shape == (BATCH, SEQ, D_MODEL) and w_in.shape[0] == DEPTH == 1
    win = w_in[0]
    hf, hb = _ln0(x.reshape(TOK, D_MODEL), in_ln_g, in_ln_b)

    zm = _inproj(hb, win, MAIN_W // 512, Z_TILE_OF_W_TILE, False, "inproj_main")
    zg = _inproj(hb, win[:, MAIN_W + LR_W:], 2 * D_MODEL // 512, tuple(range(8)), True, "inproj_gates")

    w2 = jnp.zeros((LR_W, 2 * GLA_K), F32)
    w2 = w2.at[:GLA_RANK, :GLA_K].set(w_lr_f[0]).at[GLA_RANK:, GLA_K:].set(w_lr_b[0])
    b2 = jnp.concatenate([b_lr_f[0], b_lr_b[0]]).reshape(1, -1)
    bc_f, bc_b = _decay(hb, win[:, MAIN_W:MAIN_W + LR_W].astype(BF16), w2, b2)

    cos_full, sin_signed = _rope_tables()
    qr, kr = _qkprep(zm, cos_full, sin_signed, q_norm[0], k_norm[0])
    attn = _attention(qr, kr, zm)
    o_f = _gla(zm, bc_f, False)
    o_b = _gla(zm, bc_b, True)

    wr = jnp.zeros((D_MODEL, 128), F32).at[:, :N_EXPERTS].set(w_router[0])
    br = jnp.full((1, 128), -jnp.inf, F32).at[0, :N_EXPERTS].set(b_router[0])
    vec = lambda v: v[0].reshape(1, -1)
    h1, idx_pad, gate_pad = _mixer(
        attn, o_f, o_b, zm, zg, hf, w_br_a[0].astype(BF16), w_br_b[0].astype(BF16), w_o[0].astype(BF16),
        vec(gla_norm), vec(ln1_g), vec(ln1_b), wr, br)

    sb_e, sb_nblk, sb_nvalid, tab = _routing(idx_pad[:, :TOP_K])
    y_slots = _experts(sb_e, sb_nblk, sb_nvalid, tab, h1, w_up[0], b_up[0], w_down[0], b_down[0])

    out = _tail(y_slots, gate_pad, h1, p[0].reshape(TOK, PLE_DIM), w_ple_gate[0].astype(BF16),
                w_ple_proj[0].astype(BF16), vec(ln2_g), vec(ln2_b), vec(ln3_g), vec(ln3_b))
    return out.reshape(BATCH, SEQ, D_MODEL)
```

```python
import functools

import jax
import jax.numpy as jnp
from jax import lax
from jax.experimental import pallas as pl
from jax.experimental.pallas import tpu as pltpu

F32 = jnp.float32
BF16 = jnp.bfloat16

D_MODEL = 2048
BATCH = 2
SEQ = 4096
TOK = BATCH * SEQ
PLE_DIM = 256
GRID_W = 64
ATT_HEADS = 8
ATT_KV_HEADS = 2
ATT_GROUP = ATT_HEADS // ATT_KV_HEADS
HEAD_DIM = 128
ROPE_BASE = 10000.0
GLA_HEADS = 4
GLA_DK = 128
GLA_DV = 256
GLA_RANK = 16
GLA_TAU = 16.0
GLA_CHUNK = 64
GLA_SUB = 16
N_EXPERTS = 32
TOP_K = 4
D_FF = D_MODEL
SWIGLU_LIMIT = 7.0
SWIGLU_ALPHA = 1.702
MOE_BLOCK = 128
ATT_Q = ATT_HEADS * HEAD_DIM
ATT_KV = ATT_KV_HEADS * HEAD_DIM
GLA_K = GLA_HEADS * GLA_DK
GLA_V = GLA_HEADS * GLA_DV
MAIN_W = ATT_Q + 2 * ATT_KV + 2 * GLA_K + 2 * GLA_V
LR_W = 2 * GLA_RANK
DEPTH = 1
DN_ALPHA = (2 * DEPTH) ** 0.25
LN_EPS = 1e-5
RMS_EPS = 1e-6
LOG2_E = 1.4426950408889634

Z_TILE_OF_W_TILE = (0, 1, 6, 7, 8, 2, 3, 4, 5)
Z_Q, Z_VB, Z_OG, Z_KA, Z_VA, Z_QB, Z_KB = 0, 1024, 2048, 3072, 3328, 3584, 4096

VMEM_LIMIT = 56 * 1024 * 1024

N_ROWS = TOK * TOP_K + N_EXPERTS * MOE_BLOCK
N_BLOCKS = N_ROWS // MOE_BLOCK
SB_BLOCKS = 10
SB_ROWS = SB_BLOCKS * MOE_BLOCK
N_SB = N_EXPERTS + -(-N_BLOCKS // SB_BLOCKS)
SUBLANES = 8
IDS_STRIDE = -(-2 * SB_ROWS // 1024) * 1024
FF_TILE = 256
N_FF_TILES = D_FF // FF_TILE


def _params(n_axes, vmem=None):
    return pltpu.CompilerParams(dimension_semantics=("arbitrary",) * n_axes,
                                vmem_limit_bytes=vmem or VMEM_LIMIT)


def _sigmoid(x):
    return 1.0 / (1.0 + jnp.exp(-x))


def _layer_norm(y, g, b):
    mu = jnp.mean(y, axis=-1, keepdims=True)
    yc = y - mu
    var = jnp.mean(yc * yc, axis=-1, keepdims=True)
    return yc * lax.rsqrt(var + LN_EPS) * g + b


def _dot(a, b):
    return jnp.dot(a, b, preferred_element_type=F32)


def _dot_nt(a, b):
    return lax.dot_general(a, b, (((1,), (1,)), ((), ())), preferred_element_type=F32)


def _dot_tn(a, b):
    return lax.dot_general(a, b, (((0,), (0,)), ((), ())), preferred_element_type=F32)


def _ln0_body(x_ref, g_ref, b_ref, hf_ref, hb_ref):
    y = _layer_norm(x_ref[...], g_ref[...], b_ref[...])
    hf_ref[...] = y
    hb_ref[...] = y.astype(BF16)


def _ln0(x2, g, b):
    tm = 256
    row = pl.BlockSpec((tm, D_MODEL), lambda i: (i, 0))
    vec = pl.BlockSpec((1, D_MODEL), lambda i: (0, 0))
    return pl.pallas_call(
        _ln0_body, grid=(TOK // tm,), in_specs=[row, vec, vec], out_specs=[row, row],
        out_shape=[jax.ShapeDtypeStruct((TOK, D_MODEL), F32), jax.ShapeDtypeStruct((TOK, D_MODEL), BF16)],
        compiler_params=_params(1), name="ln0")(x2, g.reshape(1, -1), b.reshape(1, -1))


def _inproj_body(perm_ref, a_ref, w_ref, o_ref, wb_ref, *, gate):
    del perm_ref

    @pl.when(pl.program_id(1) == 0)
    def _():
        wb_ref[...] = w_ref[...].astype(BF16)

    acc = _dot(a_ref[...], wb_ref[...])
    if gate:
        acc = _sigmoid(acc)
    o_ref[...] = acc.astype(o_ref.dtype)


def _inproj(hb, w, n_tiles, tile_perm, gate, name):
    tm, tn = 1024, 512
    grid_spec = pltpu.PrefetchScalarGridSpec(
        num_scalar_prefetch=1, grid=(n_tiles, TOK // tm),
        in_specs=[pl.BlockSpec((tm, D_MODEL), lambda n, m, p: (m, 0)),
                  pl.BlockSpec((D_MODEL, tn), lambda n, m, p: (0, n))],
        out_specs=pl.BlockSpec((tm, tn), lambda n, m, p: (m, p[n])),
        scratch_shapes=[pltpu.VMEM((D_MODEL, tn), BF16)])
    return pl.pallas_call(
        functools.partial(_inproj_body, gate=gate), grid_spec=grid_spec,
        out_shape=jax.ShapeDtypeStruct((TOK, n_tiles * tn), BF16),
        compiler_params=_params(2), name=name)(jnp.asarray(tile_perm, jnp.int32), hb, w)


def _decay_body(h_ref, wlr_ref, w2_ref, b2_ref, trif_ref, trib_ref, bf_ref, bb_ref):
    zlr = _dot(h_ref[...], wlr_ref[...])
    pre = jnp.dot(zlr, w2_ref[...], preferred_element_type=F32,
                  precision=lax.Precision.HIGHEST) + b2_ref[...]
    la = (jnp.minimum(pre, 0.0) - jnp.log1p(jnp.exp(-jnp.abs(pre)))) * (1.0 / GLA_TAU)
    hi = la.astype(BF16)
    lo = (la - hi.astype(F32)).astype(BF16)
    bf_ref[...] = _dot(trif_ref[...], hi[:, :GLA_K]) + _dot(trif_ref[...], lo[:, :GLA_K])
    bb_ref[...] = _dot(trib_ref[...], hi[:, GLA_K:]) + _dot(trib_ref[...], lo[:, GLA_K:])


def _decay(hb, wlr, w2, b2):
    tm = 512
    r = jnp.arange(tm)
    same = (r[:, None] // GLA_CHUNK) == (r[None, :] // GLA_CHUNK)
    trif = (same & (r[None, :] <= r[:, None])).astype(BF16)
    trib = (same & (r[None, :] >= r[:, None])).astype(BF16)
    full = lambda shape: pl.BlockSpec(shape, lambda i: (0, 0))
    out = pl.BlockSpec((tm, GLA_K), lambda i: (i, 0))
    return pl.pallas_call(
        _decay_body, grid=(TOK // tm,),
        in_specs=[pl.BlockSpec((tm, D_MODEL), lambda i: (i, 0)), full((D_MODEL, LR_W)),
                  full((LR_W, 2 * GLA_K)), full((1, 2 * GLA_K)), full((tm, tm)), full((tm, tm))],
        out_specs=[out, out],
        out_shape=[jax.ShapeDtypeStruct((TOK, GLA_K), F32)] * 2,
        compiler_params=_params(1), name="decay")(hb, wlr, w2, b2, trif, trib)


def _qkprep_body(q_ref, k_ref, cos_ref, sin_ref, qn_ref, kn_ref, qo_ref, ko_ref):
    cos = cos_ref[...]
    sin = sin_ref[...]
    lane = lax.broadcasted_iota(jnp.int32, cos.shape, 1)
    even = (lane % 2) == 0

    def one(x, gain, scale):
        x = x.astype(F32)
        x = x * lax.rsqrt(jnp.mean(x * x, axis=-1, keepdims=True) + RMS_EPS) * gain
        partner = jnp.where(even, pltpu.roll(x, HEAD_DIM - 1, 1), pltpu.roll(x, 1, 1))
        return ((x * cos + partner * sin) * scale).astype(BF16)

    for hh in range(ATT_HEADS):
        sl = slice(hh * HEAD_DIM, (hh + 1) * HEAD_DIM)
        qo_ref[:, sl] = one(q_ref[:, sl], qn_ref[...], LOG2_E * HEAD_DIM ** -0.5)
    for hh in range(ATT_KV_HEADS):
        sl = slice(hh * HEAD_DIM, (hh + 1) * HEAD_DIM)
        ko_ref[:, sl] = one(k_ref[:, sl], kn_ref[...], 1.0)


def _qkprep(zm, cos_full, sin_signed, q_norm, k_norm):
    tm = 256
    nrow = SEQ // tm
    tab = pl.BlockSpec((tm, HEAD_DIM), lambda i: (i % nrow, 0))
    vec = pl.BlockSpec((1, HEAD_DIM), lambda i: (0, 0))
    return pl.pallas_call(
        _qkprep_body, grid=(TOK // tm,),
        in_specs=[pl.BlockSpec((tm, ATT_Q), lambda i: (i, Z_Q // ATT_Q)),
                  pl.BlockSpec((tm, ATT_KV), lambda i: (i, Z_KA // ATT_KV)), tab, tab, vec, vec],
        out_specs=[pl.BlockSpec((tm, ATT_Q), lambda i: (i, 0)), pl.BlockSpec((tm, ATT_KV), lambda i: (i, 0))],
        out_shape=[jax.ShapeDtypeStruct((TOK, ATT_Q), BF16), jax.ShapeDtypeStruct((TOK, ATT_KV), BF16)],
        compiler_params=_params(1), name="qkprep")(zm, zm, cos_full, sin_signed,
                                                   q_norm.reshape(1, -1), k_norm.reshape(1, -1))


ATT_TQ = 256
ATT_TK = 512


ATT_SUB = 64


def _attn_body(q_ref, k_ref, v_ref, o_ref, qs_ref, s0_ref, s1_ref, p0_ref, p1_ref, a0_ref, a1_ref,
               acc_ref, m_ref, l_ref):
    rows = ATT_GROUP * ATT_TQ
    n_chunks = SEQ // ATT_TK
    for g in range(ATT_GROUP):
        qs_ref[g * ATT_TQ:(g + 1) * ATT_TQ, :] = q_ref[:, g * HEAD_DIM:(g + 1) * HEAD_DIM]
    m_ref[...] = jnp.full(m_ref.shape, -jnp.inf, F32)
    l_ref[...] = jnp.zeros(l_ref.shape, F32)
    acc_ref[...] = jnp.zeros(acc_ref.shape, F32)
    p1_ref[...] = jnp.zeros((rows, ATT_TK), BF16)
    a1_ref[...] = jnp.zeros(a1_ref.shape, F32)

    def chunk(c):
        return pl.ds(pl.multiple_of(c * ATT_TK, ATT_TK), ATT_TK)

    s0_ref[...] = _dot_nt(qs_ref[...], k_ref[0:ATT_TK, :])

    def step(c, s_cur, s_nxt, p_cur, p_prv, a_cur, a_prv):
        c_next = jnp.where(c + 1 < n_chunks, c + 1, 0)
        c_prev = jnp.where(c > 0, c - 1, 0)
        s_nxt[...] = _dot_nt(qs_ref[...], k_ref[chunk(c_next), :])
        acc_ref[...] = a_prv[...] * acc_ref[...] + _dot(p_prv[...], v_ref[chunk(c_prev), :])
        lane_blocks = [slice(cb * HEAD_DIM, (cb + 1) * HEAD_DIM) for cb in range(ATT_TK // HEAD_DIM)]
        m_all, l_all = m_ref[...], l_ref[...]
        sub_tiles = [slice(t * ATT_SUB, (t + 1) * ATT_SUB) for t in range(rows // ATT_SUB)]
        m_out = []
        for rr in sub_tiles:
            smax = functools.reduce(jnp.maximum, [s_cur[rr, cb] for cb in lane_blocks])
            m_out.append(jnp.maximum(m_all[rr], jnp.max(smax, axis=-1, keepdims=True)))
        l_out, a_out = [], []
        for rr, m_new in zip(sub_tiles, m_out):
            alpha = jnp.exp2(m_all[rr] - m_new)
            p = [jnp.exp2(s_cur[rr, cb] - m_new) for cb in lane_blocks]
            l_out.append(alpha * l_all[rr] + jnp.sum(functools.reduce(jnp.add, p), axis=-1, keepdims=True))
            a_out.append(alpha)
            for cb, pb in zip(lane_blocks, p):
                p_cur[rr, cb] = pb.astype(BF16)
        m_ref[...] = jnp.concatenate(m_out, axis=0)
        l_ref[...] = jnp.concatenate(l_out, axis=0)
        a_cur[...] = jnp.concatenate(a_out, axis=0)

    def pair(i, carry):
        step(2 * i, s0_ref, s1_ref, p0_ref, p1_ref, a0_ref, a1_ref)
        step(2 * i + 1, s1_ref, s0_ref, p1_ref, p0_ref, a1_ref, a0_ref)
        return carry

    lax.fori_loop(0, n_chunks // 2, pair, 0)
    acc = a1_ref[...] * acc_ref[...] + _dot(p1_ref[...], v_ref[SEQ - ATT_TK:SEQ, :])
    out = (acc / l_ref[...]).astype(BF16)
    for g in range(ATT_GROUP):
        o_ref[:, g * HEAD_DIM:(g + 1) * HEAD_DIM] = out[g * ATT_TQ:(g + 1) * ATT_TQ]


def _attention(qr, kr, zm):
    nq = SEQ // ATT_TQ
    gw = ATT_GROUP * HEAD_DIM
    rows = ATT_GROUP * ATT_TQ
    qspec = pl.BlockSpec((ATT_TQ, gw), lambda b, j, i: (b * nq + i, j))
    return pl.pallas_call(
        _attn_body, grid=(BATCH, ATT_KV_HEADS, nq),
        in_specs=[qspec,
                  pl.BlockSpec((SEQ, HEAD_DIM), lambda b, j, i: (b, j)),
                  pl.BlockSpec((SEQ, HEAD_DIM), lambda b, j, i: (b, Z_VA // HEAD_DIM + j))],
        out_specs=qspec,
        out_shape=jax.ShapeDtypeStruct((TOK, ATT_Q), BF16),
        scratch_shapes=[pltpu.VMEM((rows, HEAD_DIM), BF16),
                        pltpu.VMEM((rows, ATT_TK), F32), pltpu.VMEM((rows, ATT_TK), F32),
                        pltpu.VMEM((rows, ATT_TK), BF16), pltpu.VMEM((rows, ATT_TK), BF16),
                        pltpu.VMEM((rows, HEAD_DIM), F32), pltpu.VMEM((rows, HEAD_DIM), F32),
                        pltpu.VMEM((rows, HEAD_DIM), F32), pltpu.VMEM((rows, HEAD_DIM), F32),
                        pltpu.VMEM((rows, HEAD_DIM), F32)],
        compiler_params=_params(3), name="attn")(qr, kr, zm)


GLA_CB = 4
GLA_RB = GLA_CB * GLA_CHUNK
N_SUB = GLA_CHUNK // GLA_SUB


def _gla_body(q_ref, k_ref, v_ref, bc_ref, o_ref, st_ref, kf_ref, bs_ref, *, rev):
    @pl.when(pl.program_id(1) == 0)
    def _():
        st_ref[...] = jnp.zeros_like(st_ref)

    C, SUB = GLA_CHUNK, GLA_SUB
    rowc = lax.broadcasted_iota(jnp.int32, (C, GLA_DK), 0)
    rows_s = lax.broadcasted_iota(jnp.int32, (SUB, 128), 0)
    lane_s = lax.broadcasted_iota(jnp.int32, (SUB, 128), 1)

    def chunk(ci, carry):
        c = (GLA_CB - 1 - ci) if rev else ci
        r0 = pl.multiple_of(c * C, C)
        for hh in range(GLA_HEADS):
            ksl = slice(hh * GLA_DK, (hh + 1) * GLA_DK)
            vsl = slice(hh * GLA_DV, (hh + 1) * GLA_DV)
            q = q_ref[pl.ds(r0, C), ksl].astype(F32) * (GLA_DK ** -0.5)
            k = k_ref[pl.ds(r0, C), ksl].astype(F32)
            v = v_ref[pl.ds(r0, C), vsl]
            bc = bc_ref[pl.ds(r0, C), ksl]
            kf_ref[hh] = k
            bs_ref[hh] = bc
            st = st_ref[hh]
            blast = bc[0:1] if rev else bc[C - 1:C]
            o_inter = _dot_nt((q * jnp.exp(bc)).astype(BF16), st.astype(BF16))
            kdec = k * jnp.exp(blast - bc)
            st_ref[hh] = st * jnp.exp(blast) + _dot_tn(v, kdec.astype(BF16))

            a_rows = []
            for si in range(N_SUB):
                lo, hi = si * SUB, (si + 1) * SUB
                q_s, b_s = q[lo:hi], bc[lo:hi]
                has_earlier = (si < N_SUB - 1) if rev else (si > 0)
                if has_earlier:
                    ref_row = bc[hi:hi + 1] if rev else bc[lo - 1:lo]
                    earlier = (rowc >= hi) if rev else (rowc < lo)
                    qt = q_s * jnp.exp(b_s - ref_row)
                    kt = k * jnp.exp(jnp.where(earlier, ref_row - bc, -jnp.inf))
                    a = _dot_nt(qt.astype(BF16), kt.astype(BF16))
                else:
                    a = jnp.zeros((SUB, C), F32)
                diag = jnp.zeros((SUB, 128), F32)
                for jl in range(SUB):
                    j = lo + jl
                    d = jnp.minimum(b_s - bs_ref[hh, j:j + 1, :], 0.0)
                    col = jnp.sum(q_s * kf_ref[hh, j:j + 1, :] * jnp.exp(d), axis=-1, keepdims=True)
                    diag = jnp.where(lane_s == j, col, diag)
                keep = (lane_s >= rows_s + lo) if rev else (lane_s <= rows_s + lo)
                diag = jnp.where(keep & (lane_s >= lo) & (lane_s < hi), diag, 0.0)
                a_rows.append(a + diag[:, :C])
            a_full = jnp.concatenate(a_rows, axis=0)
            o_ref[pl.ds(r0, C), vsl] = o_inter + _dot(a_full.astype(BF16), v)
        return carry

    lax.fori_loop(0, GLA_CB, chunk, 0)


def _gla(zm, bcum, rev):
    ncb = SEQ // GLA_RB
    if rev:
        row = lambda b, c: b * ncb + (ncb - 1 - c)
    else:
        row = lambda b, c: b * ncb + c
    return pl.pallas_call(
        functools.partial(_gla_body, rev=rev), grid=(BATCH, ncb),
        in_specs=[pl.BlockSpec((GLA_RB, GLA_K), lambda b, c: (row(b, c), Z_QB // GLA_K)),
                  pl.BlockSpec((GLA_RB, GLA_K), lambda b, c: (row(b, c), Z_KB // GLA_K)),
                  pl.BlockSpec((GLA_RB, GLA_V), lambda b, c: (row(b, c), Z_VB // GLA_V)),
                  pl.BlockSpec((GLA_RB, GLA_K), lambda b, c: (row(b, c), 0))],
        out_specs=pl.BlockSpec((GLA_RB, GLA_V), lambda b, c: (row(b, c), 0)),
        out_shape=jax.ShapeDtypeStruct((TOK, GLA_V), F32),
        scratch_shapes=[pltpu.VMEM((GLA_HEADS, GLA_DV, GLA_DK), F32),
                        pltpu.VMEM((GLA_HEADS, GLA_CHUNK, GLA_DK), F32),
                        pltpu.VMEM((GLA_HEADS, GLA_CHUNK, GLA_DK), F32)],
        compiler_params=_params(2), name="gla_bwd" if rev else "gla_fwd")(zm, zm, zm, bcum)


def _mixer_body(attn_ref, of_ref, ob_ref, og_ref, ga_ref, gb_ref, h_ref,
                wa_ref, wb_ref, wo_ref, gn_ref, lg_ref, lb_ref, wr_ref, br_ref,
                h1_ref, idx_ref, gate_ref):
    osum = of_ref[...] + ob_ref[...]
    og = og_ref[...].astype(F32)
    gn = gn_ref[...]
    parts = []
    for hh in range(GLA_HEADS):
        sl = slice(hh * GLA_DV, (hh + 1) * GLA_DV)
        x = osum[:, sl]
        g = og[:, sl]
        xn = x * lax.rsqrt(jnp.mean(x * x, axis=-1, keepdims=True) + RMS_EPS) * gn
        parts.append((xn * (g * _sigmoid(g))).astype(BF16))
    onorm = jnp.concatenate(parts, axis=1)
    ya = _dot(attn_ref[...], wa_ref[...])
    yb = _dot(onorm, wb_ref[...])
    m = ga_ref[...].astype(F32) * ya + gb_ref[...].astype(F32) * yb
    mix = _dot(m.astype(BF16), wo_ref[...])
    h1 = _layer_norm(DN_ALPHA * h_ref[...] + mix, lg_ref[...], lb_ref[...])
    h1_ref[...] = h1

    logits = jnp.dot(h1, wr_ref[...], preferred_element_type=F32,
                     precision=lax.Precision.HIGHEST) + br_ref[...]
    lane = lax.broadcasted_iota(jnp.int32, logits.shape, 1)
    x = logits
    vals, idxs = [], []
    for _ in range(TOP_K):
        mx = jnp.max(x, axis=-1, keepdims=True)
        ix = jnp.min(jnp.where(x == mx, lane, 128), axis=-1, keepdims=True)
        vals.append(mx)
        idxs.append(ix)
        x = jnp.where(lane == ix, -jnp.inf, x)
    es = [jnp.exp(vv - vals[0]) for vv in vals]
    den = es[0] + es[1] + es[2] + es[3]
    idx_out = jnp.zeros(logits.shape, jnp.int32)
    gate_out = jnp.zeros(logits.shape, F32)
    for kk in range(TOP_K):
        idx_out = jnp.where(lane == kk, idxs[kk], idx_out)
        gate_out = jnp.where(lane == kk, es[kk] / den, gate_out)
    idx_ref[...] = idx_out
    gate_ref[...] = gate_out


def _mixer(attn, o_f, o_b, zm, zg, hf, wa, wb, wo, gn, lg, lb, wr, br):
    tm = 256
    const = lambda shape: pl.BlockSpec(shape, lambda i: (0,) * len(shape), pipeline_mode=pl.Buffered(1))
    row = lambda w, cb=0: pl.BlockSpec((tm, w), lambda i: (i, cb))
    return pl.pallas_call(
        _mixer_body, grid=(TOK // tm,),
        in_specs=[row(ATT_Q), row(GLA_V), row(GLA_V), row(GLA_V, Z_OG // GLA_V),
                  row(D_MODEL, 0), row(D_MODEL, 1), row(D_MODEL),
                  const((ATT_Q, D_MODEL)), const((GLA_V, D_MODEL)), const((D_MODEL, D_MODEL)),
                  const((1, GLA_DV)), const((1, D_MODEL)), const((1, D_MODEL)),
                  const((D_MODEL, 128)), const((1, 128))],
        out_specs=[row(D_MODEL), row(128), row(128)],
        out_shape=[jax.ShapeDtypeStruct((TOK, D_MODEL), F32),
                   jax.ShapeDtypeStruct((TOK, 128), jnp.int32),
                   jax.ShapeDtypeStruct((TOK, 128), F32)],
        compiler_params=_params(1), name="mixer")(attn, o_f, o_b, zm, zg, zg, hf, wa, wb, wo, gn, lg, lb, wr, br)


GROUP_SIZES = (512, 256, 128)


def _expert_body(e_ref, nb_ref, nv_ref, tab_hbm, h1_hbm, wg_ref, wl_ref, wd_ref, bg_ref, bl_ref, bd_ref,
                 y_hbm, gbuf, xb_ref, acc_ref, wgb_ref, wlb_ref, wdb_ref, ids, sem_ids, sem_g, sem_s):
    del e_ref
    s = pl.program_id(0)
    j = pl.program_id(1)
    nblk = nb_ref[s]
    slot = s & 1
    nxt = jnp.minimum(s + 1, N_SB - 1)
    nblk_next = jnp.where(s + 1 < N_SB, nb_ref[nxt], 0)
    tiles = MOE_BLOCK // SUBLANES

    def ids_copy(sb, sl):
        return pltpu.make_async_copy(tab_hbm.at[pl.ds(pl.multiple_of(sb * IDS_STRIDE, IDS_STRIDE), IDS_STRIDE)],
                                     ids.at[pl.ds(pl.multiple_of(sl * IDS_STRIDE, IDS_STRIDE), IDS_STRIDE)],
                                     sem_ids.at[sl])

    def gather_block(sl, b):
        id0 = sl * IDS_STRIDE + b * MOE_BLOCK

        def issue(t, c):
            for u in range(SUBLANES):
                tok = ids[id0 + t * SUBLANES + u]
                pltpu.make_async_copy(h1_hbm.at[pl.ds(tok >> 3, 1), pl.ds(tok & 7, 1), :],
                                      gbuf.at[pl.ds(b * tiles + t, 1), pl.ds(u, 1), :], sem_g).start()
            return c

        lax.fori_loop(0, tiles, issue, 0)

    def gather_wait_block(b):
        pltpu.make_async_copy(h1_hbm.at[pl.ds(0, tiles)], gbuf.at[pl.ds(b * tiles, tiles)], sem_g).wait()

    def loop_blocks(n, fn):
        def body(b, c):
            fn(b)
            return c

        lax.fori_loop(0, n, body, 0)

    @pl.when((s == 0) & (j == 0))
    def _():
        ids_copy(0, 0).start()
        ids_copy(0, 0).wait()
        loop_blocks(nblk, lambda b: gather_block(0, b))

    @pl.when(nblk > 0)
    def _():
        @pl.when(j == 0)
        def _():
            @pl.when(nblk_next > 0)
            def _():
                ids_copy(nxt, 1 - slot).start()

            loop_blocks(nblk, gather_wait_block)

            def take_block(b):
                x = gbuf[pl.ds(b * tiles, tiles)].reshape(MOE_BLOCK, D_MODEL)
                xb_ref[pl.ds(pl.multiple_of(b * MOE_BLOCK, MOE_BLOCK), MOE_BLOCK), :] = x.astype(BF16)
                acc_ref[pl.ds(b * tiles, tiles)] = jnp.broadcast_to(bd_ref[0], (tiles, SUBLANES, D_MODEL))

            loop_blocks(nblk, take_block)

        @pl.when((j == 1) & (nblk_next > 0))
        def _():
            ids_copy(nxt, 1 - slot).wait()

        @pl.when(j >= 1)
        def _():
            for u in range(2):
                b = (j - 1) * 2 + u

                @pl.when(b < nblk_next)
                def _():
                    gather_block(1 - slot, b)

        wgb_ref[...] = wg_ref[0].astype(BF16)
        wlb_ref[...] = wl_ref[0].astype(BF16)
        wdb_ref[...] = wd_ref[0].astype(BF16)

        def group(r0, size):
            x = xb_ref[pl.ds(pl.multiple_of(r0, MOE_BLOCK), size), :]
            g = jnp.minimum(_dot(x, wgb_ref[...]) + bg_ref[0], SWIGLU_LIMIT)
            lin = jnp.clip(_dot(x, wlb_ref[...]) + bl_ref[0], -SWIGLU_LIMIT, SWIGLU_LIMIT)
            act = g * _sigmoid(SWIGLU_ALPHA * g) * (lin + 1.0)
            upd = _dot(act.astype(BF16), wdb_ref[...])
            acc_ref[pl.ds(r0 // SUBLANES, size // SUBLANES)] += upd.reshape(size // SUBLANES, SUBLANES, D_MODEL)

        big = GROUP_SIZES[0]
        n_big = nblk // (big // MOE_BLOCK)

        def big_group(i, c):
            group(i * big, big)
            return c

        lax.fori_loop(0, n_big, big_group, 0)
        done = n_big * (big // MOE_BLOCK)
        for size in GROUP_SIZES[1:]:
            take = ((nblk - done) // (size // MOE_BLOCK)) > 0

            @pl.when(take)
            def _(done=done, size=size):
                group(done * MOE_BLOCK, size)

            done = done + jnp.where(take, size // MOE_BLOCK, 0)

        @pl.when(j == N_FF_TILES - 1)
        def _():
            nvalid = nv_ref[s]

            dst0 = slot * IDS_STRIDE + SB_ROWS

            def row_copy(t, u, dst):
                return pltpu.make_async_copy(acc_ref.at[pl.ds(t, 1), pl.ds(u, 1), :],
                                             y_hbm.at[pl.ds(dst >> 3, 1), pl.ds(dst & 7, 1), :], sem_s)

            def issue_tile(t, c):
                for u in range(SUBLANES):
                    row_copy(t, u, ids[dst0 + t * SUBLANES + u]).start()
                return c

            def issue_row(r, c):
                row_copy(r >> 3, r & 7, ids[dst0 + r]).start()
                return c

            full_tiles = nvalid // SUBLANES
            lax.fori_loop(0, full_tiles, issue_tile, 0)
            lax.fori_loop(full_tiles * SUBLANES, nvalid, issue_row, 0)

            def wait_block(b, c):
                pltpu.make_async_copy(acc_ref.at[pl.ds(0, tiles)], y_hbm.at[pl.ds(0, tiles)], sem_s).wait()
                return c

            def wait_row(r, c):
                row_copy(0, 0, 0).wait()
                return c

            nfull = nvalid // MOE_BLOCK
            lax.fori_loop(0, nfull, wait_block, 0)
            lax.fori_loop(nfull * MOE_BLOCK, nvalid, wait_row, 0)


def _experts(sb_e, sb_nblk, sb_nvalid, tab, h1, w_up, b_up, w_down, b_down):
    def jj(j, nb, s):
        return jnp.where(nb[s] > 0, j, N_FF_TILES - 1)

    grid_spec = pltpu.PrefetchScalarGridSpec(
        num_scalar_prefetch=3, grid=(N_SB, N_FF_TILES),
        in_specs=[pl.BlockSpec(memory_space=pl.ANY),
                  pl.BlockSpec(memory_space=pl.ANY),
                  pl.BlockSpec((1, D_MODEL, FF_TILE), lambda s, j, e, nb, nv: (e[s], 0, jj(j, nb, s))),
                  pl.BlockSpec((1, D_MODEL, FF_TILE), lambda s, j, e, nb, nv: (e[s], 0, N_FF_TILES + jj(j, nb, s))),
                  pl.BlockSpec((1, FF_TILE, D_MODEL), lambda s, j, e, nb, nv: (e[s], jj(j, nb, s), 0)),
                  pl.BlockSpec((1, 1, FF_TILE), lambda s, j, e, nb, nv: (e[s], 0, jj(j, nb, s))),
                  pl.BlockSpec((1, 1, FF_TILE), lambda s, j, e, nb, nv: (e[s], 0, N_FF_TILES + jj(j, nb, s))),
                  pl.BlockSpec((1, 1, D_MODEL), lambda s, j, e, nb, nv: (e[s], 0, 0))],
        out_specs=pl.BlockSpec(memory_space=pl.ANY),
        scratch_shapes=[pltpu.VMEM((SB_ROWS // SUBLANES, SUBLANES, D_MODEL), F32),
                        pltpu.VMEM((SB_ROWS, D_MODEL), BF16),
                        pltpu.VMEM((SB_ROWS // SUBLANES, SUBLANES, D_MODEL), F32),
                        pltpu.VMEM((D_MODEL, FF_TILE), BF16),
                        pltpu.VMEM((D_MODEL, FF_TILE), BF16),
                        pltpu.VMEM((FF_TILE, D_MODEL), BF16),
                        pltpu.SMEM((2 * IDS_STRIDE,), jnp.int32),
                        pltpu.SemaphoreType.DMA((2,)),
                        pltpu.SemaphoreType.DMA(()),
                        pltpu.SemaphoreType.DMA(())])
    y = pl.pallas_call(
        _expert_body, grid_spec=grid_spec,
        out_shape=jax.ShapeDtypeStruct((TOP_K * TOK // SUBLANES, SUBLANES, D_MODEL), F32),
        compiler_params=_params(2), name="experts")(
            sb_e, sb_nblk, sb_nvalid, tab, h1.reshape(TOK // SUBLANES, SUBLANES, D_MODEL), w_up, w_up, w_down,
            b_up.reshape(N_EXPERTS, 1, 2 * D_FF), b_up.reshape(N_EXPERTS, 1, 2 * D_FF),
            b_down.reshape(N_EXPERTS, 1, D_MODEL))
    return y.reshape(TOP_K * TOK, D_MODEL)


def _tail_body(y0_ref, y1_ref, y2_ref, y3_ref, gate_ref, h1_ref, p_ref, wg_ref, wp_ref,
               l2g_ref, l2b_ref, l3g_ref, l3b_ref, o_ref):
    gate = gate_ref[...]
    y = gate[:, 0:1] * y0_ref[...]
    for kk, y_ref in enumerate((y1_ref, y2_ref, y3_ref), start=1):
        y = y + gate[:, kk:kk + 1] * y_ref[...]
    h2 = _layer_norm(DN_ALPHA * h1_ref[...] + y, l2g_ref[...], l2b_ref[...])
    ple = _sigmoid(_dot(h2.astype(BF16), wg_ref[...])) * _dot(p_ref[...].astype(BF16), wp_ref[...])
    o_ref[...] = _layer_norm(DN_ALPHA * h2 + ple, l3g_ref[...], l3b_ref[...])


def _tail(y_slots, gates, h1, p2, wg, wp, l2g, l2b, l3g, l3b):
    tm = 256
    nt = TOK // tm
    const = lambda shape: pl.BlockSpec(shape, lambda i: (0,) * len(shape), pipeline_mode=pl.Buffered(1))
    row = lambda w: pl.BlockSpec((tm, w), lambda i: (i, 0))
    yspec = lambda kk: pl.BlockSpec((tm, D_MODEL), lambda i: (kk * nt + i, 0))
    vec = const((1, D_MODEL))
    return pl.pallas_call(
        _tail_body, grid=(nt,),
        in_specs=[yspec(0), yspec(1), yspec(2), yspec(3), row(128), row(D_MODEL), row(PLE_DIM),
                  const((D_MODEL, D_MODEL)), const((PLE_DIM, D_MODEL)), vec, vec, vec, vec],
        out_specs=row(D_MODEL),
        out_shape=jax.ShapeDtypeStruct((TOK, D_MODEL), F32),
        compiler_params=_params(1), name="tail")(y_slots, y_slots, y_slots, y_slots, gates, h1, p2, wg, wp,
                                                 l2g, l2b, l3g, l3b)


def _routing(top_idx):
    flat_e = top_idx.reshape(-1)
    experts = jnp.arange(N_EXPERTS, dtype=jnp.int32)
    onehot = (flat_e[:, None] == experts[None, :]).astype(jnp.int32)
    csum = jnp.cumsum(onehot, axis=0)
    rank = jnp.sum(onehot * csum, axis=1) - 1
    counts = csum[-1]
    padded = (counts + MOE_BLOCK - 1) // MOE_BLOCK * MOE_BLOCK
    padded_end = jnp.cumsum(padded)
    padded_start = padded_end - padded
    dest = (padded_start[flat_e] + rank).astype(jnp.int32)
    asg = jnp.arange(TOK * TOP_K, dtype=jnp.int32)
    asg_of_row = jnp.zeros((N_ROWS + SB_ROWS,), jnp.int32).at[dest].set(asg)
    tok = (asg_of_row // TOP_K).reshape(-1, MOE_BLOCK)
    dst = ((asg_of_row % TOP_K) * TOK + asg_of_row // TOP_K).reshape(-1, MOE_BLOCK)
    nb = padded // MOE_BLOCK
    n_sb = (nb + SB_BLOCKS - 1) // SB_BLOCKS
    sb_end = jnp.cumsum(n_sb)
    total = sb_end[-1]
    s = jnp.arange(N_SB, dtype=jnp.int32)
    s_eff = jnp.minimum(s, total - 1)
    e = jnp.minimum(jnp.sum((sb_end[None, :] <= s_eff[:, None]).astype(jnp.int32), axis=1), N_EXPERTS - 1)
    local = s_eff - (sb_end[e] - n_sb[e])
    valid = s < total
    sb_nblk = jnp.where(valid, jnp.clip(nb[e] - local * SB_BLOCKS, 0, SB_BLOCKS), 0).astype(jnp.int32)
    sb_blk = padded_start[e] // MOE_BLOCK + local * SB_BLOCKS
    sb_nvalid = jnp.where(valid, jnp.clip(counts[e] - local * SB_ROWS, 0, SB_ROWS), 0).astype(jnp.int32)
    blocks = sb_blk[:, None] + jnp.arange(SB_BLOCKS, dtype=jnp.int32)[None, :]
    tab = jnp.concatenate([tok[blocks].reshape(N_SB, SB_ROWS), dst[blocks].reshape(N_SB, SB_ROWS),
                           jnp.zeros((N_SB, IDS_STRIDE - 2 * SB_ROWS), jnp.int32)], axis=1).astype(jnp.int32).reshape(-1)
    return e.astype(jnp.int32), sb_nblk, sb_nvalid, tab


def _rope_tables():
    rows = SEQ // GRID_W
    row = jnp.repeat(jnp.arange(rows), GRID_W)
    col = jnp.tile(jnp.arange(GRID_W), rows)
    n_pairs = HEAD_DIM // 4
    inv_freq = ROPE_BASE ** (-jnp.arange(n_pairs, dtype=F32) / n_pairs)
    ang = jnp.concatenate([row[:, None] * inv_freq, col[:, None] * inv_freq], -1)
    cos_full = jnp.repeat(jnp.cos(ang), 2, axis=-1)
    sin = jnp.sin(ang)
    sin_signed = jnp.stack([-sin, sin], axis=-1).reshape(SEQ, HEAD_DIM)
    return cos_full, sin_signed


def kernel(x, p, in_ln_g, in_ln_b, w_in, q_norm, k_norm, w_lr_f, b_lr_f, w_lr_b, b_lr_b, gla_norm, w_br_a, w_br_b, w_o, ln1_g, ln1_b, w_router, b_router, w_up, b_up, w_down, b_down, ln2_g, ln2_b, w_ple_gate, w_ple_proj, ln3_g, ln3_b):
    assert x.shape == (BATCH, SEQ, D_MODEL) and w_in.shape[0] == DEPTH == 1
    win = w_in[0]
    hf, hb = _ln0(x.reshape(TOK, D_MODEL), in_ln_g, in_ln_b)

    zm = _inproj(hb, win, MAIN_W // 512, Z_TILE_OF_W_TILE, False, "inproj_main")
    zg = _inproj(hb, win[:, MAIN_W + LR_W:].astype(BF16), 2 * D_MODEL // 512, tuple(range(8)), True, "inproj_gates")

    w2 = jnp.zeros((LR_W, 2 * GLA_K), F32)
    w2 = w2.at[:GLA_RANK, :GLA_K].set(w_lr_f[0]).at[GLA_RANK:, GLA_K:].set(w_lr_b[0])
    b2 = jnp.concatenate([b_lr_f[0], b_lr_b[0]]).reshape(1, -1)
    bc_f, bc_b = _decay(hb, win[:, MAIN_W:MAIN_W + LR_W].astype(BF16), w2, b2)

    cos_full, sin_signed = _rope_tables()
    qr, kr = _qkprep(zm, cos_full, sin_signed, q_norm[0], k_norm[0])
    attn = _attention(qr, kr, zm)
    o_f = _gla(zm, bc_f, False)
    o_b = _gla(zm, bc_b, True)

    wr = jnp.zeros((D_MODEL, 128), F32).at[:, :N_EXPERTS].set(w_router[0])
    br = jnp.full((1, 128), -jnp.inf, F32).at[0, :N_EXPERTS].set(b_router[0])
    vec = lambda v: v[0].reshape(1, -1)
    h1, idx_pad, gate_pad = _mixer(
        attn, o_f, o_b, zm, zg, hf, w_br_a[0].astype(BF16), w_br_b[0].astype(BF16), w_o[0].astype(BF16),
        vec(gla_norm), vec(ln1_g), vec(ln1_b), wr, br)

    sb_e, sb_nblk, sb_nvalid, tab = _routing(idx_pad[:, :TOP_K])
    y_slots = _experts(sb_e, sb_nblk, sb_nvalid, tab, h1, w_up[0], b_up[0], w_down[0], b_down[0])

    out = _tail(y_slots, gate_pad, h1, p[0].reshape(TOK, PLE_DIM), w_ple_gate[0].astype(BF16),
                w_ple_proj[0].astype(BF16), vec(ln2_g), vec(ln2_b), vec(ln3_g), vec(ln3_b))
    return out.reshape(BATCH, SEQ, D_MODEL)
```

```python
import functools

import jax
import jax.numpy as jnp
from jax import lax
from jax.experimental import pallas as pl
from jax.experimental.pallas import tpu as pltpu

F32 = jnp.float32
BF16 = jnp.bfloat16

D_MODEL = 2048
BATCH = 2
SEQ = 4096
TOK = BATCH * SEQ
PLE_DIM = 256
GRID_W = 64
ATT_HEADS = 8
ATT_KV_HEADS = 2
ATT_GROUP = ATT_HEADS // ATT_KV_HEADS
HEAD_DIM = 128
ROPE_BASE = 10000.0
GLA_HEADS = 4
GLA_DK = 128
GLA_DV = 256
GLA_RANK = 16
GLA_TAU = 16.0
GLA_CHUNK = 64
GLA_SUB = 16
N_EXPERTS = 32
TOP_K = 4
D_FF = D_MODEL
SWIGLU_LIMIT = 7.0
SWIGLU_ALPHA = 1.702
MOE_BLOCK = 128
ATT_Q = ATT_HEADS * HEAD_DIM
ATT_KV = ATT_KV_HEADS * HEAD_DIM
GLA_K = GLA_HEADS * GLA_DK
GLA_V = GLA_HEADS * GLA_DV
MAIN_W = ATT_Q + 2 * ATT_KV + 2 * GLA_K + 2 * GLA_V
LR_W = 2 * GLA_RANK
DEPTH = 1
DN_ALPHA = (2 * DEPTH) ** 0.25
LN_EPS = 1e-5
RMS_EPS = 1e-6
LOG2_E = 1.4426950408889634

Z_TILE_OF_W_TILE = (0, 1, 6, 7, 8, 2, 3, 4, 5)
Z_Q, Z_VB, Z_OG, Z_KA, Z_VA, Z_QB, Z_KB = 0, 1024, 2048, 3072, 3328, 3584, 4096

VMEM_LIMIT = 56 * 1024 * 1024

N_ROWS = TOK * TOP_K + N_EXPERTS * MOE_BLOCK
N_BLOCKS = N_ROWS // MOE_BLOCK
SB_BLOCKS = 10
SB_ROWS = SB_BLOCKS * MOE_BLOCK
N_SB = (N_BLOCKS + (SB_BLOCKS - 1) * N_EXPERTS) // SB_BLOCKS
SUBLANES = 8
IDS_STRIDE = -(-2 * SB_ROWS // 1024) * 1024
FF_TILE = 256
N_FF_TILES = D_FF // FF_TILE


def _params(n_axes, vmem=None):
    return pltpu.CompilerParams(dimension_semantics=("arbitrary",) * n_axes,
                                vmem_limit_bytes=vmem or VMEM_LIMIT)


def _sigmoid(x):
    return 1.0 / (1.0 + jnp.exp(-x))


def _layer_norm(y, g, b):
    mu = jnp.mean(y, axis=-1, keepdims=True)
    yc = y - mu
    var = jnp.mean(yc * yc, axis=-1, keepdims=True)
    return yc * lax.rsqrt(var + LN_EPS) * g + b


def _dot(a, b):
    return jnp.dot(a, b, preferred_element_type=F32)


def _dot_nt(a, b):
    return lax.dot_general(a, b, (((1,), (1,)), ((), ())), preferred_element_type=F32)


def _dot_tn(a, b):
    return lax.dot_general(a, b, (((0,), (0,)), ((), ())), preferred_element_type=F32)


def _ln0_body(x_ref, g_ref, b_ref, hf_ref, hb_ref):
    y = _layer_norm(x_ref[...], g_ref[...], b_ref[...])
    hf_ref[...] = y
    hb_ref[...] = y.astype(BF16)


def _ln0(x2, g, b):
    tm = 256
    row = pl.BlockSpec((tm, D_MODEL), lambda i: (i, 0))
    vec = pl.BlockSpec((1, D_MODEL), lambda i: (0, 0))
    return pl.pallas_call(
        _ln0_body, grid=(TOK // tm,), in_specs=[row, vec, vec], out_specs=[row, row],
        out_shape=[jax.ShapeDtypeStruct((TOK, D_MODEL), F32), jax.ShapeDtypeStruct((TOK, D_MODEL), BF16)],
        compiler_params=_params(1), name="ln0")(x2, g.reshape(1, -1), b.reshape(1, -1))


def _inproj_body(perm_ref, a_ref, w_ref, o_ref, wb_ref, *, gate):
    del perm_ref

    @pl.when(pl.program_id(1) == 0)
    def _():
        wb_ref[...] = w_ref[...].astype(BF16)

    acc = _dot(a_ref[...], wb_ref[...])
    if gate:
        acc = _sigmoid(acc)
    o_ref[...] = acc.astype(o_ref.dtype)


def _inproj(hb, w, n_tiles, tile_perm, gate, name):
    tm, tn = 1024, 512
    grid_spec = pltpu.PrefetchScalarGridSpec(
        num_scalar_prefetch=1, grid=(n_tiles, TOK // tm),
        in_specs=[pl.BlockSpec((tm, D_MODEL), lambda n, m, p: (m, 0)),
                  pl.BlockSpec((D_MODEL, tn), lambda n, m, p: (0, n))],
        out_specs=pl.BlockSpec((tm, tn), lambda n, m, p: (m, p[n])),
        scratch_shapes=[pltpu.VMEM((D_MODEL, tn), BF16)])
    return pl.pallas_call(
        functools.partial(_inproj_body, gate=gate), grid_spec=grid_spec,
        out_shape=jax.ShapeDtypeStruct((TOK, n_tiles * tn), BF16),
        compiler_params=_params(2), name=name)(jnp.asarray(tile_perm, jnp.int32), hb, w)


def _decay_body(h_ref, wlr_ref, w2_ref, b2_ref, trif_ref, trib_ref, bf_ref, bb_ref):
    zlr = _dot(h_ref[...], wlr_ref[...])
    pre = jnp.dot(zlr, w2_ref[...], preferred_element_type=F32,
                  precision=lax.Precision.HIGHEST) + b2_ref[...]
    la = (jnp.minimum(pre, 0.0) - jnp.log1p(jnp.exp(-jnp.abs(pre)))) * (1.0 / GLA_TAU)
    hi = la.astype(BF16)
    lo = (la - hi.astype(F32)).astype(BF16)
    bf_ref[...] = _dot(trif_ref[...], hi[:, :GLA_K]) + _dot(trif_ref[...], lo[:, :GLA_K])
    bb_ref[...] = _dot(trib_ref[...], hi[:, GLA_K:]) + _dot(trib_ref[...], lo[:, GLA_K:])


def _decay(hb, wlr, w2, b2):
    tm = 512
    r = jnp.arange(tm)
    same = (r[:, None] // GLA_CHUNK) == (r[None, :] // GLA_CHUNK)
    trif = (same & (r[None, :] <= r[:, None])).astype(BF16)
    trib = (same & (r[None, :] >= r[:, None])).astype(BF16)
    full = lambda shape: pl.BlockSpec(shape, lambda i: (0, 0))
    out = pl.BlockSpec((tm, GLA_K), lambda i: (i, 0))
    return pl.pallas_call(
        _decay_body, grid=(TOK // tm,),
        in_specs=[pl.BlockSpec((tm, D_MODEL), lambda i: (i, 0)), full((D_MODEL, LR_W)),
                  full((LR_W, 2 * GLA_K)), full((1, 2 * GLA_K)), full((tm, tm)), full((tm, tm))],
        out_specs=[out, out],
        out_shape=[jax.ShapeDtypeStruct((TOK, GLA_K), F32)] * 2,
        compiler_params=_params(1), name="decay")(hb, wlr, w2, b2, trif, trib)


def _qkprep_body(q_ref, k_ref, cos_ref, sin_ref, qn_ref, kn_ref, qo_ref, ko_ref):
    cos = cos_ref[...]
    sin = sin_ref[...]
    lane = lax.broadcasted_iota(jnp.int32, cos.shape, 1)
    even = (lane % 2) == 0

    def one(x, gain, scale):
        x = x.astype(F32)
        x = x * lax.rsqrt(jnp.mean(x * x, axis=-1, keepdims=True) + RMS_EPS) * gain
        partner = jnp.where(even, pltpu.roll(x, HEAD_DIM - 1, 1), pltpu.roll(x, 1, 1))
        return ((x * cos + partner * sin) * scale).astype(BF16)

    for hh in range(ATT_HEADS):
        sl = slice(hh * HEAD_DIM, (hh + 1) * HEAD_DIM)
        qo_ref[:, sl] = one(q_ref[:, sl], qn_ref[...], LOG2_E * HEAD_DIM ** -0.5)
    for hh in range(ATT_KV_HEADS):
        sl = slice(hh * HEAD_DIM, (hh + 1) * HEAD_DIM)
        ko_ref[:, sl] = one(k_ref[:, sl], kn_ref[...], 1.0)


def _qkprep(zm, cos_full, sin_signed, q_norm, k_norm):
    tm = 256
    nrow = SEQ // tm
    tab = pl.BlockSpec((tm, HEAD_DIM), lambda i: (i % nrow, 0))
    vec = pl.BlockSpec((1, HEAD_DIM), lambda i: (0, 0))
    return pl.pallas_call(
        _qkprep_body, grid=(TOK // tm,),
        in_specs=[pl.BlockSpec((tm, ATT_Q), lambda i: (i, Z_Q // ATT_Q)),
                  pl.BlockSpec((tm, ATT_KV), lambda i: (i, Z_KA // ATT_KV)), tab, tab, vec, vec],
        out_specs=[pl.BlockSpec((tm, ATT_Q), lambda i: (i, 0)), pl.BlockSpec((tm, ATT_KV), lambda i: (i, 0))],
        out_shape=[jax.ShapeDtypeStruct((TOK, ATT_Q), BF16), jax.ShapeDtypeStruct((TOK, ATT_KV), BF16)],
        compiler_params=_params(1), name="qkprep")(zm, zm, cos_full, sin_signed,
                                                   q_norm.reshape(1, -1), k_norm.reshape(1, -1))


ATT_TQ = 256
ATT_TK = 512


ATT_SUB = 64


def _attn_body(q_ref, k_ref, v_ref, o_ref, qs_ref, s0_ref, s1_ref, p0_ref, p1_ref, a0_ref, a1_ref,
               acc_ref, m_ref, l_ref):
    rows = ATT_GROUP * ATT_TQ
    n_chunks = SEQ // ATT_TK
    for g in range(ATT_GROUP):
        qs_ref[g * ATT_TQ:(g + 1) * ATT_TQ, :] = q_ref[:, g * HEAD_DIM:(g + 1) * HEAD_DIM]
    m_ref[...] = jnp.full(m_ref.shape, -jnp.inf, F32)
    l_ref[...] = jnp.zeros(l_ref.shape, F32)
    acc_ref[...] = jnp.zeros(acc_ref.shape, F32)
    p1_ref[...] = jnp.zeros((rows, ATT_TK), BF16)
    a1_ref[...] = jnp.zeros(a1_ref.shape, F32)

    def chunk(c):
        return pl.ds(pl.multiple_of(c * ATT_TK, ATT_TK), ATT_TK)

    s0_ref[...] = _dot_nt(qs_ref[...], k_ref[0:ATT_TK, :])

    def step(c, s_cur, s_nxt, p_cur, p_prv, a_cur, a_prv):
        c_next = jnp.where(c + 1 < n_chunks, c + 1, 0)
        c_prev = jnp.where(c > 0, c - 1, 0)
        s_nxt[...] = _dot_nt(qs_ref[...], k_ref[chunk(c_next), :])
        acc_ref[...] = a_prv[...] * acc_ref[...] + _dot(p_prv[...], v_ref[chunk(c_prev), :])
        lane_blocks = [slice(cb * HEAD_DIM, (cb + 1) * HEAD_DIM) for cb in range(ATT_TK // HEAD_DIM)]
        m_all, l_all = m_ref[...], l_ref[...]
        sub_tiles = [slice(t * ATT_SUB, (t + 1) * ATT_SUB) for t in range(rows // ATT_SUB)]
        m_out = []
        for rr in sub_tiles:
            smax = functools.reduce(jnp.maximum, [s_cur[rr, cb] for cb in lane_blocks])
            m_out.append(jnp.maximum(m_all[rr], jnp.max(smax, axis=-1, keepdims=True)))
        l_out, a_out = [], []
        for rr, m_new in zip(sub_tiles, m_out):
            alpha = jnp.exp2(m_all[rr] - m_new)
            p = [jnp.exp2(s_cur[rr, cb] - m_new) for cb in lane_blocks]
            l_out.append(alpha * l_all[rr] + jnp.sum(functools.reduce(jnp.add, p), axis=-1, keepdims=True))
            a_out.append(alpha)
            for cb, pb in zip(lane_blocks, p):
                p_cur[rr, cb] = pb.astype(BF16)
        m_ref[...] = jnp.concatenate(m_out, axis=0)
        l_ref[...] = jnp.concatenate(l_out, axis=0)
        a_cur[...] = jnp.concatenate(a_out, axis=0)

    def pair(i, carry):
        step(2 * i, s0_ref, s1_ref, p0_ref, p1_ref, a0_ref, a1_ref)
        step(2 * i + 1, s1_ref, s0_ref, p1_ref, p0_ref, a1_ref, a0_ref)
        return carry

    lax.fori_loop(0, n_chunks // 2, pair, 0)
    acc = a1_ref[...] * acc_ref[...] + _dot(p1_ref[...], v_ref[SEQ - ATT_TK:SEQ, :])
    out = (acc / l_ref[...]).astype(BF16)
    for g in range(ATT_GROUP):
        o_ref[:, g * HEAD_DIM:(g + 1) * HEAD_DIM] = out[g * ATT_TQ:(g + 1) * ATT_TQ]


def _attention(qr, kr, zm):
    nq = SEQ // ATT_TQ
    gw = ATT_GROUP * HEAD_DIM
    rows = ATT_GROUP * ATT_TQ
    qspec = pl.BlockSpec((ATT_TQ, gw), lambda b, j, i: (b * nq + i, j))
    return pl.pallas_call(
        _attn_body, grid=(BATCH, ATT_KV_HEADS, nq),
        in_specs=[qspec,
                  pl.BlockSpec((SEQ, HEAD_DIM), lambda b, j, i: (b, j)),
                  pl.BlockSpec((SEQ, HEAD_DIM), lambda b, j, i: (b, Z_VA // HEAD_DIM + j))],
        out_specs=qspec,
        out_shape=jax.ShapeDtypeStruct((TOK, ATT_Q), BF16),
        scratch_shapes=[pltpu.VMEM((rows, HEAD_DIM), BF16),
                        pltpu.VMEM((rows, ATT_TK), F32), pltpu.VMEM((rows, ATT_TK), F32),
                        pltpu.VMEM((rows, ATT_TK), BF16), pltpu.VMEM((rows, ATT_TK), BF16),
                        pltpu.VMEM((rows, HEAD_DIM), F32), pltpu.VMEM((rows, HEAD_DIM), F32),
                        pltpu.VMEM((rows, HEAD_DIM), F32), pltpu.VMEM((rows, HEAD_DIM), F32),
                        pltpu.VMEM((rows, HEAD_DIM), F32)],
        compiler_params=_params(3), name="attn")(qr, kr, zm)


GLA_CB = 4
GLA_RB = GLA_CB * GLA_CHUNK
N_SUB = GLA_CHUNK // GLA_SUB


def _gla_body(q_ref, k_ref, v_ref, bc_ref, o_ref, st_ref, kf_ref, bs_ref, *, rev):
    @pl.when(pl.program_id(1) == 0)
    def _():
        st_ref[...] = jnp.zeros_like(st_ref)

    C, SUB = GLA_CHUNK, GLA_SUB
    rowc = lax.broadcasted_iota(jnp.int32, (C, GLA_DK), 0)
    rows_s = lax.broadcasted_iota(jnp.int32, (SUB, 128), 0)
    lane_s = lax.broadcasted_iota(jnp.int32, (SUB, 128), 1)

    def chunk(ci, carry):
        c = (GLA_CB - 1 - ci) if rev else ci
        r0 = pl.multiple_of(c * C, C)
        for hh in range(GLA_HEADS):
            ksl = slice(hh * GLA_DK, (hh + 1) * GLA_DK)
            vsl = slice(hh * GLA_DV, (hh + 1) * GLA_DV)
            q = q_ref[pl.ds(r0, C), ksl].astype(F32) * (GLA_DK ** -0.5)
            k = k_ref[pl.ds(r0, C), ksl].astype(F32)
            v = v_ref[pl.ds(r0, C), vsl]
            bc = bc_ref[pl.ds(r0, C), ksl]
            kf_ref[hh] = k
            bs_ref[hh] = bc
            st = st_ref[hh]
            blast = bc[0:1] if rev else bc[C - 1:C]
            o_inter = _dot_nt((q * jnp.exp(bc)).astype(BF16), st.astype(BF16))
            kdec = k * jnp.exp(blast - bc)
            st_ref[hh] = st * jnp.exp(blast) + _dot_tn(v, kdec.astype(BF16))

            a_rows = []
            for si in range(N_SUB):
                lo, hi = si * SUB, (si + 1) * SUB
                q_s, b_s = q[lo:hi], bc[lo:hi]
                has_earlier = (si < N_SUB - 1) if rev else (si > 0)
                if has_earlier:
                    ref_row = bc[hi:hi + 1] if rev else bc[lo - 1:lo]
                    earlier = (rowc >= hi) if rev else (rowc < lo)
                    qt = q_s * jnp.exp(b_s - ref_row)
                    kt = k * jnp.exp(jnp.where(earlier, ref_row - bc, -jnp.inf))
                    a = _dot_nt(qt.astype(BF16), kt.astype(BF16))
                else:
                    a = jnp.zeros((SUB, C), F32)
                diag = jnp.zeros((SUB, 128), F32)
                for jl in range(SUB):
                    j = lo + jl
                    d = jnp.minimum(b_s - bs_ref[hh, j:j + 1, :], 0.0)
                    col = jnp.sum(q_s * kf_ref[hh, j:j + 1, :] * jnp.exp(d), axis=-1, keepdims=True)
                    diag = jnp.where(lane_s == j, col, diag)
                keep = (lane_s >= rows_s + lo) if rev else (lane_s <= rows_s + lo)
                diag = jnp.where(keep & (lane_s >= lo) & (lane_s < hi), diag, 0.0)
                a_rows.append(a + diag[:, :C])
            a_full = jnp.concatenate(a_rows, axis=0)
            o_ref[pl.ds(r0, C), vsl] = o_inter + _dot(a_full.astype(BF16), v)
        return carry

    lax.fori_loop(0, GLA_CB, chunk, 0)


def _gla(zm, bcum, rev):
    ncb = SEQ // GLA_RB
    if rev:
        row = lambda b, c: b * ncb + (ncb - 1 - c)
    else:
        row = lambda b, c: b * ncb + c
    return pl.pallas_call(
        functools.partial(_gla_body, rev=rev), grid=(BATCH, ncb),
        in_specs=[pl.BlockSpec((GLA_RB, GLA_K), lambda b, c: (row(b, c), Z_QB // GLA_K)),
                  pl.BlockSpec((GLA_RB, GLA_K), lambda b, c: (row(b, c), Z_KB // GLA_K)),
                  pl.BlockSpec((GLA_RB, GLA_V), lambda b, c: (row(b, c), Z_VB // GLA_V)),
                  pl.BlockSpec((GLA_RB, GLA_K), lambda b, c: (row(b, c), 0))],
        out_specs=pl.BlockSpec((GLA_RB, GLA_V), lambda b, c: (row(b, c), 0)),
        out_shape=jax.ShapeDtypeStruct((TOK, GLA_V), F32),
        scratch_shapes=[pltpu.VMEM((GLA_HEADS, GLA_DV, GLA_DK), F32),
                        pltpu.VMEM((GLA_HEADS, GLA_CHUNK, GLA_DK), F32),
                        pltpu.VMEM((GLA_HEADS, GLA_CHUNK, GLA_DK), F32)],
        compiler_params=_params(2), name="gla_bwd" if rev else "gla_fwd")(zm, zm, zm, bcum)


def _mixer_body(attn_ref, of_ref, ob_ref, og_ref, ga_ref, gb_ref, h_ref,
                wa_ref, wb_ref, wo_ref, gn_ref, lg_ref, lb_ref, wr_ref, br_ref,
                h1_ref, idx_ref, gate_ref):
    osum = of_ref[...] + ob_ref[...]
    og = og_ref[...].astype(F32)
    gn = gn_ref[...]
    parts = []
    for hh in range(GLA_HEADS):
        sl = slice(hh * GLA_DV, (hh + 1) * GLA_DV)
        x = osum[:, sl]
        g = og[:, sl]
        xn = x * lax.rsqrt(jnp.mean(x * x, axis=-1, keepdims=True) + RMS_EPS) * gn
        parts.append((xn * (g * _sigmoid(g))).astype(BF16))
    onorm = jnp.concatenate(parts, axis=1)
    ya = _dot(attn_ref[...], wa_ref[...])
    yb = _dot(onorm, wb_ref[...])
    m = ga_ref[...].astype(F32) * ya + gb_ref[...].astype(F32) * yb
    mix = _dot(m.astype(BF16), wo_ref[...])
    h1 = _layer_norm(DN_ALPHA * h_ref[...] + mix, lg_ref[...], lb_ref[...])
    h1_ref[...] = h1

    logits = jnp.dot(h1, wr_ref[...], preferred_element_type=F32,
                     precision=lax.Precision.HIGHEST) + br_ref[...]
    lane = lax.broadcasted_iota(jnp.int32, logits.shape, 1)
    x = logits
    vals, idxs = [], []
    for _ in range(TOP_K):
        mx = jnp.max(x, axis=-1, keepdims=True)
        ix = jnp.min(jnp.where(x == mx, lane, 128), axis=-1, keepdims=True)
        vals.append(mx)
        idxs.append(ix)
        x = jnp.where(lane == ix, -jnp.inf, x)
    es = [jnp.exp(vv - vals[0]) for vv in vals]
    den = es[0] + es[1] + es[2] + es[3]
    idx_out = jnp.zeros(logits.shape, jnp.int32)
    gate_out = jnp.zeros(logits.shape, F32)
    for kk in range(TOP_K):
        idx_out = jnp.where(lane == kk, idxs[kk], idx_out)
        gate_out = jnp.where(lane == kk, es[kk] / den, gate_out)
    idx_ref[...] = idx_out
    gate_ref[...] = gate_out


def _mixer(attn, o_f, o_b, zm, zg, hf, wa, wb, wo, gn, lg, lb, wr, br):
    tm = 128
    const = lambda shape: pl.BlockSpec(shape, lambda i: (0,) * len(shape), pipeline_mode=pl.Buffered(1))
    row = lambda w, cb=0: pl.BlockSpec((tm, w), lambda i: (i, cb))
    return pl.pallas_call(
        _mixer_body, grid=(TOK // tm,),
        in_specs=[row(ATT_Q), row(GLA_V), row(GLA_V), row(GLA_V, Z_OG // GLA_V),
                  row(D_MODEL, 0), row(D_MODEL, 1), row(D_MODEL),
                  const((ATT_Q, D_MODEL)), const((GLA_V, D_MODEL)), const((D_MODEL, D_MODEL)),
                  const((1, GLA_DV)), const((1, D_MODEL)), const((1, D_MODEL)),
                  const((D_MODEL, 128)), const((1, 128))],
        out_specs=[row(D_MODEL), row(128), row(128)],
        out_shape=[jax.ShapeDtypeStruct((TOK, D_MODEL), F32),
                   jax.ShapeDtypeStruct((TOK, 128), jnp.int32),
                   jax.ShapeDtypeStruct((TOK, 128), F32)],
        compiler_params=_params(1), name="mixer")(attn, o_f, o_b, zm, zg, zg, hf, wa, wb, wo, gn, lg, lb, wr, br)


GROUP_SIZES = (512, 256, 128)
GATHER_STEPS = SB_BLOCKS // (SB_ROWS // GROUP_SIZES[0])


def _expert_body(e_ref, nb_ref, nv_ref, tab_hbm, h1_hbm, wg_ref, wl_ref, wd_ref, bg_ref, bl_ref, bd_ref,
                 y_hbm, gbuf, xb_ref, acc_ref, wgb_ref, wlb_ref, wdb_ref, ids, sem_ids, sem_g, sem_s):
    del e_ref
    s = pl.program_id(0)
    j = pl.program_id(1)
    nblk = nb_ref[s]
    slot = s & 1
    nxt = jnp.minimum(s + 1, N_SB - 1)
    nblk_next = jnp.where(s + 1 < N_SB, nb_ref[nxt], 0)
    tiles = MOE_BLOCK // SUBLANES

    def ids_copy(sb, sl):
        return pltpu.make_async_copy(tab_hbm.at[pl.ds(pl.multiple_of(sb * IDS_STRIDE, IDS_STRIDE), IDS_STRIDE)],
                                     ids.at[pl.ds(pl.multiple_of(sl * IDS_STRIDE, IDS_STRIDE), IDS_STRIDE)],
                                     sem_ids.at[sl])

    def gather_block(sl, b):
        id0 = sl * IDS_STRIDE + b * MOE_BLOCK

        def issue(t, c):
            for u in range(SUBLANES):
                tok = ids[id0 + t * SUBLANES + u]
                pltpu.make_async_copy(h1_hbm.at[pl.ds(tok >> 3, 1), pl.ds(tok & 7, 1), :],
                                      gbuf.at[pl.ds(b * tiles + t, 1), pl.ds(u, 1), :], sem_g).start()
            return c

        lax.fori_loop(0, tiles, issue, 0)

    def gather_wait_block(b):
        pltpu.make_async_copy(h1_hbm.at[pl.ds(0, tiles)], gbuf.at[pl.ds(b * tiles, tiles)], sem_g).wait()

    def loop_blocks(n, fn):
        def body(b, c):
            fn(b)
            return c

        lax.fori_loop(0, n, body, 0)

    @pl.when((s == 0) & (j == 0))
    def _():
        ids_copy(0, 0).start()
        ids_copy(0, 0).wait()
        loop_blocks(nblk, lambda b: gather_block(0, b))

    big = GROUP_SIZES[0]
    big_blocks = big // MOE_BLOCK
    n_big = nblk // big_blocks
    has_next = s + 1 < N_SB
    in_window = (j >= 1) & (j <= GATHER_STEPS) & has_next
    issued_here = GATHER_STEPS * n_big
    issued_prev = jnp.where(s > 0, GATHER_STEPS * (nb_ref[jnp.maximum(s - 1, 0)] // big_blocks), 0)

    @pl.when(j == 0)
    def _():
        loop_blocks(jnp.maximum(issued_prev, nblk), gather_wait_block)

    @pl.when(nblk > 0)
    def _():
        @pl.when(j == 0)
        def _():
            @pl.when(has_next)
            def _():
                ids_copy(nxt, 1 - slot).start()

            def take_block(b):
                x = gbuf[pl.ds(b * tiles, tiles)].reshape(MOE_BLOCK, D_MODEL)
                xb_ref[pl.ds(pl.multiple_of(b * MOE_BLOCK, MOE_BLOCK), MOE_BLOCK), :] = x.astype(BF16)
                acc_ref[pl.ds(b * tiles, tiles)] = jnp.broadcast_to(bd_ref[0], (tiles, SUBLANES, D_MODEL))

            loop_blocks(nblk, take_block)

        @pl.when((j == 1) & has_next)
        def _():
            ids_copy(nxt, 1 - slot).wait()

        @pl.when(j == GATHER_STEPS + 1)
        def _():
            for b in range(SB_BLOCKS):
                @pl.when((b >= issued_here) & (b < nblk_next))
                def _(b=b):
                    gather_block(1 - slot, b)

        wgb_ref[...] = wg_ref[0].astype(BF16)
        wlb_ref[...] = wl_ref[0].astype(BF16)
        wdb_ref[...] = wd_ref[0].astype(BF16)

        def group(r0, size):
            x = xb_ref[pl.ds(pl.multiple_of(r0, MOE_BLOCK), size), :]
            g = jnp.minimum(_dot(x, wgb_ref[...]) + bg_ref[0], SWIGLU_LIMIT)
            lin = jnp.clip(_dot(x, wlb_ref[...]) + bl_ref[0], -SWIGLU_LIMIT, SWIGLU_LIMIT)
            act = g * _sigmoid(SWIGLU_ALPHA * g) * (lin + 1.0)
            upd = _dot(act.astype(BF16), wdb_ref[...])
            acc_ref[pl.ds(r0 // SUBLANES, size // SUBLANES)] += upd.reshape(size // SUBLANES, SUBLANES, D_MODEL)

        def big_group(i, c):
            group(i * big, big)
            return c

        def big_group_and_gather(i, c):
            group(i * big, big)
            b = (j - 1) * n_big + i
            id0 = (1 - slot) * IDS_STRIDE + b * MOE_BLOCK
            for t in range(tiles):
                for u in range(SUBLANES):
                    tok = ids[id0 + t * SUBLANES + u]
                    pltpu.make_async_copy(h1_hbm.at[pl.ds(tok >> 3, 1), pl.ds(tok & 7, 1), :],
                                          gbuf.at[pl.ds(b * tiles + t, 1), pl.ds(u, 1), :], sem_g).start()
            return c

        @pl.when(in_window)
        def _():
            lax.fori_loop(0, n_big, big_group_and_gather, 0)

        @pl.when(jnp.logical_not(in_window))
        def _():
            lax.fori_loop(0, n_big, big_group, 0)

        done = n_big * big_blocks
        for size in GROUP_SIZES[1:]:
            take = ((nblk - done) // (size // MOE_BLOCK)) > 0

            @pl.when(take)
            def _(done=done, size=size):
                group(done * MOE_BLOCK, size)

            done = done + jnp.where(take, size // MOE_BLOCK, 0)

        @pl.when(j == N_FF_TILES - 1)
        def _():
            nvalid = nv_ref[s]

            dst0 = slot * IDS_STRIDE + SB_ROWS

            def row_copy(t, u, dst):
                return pltpu.make_async_copy(acc_ref.at[pl.ds(t, 1), pl.ds(u, 1), :],
                                             y_hbm.at[pl.ds(dst >> 3, 1), pl.ds(dst & 7, 1), :], sem_s)

            def issue_tile(t, c):
                for u in range(SUBLANES):
                    row_copy(t, u, ids[dst0 + t * SUBLANES + u]).start()
                return c

            def issue_row(r, c):
                row_copy(r >> 3, r & 7, ids[dst0 + r]).start()
                return c

            full_tiles = nvalid // SUBLANES
            lax.fori_loop(0, full_tiles, issue_tile, 0)
            lax.fori_loop(full_tiles * SUBLANES, nvalid, issue_row, 0)

            def wait_block(b, c):
                pltpu.make_async_copy(acc_ref.at[pl.ds(0, tiles)], y_hbm.at[pl.ds(0, tiles)], sem_s).wait()
                return c

            def wait_row(r, c):
                row_copy(0, 0, 0).wait()
                return c

            nfull = nvalid // MOE_BLOCK
            lax.fori_loop(0, nfull, wait_block, 0)
            lax.fori_loop(nfull * MOE_BLOCK, nvalid, wait_row, 0)


def _experts(sb_e, sb_nblk, sb_nvalid, tab, h1, w_up, b_up, w_down, b_down):
    def jj(j, nb, s):
        return jnp.where(nb[s] > 0, j, N_FF_TILES - 1)

    grid_spec = pltpu.PrefetchScalarGridSpec(
        num_scalar_prefetch=3, grid=(N_SB, N_FF_TILES),
        in_specs=[pl.BlockSpec(memory_space=pl.ANY),
                  pl.BlockSpec(memory_space=pl.ANY),
                  pl.BlockSpec((1, D_MODEL, FF_TILE), lambda s, j, e, nb, nv: (e[s], 0, jj(j, nb, s))),
                  pl.BlockSpec((1, D_MODEL, FF_TILE), lambda s, j, e, nb, nv: (e[s], 0, N_FF_TILES + jj(j, nb, s))),
                  pl.BlockSpec((1, FF_TILE, D_MODEL), lambda s, j, e, nb, nv: (e[s], jj(j, nb, s), 0)),
                  pl.BlockSpec((1, 1, FF_TILE), lambda s, j, e, nb, nv: (e[s], 0, jj(j, nb, s))),
                  pl.BlockSpec((1, 1, FF_TILE), lambda s, j, e, nb, nv: (e[s], 0, N_FF_TILES + jj(j, nb, s))),
                  pl.BlockSpec((1, 1, D_MODEL), lambda s, j, e, nb, nv: (e[s], 0, 0))],
        out_specs=pl.BlockSpec(memory_space=pl.ANY),
        scratch_shapes=[pltpu.VMEM((SB_ROWS // SUBLANES, SUBLANES, D_MODEL), F32),
                        pltpu.VMEM((SB_ROWS, D_MODEL), BF16),
                        pltpu.VMEM((SB_ROWS // SUBLANES, SUBLANES, D_MODEL), F32),
                        pltpu.VMEM((D_MODEL, FF_TILE), BF16),
                        pltpu.VMEM((D_MODEL, FF_TILE), BF16),
                        pltpu.VMEM((FF_TILE, D_MODEL), BF16),
                        pltpu.SMEM((2 * IDS_STRIDE,), jnp.int32),
                        pltpu.SemaphoreType.DMA((2,)),
                        pltpu.SemaphoreType.DMA(()),
                        pltpu.SemaphoreType.DMA(())])
    y = pl.pallas_call(
        _expert_body, grid_spec=grid_spec,
        out_shape=jax.ShapeDtypeStruct((TOP_K * TOK // SUBLANES, SUBLANES, D_MODEL), F32),
        compiler_params=_params(2), name="experts")(
            sb_e, sb_nblk, sb_nvalid, tab, h1.reshape(TOK // SUBLANES, SUBLANES, D_MODEL), w_up, w_up, w_down,
            b_up.reshape(N_EXPERTS, 1, 2 * D_FF), b_up.reshape(N_EXPERTS, 1, 2 * D_FF),
            b_down.reshape(N_EXPERTS, 1, D_MODEL))
    return y.reshape(TOP_K * TOK, D_MODEL)


def _tail_body(y0_ref, y1_ref, y2_ref, y3_ref, gate_ref, h1_ref, p_ref, wg_ref, wp_ref,
               l2g_ref, l2b_ref, l3g_ref, l3b_ref, o_ref):
    gate = gate_ref[...]
    y = gate[:, 0:1] * y0_ref[...]
    for kk, y_ref in enumerate((y1_ref, y2_ref, y3_ref), start=1):
        y = y + gate[:, kk:kk + 1] * y_ref[...]
    h2 = _layer_norm(DN_ALPHA * h1_ref[...] + y, l2g_ref[...], l2b_ref[...])
    ple = _sigmoid(_dot(h2.astype(BF16), wg_ref[...])) * _dot(p_ref[...].astype(BF16), wp_ref[...])
    o_ref[...] = _layer_norm(DN_ALPHA * h2 + ple, l3g_ref[...], l3b_ref[...])


def _tail(y_slots, gates, h1, p2, wg, wp, l2g, l2b, l3g, l3b):
    tm = 256
    nt = TOK // tm
    const = lambda shape: pl.BlockSpec(shape, lambda i: (0,) * len(shape), pipeline_mode=pl.Buffered(1))
    row = lambda w: pl.BlockSpec((tm, w), lambda i: (i, 0))
    yspec = lambda kk: pl.BlockSpec((tm, D_MODEL), lambda i: (kk * nt + i, 0))
    vec = const((1, D_MODEL))
    return pl.pallas_call(
        _tail_body, grid=(nt,),
        in_specs=[yspec(0), yspec(1), yspec(2), yspec(3), row(128), row(D_MODEL), row(PLE_DIM),
                  const((D_MODEL, D_MODEL)), const((PLE_DIM, D_MODEL)), vec, vec, vec, vec],
        out_specs=row(D_MODEL),
        out_shape=jax.ShapeDtypeStruct((TOK, D_MODEL), F32),
        compiler_params=_params(1), name="tail")(y_slots, y_slots, y_slots, y_slots, gates, h1, p2, wg, wp,
                                                 l2g, l2b, l3g, l3b)


def _routing(top_idx):
    flat_e = top_idx.reshape(-1)
    experts = jnp.arange(N_EXPERTS, dtype=jnp.int32)
    onehot = (flat_e[:, None] == experts[None, :]).astype(jnp.int32)
    csum = jnp.cumsum(onehot, axis=0)
    rank = jnp.sum(onehot * csum, axis=1) - 1
    counts = csum[-1]
    padded = (counts + MOE_BLOCK - 1) // MOE_BLOCK * MOE_BLOCK
    padded_end = jnp.cumsum(padded)
    padded_start = padded_end - padded
    dest = (padded_start[flat_e] + rank).astype(jnp.int32)
    asg = jnp.arange(TOK * TOP_K, dtype=jnp.int32)
    asg_of_row = jnp.zeros((N_ROWS + SB_ROWS,), jnp.int32).at[dest].set(asg)
    tok = (asg_of_row // TOP_K).reshape(-1, MOE_BLOCK)
    dst = ((asg_of_row % TOP_K) * TOK + asg_of_row // TOP_K).reshape(-1, MOE_BLOCK)
    nb = padded // MOE_BLOCK
    n_sb = (nb + SB_BLOCKS - 1) // SB_BLOCKS
    sb_end = jnp.cumsum(n_sb)
    total = sb_end[-1]
    s = jnp.arange(N_SB, dtype=jnp.int32)
    s_eff = jnp.minimum(s, total - 1)
    e = jnp.minimum(jnp.sum((sb_end[None, :] <= s_eff[:, None]).astype(jnp.int32), axis=1), N_EXPERTS - 1)
    local = s_eff - (sb_end[e] - n_sb[e])
    valid = s < total
    sb_nblk = jnp.where(valid, jnp.clip(nb[e] - local * SB_BLOCKS, 0, SB_BLOCKS), 0).astype(jnp.int32)
    sb_blk = padded_start[e] // MOE_BLOCK + local * SB_BLOCKS
    sb_nvalid = jnp.where(valid, jnp.clip(counts[e] - local * SB_ROWS, 0, SB_ROWS), 0).astype(jnp.int32)
    blocks = sb_blk[:, None] + jnp.arange(SB_BLOCKS, dtype=jnp.int32)[None, :]
    tab = jnp.concatenate([tok[blocks].reshape(N_SB, SB_ROWS), dst[blocks].reshape(N_SB, SB_ROWS),
                           jnp.zeros((N_SB, IDS_STRIDE - 2 * SB_ROWS), jnp.int32)], axis=1).astype(jnp.int32).reshape(-1)
    return e.astype(jnp.int32), sb_nblk, sb_nvalid, tab


def _rope_tables():
    rows = SEQ // GRID_W
    row = jnp.repeat(jnp.arange(rows), GRID_W)
    col = jnp.tile(jnp.arange(GRID_W), rows)
    n_pairs = HEAD_DIM // 4
    inv_freq = ROPE_BASE ** (-jnp.arange(n_pairs, dtype=F32) / n_pairs)
    ang = jnp.concatenate([row[:, None] * inv_freq, col[:, None] * inv_freq], -1)
    cos_full = jnp.repeat(jnp.cos(ang), 2, axis=-1)
    sin = jnp.sin(ang)
    sin_signed = jnp.stack([-sin, sin], axis=-1).reshape(SEQ, HEAD_DIM)
    return cos_full, sin_signed


def kernel(x, p, in_ln_g, in_ln_b, w_in, q_norm, k_norm, w_lr_f, b_lr_f, w_lr_b, b_lr_b, gla_norm, w_br_a, w_br_b, w_o, ln1_g, ln1_b, w_router, b_router, w_up, b_up, w_down, b_down, ln2_g, ln2_b, w_ple_gate, w_ple_proj, ln3_g, ln3_b):
    assert x.shape == (BATCH, SEQ, D_MODEL) and w_in.shape[0] == DEPTH == 1
    win = w_in[0]
    hf, hb = _ln0(x.reshape(TOK, D_MODEL), in_ln_g, in_ln_b)

    zm = _inproj(hb, win, MAIN_W // 512, Z_TILE_OF_W_TILE, False, "inproj_main")
    zg = _inproj(hb, win[:, MAIN_W + LR_W:].astype(BF16), 2 * D_MODEL // 512, tuple(range(8)), True, "inproj_gates")

    w2 = jnp.zeros((LR_W, 2 * GLA_K), F32)
    w2 = w2.at[:GLA_RANK, :GLA_K].set(w_lr_f[0]).at[GLA_RANK:, GLA_K:].set(w_lr_b[0])
    b2 = jnp.concatenate([b_lr_f[0], b_lr_b[0]]).reshape(1, -1)
    bc_f, bc_b = _decay(hb, win[:, MAIN_W:MAIN_W + LR_W].astype(BF16), w2, b2)

    cos_full, sin_signed = _rope_tables()
    qr, kr = _qkprep(zm, cos_full, sin_signed, q_norm[0], k_norm[0])
    attn = _attention(qr, kr, zm)
    o_f = _gla(zm, bc_f, False)
    o_b = _gla(zm, bc_b, True)

    wr = jnp.zeros((D_MODEL, 128), F32).at[:, :N_EXPERTS].set(w_router[0])
    br = jnp.full((1, 128), -jnp.inf, F32).at[0, :N_EXPERTS].set(b_router[0])
    vec = lambda v: v[0].reshape(1, -1)
    h1, idx_pad, gate_pad = _mixer(
        attn, o_f, o_b, zm, zg, hf, w_br_a[0].astype(BF16), w_br_b[0].astype(BF16), w_o[0].astype(BF16),
        vec(gla_norm), vec(ln1_g), vec(ln1_b), wr, br)

    sb_e, sb_nblk, sb_nvalid, tab = _routing(idx_pad[:, :TOP_K])
    y_slots = _experts(sb_e, sb_nblk, sb_nvalid, tab, h1, w_up[0], b_up[0], w_down[0], b_down[0])

    out = _tail(y_slots, gate_pad, h1, p[0].reshape(TOK, PLE_DIM), w_ple_gate[0].astype(BF16),
                w_ple_proj[0].astype(BF16), vec(ln2_g), vec(ln2_b), vec(ln3_g), vec(ln3_b))
    return out.reshape(BATCH, SEQ, D_MODEL)
```

```python
import functools

import jax
import jax.numpy as jnp
from jax import lax
from jax.experimental import pallas as pl
from jax.experimental.pallas import tpu as pltpu

F32 = jnp.float32
BF16 = jnp.bfloat16

D_MODEL = 2048
BATCH = 2
SEQ = 4096
TOK = BATCH * SEQ
PLE_DIM = 256
GRID_W = 64
ATT_HEADS = 8
ATT_KV_HEADS = 2
ATT_GROUP = ATT_HEADS // ATT_KV_HEADS
HEAD_DIM = 128
ROPE_BASE = 10000.0
GLA_HEADS = 4
GLA_DK = 128
GLA_DV = 256
GLA_RANK = 16
GLA_TAU = 16.0
GLA_CHUNK = 64
GLA_SUB = 16
N_EXPERTS = 32
TOP_K = 4
D_FF = D_MODEL
SWIGLU_LIMIT = 7.0
SWIGLU_ALPHA = 1.702
MOE_BLOCK = 128
ATT_Q = ATT_HEADS * HEAD_DIM
ATT_KV = ATT_KV_HEADS * HEAD_DIM
GLA_K = GLA_HEADS * GLA_DK
GLA_V = GLA_HEADS * GLA_DV
MAIN_W = ATT_Q + 2 * ATT_KV + 2 * GLA_K + 2 * GLA_V
LR_W = 2 * GLA_RANK
DEPTH = 1
DN_ALPHA = (2 * DEPTH) ** 0.25
LN_EPS = 1e-5
RMS_EPS = 1e-6
LOG2_E = 1.4426950408889634

Z_TILE_OF_W_TILE = (0, 1, 6, 7, 8, 2, 3, 4, 5)
Z_Q, Z_VB, Z_OG, Z_KA, Z_VA, Z_QB, Z_KB = 0, 1024, 2048, 3072, 3328, 3584, 4096

VMEM_LIMIT = 56 * 1024 * 1024

N_ROWS = TOK * TOP_K + N_EXPERTS * MOE_BLOCK
N_BLOCKS = N_ROWS // MOE_BLOCK
SB_BLOCKS = 10
SB_ROWS = SB_BLOCKS * MOE_BLOCK
N_SB = (N_BLOCKS + (SB_BLOCKS - 1) * N_EXPERTS) // SB_BLOCKS
SUBLANES = 8
IDS_STRIDE = -(-2 * SB_ROWS // 1024) * 1024
FF_TILE = 256
N_FF_TILES = D_FF // FF_TILE


def _params(n_axes, vmem=None):
    return pltpu.CompilerParams(dimension_semantics=("arbitrary",) * n_axes,
                                vmem_limit_bytes=vmem or VMEM_LIMIT)


def _sigmoid(x):
    return 1.0 / (1.0 + jnp.exp(-x))


def _layer_norm(y, g, b):
    mu = jnp.mean(y, axis=-1, keepdims=True)
    yc = y - mu
    var = jnp.mean(yc * yc, axis=-1, keepdims=True)
    return yc * lax.rsqrt(var + LN_EPS) * g + b


def _dot(a, b):
    return jnp.dot(a, b, preferred_element_type=F32)


def _hi_lo(x):
    hi = x.astype(BF16)
    return hi, (x - hi.astype(F32)).astype(BF16)


def _dot_split(x, w_ref):
    hi, lo = _hi_lo(x)
    return _dot(hi, w_ref[0]) + _dot(hi, w_ref[1]) + _dot(lo, w_ref[0])


def _dot_nt(a, b):
    return lax.dot_general(a, b, (((1,), (1,)), ((), ())), preferred_element_type=F32)


def _dot_tn(a, b):
    return lax.dot_general(a, b, (((0,), (0,)), ((), ())), preferred_element_type=F32)


def _ln0_body(x_ref, g_ref, b_ref, hf_ref, hb_ref):
    y = _layer_norm(x_ref[...], g_ref[...], b_ref[...])
    hf_ref[...] = y
    hb_ref[...] = y.astype(BF16)


def _ln0(x2, g, b):
    tm = 256
    row = pl.BlockSpec((tm, D_MODEL), lambda i: (i, 0))
    vec = pl.BlockSpec((1, D_MODEL), lambda i: (0, 0))
    return pl.pallas_call(
        _ln0_body, grid=(TOK // tm,), in_specs=[row, vec, vec], out_specs=[row, row],
        out_shape=[jax.ShapeDtypeStruct((TOK, D_MODEL), F32), jax.ShapeDtypeStruct((TOK, D_MODEL), BF16)],
        compiler_params=_params(1), name="ln0")(x2, g.reshape(1, -1), b.reshape(1, -1))


def _inproj_body(perm_ref, a_ref, w_ref, o_ref, wb_ref, *, gate):
    del perm_ref

    @pl.when(pl.program_id(1) == 0)
    def _():
        wb_ref[...] = w_ref[...].astype(BF16)

    acc = _dot(a_ref[...], wb_ref[...])
    if gate:
        acc = _sigmoid(acc)
    o_ref[...] = acc.astype(o_ref.dtype)


def _inproj(hb, w, n_tiles, tile_perm, gate, name):
    tm, tn = 1024, 512
    grid_spec = pltpu.PrefetchScalarGridSpec(
        num_scalar_prefetch=1, grid=(n_tiles, TOK // tm),
        in_specs=[pl.BlockSpec((tm, D_MODEL), lambda n, m, p: (m, 0)),
                  pl.BlockSpec((D_MODEL, tn), lambda n, m, p: (0, n))],
        out_specs=pl.BlockSpec((tm, tn), lambda n, m, p: (m, p[n])),
        scratch_shapes=[pltpu.VMEM((D_MODEL, tn), BF16)])
    return pl.pallas_call(
        functools.partial(_inproj_body, gate=gate), grid_spec=grid_spec,
        out_shape=jax.ShapeDtypeStruct((TOK, n_tiles * tn), BF16),
        compiler_params=_params(2), name=name)(jnp.asarray(tile_perm, jnp.int32), hb, w)


def _decay_body(h_ref, wlr_ref, w2_ref, b2_ref, trif_ref, trib_ref, bf_ref, bb_ref):
    zlr = _dot(h_ref[...], wlr_ref[...])
    pre = _dot_split(zlr, w2_ref) + b2_ref[...]
    la = (jnp.minimum(pre, 0.0) - jnp.log1p(jnp.exp(-jnp.abs(pre)))) * (1.0 / GLA_TAU)
    hi = la.astype(BF16)
    lo = (la - hi.astype(F32)).astype(BF16)
    bf_ref[...] = _dot(trif_ref[...], hi[:, :GLA_K]) + _dot(trif_ref[...], lo[:, :GLA_K])
    bb_ref[...] = _dot(trib_ref[...], hi[:, GLA_K:]) + _dot(trib_ref[...], lo[:, GLA_K:])


def _decay(hb, wlr, w2, b2):
    tm = 512
    r = jnp.arange(tm)
    same = (r[:, None] // GLA_CHUNK) == (r[None, :] // GLA_CHUNK)
    trif = (same & (r[None, :] <= r[:, None])).astype(BF16)
    trib = (same & (r[None, :] >= r[:, None])).astype(BF16)
    full = lambda shape: pl.BlockSpec(shape, lambda i: (0,) * len(shape))
    out = pl.BlockSpec((tm, GLA_K), lambda i: (i, 0))
    return pl.pallas_call(
        _decay_body, grid=(TOK // tm,),
        in_specs=[pl.BlockSpec((tm, D_MODEL), lambda i: (i, 0)), full((D_MODEL, LR_W)),
                  full((2, LR_W, 2 * GLA_K)), full((1, 2 * GLA_K)), full((tm, tm)), full((tm, tm))],
        out_specs=[out, out],
        out_shape=[jax.ShapeDtypeStruct((TOK, GLA_K), F32)] * 2,
        compiler_params=_params(1), name="decay")(hb, wlr, w2, b2, trif, trib)


def _qkprep_body(q_ref, k_ref, cos_ref, sin_ref, qn_ref, kn_ref, qo_ref, ko_ref):
    cos = cos_ref[...]
    sin = sin_ref[...]
    lane = lax.broadcasted_iota(jnp.int32, cos.shape, 1)
    even = (lane % 2) == 0

    def one(x, gain, scale):
        x = x.astype(F32)
        x = x * lax.rsqrt(jnp.mean(x * x, axis=-1, keepdims=True) + RMS_EPS) * gain
        partner = jnp.where(even, pltpu.roll(x, HEAD_DIM - 1, 1), pltpu.roll(x, 1, 1))
        return ((x * cos + partner * sin) * scale).astype(BF16)

    for hh in range(ATT_HEADS):
        sl = slice(hh * HEAD_DIM, (hh + 1) * HEAD_DIM)
        qo_ref[:, sl] = one(q_ref[:, sl], qn_ref[...], LOG2_E * HEAD_DIM ** -0.5)
    for hh in range(ATT_KV_HEADS):
        sl = slice(hh * HEAD_DIM, (hh + 1) * HEAD_DIM)
        ko_ref[:, sl] = one(k_ref[:, sl], kn_ref[...], 1.0)


def _qkprep(zm, cos_full, sin_signed, q_norm, k_norm):
    tm = 256
    nrow = SEQ // tm
    tab = pl.BlockSpec((tm, HEAD_DIM), lambda i: (i % nrow, 0))
    vec = pl.BlockSpec((1, HEAD_DIM), lambda i: (0, 0))
    return pl.pallas_call(
        _qkprep_body, grid=(TOK // tm,),
        in_specs=[pl.BlockSpec((tm, ATT_Q), lambda i: (i, Z_Q // ATT_Q)),
                  pl.BlockSpec((tm, ATT_KV), lambda i: (i, Z_KA // ATT_KV)), tab, tab, vec, vec],
        out_specs=[pl.BlockSpec((tm, ATT_Q), lambda i: (i, 0)), pl.BlockSpec((tm, ATT_KV), lambda i: (i, 0))],
        out_shape=[jax.ShapeDtypeStruct((TOK, ATT_Q), BF16), jax.ShapeDtypeStruct((TOK, ATT_KV), BF16)],
        compiler_params=_params(1), name="qkprep")(zm, zm, cos_full, sin_signed,
                                                   q_norm.reshape(1, -1), k_norm.reshape(1, -1))


ATT_TQ = 256
ATT_TK = 512


ATT_SUB = 64


def _attn_body(q_ref, k_ref, v_ref, o_ref, qs_ref, s0_ref, s1_ref, p0_ref, p1_ref, a0_ref, a1_ref,
               acc_ref, m_ref, l_ref):
    rows = ATT_GROUP * ATT_TQ
    n_chunks = SEQ // ATT_TK
    for g in range(ATT_GROUP):
        qs_ref[g * ATT_TQ:(g + 1) * ATT_TQ, :] = q_ref[:, g * HEAD_DIM:(g + 1) * HEAD_DIM]
    m_ref[...] = jnp.full(m_ref.shape, -jnp.inf, F32)
    l_ref[...] = jnp.zeros(l_ref.shape, F32)
    acc_ref[...] = jnp.zeros(acc_ref.shape, F32)
    p1_ref[...] = jnp.zeros((rows, ATT_TK), BF16)
    a1_ref[...] = jnp.zeros(a1_ref.shape, F32)

    def chunk(c):
        return pl.ds(pl.multiple_of(c * ATT_TK, ATT_TK), ATT_TK)

    s0_ref[...] = _dot_nt(qs_ref[...], k_ref[0:ATT_TK, :])

    def step(c, s_cur, s_nxt, p_cur, p_prv, a_cur, a_prv):
        c_next = jnp.where(c + 1 < n_chunks, c + 1, 0)
        c_prev = jnp.where(c > 0, c - 1, 0)
        s_nxt[...] = _dot_nt(qs_ref[...], k_ref[chunk(c_next), :])
        acc_ref[...] = a_prv[...] * acc_ref[...] + _dot(p_prv[...], v_ref[chunk(c_prev), :])
        lane_blocks = [slice(cb * HEAD_DIM, (cb + 1) * HEAD_DIM) for cb in range(ATT_TK // HEAD_DIM)]
        m_all, l_all = m_ref[...], l_ref[...]
        sub_tiles = [slice(t * ATT_SUB, (t + 1) * ATT_SUB) for t in range(rows // ATT_SUB)]
        m_out = []
        for rr in sub_tiles:
            smax = functools.reduce(jnp.maximum, [s_cur[rr, cb] for cb in lane_blocks])
            m_out.append(jnp.maximum(m_all[rr], jnp.max(smax, axis=-1, keepdims=True)))
        l_out, a_out = [], []
        for rr, m_new in zip(sub_tiles, m_out):
            alpha = jnp.exp2(m_all[rr] - m_new)
            p = [jnp.exp2(s_cur[rr, cb] - m_new) for cb in lane_blocks]
            l_out.append(alpha * l_all[rr] + jnp.sum(functools.reduce(jnp.add, p), axis=-1, keepdims=True))
            a_out.append(alpha)
            for cb, pb in zip(lane_blocks, p):
                p_cur[rr, cb] = pb.astype(BF16)
        m_ref[...] = jnp.concatenate(m_out, axis=0)
        l_ref[...] = jnp.concatenate(l_out, axis=0)
        a_cur[...] = jnp.concatenate(a_out, axis=0)

    def pair(i, carry):
        step(2 * i, s0_ref, s1_ref, p0_ref, p1_ref, a0_ref, a1_ref)
        step(2 * i + 1, s1_ref, s0_ref, p1_ref, p0_ref, a1_ref, a0_ref)
        return carry

    lax.fori_loop(0, n_chunks // 2, pair, 0)
    acc = a1_ref[...] * acc_ref[...] + _dot(p1_ref[...], v_ref[SEQ - ATT_TK:SEQ, :])
    out = (acc / l_ref[...]).astype(BF16)
    for g in range(ATT_GROUP):
        o_ref[:, g * HEAD_DIM:(g + 1) * HEAD_DIM] = out[g * ATT_TQ:(g + 1) * ATT_TQ]


def _attention(qr, kr, zm):
    nq = SEQ // ATT_TQ
    gw = ATT_GROUP * HEAD_DIM
    rows = ATT_GROUP * ATT_TQ
    qspec = pl.BlockSpec((ATT_TQ, gw), lambda b, j, i: (b * nq + i, j))
    return pl.pallas_call(
        _attn_body, grid=(BATCH, ATT_KV_HEADS, nq),
        in_specs=[qspec,
                  pl.BlockSpec((SEQ, HEAD_DIM), lambda b, j, i: (b, j)),
                  pl.BlockSpec((SEQ, HEAD_DIM), lambda b, j, i: (b, Z_VA // HEAD_DIM + j))],
        out_specs=qspec,
        out_shape=jax.ShapeDtypeStruct((TOK, ATT_Q), BF16),
        scratch_shapes=[pltpu.VMEM((rows, HEAD_DIM), BF16),
                        pltpu.VMEM((rows, ATT_TK), F32), pltpu.VMEM((rows, ATT_TK), F32),
                        pltpu.VMEM((rows, ATT_TK), BF16), pltpu.VMEM((rows, ATT_TK), BF16),
                        pltpu.VMEM((rows, HEAD_DIM), F32), pltpu.VMEM((rows, HEAD_DIM), F32),
                        pltpu.VMEM((rows, HEAD_DIM), F32), pltpu.VMEM((rows, HEAD_DIM), F32),
                        pltpu.VMEM((rows, HEAD_DIM), F32)],
        compiler_params=_params(3), name="attn")(qr, kr, zm)


GLA_CB = 4
GLA_RB = GLA_CB * GLA_CHUNK
N_SUB = GLA_CHUNK // GLA_SUB


def _gla_body(q_ref, k_ref, v_ref, bc_ref, o_ref, st_ref, kf_ref, bs_ref, *, rev):
    @pl.when(pl.program_id(1) == 0)
    def _():
        st_ref[...] = jnp.zeros_like(st_ref)

    C, SUB = GLA_CHUNK, GLA_SUB
    rowc = lax.broadcasted_iota(jnp.int32, (C, GLA_DK), 0)
    rows_s = lax.broadcasted_iota(jnp.int32, (SUB, 128), 0)
    lane_s = lax.broadcasted_iota(jnp.int32, (SUB, 128), 1)

    def chunk(ci, carry):
        c = (GLA_CB - 1 - ci) if rev else ci
        r0 = pl.multiple_of(c * C, C)
        for hh in range(GLA_HEADS):
            ksl = slice(hh * GLA_DK, (hh + 1) * GLA_DK)
            vsl = slice(hh * GLA_DV, (hh + 1) * GLA_DV)
            q = q_ref[pl.ds(r0, C), ksl].astype(F32) * (GLA_DK ** -0.5)
            k = k_ref[pl.ds(r0, C), ksl].astype(F32)
            v = v_ref[pl.ds(r0, C), vsl]
            bc = bc_ref[pl.ds(r0, C), ksl]
            kf_ref[hh] = k
            bs_ref[hh] = bc
            st = st_ref[hh]
            blast = bc[0:1] if rev else bc[C - 1:C]
            o_inter = _dot_nt((q * jnp.exp(bc)).astype(BF16), st.astype(BF16))
            kdec = k * jnp.exp(blast - bc)
            st_ref[hh] = st * jnp.exp(blast) + _dot_tn(v, kdec.astype(BF16))

            a_rows = []
            for si in range(N_SUB):
                lo, hi = si * SUB, (si + 1) * SUB
                q_s, b_s = q[lo:hi], bc[lo:hi]
                has_earlier = (si < N_SUB - 1) if rev else (si > 0)
                if has_earlier:
                    ref_row = bc[hi:hi + 1] if rev else bc[lo - 1:lo]
                    earlier = (rowc >= hi) if rev else (rowc < lo)
                    qt = q_s * jnp.exp(b_s - ref_row)
                    kt = k * jnp.exp(jnp.where(earlier, ref_row - bc, -jnp.inf))
                    a = _dot_nt(qt.astype(BF16), kt.astype(BF16))
                else:
                    a = jnp.zeros((SUB, C), F32)
                diag = jnp.zeros((SUB, 128), F32)
                for jl in range(SUB):
                    j = lo + jl
                    d = jnp.minimum(b_s - bs_ref[hh, j:j + 1, :], 0.0)
                    col = jnp.sum(q_s * kf_ref[hh, j:j + 1, :] * jnp.exp(d), axis=-1, keepdims=True)
                    diag = jnp.where(lane_s == j, col, diag)
                keep = (lane_s >= rows_s + lo) if rev else (lane_s <= rows_s + lo)
                diag = jnp.where(keep & (lane_s >= lo) & (lane_s < hi), diag, 0.0)
                a_rows.append(a + diag[:, :C])
            a_full = jnp.concatenate(a_rows, axis=0)
            o_ref[pl.ds(r0, C), vsl] = o_inter + _dot(a_full.astype(BF16), v)
        return carry

    lax.fori_loop(0, GLA_CB, chunk, 0)


def _gla(zm, bcum, rev):
    ncb = SEQ // GLA_RB
    if rev:
        row = lambda b, c: b * ncb + (ncb - 1 - c)
    else:
        row = lambda b, c: b * ncb + c
    return pl.pallas_call(
        functools.partial(_gla_body, rev=rev), grid=(BATCH, ncb),
        in_specs=[pl.BlockSpec((GLA_RB, GLA_K), lambda b, c: (row(b, c), Z_QB // GLA_K)),
                  pl.BlockSpec((GLA_RB, GLA_K), lambda b, c: (row(b, c), Z_KB // GLA_K)),
                  pl.BlockSpec((GLA_RB, GLA_V), lambda b, c: (row(b, c), Z_VB // GLA_V)),
                  pl.BlockSpec((GLA_RB, GLA_K), lambda b, c: (row(b, c), 0))],
        out_specs=pl.BlockSpec((GLA_RB, GLA_V), lambda b, c: (row(b, c), 0)),
        out_shape=jax.ShapeDtypeStruct((TOK, GLA_V), F32),
        scratch_shapes=[pltpu.VMEM((GLA_HEADS, GLA_DV, GLA_DK), F32),
                        pltpu.VMEM((GLA_HEADS, GLA_CHUNK, GLA_DK), F32),
                        pltpu.VMEM((GLA_HEADS, GLA_CHUNK, GLA_DK), F32)],
        compiler_params=_params(2), name="gla_bwd" if rev else "gla_fwd")(zm, zm, zm, bcum)


MIX_TM = 512


def _onorm_body(of_ref, ob_ref, og_ref, gn_ref, o_ref):
    gn = gn_ref[...]
    for hh in range(GLA_HEADS):
        sl = slice(hh * GLA_DV, (hh + 1) * GLA_DV)
        x = of_ref[:, sl] + ob_ref[:, sl]
        g = og_ref[:, sl].astype(F32)
        xn = x * lax.rsqrt(jnp.mean(x * x, axis=-1, keepdims=True) + RMS_EPS) * gn
        o_ref[:, sl] = (xn * (g * _sigmoid(g))).astype(BF16)


def _merge_body(attn_ref, on_ref, ga_ref, gb_ref, wa_ref, wb_ref, m_ref):
    ya = _dot(attn_ref[...], wa_ref[...])
    yb = _dot(on_ref[...], wb_ref[...])
    m_ref[...] = (ga_ref[...].astype(F32) * ya + gb_ref[...].astype(F32) * yb).astype(BF16)


def _outproj_body(m_ref, h_ref, wo_ref, lg_ref, lb_ref, wr_ref, br_ref, h1_ref, idx_ref, gate_ref):
    mix = _dot(m_ref[...], wo_ref[...])
    h1 = _layer_norm(DN_ALPHA * h_ref[...] + mix, lg_ref[...], lb_ref[...])
    h1_ref[...] = h1

    logits = _dot_split(h1, wr_ref) + br_ref[...]
    lane = lax.broadcasted_iota(jnp.int32, logits.shape, 1)
    x = logits
    vals, idxs = [], []
    for _ in range(TOP_K):
        mx = jnp.max(x, axis=-1, keepdims=True)
        ix = jnp.min(jnp.where(x == mx, lane, 128), axis=-1, keepdims=True)
        vals.append(mx)
        idxs.append(ix)
        x = jnp.where(lane == ix, -jnp.inf, x)
    es = [jnp.exp(vv - vals[0]) for vv in vals]
    den = es[0] + es[1] + es[2] + es[3]
    idx_out = jnp.zeros(logits.shape, jnp.int32)
    gate_out = jnp.zeros(logits.shape, F32)
    for kk in range(TOP_K):
        idx_out = jnp.where(lane == kk, idxs[kk], idx_out)
        gate_out = jnp.where(lane == kk, es[kk] / den, gate_out)
    idx_ref[...] = idx_out
    gate_ref[...] = gate_out


def _mixer(attn, o_f, o_b, zm, zg, hf, wa, wb, wo, gn, lg, lb, wr, br):
    tm = MIX_TM
    const = lambda shape: pl.BlockSpec(shape, lambda i: (0,) * len(shape), pipeline_mode=pl.Buffered(1))
    row = lambda w, cb=0: pl.BlockSpec((tm, w), lambda i: (i, cb))
    grid = (TOK // tm,)
    onorm = pl.pallas_call(
        _onorm_body, grid=grid,
        in_specs=[row(GLA_V), row(GLA_V), row(GLA_V, Z_OG // GLA_V), const((1, GLA_DV))],
        out_specs=row(GLA_V), out_shape=jax.ShapeDtypeStruct((TOK, GLA_V), BF16),
        compiler_params=_params(1), name="onorm")(o_f, o_b, zm, gn)
    merged = pl.pallas_call(
        _merge_body, grid=grid,
        in_specs=[row(ATT_Q), row(GLA_V), row(D_MODEL, 0), row(D_MODEL, 1),
                  const((ATT_Q, D_MODEL)), const((GLA_V, D_MODEL))],
        out_specs=row(D_MODEL), out_shape=jax.ShapeDtypeStruct((TOK, D_MODEL), BF16),
        compiler_params=_params(1), name="merge")(attn, onorm, zg, zg, wa, wb)
    return pl.pallas_call(
        _outproj_body, grid=grid,
        in_specs=[row(D_MODEL), row(D_MODEL), const((D_MODEL, D_MODEL)), const((1, D_MODEL)), const((1, D_MODEL)),
                  const((2, D_MODEL, 128)), const((1, 128))],
        out_specs=[row(D_MODEL), row(128), row(128)],
        out_shape=[jax.ShapeDtypeStruct((TOK, D_MODEL), F32),
                   jax.ShapeDtypeStruct((TOK, 128), jnp.int32),
                   jax.ShapeDtypeStruct((TOK, 128), F32)],
        compiler_params=_params(1), name="outproj")(merged, hf, wo, lg, lb, wr, br)


GROUP_SIZES = (512, 256, 128)
GATHER_STEPS = SB_BLOCKS // (SB_ROWS // GROUP_SIZES[0])


def _expert_body(e_ref, nb_ref, nv_ref, tab_hbm, h1_hbm, wg_ref, wl_ref, wd_ref, bg_ref, bl_ref, bd_ref,
                 y_hbm, gbuf, xb_ref, acc_ref, wgb_ref, wlb_ref, wdb_ref, ids, sem_ids, sem_g, sem_s):
    del e_ref
    s = pl.program_id(0)
    j = pl.program_id(1)
    nblk = nb_ref[s]
    slot = s & 1
    nxt = jnp.minimum(s + 1, N_SB - 1)
    nblk_next = jnp.where(s + 1 < N_SB, nb_ref[nxt], 0)
    tiles = MOE_BLOCK // SUBLANES

    def ids_copy(sb, sl):
        return pltpu.make_async_copy(tab_hbm.at[pl.ds(pl.multiple_of(sb * IDS_STRIDE, IDS_STRIDE), IDS_STRIDE)],
                                     ids.at[pl.ds(pl.multiple_of(sl * IDS_STRIDE, IDS_STRIDE), IDS_STRIDE)],
                                     sem_ids.at[sl])

    def gather_block(sl, b):
        id0 = sl * IDS_STRIDE + b * MOE_BLOCK

        def issue(t, c):
            for u in range(SUBLANES):
                tok = ids[id0 + t * SUBLANES + u]
                pltpu.make_async_copy(h1_hbm.at[pl.ds(tok >> 3, 1), pl.ds(tok & 7, 1), :],
                                      gbuf.at[pl.ds(b * tiles + t, 1), pl.ds(u, 1), :], sem_g).start()
            return c

        lax.fori_loop(0, tiles, issue, 0)

    def gather_wait_block(b):
        pltpu.make_async_copy(h1_hbm.at[pl.ds(0, tiles)], gbuf.at[pl.ds(b * tiles, tiles)], sem_g).wait()

    def loop_blocks(n, fn):
        def body(b, c):
            fn(b)
            return c

        lax.fori_loop(0, n, body, 0)

    @pl.when((s == 0) & (j == 0))
    def _():
        ids_copy(0, 0).start()
        ids_copy(0, 0).wait()
        loop_blocks(nblk, lambda b: gather_block(0, b))

    big = GROUP_SIZES[0]
    big_blocks = big // MOE_BLOCK
    n_big = nblk // big_blocks
    has_next = s + 1 < N_SB
    in_window = (j >= 1) & (j <= GATHER_STEPS) & has_next
    issued_here = GATHER_STEPS * n_big
    issued_prev = jnp.where(s > 0, GATHER_STEPS * (nb_ref[jnp.maximum(s - 1, 0)] // big_blocks), 0)

    @pl.when(j == 0)
    def _():
        loop_blocks(jnp.maximum(issued_prev, nblk), gather_wait_block)

    @pl.when(nblk > 0)
    def _():
        @pl.when(j == 0)
        def _():
            @pl.when(has_next)
            def _():
                ids_copy(nxt, 1 - slot).start()

            def take_block(b):
                x = gbuf[pl.ds(b * tiles, tiles)].reshape(MOE_BLOCK, D_MODEL)
                xb_ref[pl.ds(pl.multiple_of(b * MOE_BLOCK, MOE_BLOCK), MOE_BLOCK), :] = x.astype(BF16)
                acc_ref[pl.ds(b * tiles, tiles)] = jnp.broadcast_to(bd_ref[0], (tiles, SUBLANES, D_MODEL))

            loop_blocks(nblk, take_block)

        @pl.when((j == 1) & has_next)
        def _():
            ids_copy(nxt, 1 - slot).wait()

        @pl.when(j == GATHER_STEPS + 1)
        def _():
            for b in range(SB_BLOCKS):
                @pl.when((b >= issued_here) & (b < nblk_next))
                def _(b=b):
                    gather_block(1 - slot, b)

        wgb_ref[...] = wg_ref[0].astype(BF16)
        wlb_ref[...] = wl_ref[0].astype(BF16)
        wdb_ref[...] = wd_ref[0].astype(BF16)

        def group(r0, size):
            x = xb_ref[pl.ds(pl.multiple_of(r0, MOE_BLOCK), size), :]
            g = jnp.minimum(_dot(x, wgb_ref[...]) + bg_ref[0], SWIGLU_LIMIT)
            lin = jnp.clip(_dot(x, wlb_ref[...]) + bl_ref[0], -SWIGLU_LIMIT, SWIGLU_LIMIT)
            act = g * _sigmoid(SWIGLU_ALPHA * g) * (lin + 1.0)
            upd = _dot(act.astype(BF16), wdb_ref[...])
            acc_ref[pl.ds(r0 // SUBLANES, size // SUBLANES)] += upd.reshape(size // SUBLANES, SUBLANES, D_MODEL)

        def big_group(i, c):
            group(i * big, big)
            return c

        def big_group_and_gather(i, c):
            group(i * big, big)
            b = (j - 1) * n_big + i
            id0 = (1 - slot) * IDS_STRIDE + b * MOE_BLOCK
            for t in range(tiles):
                for u in range(SUBLANES):
                    tok = ids[id0 + t * SUBLANES + u]
                    pltpu.make_async_copy(h1_hbm.at[pl.ds(tok >> 3, 1), pl.ds(tok & 7, 1), :],
                                          gbuf.at[pl.ds(b * tiles + t, 1), pl.ds(u, 1), :], sem_g).start()
            return c

        @pl.when(in_window)
        def _():
            lax.fori_loop(0, n_big, big_group_and_gather, 0)

        @pl.when(jnp.logical_not(in_window))
        def _():
            lax.fori_loop(0, n_big, big_group, 0)

        done = n_big * big_blocks
        for size in GROUP_SIZES[1:]:
            take = ((nblk - done) // (size // MOE_BLOCK)) > 0

            @pl.when(take)
            def _(done=done, size=size):
                group(done * MOE_BLOCK, size)

            done = done + jnp.where(take, size // MOE_BLOCK, 0)

        @pl.when(j == N_FF_TILES - 1)
        def _():
            nvalid = nv_ref[s]

            dst0 = slot * IDS_STRIDE + SB_ROWS

            def row_copy(t, u, dst):
                return pltpu.make_async_copy(acc_ref.at[pl.ds(t, 1), pl.ds(u, 1), :],
                                             y_hbm.at[pl.ds(dst >> 3, 1), pl.ds(dst & 7, 1), :], sem_s)

            def issue_tile(t, c):
                for u in range(SUBLANES):
                    row_copy(t, u, ids[dst0 + t * SUBLANES + u]).start()
                return c

            def issue_row(r, c):
                row_copy(r >> 3, r & 7, ids[dst0 + r]).start()
                return c

            full_tiles = nvalid // SUBLANES
            lax.fori_loop(0, full_tiles, issue_tile, 0)
            lax.fori_loop(full_tiles * SUBLANES, nvalid, issue_row, 0)

            def wait_block(b, c):
                pltpu.make_async_copy(acc_ref.at[pl.ds(0, tiles)], y_hbm.at[pl.ds(0, tiles)], sem_s).wait()
                return c

            def wait_row(r, c):
                row_copy(0, 0, 0).wait()
                return c

            nfull = nvalid // MOE_BLOCK
            lax.fori_loop(0, nfull, wait_block, 0)
            lax.fori_loop(nfull * MOE_BLOCK, nvalid, wait_row, 0)


def _experts(sb_e, sb_nblk, sb_nvalid, tab, h1, w_up, b_up, w_down, b_down):
    def jj(j, nb, s):
        return jnp.where(nb[s] > 0, j, N_FF_TILES - 1)

    grid_spec = pltpu.PrefetchScalarGridSpec(
        num_scalar_prefetch=3, grid=(N_SB, N_FF_TILES),
        in_specs=[pl.BlockSpec(memory_space=pl.ANY),
                  pl.BlockSpec(memory_space=pl.ANY),
                  pl.BlockSpec((1, D_MODEL, FF_TILE), lambda s, j, e, nb, nv: (e[s], 0, jj(j, nb, s))),
                  pl.BlockSpec((1, D_MODEL, FF_TILE), lambda s, j, e, nb, nv: (e[s], 0, N_FF_TILES + jj(j, nb, s))),
                  pl.BlockSpec((1, FF_TILE, D_MODEL), lambda s, j, e, nb, nv: (e[s], jj(j, nb, s), 0)),
                  pl.BlockSpec((1, 1, FF_TILE), lambda s, j, e, nb, nv: (e[s], 0, jj(j, nb, s))),
                  pl.BlockSpec((1, 1, FF_TILE), lambda s, j, e, nb, nv: (e[s], 0, N_FF_TILES + jj(j, nb, s))),
                  pl.BlockSpec((1, 1, D_MODEL), lambda s, j, e, nb, nv: (e[s], 0, 0))],
        out_specs=pl.BlockSpec(memory_space=pl.ANY),
        scratch_shapes=[pltpu.VMEM((SB_ROWS // SUBLANES, SUBLANES, D_MODEL), F32),
                        pltpu.VMEM((SB_ROWS, D_MODEL), BF16),
                        pltpu.VMEM((SB_ROWS // SUBLANES, SUBLANES, D_MODEL), F32),
                        pltpu.VMEM((D_MODEL, FF_TILE), BF16),
                        pltpu.VMEM((D_MODEL, FF_TILE), BF16),
                        pltpu.VMEM((FF_TILE, D_MODEL), BF16),
                        pltpu.SMEM((2 * IDS_STRIDE,), jnp.int32),
                        pltpu.SemaphoreType.DMA((2,)),
                        pltpu.SemaphoreType.DMA(()),
                        pltpu.SemaphoreType.DMA(())])
    y = pl.pallas_call(
        _expert_body, grid_spec=grid_spec,
        out_shape=jax.ShapeDtypeStruct((TOP_K * TOK // SUBLANES, SUBLANES, D_MODEL), F32),
        compiler_params=_params(2), name="experts")(
            sb_e, sb_nblk, sb_nvalid, tab, h1.reshape(TOK // SUBLANES, SUBLANES, D_MODEL), w_up, w_up, w_down,
            b_up.reshape(N_EXPERTS, 1, 2 * D_FF), b_up.reshape(N_EXPERTS, 1, 2 * D_FF),
            b_down.reshape(N_EXPERTS, 1, D_MODEL))
    return y.reshape(TOP_K * TOK, D_MODEL)


def _tail_body(y0_ref, y1_ref, y2_ref, y3_ref, gate_ref, h1_ref, p_ref, wg_ref, wp_ref,
               l2g_ref, l2b_ref, l3g_ref, l3b_ref, o_ref):
    gate = gate_ref[...]
    y = gate[:, 0:1] * y0_ref[...]
    for kk, y_ref in enumerate((y1_ref, y2_ref, y3_ref), start=1):
        y = y + gate[:, kk:kk + 1] * y_ref[...]
    h2 = _layer_norm(DN_ALPHA * h1_ref[...] + y, l2g_ref[...], l2b_ref[...])
    ple = _sigmoid(_dot(h2.astype(BF16), wg_ref[...])) * _dot(p_ref[...].astype(BF16), wp_ref[...])
    o_ref[...] = _layer_norm(DN_ALPHA * h2 + ple, l3g_ref[...], l3b_ref[...])


def _tail(y_slots, gates, h1, p2, wg, wp, l2g, l2b, l3g, l3b):
    tm = 256
    nt = TOK // tm
    const = lambda shape: pl.BlockSpec(shape, lambda i: (0,) * len(shape), pipeline_mode=pl.Buffered(1))
    row = lambda w: pl.BlockSpec((tm, w), lambda i: (i, 0))
    yspec = lambda kk: pl.BlockSpec((tm, D_MODEL), lambda i: (kk * nt + i, 0))
    vec = const((1, D_MODEL))
    return pl.pallas_call(
        _tail_body, grid=(nt,),
        in_specs=[yspec(0), yspec(1), yspec(2), yspec(3), row(128), row(D_MODEL), row(PLE_DIM),
                  const((D_MODEL, D_MODEL)), const((PLE_DIM, D_MODEL)), vec, vec, vec, vec],
        out_specs=row(D_MODEL),
        out_shape=jax.ShapeDtypeStruct((TOK, D_MODEL), F32),
        compiler_params=_params(1), name="tail")(y_slots, y_slots, y_slots, y_slots, gates, h1, p2, wg, wp,
                                                 l2g, l2b, l3g, l3b)


def _routing(top_idx):
    flat_e = top_idx.reshape(-1)
    experts = jnp.arange(N_EXPERTS, dtype=jnp.int32)
    onehot = (flat_e[:, None] == experts[None, :]).astype(jnp.int32)
    csum = jnp.cumsum(onehot, axis=0)
    rank = jnp.sum(onehot * csum, axis=1) - 1
    counts = csum[-1]
    padded = (counts + MOE_BLOCK - 1) // MOE_BLOCK * MOE_BLOCK
    padded_end = jnp.cumsum(padded)
    padded_start = padded_end - padded
    dest = (padded_start[flat_e] + rank).astype(jnp.int32)
    asg = jnp.arange(TOK * TOP_K, dtype=jnp.int32)
    asg_of_row = jnp.zeros((N_ROWS + SB_ROWS,), jnp.int32).at[dest].set(asg)
    tok = (asg_of_row // TOP_K).reshape(-1, MOE_BLOCK)
    dst = ((asg_of_row % TOP_K) * TOK + asg_of_row // TOP_K).reshape(-1, MOE_BLOCK)
    nb = padded // MOE_BLOCK
    n_sb = (nb + SB_BLOCKS - 1) // SB_BLOCKS
    sb_end = jnp.cumsum(n_sb)
    total = sb_end[-1]
    s = jnp.arange(N_SB, dtype=jnp.int32)
    s_eff = jnp.minimum(s, total - 1)
    e = jnp.minimum(jnp.sum((sb_end[None, :] <= s_eff[:, None]).astype(jnp.int32), axis=1), N_EXPERTS - 1)
    local = s_eff - (sb_end[e] - n_sb[e])
    valid = s < total
    sb_nblk = jnp.where(valid, jnp.clip(nb[e] - local * SB_BLOCKS, 0, SB_BLOCKS), 0).astype(jnp.int32)
    sb_blk = padded_start[e] // MOE_BLOCK + local * SB_BLOCKS
    sb_nvalid = jnp.where(valid, jnp.clip(counts[e] - local * SB_ROWS, 0, SB_ROWS), 0).astype(jnp.int32)
    blocks = sb_blk[:, None] + jnp.arange(SB_BLOCKS, dtype=jnp.int32)[None, :]
    tab = jnp.concatenate([tok[blocks].reshape(N_SB, SB_ROWS), dst[blocks].reshape(N_SB, SB_ROWS),
                           jnp.zeros((N_SB, IDS_STRIDE - 2 * SB_ROWS), jnp.int32)], axis=1).astype(jnp.int32).reshape(-1)
    return e.astype(jnp.int32), sb_nblk, sb_nvalid, tab


def _rope_tables():
    rows = SEQ // GRID_W
    row = jnp.repeat(jnp.arange(rows), GRID_W)
    col = jnp.tile(jnp.arange(GRID_W), rows)
    n_pairs = HEAD_DIM // 4
    inv_freq = ROPE_BASE ** (-jnp.arange(n_pairs, dtype=F32) / n_pairs)
    ang = jnp.concatenate([row[:, None] * inv_freq, col[:, None] * inv_freq], -1)
    cos_full = jnp.repeat(jnp.cos(ang), 2, axis=-1)
    sin = jnp.sin(ang)
    sin_signed = jnp.stack([-sin, sin], axis=-1).reshape(SEQ, HEAD_DIM)
    return cos_full, sin_signed


def kernel(x, p, in_ln_g, in_ln_b, w_in, q_norm, k_norm, w_lr_f, b_lr_f, w_lr_b, b_lr_b, gla_norm, w_br_a, w_br_b, w_o, ln1_g, ln1_b, w_router, b_router, w_up, b_up, w_down, b_down, ln2_g, ln2_b, w_ple_gate, w_ple_proj, ln3_g, ln3_b):
    assert x.shape == (BATCH, SEQ, D_MODEL) and w_in.shape[0] == DEPTH == 1
    win = w_in[0]
    hf, hb = _ln0(x.reshape(TOK, D_MODEL), in_ln_g, in_ln_b)

    zm = _inproj(hb, win, MAIN_W // 512, Z_TILE_OF_W_TILE, False, "inproj_main")
    zg = _inproj(hb, win[:, MAIN_W + LR_W:].astype(BF16), 2 * D_MODEL // 512, tuple(range(8)), True, "inproj_gates")

    w2 = jnp.zeros((LR_W, 2 * GLA_K), F32)
    w2 = w2.at[:GLA_RANK, :GLA_K].set(w_lr_f[0]).at[GLA_RANK:, GLA_K:].set(w_lr_b[0])
    b2 = jnp.concatenate([b_lr_f[0], b_lr_b[0]]).reshape(1, -1)
    bc_f, bc_b = _decay(hb, win[:, MAIN_W:MAIN_W + LR_W].astype(BF16), jnp.stack(_hi_lo(w2)), b2)

    cos_full, sin_signed = _rope_tables()
    qr, kr = _qkprep(zm, cos_full, sin_signed, q_norm[0], k_norm[0])
    attn = _attention(qr, kr, zm)
    o_f = _gla(zm, bc_f, False)
    o_b = _gla(zm, bc_b, True)

    wr = jnp.zeros((D_MODEL, 128), F32).at[:, :N_EXPERTS].set(w_router[0])
    br = jnp.full((1, 128), -jnp.inf, F32).at[0, :N_EXPERTS].set(b_router[0])
    vec = lambda v: v[0].reshape(1, -1)
    h1, idx_pad, gate_pad = _mixer(
        attn, o_f, o_b, zm, zg, hf, w_br_a[0].astype(BF16), w_br_b[0].astype(BF16), w_o[0].astype(BF16),
        vec(gla_norm), vec(ln1_g), vec(ln1_b), jnp.stack(_hi_lo(wr)), br)

    sb_e, sb_nblk, sb_nvalid, tab = _routing(idx_pad[:, :TOP_K])
    y_slots = _experts(sb_e, sb_nblk, sb_nvalid, tab, h1, w_up[0], b_up[0], w_down[0], b_down[0])

    out = _tail(y_slots, gate_pad, h1, p[0].reshape(TOK, PLE_DIM), w_ple_gate[0].astype(BF16),
                w_ple_proj[0].astype(BF16), vec(ln2_g), vec(ln2_b), vec(ln3_g), vec(ln3_b))
    return out.reshape(BATCH, SEQ, D_MODEL)
```

```python
import functools

import jax
import jax.numpy as jnp
from jax import lax
from jax.experimental import pallas as pl
from jax.experimental.pallas import tpu as pltpu

F32 = jnp.float32
BF16 = jnp.bfloat16

D_MODEL = 2048
BATCH = 2
SEQ = 4096
TOK = BATCH * SEQ
PLE_DIM = 256
GRID_W = 64
ATT_HEADS = 8
ATT_KV_HEADS = 2
ATT_GROUP = ATT_HEADS // ATT_KV_HEADS
HEAD_DIM = 128
ROPE_BASE = 10000.0
GLA_HEADS = 4
GLA_DK = 128
GLA_DV = 256
GLA_RANK = 16
GLA_TAU = 16.0
GLA_CHUNK = 64
GLA_SUB = 16
N_EXPERTS = 32
TOP_K = 4
D_FF = D_MODEL
SWIGLU_LIMIT = 7.0
SWIGLU_ALPHA = 1.702
MOE_BLOCK = 128
ATT_Q = ATT_HEADS * HEAD_DIM
ATT_KV = ATT_KV_HEADS * HEAD_DIM
GLA_K = GLA_HEADS * GLA_DK
GLA_V = GLA_HEADS * GLA_DV
MAIN_W = ATT_Q + 2 * ATT_KV + 2 * GLA_K + 2 * GLA_V
LR_W = 2 * GLA_RANK
DEPTH = 1
DN_ALPHA = (2 * DEPTH) ** 0.25
LN_EPS = 1e-5
RMS_EPS = 1e-6
LOG2_E = 1.4426950408889634

Z_TILE_OF_W_TILE = (0, 1, 6, 7, 8, 2, 3, 4, 5)
Z_Q, Z_VB, Z_OG, Z_KA, Z_VA, Z_QB, Z_KB = 0, 1024, 2048, 3072, 3328, 3584, 4096

VMEM_LIMIT = 56 * 1024 * 1024

N_ROWS = TOK * TOP_K + N_EXPERTS * MOE_BLOCK
N_BLOCKS = N_ROWS // MOE_BLOCK
SB_BLOCKS = 10
SB_ROWS = SB_BLOCKS * MOE_BLOCK
N_SB = (N_BLOCKS + (SB_BLOCKS - 1) * N_EXPERTS) // SB_BLOCKS
SUBLANES = 8
IDS_STRIDE = -(-2 * SB_ROWS // 1024) * 1024
FF_TILE = 256
N_FF_TILES = D_FF // FF_TILE


def _params(n_axes, vmem=None):
    return pltpu.CompilerParams(dimension_semantics=("arbitrary",) * n_axes,
                                vmem_limit_bytes=vmem or VMEM_LIMIT)


def _sigmoid(x):
    return 1.0 / (1.0 + jnp.exp(-x))


def _layer_norm(y, g, b):
    mu = jnp.mean(y, axis=-1, keepdims=True)
    yc = y - mu
    var = jnp.mean(yc * yc, axis=-1, keepdims=True)
    return yc * lax.rsqrt(var + LN_EPS) * g + b


def _dot(a, b):
    return jnp.dot(a, b, preferred_element_type=F32)


def _hi_lo(x):
    hi = x.astype(BF16)
    return hi, (x - hi.astype(F32)).astype(BF16)


def _dot_split(x, w_ref):
    hi, lo = _hi_lo(x)
    return _dot(hi, w_ref[0]) + _dot(hi, w_ref[1]) + _dot(lo, w_ref[0])


def _dot_nt(a, b):
    return lax.dot_general(a, b, (((1,), (1,)), ((), ())), preferred_element_type=F32)


def _dot_tn(a, b):
    return lax.dot_general(a, b, (((0,), (0,)), ((), ())), preferred_element_type=F32)


def _ln0_body(x_ref, g_ref, b_ref, hf_ref, hb_ref):
    y = _layer_norm(x_ref[...], g_ref[...], b_ref[...])
    hf_ref[...] = y
    hb_ref[...] = y.astype(BF16)


def _ln0(x2, g, b):
    tm = 256
    row = pl.BlockSpec((tm, D_MODEL), lambda i: (i, 0))
    vec = pl.BlockSpec((1, D_MODEL), lambda i: (0, 0))
    return pl.pallas_call(
        _ln0_body, grid=(TOK // tm,), in_specs=[row, vec, vec], out_specs=[row, row],
        out_shape=[jax.ShapeDtypeStruct((TOK, D_MODEL), F32), jax.ShapeDtypeStruct((TOK, D_MODEL), BF16)],
        compiler_params=_params(1), name="ln0")(x2, g.reshape(1, -1), b.reshape(1, -1))


def _inproj_body(perm_ref, a_ref, w_ref, o_ref, wb_ref, *, gate):
    del perm_ref

    @pl.when(pl.program_id(1) == 0)
    def _():
        wb_ref[...] = w_ref[...].astype(BF16)

    acc = _dot(a_ref[...], wb_ref[...])
    if gate:
        acc = _sigmoid(acc)
    o_ref[...] = acc.astype(o_ref.dtype)


def _inproj(hb, w, n_tiles, tile_perm, gate, name):
    tm, tn = 1024, 512
    grid_spec = pltpu.PrefetchScalarGridSpec(
        num_scalar_prefetch=1, grid=(n_tiles, TOK // tm),
        in_specs=[pl.BlockSpec((tm, D_MODEL), lambda n, m, p: (m, 0)),
                  pl.BlockSpec((D_MODEL, tn), lambda n, m, p: (0, n))],
        out_specs=pl.BlockSpec((tm, tn), lambda n, m, p: (m, p[n])),
        scratch_shapes=[pltpu.VMEM((D_MODEL, tn), BF16)])
    return pl.pallas_call(
        functools.partial(_inproj_body, gate=gate), grid_spec=grid_spec,
        out_shape=jax.ShapeDtypeStruct((TOK, n_tiles * tn), BF16),
        compiler_params=_params(2), name=name)(jnp.asarray(tile_perm, jnp.int32), hb, w)


def _wprep_body(a_ref, b_ref, g_ref, lr_ref):
    g_ref[...] = jnp.concatenate([a_ref[:, LR_W:], b_ref[:, :LR_W]], axis=1).astype(BF16)

    @pl.when(pl.program_id(0) == 0)
    def _():
        lane = lax.broadcasted_iota(jnp.int32, (D_MODEL, 128), 1)
        lr_ref[...] = jnp.where(lane < LR_W, a_ref[:, :128], 0.0).astype(BF16)


def _gate_weights(win):
    tn = 512
    first = MAIN_W // tn
    return pl.pallas_call(
        _wprep_body, grid=(2 * D_MODEL // tn,),
        in_specs=[pl.BlockSpec((D_MODEL, tn), lambda i: (0, first + i)),
                  pl.BlockSpec((D_MODEL, 128), lambda i: (0, (first + i + 1) * (tn // 128)))],
        out_specs=[pl.BlockSpec((D_MODEL, tn), lambda i: (0, i)), pl.BlockSpec((D_MODEL, 128), lambda i: (0, 0))],
        out_shape=[jax.ShapeDtypeStruct((D_MODEL, 2 * D_MODEL), BF16), jax.ShapeDtypeStruct((D_MODEL, 128), BF16)],
        compiler_params=_params(1), name="gate_w")(win, win)


def _decay_body(h_ref, wlr_ref, w2_ref, b2_ref, trif_ref, trib_ref, bf_ref, bb_ref):
    zlr = _dot(h_ref[...], wlr_ref[...])
    pre = _dot_split(zlr, w2_ref) + b2_ref[...]
    la = (jnp.minimum(pre, 0.0) - jnp.log1p(jnp.exp(-jnp.abs(pre)))) * (1.0 / GLA_TAU)
    hi = la.astype(BF16)
    lo = (la - hi.astype(F32)).astype(BF16)
    bf_ref[...] = _dot(trif_ref[...], hi[:, :GLA_K]) + _dot(trif_ref[...], lo[:, :GLA_K])
    bb_ref[...] = _dot(trib_ref[...], hi[:, GLA_K:]) + _dot(trib_ref[...], lo[:, GLA_K:])


def _decay(hb, wlr, w2, b2):
    tm = 512
    r = jnp.arange(tm)
    same = (r[:, None] // GLA_CHUNK) == (r[None, :] // GLA_CHUNK)
    trif = (same & (r[None, :] <= r[:, None])).astype(BF16)
    trib = (same & (r[None, :] >= r[:, None])).astype(BF16)
    full = lambda shape: pl.BlockSpec(shape, lambda i: (0,) * len(shape))
    out = pl.BlockSpec((tm, GLA_K), lambda i: (i, 0))
    return pl.pallas_call(
        _decay_body, grid=(TOK // tm,),
        in_specs=[pl.BlockSpec((tm, D_MODEL), lambda i: (i, 0)), full((D_MODEL, 128)),
                  full((2, 128, 2 * GLA_K)), full((1, 2 * GLA_K)), full((tm, tm)), full((tm, tm))],
        out_specs=[out, out],
        out_shape=[jax.ShapeDtypeStruct((TOK, GLA_K), F32)] * 2,
        compiler_params=_params(1), name="decay")(hb, wlr, w2, b2, trif, trib)


def _qkprep_body(q_ref, k_ref, cos_ref, sin_ref, qn_ref, kn_ref, qo_ref, ko_ref):
    cos = cos_ref[...]
    sin = sin_ref[...]
    lane = lax.broadcasted_iota(jnp.int32, cos.shape, 1)
    even = (lane % 2) == 0

    def one(x, gain, scale):
        x = x.astype(F32)
        x = x * lax.rsqrt(jnp.mean(x * x, axis=-1, keepdims=True) + RMS_EPS) * gain
        partner = jnp.where(even, pltpu.roll(x, HEAD_DIM - 1, 1), pltpu.roll(x, 1, 1))
        return ((x * cos + partner * sin) * scale).astype(BF16)

    for hh in range(ATT_HEADS):
        sl = slice(hh * HEAD_DIM, (hh + 1) * HEAD_DIM)
        qo_ref[:, sl] = one(q_ref[:, sl], qn_ref[...], LOG2_E * HEAD_DIM ** -0.5)
    for hh in range(ATT_KV_HEADS):
        sl = slice(hh * HEAD_DIM, (hh + 1) * HEAD_DIM)
        ko_ref[:, sl] = one(k_ref[:, sl], kn_ref[...], 1.0)


def _qkprep(zm, cos_full, sin_signed, q_norm, k_norm):
    tm = 256
    nrow = SEQ // tm
    tab = pl.BlockSpec((tm, HEAD_DIM), lambda i: (i % nrow, 0))
    vec = pl.BlockSpec((1, HEAD_DIM), lambda i: (0, 0))
    return pl.pallas_call(
        _qkprep_body, grid=(TOK // tm,),
        in_specs=[pl.BlockSpec((tm, ATT_Q), lambda i: (i, Z_Q // ATT_Q)),
                  pl.BlockSpec((tm, ATT_KV), lambda i: (i, Z_KA // ATT_KV)), tab, tab, vec, vec],
        out_specs=[pl.BlockSpec((tm, ATT_Q), lambda i: (i, 0)), pl.BlockSpec((tm, ATT_KV), lambda i: (i, 0))],
        out_shape=[jax.ShapeDtypeStruct((TOK, ATT_Q), BF16), jax.ShapeDtypeStruct((TOK, ATT_KV), BF16)],
        compiler_params=_params(1), name="qkprep")(zm, zm, cos_full, sin_signed,
                                                   q_norm.reshape(1, -1), k_norm.reshape(1, -1))


ATT_TQ = 256
ATT_TK = 512


ATT_SUB = 64


def _attn_body(q_ref, k_ref, v_ref, o_ref, qs_ref, s0_ref, s1_ref, p0_ref, p1_ref, a0_ref, a1_ref,
               acc_ref, m_ref, l_ref):
    rows = ATT_GROUP * ATT_TQ
    n_chunks = SEQ // ATT_TK
    for g in range(ATT_GROUP):
        qs_ref[g * ATT_TQ:(g + 1) * ATT_TQ, :] = q_ref[:, g * HEAD_DIM:(g + 1) * HEAD_DIM]
    m_ref[...] = jnp.full(m_ref.shape, -jnp.inf, F32)
    l_ref[...] = jnp.zeros(l_ref.shape, F32)
    acc_ref[...] = jnp.zeros(acc_ref.shape, F32)
    p1_ref[...] = jnp.zeros((rows, ATT_TK), BF16)
    a1_ref[...] = jnp.zeros(a1_ref.shape, F32)

    def chunk(c):
        return pl.ds(pl.multiple_of(c * ATT_TK, ATT_TK), ATT_TK)

    s0_ref[...] = _dot_nt(qs_ref[...], k_ref[0:ATT_TK, :])

    def step(c, s_cur, s_nxt, p_cur, p_prv, a_cur, a_prv):
        c_next = jnp.where(c + 1 < n_chunks, c + 1, 0)
        c_prev = jnp.where(c > 0, c - 1, 0)
        s_nxt[...] = _dot_nt(qs_ref[...], k_ref[chunk(c_next), :])
        acc_ref[...] = a_prv[...] * acc_ref[...] + _dot(p_prv[...], v_ref[chunk(c_prev), :])
        lane_blocks = [slice(cb * HEAD_DIM, (cb + 1) * HEAD_DIM) for cb in range(ATT_TK // HEAD_DIM)]
        m_all, l_all = m_ref[...], l_ref[...]
        sub_tiles = [slice(t * ATT_SUB, (t + 1) * ATT_SUB) for t in range(rows // ATT_SUB)]
        m_out = []
        for rr in sub_tiles:
            smax = functools.reduce(jnp.maximum, [s_cur[rr, cb] for cb in lane_blocks])
            m_out.append(jnp.maximum(m_all[rr], jnp.max(smax, axis=-1, keepdims=True)))
        l_out, a_out = [], []
        for rr, m_new in zip(sub_tiles, m_out):
            alpha = jnp.exp2(m_all[rr] - m_new)
            p = [jnp.exp2(s_cur[rr, cb] - m_new) for cb in lane_blocks]
            l_out.append(alpha * l_all[rr] + jnp.sum(functools.reduce(jnp.add, p), axis=-1, keepdims=True))
            a_out.append(alpha)
            for cb, pb in zip(lane_blocks, p):
                p_cur[rr, cb] = pb.astype(BF16)
        m_ref[...] = jnp.concatenate(m_out, axis=0)
        l_ref[...] = jnp.concatenate(l_out, axis=0)
        a_cur[...] = jnp.concatenate(a_out, axis=0)

    def pair(i, carry):
        step(2 * i, s0_ref, s1_ref, p0_ref, p1_ref, a0_ref, a1_ref)
        step(2 * i + 1, s1_ref, s0_ref, p1_ref, p0_ref, a1_ref, a0_ref)
        return carry

    lax.fori_loop(0, n_chunks // 2, pair, 0)
    acc = a1_ref[...] * acc_ref[...] + _dot(p1_ref[...], v_ref[SEQ - ATT_TK:SEQ, :])
    out = (acc / l_ref[...]).astype(BF16)
    for g in range(ATT_GROUP):
        o_ref[:, g * HEAD_DIM:(g + 1) * HEAD_DIM] = out[g * ATT_TQ:(g + 1) * ATT_TQ]


def _attention(qr, kr, zm):
    nq = SEQ // ATT_TQ
    gw = ATT_GROUP * HEAD_DIM
    rows = ATT_GROUP * ATT_TQ
    qspec = pl.BlockSpec((ATT_TQ, gw), lambda b, j, i: (b * nq + i, j))
    return pl.pallas_call(
        _attn_body, grid=(BATCH, ATT_KV_HEADS, nq),
        in_specs=[qspec,
                  pl.BlockSpec((SEQ, HEAD_DIM), lambda b, j, i: (b, j)),
                  pl.BlockSpec((SEQ, HEAD_DIM), lambda b, j, i: (b, Z_VA // HEAD_DIM + j))],
        out_specs=qspec,
        out_shape=jax.ShapeDtypeStruct((TOK, ATT_Q), BF16),
        scratch_shapes=[pltpu.VMEM((rows, HEAD_DIM), BF16),
                        pltpu.VMEM((rows, ATT_TK), F32), pltpu.VMEM((rows, ATT_TK), F32),
                        pltpu.VMEM((rows, ATT_TK), BF16), pltpu.VMEM((rows, ATT_TK), BF16),
                        pltpu.VMEM((rows, HEAD_DIM), F32), pltpu.VMEM((rows, HEAD_DIM), F32),
                        pltpu.VMEM((rows, HEAD_DIM), F32), pltpu.VMEM((rows, HEAD_DIM), F32),
                        pltpu.VMEM((rows, HEAD_DIM), F32)],
        compiler_params=_params(3), name="attn")(qr, kr, zm)


GLA_CB = 4
GLA_RB = GLA_CB * GLA_CHUNK
N_SUB = GLA_CHUNK // GLA_SUB


def _gla_body(q_ref, k_ref, v_ref, bc_ref, o_ref, st_ref, kf_ref, bs_ref, *, rev):
    @pl.when(pl.program_id(1) == 0)
    def _():
        st_ref[...] = jnp.zeros_like(st_ref)

    C, SUB = GLA_CHUNK, GLA_SUB
    rowc = lax.broadcasted_iota(jnp.int32, (C, GLA_DK), 0)
    rows_s = lax.broadcasted_iota(jnp.int32, (SUB, 128), 0)
    lane_s = lax.broadcasted_iota(jnp.int32, (SUB, 128), 1)

    def chunk(ci, carry):
        c = (GLA_CB - 1 - ci) if rev else ci
        r0 = pl.multiple_of(c * C, C)
        for hh in range(GLA_HEADS):
            ksl = slice(hh * GLA_DK, (hh + 1) * GLA_DK)
            vsl = slice(hh * GLA_DV, (hh + 1) * GLA_DV)
            q = q_ref[pl.ds(r0, C), ksl].astype(F32) * (GLA_DK ** -0.5)
            k = k_ref[pl.ds(r0, C), ksl].astype(F32)
            v = v_ref[pl.ds(r0, C), vsl]
            bc = bc_ref[pl.ds(r0, C), ksl]
            kf_ref[hh] = k
            bs_ref[hh] = bc
            st = st_ref[hh]
            blast = bc[0:1] if rev else bc[C - 1:C]
            o_inter = _dot_nt((q * jnp.exp(bc)).astype(BF16), st.astype(BF16))
            kdec = k * jnp.exp(blast - bc)
            st_ref[hh] = st * jnp.exp(blast) + _dot_tn(v, kdec.astype(BF16))

            a_rows = []
            for si in range(N_SUB):
                lo, hi = si * SUB, (si + 1) * SUB
                q_s, b_s = q[lo:hi], bc[lo:hi]
                has_earlier = (si < N_SUB - 1) if rev else (si > 0)
                if has_earlier:
                    ref_row = bc[hi:hi + 1] if rev else bc[lo - 1:lo]
                    earlier = (rowc >= hi) if rev else (rowc < lo)
                    qt = q_s * jnp.exp(b_s - ref_row)
                    kt = k * jnp.exp(jnp.where(earlier, ref_row - bc, -jnp.inf))
                    a = _dot_nt(qt.astype(BF16), kt.astype(BF16))
                else:
                    a = jnp.zeros((SUB, C), F32)
                diag = jnp.zeros((SUB, 128), F32)
                for jl in range(SUB):
                    j = lo + jl
                    d = jnp.minimum(b_s - bs_ref[hh, j:j + 1, :], 0.0)
                    col = jnp.sum(q_s * kf_ref[hh, j:j + 1, :] * jnp.exp(d), axis=-1, keepdims=True)
                    diag = jnp.where(lane_s == j, col, diag)
                keep = (lane_s >= rows_s + lo) if rev else (lane_s <= rows_s + lo)
                diag = jnp.where(keep & (lane_s >= lo) & (lane_s < hi), diag, 0.0)
                a_rows.append(a + diag[:, :C])
            a_full = jnp.concatenate(a_rows, axis=0)
            o_ref[pl.ds(r0, C), vsl] = o_inter + _dot(a_full.astype(BF16), v)
        return carry

    lax.fori_loop(0, GLA_CB, chunk, 0)


def _gla(zm, bcum, rev):
    ncb = SEQ // GLA_RB
    if rev:
        row = lambda b, c: b * ncb + (ncb - 1 - c)
    else:
        row = lambda b, c: b * ncb + c
    return pl.pallas_call(
        functools.partial(_gla_body, rev=rev), grid=(BATCH, ncb),
        in_specs=[pl.BlockSpec((GLA_RB, GLA_K), lambda b, c: (row(b, c), Z_QB // GLA_K)),
                  pl.BlockSpec((GLA_RB, GLA_K), lambda b, c: (row(b, c), Z_KB // GLA_K)),
                  pl.BlockSpec((GLA_RB, GLA_V), lambda b, c: (row(b, c), Z_VB // GLA_V)),
                  pl.BlockSpec((GLA_RB, GLA_K), lambda b, c: (row(b, c), 0))],
        out_specs=pl.BlockSpec((GLA_RB, GLA_V), lambda b, c: (row(b, c), 0)),
        out_shape=jax.ShapeDtypeStruct((TOK, GLA_V), F32),
        scratch_shapes=[pltpu.VMEM((GLA_HEADS, GLA_DV, GLA_DK), F32),
                        pltpu.VMEM((GLA_HEADS, GLA_CHUNK, GLA_DK), F32),
                        pltpu.VMEM((GLA_HEADS, GLA_CHUNK, GLA_DK), F32)],
        compiler_params=_params(2), name="gla_bwd" if rev else "gla_fwd")(zm, zm, zm, bcum)


MIX_TM = 512


def _onorm_body(of_ref, ob_ref, og_ref, gn_ref, o_ref):
    gn = gn_ref[...]
    for hh in range(GLA_HEADS):
        sl = slice(hh * GLA_DV, (hh + 1) * GLA_DV)
        x = of_ref[:, sl] + ob_ref[:, sl]
        g = og_ref[:, sl].astype(F32)
        xn = x * lax.rsqrt(jnp.mean(x * x, axis=-1, keepdims=True) + RMS_EPS) * gn
        o_ref[:, sl] = (xn * (g * _sigmoid(g))).astype(BF16)


def _merge_body(attn_ref, on_ref, ga_ref, gb_ref, wa_ref, wb_ref, m_ref):
    ya = _dot(attn_ref[...], wa_ref[...])
    yb = _dot(on_ref[...], wb_ref[...])
    m_ref[...] = (ga_ref[...].astype(F32) * ya + gb_ref[...].astype(F32) * yb).astype(BF16)


def _outproj_body(m_ref, h_ref, wo_ref, lg_ref, lb_ref, wr_ref, br_ref, h1_ref, idx_ref, gate_ref):
    mix = _dot(m_ref[...], wo_ref[...])
    h1 = _layer_norm(DN_ALPHA * h_ref[...] + mix, lg_ref[...], lb_ref[...])
    h1_ref[...] = h1

    logits = _dot_split(h1, wr_ref) + br_ref[...]
    lane = lax.broadcasted_iota(jnp.int32, logits.shape, 1)
    x = logits
    vals, idxs = [], []
    for _ in range(TOP_K):
        mx = jnp.max(x, axis=-1, keepdims=True)
        ix = jnp.min(jnp.where(x == mx, lane, 128), axis=-1, keepdims=True)
        vals.append(mx)
        idxs.append(ix)
        x = jnp.where(lane == ix, -jnp.inf, x)
    es = [jnp.exp(vv - vals[0]) for vv in vals]
    den = es[0] + es[1] + es[2] + es[3]
    idx_out = jnp.zeros(logits.shape, jnp.int32)
    gate_out = jnp.zeros(logits.shape, F32)
    for kk in range(TOP_K):
        idx_out = jnp.where(lane == kk, idxs[kk], idx_out)
        gate_out = jnp.where(lane == kk, es[kk] / den, gate_out)
    idx_ref[...] = idx_out
    gate_ref[...] = gate_out


def _mixer(attn, o_f, o_b, zm, zg, hf, wa, wb, wo, gn, lg, lb, wr, br):
    tm = MIX_TM
    const = lambda shape: pl.BlockSpec(shape, lambda i: (0,) * len(shape), pipeline_mode=pl.Buffered(1))
    row = lambda w, cb=0: pl.BlockSpec((tm, w), lambda i: (i, cb))
    grid = (TOK // tm,)
    onorm = pl.pallas_call(
        _onorm_body, grid=grid,
        in_specs=[row(GLA_V), row(GLA_V), row(GLA_V, Z_OG // GLA_V), const((1, GLA_DV))],
        out_specs=row(GLA_V), out_shape=jax.ShapeDtypeStruct((TOK, GLA_V), BF16),
        compiler_params=_params(1), name="onorm")(o_f, o_b, zm, gn)
    merged = pl.pallas_call(
        _merge_body, grid=grid,
        in_specs=[row(ATT_Q), row(GLA_V), row(D_MODEL, 0), row(D_MODEL, 1),
                  const((ATT_Q, D_MODEL)), const((GLA_V, D_MODEL))],
        out_specs=row(D_MODEL), out_shape=jax.ShapeDtypeStruct((TOK, D_MODEL), BF16),
        compiler_params=_params(1), name="merge")(attn, onorm, zg, zg, wa, wb)
    return pl.pallas_call(
        _outproj_body, grid=grid,
        in_specs=[row(D_MODEL), row(D_MODEL), const((D_MODEL, D_MODEL)), const((1, D_MODEL)), const((1, D_MODEL)),
                  const((2, D_MODEL, 128)), const((1, 128))],
        out_specs=[row(D_MODEL), row(128), row(128)],
        out_shape=[jax.ShapeDtypeStruct((TOK, D_MODEL), F32),
                   jax.ShapeDtypeStruct((TOK, 128), jnp.int32),
                   jax.ShapeDtypeStruct((TOK, 128), F32)],
        compiler_params=_params(1), name="outproj")(merged, hf, wo, lg, lb, wr, br)


GROUP_SIZES = (512, 256, 128)
GATHER_STEPS = SB_BLOCKS // (SB_ROWS // GROUP_SIZES[0])


def _expert_body(e_ref, nb_ref, nv_ref, tab_hbm, h1_hbm, wg_ref, wl_ref, wd_ref, bg_ref, bl_ref, bd_ref,
                 y_hbm, gbuf, xb_ref, acc_ref, wgb_ref, wlb_ref, wdb_ref, ids, sem_ids, sem_g, sem_s):
    del e_ref
    s = pl.program_id(0)
    j = pl.program_id(1)
    nblk = nb_ref[s]
    slot = s & 1
    has_next = s + 1 < pl.num_programs(0)
    nxt = jnp.minimum(s + 1, N_SB - 1)
    nblk_next = jnp.where(has_next, nb_ref[nxt], 0)
    tiles = MOE_BLOCK // SUBLANES

    def ids_copy(sb, sl):
        return pltpu.make_async_copy(tab_hbm.at[pl.ds(pl.multiple_of(sb * IDS_STRIDE, IDS_STRIDE), IDS_STRIDE)],
                                     ids.at[pl.ds(pl.multiple_of(sl * IDS_STRIDE, IDS_STRIDE), IDS_STRIDE)],
                                     sem_ids.at[sl])

    def gather_block(sl, b):
        id0 = sl * IDS_STRIDE + b * MOE_BLOCK

        def issue(t, c):
            for u in range(SUBLANES):
                tok = ids[id0 + t * SUBLANES + u]
                pltpu.make_async_copy(h1_hbm.at[pl.ds(tok >> 3, 1), pl.ds(tok & 7, 1), :],
                                      gbuf.at[pl.ds(b * tiles + t, 1), pl.ds(u, 1), :], sem_g).start()
            return c

        lax.fori_loop(0, tiles, issue, 0)

    def gather_wait_block(b):
        pltpu.make_async_copy(h1_hbm.at[pl.ds(0, tiles)], gbuf.at[pl.ds(b * tiles, tiles)], sem_g).wait()

    def loop_blocks(n, fn):
        def body(b, c):
            fn(b)
            return c

        lax.fori_loop(0, n, body, 0)

    @pl.when((s == 0) & (j == 0))
    def _():
        ids_copy(0, 0).start()
        ids_copy(0, 0).wait()
        loop_blocks(nblk, lambda b: gather_block(0, b))

    big = GROUP_SIZES[0]
    big_blocks = big // MOE_BLOCK
    n_big = nblk // big_blocks
    in_window = (j >= 1) & (j <= GATHER_STEPS) & has_next
    issued_here = GATHER_STEPS * n_big
    issued_prev = jnp.where(s > 0, GATHER_STEPS * (nb_ref[jnp.maximum(s - 1, 0)] // big_blocks), 0)

    @pl.when(j == 0)
    def _():
        loop_blocks(jnp.maximum(issued_prev, nblk), gather_wait_block)

    @pl.when(nblk > 0)
    def _():
        @pl.when(j == 0)
        def _():
            @pl.when(has_next)
            def _():
                ids_copy(nxt, 1 - slot).start()

            def take_block(b):
                x = gbuf[pl.ds(b * tiles, tiles)].reshape(MOE_BLOCK, D_MODEL)
                xb_ref[pl.ds(pl.multiple_of(b * MOE_BLOCK, MOE_BLOCK), MOE_BLOCK), :] = x.astype(BF16)
                acc_ref[pl.ds(b * tiles, tiles)] = jnp.broadcast_to(bd_ref[0], (tiles, SUBLANES, D_MODEL))

            loop_blocks(nblk, take_block)

        @pl.when((j == 1) & has_next)
        def _():
            ids_copy(nxt, 1 - slot).wait()

        @pl.when(j == GATHER_STEPS + 1)
        def _():
            for b in range(SB_BLOCKS):
                @pl.when((b >= issued_here) & (b < nblk_next))
                def _(b=b):
                    gather_block(1 - slot, b)

        wgb_ref[...] = wg_ref[0].astype(BF16)
        wlb_ref[...] = wl_ref[0].astype(BF16)
        wdb_ref[...] = wd_ref[0].astype(BF16)

        def group(r0, size):
            x = xb_ref[pl.ds(pl.multiple_of(r0, MOE_BLOCK), size), :]
            g = jnp.minimum(_dot(x, wgb_ref[...]) + bg_ref[0], SWIGLU_LIMIT)
            lin = jnp.clip(_dot(x, wlb_ref[...]) + bl_ref[0], -SWIGLU_LIMIT, SWIGLU_LIMIT)
            act = g * _sigmoid(SWIGLU_ALPHA * g) * (lin + 1.0)
            upd = _dot(act.astype(BF16), wdb_ref[...])
            acc_ref[pl.ds(r0 // SUBLANES, size // SUBLANES)] += upd.reshape(size // SUBLANES, SUBLANES, D_MODEL)

        def big_group(i, c):
            group(i * big, big)
            return c

        def big_group_and_gather(i, c):
            group(i * big, big)
            b = (j - 1) * n_big + i
            id0 = (1 - slot) * IDS_STRIDE + b * MOE_BLOCK
            for t in range(tiles):
                for u in range(SUBLANES):
                    tok = ids[id0 + t * SUBLANES + u]
                    pltpu.make_async_copy(h1_hbm.at[pl.ds(tok >> 3, 1), pl.ds(tok & 7, 1), :],
                                          gbuf.at[pl.ds(b * tiles + t, 1), pl.ds(u, 1), :], sem_g).start()
            return c

        @pl.when(in_window)
        def _():
            lax.fori_loop(0, n_big, big_group_and_gather, 0)

        @pl.when(jnp.logical_not(in_window))
        def _():
            lax.fori_loop(0, n_big, big_group, 0)

        done = n_big * big_blocks
        for size in GROUP_SIZES[1:]:
            take = ((nblk - done) // (size // MOE_BLOCK)) > 0

            @pl.when(take)
            def _(done=done, size=size):
                group(done * MOE_BLOCK, size)

            done = done + jnp.where(take, size // MOE_BLOCK, 0)

        @pl.when(j == N_FF_TILES - 1)
        def _():
            nvalid = nv_ref[s]

            dst0 = slot * IDS_STRIDE + SB_ROWS

            def row_copy(t, u, dst):
                return pltpu.make_async_copy(acc_ref.at[pl.ds(t, 1), pl.ds(u, 1), :],
                                             y_hbm.at[pl.ds(dst >> 3, 1), pl.ds(dst & 7, 1), :], sem_s)

            def issue_tile(t, c):
                for u in range(SUBLANES):
                    row_copy(t, u, ids[dst0 + t * SUBLANES + u]).start()
                return c

            def issue_row(r, c):
                row_copy(r >> 3, r & 7, ids[dst0 + r]).start()
                return c

            full_tiles = nvalid // SUBLANES
            lax.fori_loop(0, full_tiles, issue_tile, 0)
            lax.fori_loop(full_tiles * SUBLANES, nvalid, issue_row, 0)

            def wait_block(b, c):
                pltpu.make_async_copy(acc_ref.at[pl.ds(0, tiles)], y_hbm.at[pl.ds(0, tiles)], sem_s).wait()
                return c

            def wait_row(r, c):
                row_copy(0, 0, 0).wait()
                return c

            nfull = nvalid // MOE_BLOCK
            lax.fori_loop(0, nfull, wait_block, 0)
            lax.fori_loop(nfull * MOE_BLOCK, nvalid, wait_row, 0)


def _experts(n_sb, sb_e, sb_nblk, sb_nvalid, tab, h1, w_up, b_up, w_down, b_down):
    def jj(j, nb, s):
        return jnp.where(nb[s] > 0, j, N_FF_TILES - 1)

    grid_spec = pltpu.PrefetchScalarGridSpec(
        num_scalar_prefetch=3, grid=(n_sb, N_FF_TILES),
        in_specs=[pl.BlockSpec(memory_space=pl.ANY),
                  pl.BlockSpec(memory_space=pl.ANY),
                  pl.BlockSpec((1, D_MODEL, FF_TILE), lambda s, j, e, nb, nv: (e[s], 0, jj(j, nb, s))),
                  pl.BlockSpec((1, D_MODEL, FF_TILE), lambda s, j, e, nb, nv: (e[s], 0, N_FF_TILES + jj(j, nb, s))),
                  pl.BlockSpec((1, FF_TILE, D_MODEL), lambda s, j, e, nb, nv: (e[s], jj(j, nb, s), 0)),
                  pl.BlockSpec((1, 1, FF_TILE), lambda s, j, e, nb, nv: (e[s], 0, jj(j, nb, s))),
                  pl.BlockSpec((1, 1, FF_TILE), lambda s, j, e, nb, nv: (e[s], 0, N_FF_TILES + jj(j, nb, s))),
                  pl.BlockSpec((1, 1, D_MODEL), lambda s, j, e, nb, nv: (e[s], 0, 0))],
        out_specs=pl.BlockSpec(memory_space=pl.ANY),
        scratch_shapes=[pltpu.VMEM((SB_ROWS // SUBLANES, SUBLANES, D_MODEL), F32),
                        pltpu.VMEM((SB_ROWS, D_MODEL), BF16),
                        pltpu.VMEM((SB_ROWS // SUBLANES, SUBLANES, D_MODEL), F32),
                        pltpu.VMEM((D_MODEL, FF_TILE), BF16),
                        pltpu.VMEM((D_MODEL, FF_TILE), BF16),
                        pltpu.VMEM((FF_TILE, D_MODEL), BF16),
                        pltpu.SMEM((2 * IDS_STRIDE,), jnp.int32),
                        pltpu.SemaphoreType.DMA((2,)),
                        pltpu.SemaphoreType.DMA(()),
                        pltpu.SemaphoreType.DMA(())])
    y = pl.pallas_call(
        _expert_body, grid_spec=grid_spec,
        out_shape=jax.ShapeDtypeStruct((TOP_K * TOK // SUBLANES, SUBLANES, D_MODEL), F32),
        compiler_params=_params(2), name="experts")(
            sb_e, sb_nblk, sb_nvalid, tab, h1.reshape(TOK // SUBLANES, SUBLANES, D_MODEL), w_up, w_up, w_down,
            b_up.reshape(N_EXPERTS, 1, 2 * D_FF), b_up.reshape(N_EXPERTS, 1, 2 * D_FF),
            b_down.reshape(N_EXPERTS, 1, D_MODEL))
    return y.reshape(TOP_K * TOK, D_MODEL)


def _tail_body(y0_ref, y1_ref, y2_ref, y3_ref, gate_ref, h1_ref, p_ref, wg_ref, wp_ref,
               l2g_ref, l2b_ref, l3g_ref, l3b_ref, o_ref):
    gate = gate_ref[...]
    y = gate[:, 0:1] * y0_ref[...]
    for kk, y_ref in enumerate((y1_ref, y2_ref, y3_ref), start=1):
        y = y + gate[:, kk:kk + 1] * y_ref[...]
    h2 = _layer_norm(DN_ALPHA * h1_ref[...] + y, l2g_ref[...], l2b_ref[...])
    ple = _sigmoid(_dot(h2.astype(BF16), wg_ref[...])) * _dot(p_ref[...].astype(BF16), wp_ref[...])
    o_ref[...] = _layer_norm(DN_ALPHA * h2 + ple, l3g_ref[...], l3b_ref[...])


def _tail(y_slots, gates, h1, p2, wg, wp, l2g, l2b, l3g, l3b):
    tm = 256
    nt = TOK // tm
    const = lambda shape: pl.BlockSpec(shape, lambda i: (0,) * len(shape), pipeline_mode=pl.Buffered(1))
    row = lambda w: pl.BlockSpec((tm, w), lambda i: (i, 0))
    yspec = lambda kk: pl.BlockSpec((tm, D_MODEL), lambda i: (kk * nt + i, 0))
    vec = const((1, D_MODEL))
    return pl.pallas_call(
        _tail_body, grid=(nt,),
        in_specs=[yspec(0), yspec(1), yspec(2), yspec(3), row(128), row(D_MODEL), row(PLE_DIM),
                  const((D_MODEL, D_MODEL)), const((PLE_DIM, D_MODEL)), vec, vec, vec, vec],
        out_specs=row(D_MODEL),
        out_shape=jax.ShapeDtypeStruct((TOK, D_MODEL), F32),
        compiler_params=_params(1), name="tail")(y_slots, y_slots, y_slots, y_slots, gates, h1, p2, wg, wp,
                                                 l2g, l2b, l3g, l3b)


def _routing(top_idx):
    flat_e = top_idx.reshape(-1)
    experts = jnp.arange(N_EXPERTS, dtype=jnp.int32)
    onehot = (flat_e[:, None] == experts[None, :]).astype(jnp.int32)
    csum = jnp.cumsum(onehot, axis=0)
    rank = jnp.sum(onehot * csum, axis=1) - 1
    counts = csum[-1]
    padded = (counts + MOE_BLOCK - 1) // MOE_BLOCK * MOE_BLOCK
    padded_end = jnp.cumsum(padded)
    padded_start = padded_end - padded
    dest = (padded_start[flat_e] + rank).astype(jnp.int32)
    asg = jnp.arange(TOK * TOP_K, dtype=jnp.int32)
    asg_of_row = jnp.zeros((N_ROWS + SB_ROWS,), jnp.int32).at[dest].set(asg)
    tok = (asg_of_row // TOP_K).reshape(-1, MOE_BLOCK)
    dst = ((asg_of_row % TOP_K) * TOK + asg_of_row // TOP_K).reshape(-1, MOE_BLOCK)
    nb = padded // MOE_BLOCK
    n_sb = (nb + SB_BLOCKS - 1) // SB_BLOCKS
    sb_end = jnp.cumsum(n_sb)
    total = sb_end[-1]
    s = jnp.arange(N_SB, dtype=jnp.int32)
    s_eff = jnp.minimum(s, total - 1)
    e = jnp.minimum(jnp.sum((sb_end[None, :] <= s_eff[:, None]).astype(jnp.int32), axis=1), N_EXPERTS - 1)
    local = s_eff - (sb_end[e] - n_sb[e])
    valid = s < total
    sb_nblk = jnp.where(valid, jnp.clip(nb[e] - local * SB_BLOCKS, 0, SB_BLOCKS), 0).astype(jnp.int32)
    sb_blk = padded_start[e] // MOE_BLOCK + local * SB_BLOCKS
    sb_nvalid = jnp.where(valid, jnp.clip(counts[e] - local * SB_ROWS, 0, SB_ROWS), 0).astype(jnp.int32)
    blocks = sb_blk[:, None] + jnp.arange(SB_BLOCKS, dtype=jnp.int32)[None, :]
    tab = jnp.concatenate([tok[blocks].reshape(N_SB, SB_ROWS), dst[blocks].reshape(N_SB, SB_ROWS),
                           jnp.zeros((N_SB, IDS_STRIDE - 2 * SB_ROWS), jnp.int32)], axis=1).astype(jnp.int32).reshape(-1)
    return total.astype(jnp.int32), e.astype(jnp.int32), sb_nblk, sb_nvalid, tab


def _rope_tables():
    rows = SEQ // GRID_W
    row = jnp.repeat(jnp.arange(rows), GRID_W)
    col = jnp.tile(jnp.arange(GRID_W), rows)
    n_pairs = HEAD_DIM // 4
    inv_freq = ROPE_BASE ** (-jnp.arange(n_pairs, dtype=F32) / n_pairs)
    ang = jnp.concatenate([row[:, None] * inv_freq, col[:, None] * inv_freq], -1)
    cos_full = jnp.repeat(jnp.cos(ang), 2, axis=-1)
    sin = jnp.sin(ang)
    sin_signed = jnp.stack([-sin, sin], axis=-1).reshape(SEQ, HEAD_DIM)
    return cos_full, sin_signed


def kernel(x, p, in_ln_g, in_ln_b, w_in, q_norm, k_norm, w_lr_f, b_lr_f, w_lr_b, b_lr_b, gla_norm, w_br_a, w_br_b, w_o, ln1_g, ln1_b, w_router, b_router, w_up, b_up, w_down, b_down, ln2_g, ln2_b, w_ple_gate, w_ple_proj, ln3_g, ln3_b):
    assert x.shape == (BATCH, SEQ, D_MODEL) and w_in.shape[0] == DEPTH == 1
    win = w_in[0]
    hf, hb = _ln0(x.reshape(TOK, D_MODEL), in_ln_g, in_ln_b)

    zm = _inproj(hb, win, MAIN_W // 512, Z_TILE_OF_W_TILE, False, "inproj_main")
    w_gate, w_lr = _gate_weights(win)
    zg = _inproj(hb, w_gate, 2 * D_MODEL // 512, tuple(range(8)), True, "inproj_gates")

    w2 = jnp.zeros((128, 2 * GLA_K), F32)
    w2 = w2.at[:GLA_RANK, :GLA_K].set(w_lr_f[0]).at[GLA_RANK:LR_W, GLA_K:].set(w_lr_b[0])
    b2 = jnp.concatenate([b_lr_f[0], b_lr_b[0]]).reshape(1, -1)
    bc_f, bc_b = _decay(hb, w_lr, jnp.stack(_hi_lo(w2)), b2)

    cos_full, sin_signed = _rope_tables()
    qr, kr = _qkprep(zm, cos_full, sin_signed, q_norm[0], k_norm[0])
    attn = _attention(qr, kr, zm)
    o_f = _gla(zm, bc_f, False)
    o_b = _gla(zm, bc_b, True)

    wr = jnp.zeros((D_MODEL, 128), F32).at[:, :N_EXPERTS].set(w_router[0])
    br = jnp.full((1, 128), -jnp.inf, F32).at[0, :N_EXPERTS].set(b_router[0])
    vec = lambda v: v[0].reshape(1, -1)
    h1, idx_pad, gate_pad = _mixer(
        attn, o_f, o_b, zm, zg, hf, w_br_a[0].astype(BF16), w_br_b[0].astype(BF16), w_o[0].astype(BF16),
        vec(gla_norm), vec(ln1_g), vec(ln1_b), jnp.stack(_hi_lo(wr)), br)

    n_sb, sb_e, sb_nblk, sb_nvalid, tab = _routing(idx_pad[:, :TOP_K])
    y_slots = _experts(n_sb, sb_e, sb_nblk, sb_nvalid, tab, h1, w_up[0], b_up[0], w_down[0], b_down[0])

    out = _tail(y_slots, gate_pad, h1, p[0].reshape(TOK, PLE_DIM), w_ple_gate[0].astype(BF16),
                w_ple_proj[0].astype(BF16), vec(ln2_g), vec(ln2_b), vec(ln3_g), vec(ln3_b))
    return out.reshape(BATCH, SEQ, D_MODEL)
```

```python
import functools

import jax
import jax.numpy as jnp
from jax import lax
from jax.experimental import pallas as pl
from jax.experimental.pallas import tpu as pltpu

F32 = jnp.float32
BF16 = jnp.bfloat16

D_MODEL = 2048
BATCH = 2
SEQ = 4096
TOK = BATCH * SEQ
PLE_DIM = 256
GRID_W = 64
ATT_HEADS = 8
ATT_KV_HEADS = 2
ATT_GROUP = ATT_HEADS // ATT_KV_HEADS
HEAD_DIM = 128
ROPE_BASE = 10000.0
GLA_HEADS = 4
GLA_DK = 128
GLA_DV = 256
GLA_RANK = 16
GLA_TAU = 16.0
GLA_CHUNK = 64
GLA_SUB = 16
N_EXPERTS = 32
TOP_K = 4
D_FF = D_MODEL
SWIGLU_LIMIT = 7.0
SWIGLU_ALPHA = 1.702
MOE_BLOCK = 128
ATT_Q = ATT_HEADS * HEAD_DIM
ATT_KV = ATT_KV_HEADS * HEAD_DIM
GLA_K = GLA_HEADS * GLA_DK
GLA_V = GLA_HEADS * GLA_DV
MAIN_W = ATT_Q + 2 * ATT_KV + 2 * GLA_K + 2 * GLA_V
LR_W = 2 * GLA_RANK
DEPTH = 1
DN_ALPHA = (2 * DEPTH) ** 0.25
LN_EPS = 1e-5
RMS_EPS = 1e-6
LOG2_E = 1.4426950408889634

Z_TILE_OF_W_TILE = (0, 1, 6, 7, 8, 2, 3, 4, 5)
Z_Q, Z_VB, Z_OG, Z_KA, Z_VA, Z_QB, Z_KB = 0, 1024, 2048, 3072, 3328, 3584, 4096

VMEM_LIMIT = 56 * 1024 * 1024

N_ROWS = TOK * TOP_K + N_EXPERTS * MOE_BLOCK
N_BLOCKS = N_ROWS // MOE_BLOCK
SB_BLOCKS = 10
SB_ROWS = SB_BLOCKS * MOE_BLOCK
N_SB = (N_BLOCKS + (SB_BLOCKS - 1) * N_EXPERTS) // SB_BLOCKS
SUBLANES = 8
IDS_STRIDE = -(-2 * SB_ROWS // 1024) * 1024
FF_TILE = 256
N_FF_TILES = D_FF // FF_TILE


def _params(n_axes, vmem=None):
    return pltpu.CompilerParams(dimension_semantics=("arbitrary",) * n_axes,
                                vmem_limit_bytes=vmem or VMEM_LIMIT)


def _sigmoid(x):
    return 1.0 / (1.0 + jnp.exp(-x))


def _layer_norm(y, g, b):
    mu = jnp.mean(y, axis=-1, keepdims=True)
    yc = y - mu
    var = jnp.mean(yc * yc, axis=-1, keepdims=True)
    return yc * lax.rsqrt(var + LN_EPS) * g + b


def _dot(a, b):
    return jnp.dot(a, b, preferred_element_type=F32)


def _hi_lo(x):
    hi = x.astype(BF16)
    return hi, (x - hi.astype(F32)).astype(BF16)


def _dot_split(x, w_ref):
    hi, lo = _hi_lo(x)
    return _dot(hi, w_ref[0]) + _dot(hi, w_ref[1]) + _dot(lo, w_ref[0])


def _dot_nt(a, b):
    return lax.dot_general(a, b, (((1,), (1,)), ((), ())), preferred_element_type=F32)


def _dot_tn(a, b):
    return lax.dot_general(a, b, (((0,), (0,)), ((), ())), preferred_element_type=F32)


def _ln0_body(x_ref, g_ref, b_ref, hf_ref, hb_ref):
    y = _layer_norm(x_ref[...], g_ref[...], b_ref[...])
    hf_ref[...] = y
    hb_ref[...] = y.astype(BF16)


def _ln0(x2, g, b):
    tm = 256
    row = pl.BlockSpec((tm, D_MODEL), lambda i: (i, 0))
    vec = pl.BlockSpec((1, D_MODEL), lambda i: (0, 0))
    return pl.pallas_call(
        _ln0_body, grid=(TOK // tm,), in_specs=[row, vec, vec], out_specs=[row, row],
        out_shape=[jax.ShapeDtypeStruct((TOK, D_MODEL), F32), jax.ShapeDtypeStruct((TOK, D_MODEL), BF16)],
        compiler_params=_params(1), name="ln0")(x2, g.reshape(1, -1), b.reshape(1, -1))


def _inproj_body(perm_ref, a_ref, w_ref, o_ref, wb_ref, *, gate, w_transposed):
    del perm_ref

    @pl.when(pl.program_id(1) == 0)
    def _():
        w = w_ref[...].T if w_transposed else w_ref[...]
        wb_ref[...] = w.astype(BF16)

    acc = _dot(a_ref[...], wb_ref[...])
    if gate:
        acc = _sigmoid(acc)
    o_ref[...] = acc.astype(o_ref.dtype)


def _inproj(hb, w, n_tiles, tile_perm, gate, w_transposed, name):
    tm, tn = 1024, 512
    w_spec = (pl.BlockSpec((tn, D_MODEL), lambda n, m, p: (n, 0)) if w_transposed
              else pl.BlockSpec((D_MODEL, tn), lambda n, m, p: (0, n)))
    grid_spec = pltpu.PrefetchScalarGridSpec(
        num_scalar_prefetch=1, grid=(n_tiles, TOK // tm),
        in_specs=[pl.BlockSpec((tm, D_MODEL), lambda n, m, p: (m, 0)), w_spec],
        out_specs=pl.BlockSpec((tm, tn), lambda n, m, p: (m, p[n])),
        scratch_shapes=[pltpu.VMEM((D_MODEL, tn), BF16)])
    return pl.pallas_call(
        functools.partial(_inproj_body, gate=gate, w_transposed=w_transposed), grid_spec=grid_spec,
        out_shape=jax.ShapeDtypeStruct((TOK, n_tiles * tn), BF16),
        compiler_params=_params(2), name=name)(jnp.asarray(tile_perm, jnp.int32), hb, w)


def _wprep_body(a_ref, b_ref, g_ref, lr_ref):
    g_ref[...] = jnp.concatenate([a_ref[LR_W:, :], b_ref[...]], axis=0).T.astype(BF16)

    @pl.when(pl.program_id(0) == 0)
    def _():
        lane = lax.broadcasted_iota(jnp.int32, (D_MODEL, 128), 1)
        lr_ref[...] = jnp.where(lane < LR_W, a_ref[:128, :].T, 0.0).astype(BF16)


def _gate_weights(wt):
    tn = 512
    first = MAIN_W // tn
    return pl.pallas_call(
        _wprep_body, grid=(2 * D_MODEL // tn,),
        in_specs=[pl.BlockSpec((tn, D_MODEL), lambda i: (first + i, 0)),
                  pl.BlockSpec((LR_W, D_MODEL), lambda i: ((first + i + 1) * (tn // LR_W), 0))],
        out_specs=[pl.BlockSpec((D_MODEL, tn), lambda i: (0, i)), pl.BlockSpec((D_MODEL, 128), lambda i: (0, 0))],
        out_shape=[jax.ShapeDtypeStruct((D_MODEL, 2 * D_MODEL), BF16), jax.ShapeDtypeStruct((D_MODEL, 128), BF16)],
        compiler_params=_params(1), name="gate_w")(wt, wt)


def _decay_body(h_ref, wlr_ref, w2_ref, b2_ref, trif_ref, trib_ref, bf_ref, bb_ref):
    zlr = _dot(h_ref[...], wlr_ref[...])
    pre = _dot_split(zlr, w2_ref) + b2_ref[...]
    la = (jnp.minimum(pre, 0.0) - jnp.log1p(jnp.exp(-jnp.abs(pre)))) * (1.0 / GLA_TAU)
    hi = la.astype(BF16)
    lo = (la - hi.astype(F32)).astype(BF16)
    bf_ref[...] = _dot(trif_ref[...], hi[:, :GLA_K]) + _dot(trif_ref[...], lo[:, :GLA_K])
    bb_ref[...] = _dot(trib_ref[...], hi[:, GLA_K:]) + _dot(trib_ref[...], lo[:, GLA_K:])


def _decay(hb, wlr, w2, b2):
    tm = 512
    r = jnp.arange(tm)
    same = (r[:, None] // GLA_CHUNK) == (r[None, :] // GLA_CHUNK)
    trif = (same & (r[None, :] <= r[:, None])).astype(BF16)
    trib = (same & (r[None, :] >= r[:, None])).astype(BF16)
    full = lambda shape: pl.BlockSpec(shape, lambda i: (0,) * len(shape))
    out = pl.BlockSpec((tm, GLA_K), lambda i: (i, 0))
    return pl.pallas_call(
        _decay_body, grid=(TOK // tm,),
        in_specs=[pl.BlockSpec((tm, D_MODEL), lambda i: (i, 0)), full((D_MODEL, 128)),
                  full((2, 128, 2 * GLA_K)), full((1, 2 * GLA_K)), full((tm, tm)), full((tm, tm))],
        out_specs=[out, out],
        out_shape=[jax.ShapeDtypeStruct((TOK, GLA_K), F32)] * 2,
        compiler_params=_params(1), name="decay")(hb, wlr, w2, b2, trif, trib)


def _qkprep_body(q_ref, k_ref, cos_ref, sin_ref, qn_ref, kn_ref, qo_ref, ko_ref):
    cos = cos_ref[...]
    sin = sin_ref[...]
    lane = lax.broadcasted_iota(jnp.int32, cos.shape, 1)
    even = (lane % 2) == 0

    def one(x, gain, scale):
        x = x.astype(F32)
        x = x * lax.rsqrt(jnp.mean(x * x, axis=-1, keepdims=True) + RMS_EPS) * gain
        partner = jnp.where(even, pltpu.roll(x, HEAD_DIM - 1, 1), pltpu.roll(x, 1, 1))
        return ((x * cos + partner * sin) * scale).astype(BF16)

    for hh in range(ATT_HEADS):
        sl = slice(hh * HEAD_DIM, (hh + 1) * HEAD_DIM)
        qo_ref[:, sl] = one(q_ref[:, sl], qn_ref[...], LOG2_E * HEAD_DIM ** -0.5)
    for hh in range(ATT_KV_HEADS):
        sl = slice(hh * HEAD_DIM, (hh + 1) * HEAD_DIM)
        ko_ref[:, sl] = one(k_ref[:, sl], kn_ref[...], 1.0)


def _qkprep(zm, cos_full, sin_signed, q_norm, k_norm):
    tm = 256
    nrow = SEQ // tm
    tab = pl.BlockSpec((tm, HEAD_DIM), lambda i: (i % nrow, 0))
    vec = pl.BlockSpec((1, HEAD_DIM), lambda i: (0, 0))
    return pl.pallas_call(
        _qkprep_body, grid=(TOK // tm,),
        in_specs=[pl.BlockSpec((tm, ATT_Q), lambda i: (i, Z_Q // ATT_Q)),
                  pl.BlockSpec((tm, ATT_KV), lambda i: (i, Z_KA // ATT_KV)), tab, tab, vec, vec],
        out_specs=[pl.BlockSpec((tm, ATT_Q), lambda i: (i, 0)), pl.BlockSpec((tm, ATT_KV), lambda i: (i, 0))],
        out_shape=[jax.ShapeDtypeStruct((TOK, ATT_Q), BF16), jax.ShapeDtypeStruct((TOK, ATT_KV), BF16)],
        compiler_params=_params(1), name="qkprep")(zm, zm, cos_full, sin_signed,
                                                   q_norm.reshape(1, -1), k_norm.reshape(1, -1))


ATT_TQ = 256
ATT_TK = 512


ATT_SUB = 64


def _attn_body(q_ref, k_ref, v_ref, o_ref, qs_ref, s0_ref, s1_ref, p0_ref, p1_ref, a0_ref, a1_ref,
               acc_ref, m_ref, l_ref):
    rows = ATT_GROUP * ATT_TQ
    n_chunks = SEQ // ATT_TK
    for g in range(ATT_GROUP):
        qs_ref[g * ATT_TQ:(g + 1) * ATT_TQ, :] = q_ref[:, g * HEAD_DIM:(g + 1) * HEAD_DIM]
    m_ref[...] = jnp.full(m_ref.shape, -jnp.inf, F32)
    l_ref[...] = jnp.zeros(l_ref.shape, F32)
    acc_ref[...] = jnp.zeros(acc_ref.shape, F32)
    p1_ref[...] = jnp.zeros((rows, ATT_TK), BF16)
    a1_ref[...] = jnp.zeros(a1_ref.shape, F32)

    def chunk(c):
        return pl.ds(pl.multiple_of(c * ATT_TK, ATT_TK), ATT_TK)

    s0_ref[...] = _dot_nt(qs_ref[...], k_ref[0:ATT_TK, :])

    def step(c, s_cur, s_nxt, p_cur, p_prv, a_cur, a_prv):
        c_next = jnp.where(c + 1 < n_chunks, c + 1, 0)
        c_prev = jnp.where(c > 0, c - 1, 0)
        s_nxt[...] = _dot_nt(qs_ref[...], k_ref[chunk(c_next), :])
        acc_ref[...] = a_prv[...] * acc_ref[...] + _dot(p_prv[...], v_ref[chunk(c_prev), :])
        lane_blocks = [slice(cb * HEAD_DIM, (cb + 1) * HEAD_DIM) for cb in range(ATT_TK // HEAD_DIM)]
        m_all, l_all = m_ref[...], l_ref[...]
        sub_tiles = [slice(t * ATT_SUB, (t + 1) * ATT_SUB) for t in range(rows // ATT_SUB)]
        m_out = []
        for rr in sub_tiles:
            smax = functools.reduce(jnp.maximum, [s_cur[rr, cb] for cb in lane_blocks])
            m_out.append(jnp.maximum(m_all[rr], jnp.max(smax, axis=-1, keepdims=True)))
        l_out, a_out = [], []
        for rr, m_new in zip(sub_tiles, m_out):
            alpha = jnp.exp2(m_all[rr] - m_new)
            p = [jnp.exp2(s_cur[rr, cb] - m_new) for cb in lane_blocks]
            l_out.append(alpha * l_all[rr] + jnp.sum(functools.reduce(jnp.add, p), axis=-1, keepdims=True))
            a_out.append(alpha)
            for cb, pb in zip(lane_blocks, p):
                p_cur[rr, cb] = pb.astype(BF16)
        m_ref[...] = jnp.concatenate(m_out, axis=0)
        l_ref[...] = jnp.concatenate(l_out, axis=0)
        a_cur[...] = jnp.concatenate(a_out, axis=0)

    def pair(i, carry):
        step(2 * i, s0_ref, s1_ref, p0_ref, p1_ref, a0_ref, a1_ref)
        step(2 * i + 1, s1_ref, s0_ref, p1_ref, p0_ref, a1_ref, a0_ref)
        return carry

    lax.fori_loop(0, n_chunks // 2, pair, 0)
    acc = a1_ref[...] * acc_ref[...] + _dot(p1_ref[...], v_ref[SEQ - ATT_TK:SEQ, :])
    out = (acc / l_ref[...]).astype(BF16)
    for g in range(ATT_GROUP):
        o_ref[:, g * HEAD_DIM:(g + 1) * HEAD_DIM] = out[g * ATT_TQ:(g + 1) * ATT_TQ]


def _attention(qr, kr, zm):
    nq = SEQ // ATT_TQ
    gw = ATT_GROUP * HEAD_DIM
    rows = ATT_GROUP * ATT_TQ
    qspec = pl.BlockSpec((ATT_TQ, gw), lambda b, j, i: (b * nq + i, j))
    return pl.pallas_call(
        _attn_body, grid=(BATCH, ATT_KV_HEADS, nq),
        in_specs=[qspec,
                  pl.BlockSpec((SEQ, HEAD_DIM), lambda b, j, i: (b, j)),
                  pl.BlockSpec((SEQ, HEAD_DIM), lambda b, j, i: (b, Z_VA // HEAD_DIM + j))],
        out_specs=qspec,
        out_shape=jax.ShapeDtypeStruct((TOK, ATT_Q), BF16),
        scratch_shapes=[pltpu.VMEM((rows, HEAD_DIM), BF16),
                        pltpu.VMEM((rows, ATT_TK), F32), pltpu.VMEM((rows, ATT_TK), F32),
                        pltpu.VMEM((rows, ATT_TK), BF16), pltpu.VMEM((rows, ATT_TK), BF16),
                        pltpu.VMEM((rows, HEAD_DIM), F32), pltpu.VMEM((rows, HEAD_DIM), F32),
                        pltpu.VMEM((rows, HEAD_DIM), F32), pltpu.VMEM((rows, HEAD_DIM), F32),
                        pltpu.VMEM((rows, HEAD_DIM), F32)],
        compiler_params=_params(3), name="attn")(qr, kr, zm)


GLA_CB = 4
GLA_RB = GLA_CB * GLA_CHUNK
N_SUB = GLA_CHUNK // GLA_SUB


def _gla_body(q_ref, k_ref, v_ref, bc_ref, o_ref, st_ref, kf_ref, bs_ref, *, rev):
    @pl.when(pl.program_id(1) == 0)
    def _():
        st_ref[...] = jnp.zeros_like(st_ref)

    C, SUB = GLA_CHUNK, GLA_SUB
    rowc = lax.broadcasted_iota(jnp.int32, (C, GLA_DK), 0)
    rows_s = lax.broadcasted_iota(jnp.int32, (SUB, 128), 0)
    lane_s = lax.broadcasted_iota(jnp.int32, (SUB, 128), 1)

    def chunk(ci, carry):
        c = (GLA_CB - 1 - ci) if rev else ci
        r0 = pl.multiple_of(c * C, C)
        for hh in range(GLA_HEADS):
            ksl = slice(hh * GLA_DK, (hh + 1) * GLA_DK)
            vsl = slice(hh * GLA_DV, (hh + 1) * GLA_DV)
            q = q_ref[pl.ds(r0, C), ksl].astype(F32) * (GLA_DK ** -0.5)
            k = k_ref[pl.ds(r0, C), ksl].astype(F32)
            v = v_ref[pl.ds(r0, C), vsl]
            bc = bc_ref[pl.ds(r0, C), ksl]
            kf_ref[hh] = k
            bs_ref[hh] = bc
            st = st_ref[hh]
            blast = bc[0:1] if rev else bc[C - 1:C]
            o_inter = _dot_nt((q * jnp.exp(bc)).astype(BF16), st.astype(BF16))
            kdec = k * jnp.exp(blast - bc)
            st_ref[hh] = st * jnp.exp(blast) + _dot_tn(v, kdec.astype(BF16))

            a_rows = []
            for si in range(N_SUB):
                lo, hi = si * SUB, (si + 1) * SUB
                q_s, b_s = q[lo:hi], bc[lo:hi]
                has_earlier = (si < N_SUB - 1) if rev else (si > 0)
                if has_earlier:
                    ref_row = bc[hi:hi + 1] if rev else bc[lo - 1:lo]
                    earlier = (rowc >= hi) if rev else (rowc < lo)
                    qt = q_s * jnp.exp(b_s - ref_row)
                    kt = k * jnp.exp(jnp.where(earlier, ref_row - bc, -jnp.inf))
                    a = _dot_nt(qt.astype(BF16), kt.astype(BF16))
                else:
                    a = jnp.zeros((SUB, C), F32)
                diag = jnp.zeros((SUB, 128), F32)
                for jl in range(SUB):
                    j = lo + jl
                    d = jnp.minimum(b_s - bs_ref[hh, j:j + 1, :], 0.0)
                    col = jnp.sum(q_s * kf_ref[hh, j:j + 1, :] * jnp.exp(d), axis=-1, keepdims=True)
                    diag = jnp.where(lane_s == j, col, diag)
                keep = (lane_s >= rows_s + lo) if rev else (lane_s <= rows_s + lo)
                diag = jnp.where(keep & (lane_s >= lo) & (lane_s < hi), diag, 0.0)
                a_rows.append(a + diag[:, :C])
            a_full = jnp.concatenate(a_rows, axis=0)
            o_ref[pl.ds(r0, C), vsl] = o_inter + _dot(a_full.astype(BF16), v)
        return carry

    lax.fori_loop(0, GLA_CB, chunk, 0)


def _gla(zm, bcum, rev):
    ncb = SEQ // GLA_RB
    if rev:
        row = lambda b, c: b * ncb + (ncb - 1 - c)
    else:
        row = lambda b, c: b * ncb + c
    return pl.pallas_call(
        functools.partial(_gla_body, rev=rev), grid=(BATCH, ncb),
        in_specs=[pl.BlockSpec((GLA_RB, GLA_K), lambda b, c: (row(b, c), Z_QB // GLA_K)),
                  pl.BlockSpec((GLA_RB, GLA_K), lambda b, c: (row(b, c), Z_KB // GLA_K)),
                  pl.BlockSpec((GLA_RB, GLA_V), lambda b, c: (row(b, c), Z_VB // GLA_V)),
                  pl.BlockSpec((GLA_RB, GLA_K), lambda b, c: (row(b, c), 0))],
        out_specs=pl.BlockSpec((GLA_RB, GLA_V), lambda b, c: (row(b, c), 0)),
        out_shape=jax.ShapeDtypeStruct((TOK, GLA_V), F32),
        scratch_shapes=[pltpu.VMEM((GLA_HEADS, GLA_DV, GLA_DK), F32),
                        pltpu.VMEM((GLA_HEADS, GLA_CHUNK, GLA_DK), F32),
                        pltpu.VMEM((GLA_HEADS, GLA_CHUNK, GLA_DK), F32)],
        compiler_params=_params(2), name="gla_bwd" if rev else "gla_fwd")(zm, zm, zm, bcum)


MIX_TM = 512


def _onorm_body(of_ref, ob_ref, og_ref, gn_ref, o_ref):
    gn = gn_ref[...]
    for hh in range(GLA_HEADS):
        sl = slice(hh * GLA_DV, (hh + 1) * GLA_DV)
        x = of_ref[:, sl] + ob_ref[:, sl]
        g = og_ref[:, sl].astype(F32)
        xn = x * lax.rsqrt(jnp.mean(x * x, axis=-1, keepdims=True) + RMS_EPS) * gn
        o_ref[:, sl] = (xn * (g * _sigmoid(g))).astype(BF16)


def _merge_body(attn_ref, on_ref, ga_ref, gb_ref, wa_ref, wb_ref, m_ref):
    ya = _dot(attn_ref[...], wa_ref[...])
    yb = _dot(on_ref[...], wb_ref[...])
    m_ref[...] = (ga_ref[...].astype(F32) * ya + gb_ref[...].astype(F32) * yb).astype(BF16)


def _outproj_body(m_ref, h_ref, wo_ref, lg_ref, lb_ref, wr_ref, br_ref, h1_ref, idx_ref, gate_ref):
    mix = _dot(m_ref[...], wo_ref[...])
    h1 = _layer_norm(DN_ALPHA * h_ref[...] + mix, lg_ref[...], lb_ref[...])
    h1_ref[...] = h1

    logits = _dot_split(h1, wr_ref) + br_ref[...]
    lane = lax.broadcasted_iota(jnp.int32, logits.shape, 1)
    x = logits
    vals, idxs = [], []
    for _ in range(TOP_K):
        mx = jnp.max(x, axis=-1, keepdims=True)
        ix = jnp.min(jnp.where(x == mx, lane, 128), axis=-1, keepdims=True)
        vals.append(mx)
        idxs.append(ix)
        x = jnp.where(lane == ix, -jnp.inf, x)
    es = [jnp.exp(vv - vals[0]) for vv in vals]
    den = es[0] + es[1] + es[2] + es[3]
    idx_out = jnp.zeros(logits.shape, jnp.int32)
    gate_out = jnp.zeros(logits.shape, F32)
    for kk in range(TOP_K):
        idx_out = jnp.where(lane == kk, idxs[kk], idx_out)
        gate_out = jnp.where(lane == kk, es[kk] / den, gate_out)
    idx_ref[...] = idx_out
    gate_ref[...] = gate_out


def _mixer(attn, o_f, o_b, zm, zg, hf, wa, wb, wo, gn, lg, lb, wr, br):
    tm = MIX_TM
    const = lambda shape: pl.BlockSpec(shape, lambda i: (0,) * len(shape), pipeline_mode=pl.Buffered(1))
    row = lambda w, cb=0: pl.BlockSpec((tm, w), lambda i: (i, cb))
    grid = (TOK // tm,)
    onorm = pl.pallas_call(
        _onorm_body, grid=grid,
        in_specs=[row(GLA_V), row(GLA_V), row(GLA_V, Z_OG // GLA_V), const((1, GLA_DV))],
        out_specs=row(GLA_V), out_shape=jax.ShapeDtypeStruct((TOK, GLA_V), BF16),
        compiler_params=_params(1), name="onorm")(o_f, o_b, zm, gn)
    merged = pl.pallas_call(
        _merge_body, grid=grid,
        in_specs=[row(ATT_Q), row(GLA_V), row(D_MODEL, 0), row(D_MODEL, 1),
                  const((ATT_Q, D_MODEL)), const((GLA_V, D_MODEL))],
        out_specs=row(D_MODEL), out_shape=jax.ShapeDtypeStruct((TOK, D_MODEL), BF16),
        compiler_params=_params(1), name="merge")(attn, onorm, zg, zg, wa, wb)
    return pl.pallas_call(
        _outproj_body, grid=grid,
        in_specs=[row(D_MODEL), row(D_MODEL), const((D_MODEL, D_MODEL)), const((1, D_MODEL)), const((1, D_MODEL)),
                  const((2, D_MODEL, 128)), const((1, 128))],
        out_specs=[row(D_MODEL), row(128), row(128)],
        out_shape=[jax.ShapeDtypeStruct((TOK, D_MODEL), F32),
                   jax.ShapeDtypeStruct((TOK, 128), jnp.int32),
                   jax.ShapeDtypeStruct((TOK, 128), F32)],
        compiler_params=_params(1), name="outproj")(merged, hf, wo, lg, lb, wr, br)


GROUP_SIZES = (512, 256, 128)
GATHER_STEPS = SB_BLOCKS // (SB_ROWS // GROUP_SIZES[0])


def _expert_body(e_ref, nb_ref, nv_ref, tab_hbm, h1_hbm, wg_ref, wl_ref, wd_ref, bg_ref, bl_ref, bd_ref,
                 y_hbm, gbuf, xb_ref, acc_ref, wgb_ref, wlb_ref, wdb_ref, ids, sem_ids, sem_g, sem_s):
    del e_ref
    s = pl.program_id(0)
    j = pl.program_id(1)
    nblk = nb_ref[s]
    slot = s & 1
    has_next = s + 1 < pl.num_programs(0)
    nxt = jnp.minimum(s + 1, N_SB - 1)
    nblk_next = jnp.where(has_next, nb_ref[nxt], 0)
    tiles = MOE_BLOCK // SUBLANES

    def ids_copy(sb, sl):
        return pltpu.make_async_copy(tab_hbm.at[pl.ds(pl.multiple_of(sb * IDS_STRIDE, IDS_STRIDE), IDS_STRIDE)],
                                     ids.at[pl.ds(pl.multiple_of(sl * IDS_STRIDE, IDS_STRIDE), IDS_STRIDE)],
                                     sem_ids.at[sl])

    def gather_block(sl, b):
        id0 = sl * IDS_STRIDE + b * MOE_BLOCK

        def issue(t, c):
            for u in range(SUBLANES):
                tok = ids[id0 + t * SUBLANES + u]
                pltpu.make_async_copy(h1_hbm.at[pl.ds(tok >> 3, 1), pl.ds(tok & 7, 1), :],
                                      gbuf.at[pl.ds(b * tiles + t, 1), pl.ds(u, 1), :], sem_g).start()
            return c

        lax.fori_loop(0, tiles, issue, 0)

    def gather_wait_block(b):
        pltpu.make_async_copy(h1_hbm.at[pl.ds(0, tiles)], gbuf.at[pl.ds(b * tiles, tiles)], sem_g).wait()

    def loop_blocks(n, fn):
        def body(b, c):
            fn(b)
            return c

        lax.fori_loop(0, n, body, 0)

    @pl.when((s == 0) & (j == 0))
    def _():
        ids_copy(0, 0).start()
        ids_copy(0, 0).wait()
        loop_blocks(nblk, lambda b: gather_block(0, b))

    big = GROUP_SIZES[0]
    big_blocks = big // MOE_BLOCK
    n_big = nblk // big_blocks
    in_window = (j >= 1) & (j <= GATHER_STEPS) & has_next
    issued_here = GATHER_STEPS * n_big
    issued_prev = jnp.where(s > 0, GATHER_STEPS * (nb_ref[jnp.maximum(s - 1, 0)] // big_blocks), 0)

    @pl.when(j == 0)
    def _():
        loop_blocks(jnp.maximum(issued_prev, nblk), gather_wait_block)

    @pl.when(nblk > 0)
    def _():
        @pl.when(j == 0)
        def _():
            @pl.when(has_next)
            def _():
                ids_copy(nxt, 1 - slot).start()

            def take_block(b):
                x = gbuf[pl.ds(b * tiles, tiles)].reshape(MOE_BLOCK, D_MODEL)
                xb_ref[pl.ds(pl.multiple_of(b * MOE_BLOCK, MOE_BLOCK), MOE_BLOCK), :] = x.astype(BF16)
                acc_ref[pl.ds(b * tiles, tiles)] = jnp.broadcast_to(bd_ref[0], (tiles, SUBLANES, D_MODEL))

            loop_blocks(nblk, take_block)

        @pl.when((j == 1) & has_next)
        def _():
            ids_copy(nxt, 1 - slot).wait()

        @pl.when(j == GATHER_STEPS + 1)
        def _():
            for b in range(SB_BLOCKS):
                @pl.when((b >= issued_here) & (b < nblk_next))
                def _(b=b):
                    gather_block(1 - slot, b)

        wgb_ref[...] = wg_ref[0].astype(BF16)
        wlb_ref[...] = wl_ref[0].astype(BF16)
        wdb_ref[...] = wd_ref[0].astype(BF16)

        def group(r0, size):
            x = xb_ref[pl.ds(pl.multiple_of(r0, MOE_BLOCK), size), :]
            g = jnp.minimum(_dot(x, wgb_ref[...]) + bg_ref[0], SWIGLU_LIMIT)
            lin = jnp.clip(_dot(x, wlb_ref[...]) + bl_ref[0], -SWIGLU_LIMIT, SWIGLU_LIMIT)
            act = g * _sigmoid(SWIGLU_ALPHA * g) * (lin + 1.0)
            upd = _dot(act.astype(BF16), wdb_ref[...])
            acc_ref[pl.ds(r0 // SUBLANES, size // SUBLANES)] += upd.reshape(size // SUBLANES, SUBLANES, D_MODEL)

        def big_group(i, c):
            group(i * big, big)
            return c

        def big_group_and_gather(i, c):
            group(i * big, big)
            b = (j - 1) * n_big + i
            id0 = (1 - slot) * IDS_STRIDE + b * MOE_BLOCK
            for t in range(tiles):
                for u in range(SUBLANES):
                    tok = ids[id0 + t * SUBLANES + u]
                    pltpu.make_async_copy(h1_hbm.at[pl.ds(tok >> 3, 1), pl.ds(tok & 7, 1), :],
                                          gbuf.at[pl.ds(b * tiles + t, 1), pl.ds(u, 1), :], sem_g).start()
            return c

        @pl.when(in_window)
        def _():
            lax.fori_loop(0, n_big, big_group_and_gather, 0)

        @pl.when(jnp.logical_not(in_window))
        def _():
            lax.fori_loop(0, n_big, big_group, 0)

        done = n_big * big_blocks
        for size in GROUP_SIZES[1:]:
            take = ((nblk - done) // (size // MOE_BLOCK)) > 0

            @pl.when(take)
            def _(done=done, size=size):
                group(done * MOE_BLOCK, size)

            done = done + jnp.where(take, size // MOE_BLOCK, 0)

        @pl.when(j == N_FF_TILES - 1)
        def _():
            nvalid = nv_ref[s]

            dst0 = slot * IDS_STRIDE + SB_ROWS

            def row_copy(t, u, dst):
                return pltpu.make_async_copy(acc_ref.at[pl.ds(t, 1), pl.ds(u, 1), :],
                                             y_hbm.at[pl.ds(dst >> 3, 1), pl.ds(dst & 7, 1), :], sem_s)

            def issue_tile(t, c):
                for u in range(SUBLANES):
                    row_copy(t, u, ids[dst0 + t * SUBLANES + u]).start()
                return c

            def issue_row(r, c):
                row_copy(r >> 3, r & 7, ids[dst0 + r]).start()
                return c

            full_tiles = nvalid // SUBLANES
            lax.fori_loop(0, full_tiles, issue_tile, 0)
            lax.fori_loop(full_tiles * SUBLANES, nvalid, issue_row, 0)

            def wait_block(b, c):
                pltpu.make_async_copy(acc_ref.at[pl.ds(0, tiles)], y_hbm.at[pl.ds(0, tiles)], sem_s).wait()
                return c

            def wait_row(r, c):
                row_copy(0, 0, 0).wait()
                return c

            nfull = nvalid // MOE_BLOCK
            lax.fori_loop(0, nfull, wait_block, 0)
            lax.fori_loop(nfull * MOE_BLOCK, nvalid, wait_row, 0)


def _experts(n_sb, sb_e, sb_nblk, sb_nvalid, tab, h1, w_up, b_up, w_down, b_down):
    def jj(j, nb, s):
        return jnp.where(nb[s] > 0, j, N_FF_TILES - 1)

    grid_spec = pltpu.PrefetchScalarGridSpec(
        num_scalar_prefetch=3, grid=(n_sb, N_FF_TILES),
        in_specs=[pl.BlockSpec(memory_space=pl.ANY),
                  pl.BlockSpec(memory_space=pl.ANY),
                  pl.BlockSpec((1, D_MODEL, FF_TILE), lambda s, j, e, nb, nv: (e[s], 0, jj(j, nb, s))),
                  pl.BlockSpec((1, D_MODEL, FF_TILE), lambda s, j, e, nb, nv: (e[s], 0, N_FF_TILES + jj(j, nb, s))),
                  pl.BlockSpec((1, FF_TILE, D_MODEL), lambda s, j, e, nb, nv: (e[s], jj(j, nb, s), 0)),
                  pl.BlockSpec((1, 1, FF_TILE), lambda s, j, e, nb, nv: (e[s], 0, jj(j, nb, s))),
                  pl.BlockSpec((1, 1, FF_TILE), lambda s, j, e, nb, nv: (e[s], 0, N_FF_TILES + jj(j, nb, s))),
                  pl.BlockSpec((1, 1, D_MODEL), lambda s, j, e, nb, nv: (e[s], 0, 0))],
        out_specs=pl.BlockSpec(memory_space=pl.ANY),
        scratch_shapes=[pltpu.VMEM((SB_ROWS // SUBLANES, SUBLANES, D_MODEL), F32),
                        pltpu.VMEM((SB_ROWS, D_MODEL), BF16),
                        pltpu.VMEM((SB_ROWS // SUBLANES, SUBLANES, D_MODEL), F32),
                        pltpu.VMEM((D_MODEL, FF_TILE), BF16),
                        pltpu.VMEM((D_MODEL, FF_TILE), BF16),
                        pltpu.VMEM((FF_TILE, D_MODEL), BF16),
                        pltpu.SMEM((2 * IDS_STRIDE,), jnp.int32),
                        pltpu.SemaphoreType.DMA((2,)),
                        pltpu.SemaphoreType.DMA(()),
                        pltpu.SemaphoreType.DMA(())])
    y = pl.pallas_call(
        _expert_body, grid_spec=grid_spec,
        out_shape=jax.ShapeDtypeStruct((TOP_K * TOK // SUBLANES, SUBLANES, D_MODEL), F32),
        compiler_params=_params(2), name="experts")(
            sb_e, sb_nblk, sb_nvalid, tab, h1.reshape(TOK // SUBLANES, SUBLANES, D_MODEL), w_up, w_up, w_down,
            b_up.reshape(N_EXPERTS, 1, 2 * D_FF), b_up.reshape(N_EXPERTS, 1, 2 * D_FF),
            b_down.reshape(N_EXPERTS, 1, D_MODEL))
    return y.reshape(TOP_K * TOK, D_MODEL)


def _tail_body(y0_ref, y1_ref, y2_ref, y3_ref, gate_ref, h1_ref, p_ref, wg_ref, wp_ref,
               l2g_ref, l2b_ref, l3g_ref, l3b_ref, o_ref):
    gate = gate_ref[...]
    y = gate[:, 0:1] * y0_ref[...]
    for kk, y_ref in enumerate((y1_ref, y2_ref, y3_ref), start=1):
        y = y + gate[:, kk:kk + 1] * y_ref[...]
    h2 = _layer_norm(DN_ALPHA * h1_ref[...] + y, l2g_ref[...], l2b_ref[...])
    ple = _sigmoid(_dot(h2.astype(BF16), wg_ref[...])) * _dot(p_ref[...].astype(BF16), wp_ref[...])
    o_ref[...] = _layer_norm(DN_ALPHA * h2 + ple, l3g_ref[...], l3b_ref[...])


def _tail(y_slots, gates, h1, p2, wg, wp, l2g, l2b, l3g, l3b):
    tm = 256
    nt = TOK // tm
    const = lambda shape: pl.BlockSpec(shape, lambda i: (0,) * len(shape), pipeline_mode=pl.Buffered(1))
    row = lambda w: pl.BlockSpec((tm, w), lambda i: (i, 0))
    yspec = lambda kk: pl.BlockSpec((tm, D_MODEL), lambda i: (kk * nt + i, 0))
    vec = const((1, D_MODEL))
    return pl.pallas_call(
        _tail_body, grid=(nt,),
        in_specs=[yspec(0), yspec(1), yspec(2), yspec(3), row(128), row(D_MODEL), row(PLE_DIM),
                  const((D_MODEL, D_MODEL)), const((PLE_DIM, D_MODEL)), vec, vec, vec, vec],
        out_specs=row(D_MODEL),
        out_shape=jax.ShapeDtypeStruct((TOK, D_MODEL), F32),
        compiler_params=_params(1), name="tail")(y_slots, y_slots, y_slots, y_slots, gates, h1, p2, wg, wp,
                                                 l2g, l2b, l3g, l3b)


def _routing(top_idx):
    flat_e = top_idx.reshape(-1)
    experts = jnp.arange(N_EXPERTS, dtype=jnp.int32)
    onehot = (flat_e[:, None] == experts[None, :]).astype(jnp.int32)
    csum = jnp.cumsum(onehot, axis=0)
    rank = jnp.sum(onehot * csum, axis=1) - 1
    counts = csum[-1]
    padded = (counts + MOE_BLOCK - 1) // MOE_BLOCK * MOE_BLOCK
    padded_end = jnp.cumsum(padded)
    padded_start = padded_end - padded
    dest = (padded_start[flat_e] + rank).astype(jnp.int32)
    asg = jnp.arange(TOK * TOP_K, dtype=jnp.int32)
    asg_of_row = jnp.zeros((N_ROWS + SB_ROWS,), jnp.int32).at[dest].set(asg)
    tok = (asg_of_row // TOP_K).reshape(-1, MOE_BLOCK)
    dst = ((asg_of_row % TOP_K) * TOK + asg_of_row // TOP_K).reshape(-1, MOE_BLOCK)
    nb = padded // MOE_BLOCK
    n_sb = (nb + SB_BLOCKS - 1) // SB_BLOCKS
    sb_end = jnp.cumsum(n_sb)
    total = sb_end[-1]
    s = jnp.arange(N_SB, dtype=jnp.int32)
    s_eff = jnp.minimum(s, total - 1)
    e = jnp.minimum(jnp.sum((sb_end[None, :] <= s_eff[:, None]).astype(jnp.int32), axis=1), N_EXPERTS - 1)
    local = s_eff - (sb_end[e] - n_sb[e])
    valid = s < total
    sb_nblk = jnp.where(valid, jnp.clip(nb[e] - local * SB_BLOCKS, 0, SB_BLOCKS), 0).astype(jnp.int32)
    sb_blk = padded_start[e] // MOE_BLOCK + local * SB_BLOCKS
    sb_nvalid = jnp.where(valid, jnp.clip(counts[e] - local * SB_ROWS, 0, SB_ROWS), 0).astype(jnp.int32)
    blocks = sb_blk[:, None] + jnp.arange(SB_BLOCKS, dtype=jnp.int32)[None, :]
    tab = jnp.concatenate([tok[blocks].reshape(N_SB, SB_ROWS), dst[blocks].reshape(N_SB, SB_ROWS),
                           jnp.zeros((N_SB, IDS_STRIDE - 2 * SB_ROWS), jnp.int32)], axis=1).astype(jnp.int32).reshape(-1)
    return total.astype(jnp.int32), e.astype(jnp.int32), sb_nblk, sb_nvalid, tab


def _rope_tables():
    rows = SEQ // GRID_W
    row = jnp.repeat(jnp.arange(rows), GRID_W)
    col = jnp.tile(jnp.arange(GRID_W), rows)
    n_pairs = HEAD_DIM // 4
    inv_freq = ROPE_BASE ** (-jnp.arange(n_pairs, dtype=F32) / n_pairs)
    ang = jnp.concatenate([row[:, None] * inv_freq, col[:, None] * inv_freq], -1)
    cos_full = jnp.repeat(jnp.cos(ang), 2, axis=-1)
    sin = jnp.sin(ang)
    sin_signed = jnp.stack([-sin, sin], axis=-1).reshape(SEQ, HEAD_DIM)
    return cos_full, sin_signed


def kernel(x, p, in_ln_g, in_ln_b, w_in, q_norm, k_norm, w_lr_f, b_lr_f, w_lr_b, b_lr_b, gla_norm, w_br_a, w_br_b, w_o, ln1_g, ln1_b, w_router, b_router, w_up, b_up, w_down, b_down, ln2_g, ln2_b, w_ple_gate, w_ple_proj, ln3_g, ln3_b):
    assert x.shape == (BATCH, SEQ, D_MODEL) and w_in.shape[0] == DEPTH == 1
    wt = w_in[0].T
    hf, hb = _ln0(x.reshape(TOK, D_MODEL), in_ln_g, in_ln_b)

    zm = _inproj(hb, wt, MAIN_W // 512, Z_TILE_OF_W_TILE, False, True, "inproj_main")
    w_gate, w_lr = _gate_weights(wt)
    zg = _inproj(hb, w_gate, 2 * D_MODEL // 512, tuple(range(8)), True, False, "inproj_gates")

    w2 = jnp.zeros((128, 2 * GLA_K), F32)
    w2 = w2.at[:GLA_RANK, :GLA_K].set(w_lr_f[0]).at[GLA_RANK:LR_W, GLA_K:].set(w_lr_b[0])
    b2 = jnp.concatenate([b_lr_f[0], b_lr_b[0]]).reshape(1, -1)
    bc_f, bc_b = _decay(hb, w_lr, jnp.stack(_hi_lo(w2)), b2)

    cos_full, sin_signed = _rope_tables()
    qr, kr = _qkprep(zm, cos_full, sin_signed, q_norm[0], k_norm[0])
    attn = _attention(qr, kr, zm)
    o_f = _gla(zm, bc_f, False)
    o_b = _gla(zm, bc_b, True)

    wr = jnp.zeros((D_MODEL, 128), F32).at[:, :N_EXPERTS].set(w_router[0])
    br = jnp.full((1, 128), -jnp.inf, F32).at[0, :N_EXPERTS].set(b_router[0])
    vec = lambda v: v[0].reshape(1, -1)
    h1, idx_pad, gate_pad = _mixer(
        attn, o_f, o_b, zm, zg, hf, w_br_a[0].astype(BF16), w_br_b[0].astype(BF16), w_o[0].astype(BF16),
        vec(gla_norm), vec(ln1_g), vec(ln1_b), jnp.stack(_hi_lo(wr)), br)

    n_sb, sb_e, sb_nblk, sb_nvalid, tab = _routing(idx_pad[:, :TOP_K])
    y_slots = _experts(n_sb, sb_e, sb_nblk, sb_nvalid, tab, h1, w_up[0], b_up[0], w_down[0], b_down[0])

    out = _tail(y_slots, gate_pad, h1, p[0].reshape(TOK, PLE_DIM), w_ple_gate[0].astype(BF16),
                w_ple_proj[0].astype(BF16), vec(ln2_g), vec(ln2_b), vec(ln3_g), vec(ln3_b))
    return out.reshape(BATCH, SEQ, D_MODEL)
```

```python
import functools

import jax
import jax.numpy as jnp
from jax import lax
from jax.experimental import pallas as pl
from jax.experimental.pallas import tpu as pltpu

F32 = jnp.float32
BF16 = jnp.bfloat16

D_MODEL = 2048
BATCH = 2
SEQ = 4096
TOK = BATCH * SEQ
PLE_DIM = 256
GRID_W = 64
ATT_HEADS = 8
ATT_KV_HEADS = 2
ATT_GROUP = ATT_HEADS // ATT_KV_HEADS
HEAD_DIM = 128
ROPE_BASE = 10000.0
GLA_HEADS = 4
GLA_DK = 128
GLA_DV = 256
GLA_RANK = 16
GLA_TAU = 16.0
GLA_CHUNK = 64
GLA_SUB = 16
N_EXPERTS = 32
TOP_K = 4
D_FF = D_MODEL
SWIGLU_LIMIT = 7.0
SWIGLU_ALPHA = 1.702
MOE_BLOCK = 128
ATT_Q = ATT_HEADS * HEAD_DIM
ATT_KV = ATT_KV_HEADS * HEAD_DIM
GLA_K = GLA_HEADS * GLA_DK
GLA_V = GLA_HEADS * GLA_DV
MAIN_W = ATT_Q + 2 * ATT_KV + 2 * GLA_K + 2 * GLA_V
LR_W = 2 * GLA_RANK
DEPTH = 1
DN_ALPHA = (2 * DEPTH) ** 0.25
LN_EPS = 1e-5
RMS_EPS = 1e-6
LOG2_E = 1.4426950408889634

Z_TILE_OF_W_TILE = (0, 1, 6, 7, 8, 2, 3, 4, 5)
Z_Q, Z_VB, Z_OG, Z_KA, Z_VA, Z_QB, Z_KB = 0, 1024, 2048, 3072, 3328, 3584, 4096

VMEM_LIMIT = 56 * 1024 * 1024

N_ROWS = TOK * TOP_K + N_EXPERTS * MOE_BLOCK
N_BLOCKS = N_ROWS // MOE_BLOCK
SB_BLOCKS = 10
SB_ROWS = SB_BLOCKS * MOE_BLOCK
N_SB = (N_BLOCKS + (SB_BLOCKS - 1) * N_EXPERTS) // SB_BLOCKS
SUBLANES = 8
IDS_STRIDE = -(-2 * SB_ROWS // 1024) * 1024
FF_TILE = 256
N_FF_TILES = D_FF // FF_TILE


def _params(n_axes, vmem=None):
    return pltpu.CompilerParams(dimension_semantics=("arbitrary",) * n_axes,
                                vmem_limit_bytes=vmem or VMEM_LIMIT)


def _sigmoid(x):
    return 1.0 / (1.0 + jnp.exp(-x))


def _layer_norm(y, g, b):
    mu = jnp.mean(y, axis=-1, keepdims=True)
    yc = y - mu
    var = jnp.mean(yc * yc, axis=-1, keepdims=True)
    return yc * lax.rsqrt(var + LN_EPS) * g + b


def _dot(a, b):
    return jnp.dot(a, b, preferred_element_type=F32)


def _hi_lo(x):
    hi = x.astype(BF16)
    return hi, (x - hi.astype(F32)).astype(BF16)


def _dot_split(x, w_ref):
    hi, lo = _hi_lo(x)
    return _dot(hi, w_ref[0]) + _dot(hi, w_ref[1]) + _dot(lo, w_ref[0])


def _dot_nt(a, b):
    return lax.dot_general(a, b, (((1,), (1,)), ((), ())), preferred_element_type=F32)


def _dot_tn(a, b):
    return lax.dot_general(a, b, (((0,), (0,)), ((), ())), preferred_element_type=F32)


def _ln0_body(x_ref, g_ref, b_ref, hf_ref, hb_ref):
    y = _layer_norm(x_ref[...], g_ref[...], b_ref[...])
    hf_ref[...] = y
    hb_ref[...] = y.astype(BF16)


def _ln0(x2, g, b):
    tm = 256
    row = pl.BlockSpec((tm, D_MODEL), lambda i: (i, 0))
    vec = pl.BlockSpec((1, D_MODEL), lambda i: (0, 0))
    return pl.pallas_call(
        _ln0_body, grid=(TOK // tm,), in_specs=[row, vec, vec], out_specs=[row, row],
        out_shape=[jax.ShapeDtypeStruct((TOK, D_MODEL), F32), jax.ShapeDtypeStruct((TOK, D_MODEL), BF16)],
        compiler_params=_params(1), name="ln0")(x2, g.reshape(1, -1), b.reshape(1, -1))


def _inproj_body(perm_ref, a_ref, w_ref, o_ref, wb_ref, *, gate, w_transposed):
    del perm_ref

    @pl.when(pl.program_id(1) == 0)
    def _():
        w = w_ref[...].T if w_transposed else w_ref[...]
        wb_ref[...] = w.astype(BF16)

    acc = _dot(a_ref[...], wb_ref[...])
    if gate:
        acc = _sigmoid(acc)
    o_ref[...] = acc.astype(o_ref.dtype)


def _inproj(hb, w, n_tiles, tile_perm, gate, w_transposed, name):
    tm, tn = 1024, 512
    w_spec = (pl.BlockSpec((tn, D_MODEL), lambda n, m, p: (n, 0)) if w_transposed
              else pl.BlockSpec((D_MODEL, tn), lambda n, m, p: (0, n)))
    grid_spec = pltpu.PrefetchScalarGridSpec(
        num_scalar_prefetch=1, grid=(n_tiles, TOK // tm),
        in_specs=[pl.BlockSpec((tm, D_MODEL), lambda n, m, p: (m, 0)), w_spec],
        out_specs=pl.BlockSpec((tm, tn), lambda n, m, p: (m, p[n])),
        scratch_shapes=[pltpu.VMEM((D_MODEL, tn), BF16)])
    return pl.pallas_call(
        functools.partial(_inproj_body, gate=gate, w_transposed=w_transposed), grid_spec=grid_spec,
        out_shape=jax.ShapeDtypeStruct((TOK, n_tiles * tn), BF16),
        compiler_params=_params(2), name=name)(jnp.asarray(tile_perm, jnp.int32), hb, w)


def _wprep_body(a_ref, b_ref, g_ref, lr_ref):
    g_ref[...] = jnp.concatenate([a_ref[LR_W:, :], b_ref[...]], axis=0).T.astype(BF16)

    @pl.when(pl.program_id(0) == 0)
    def _():
        lane = lax.broadcasted_iota(jnp.int32, (D_MODEL, 128), 1)
        lr_ref[...] = jnp.where(lane < LR_W, a_ref[:128, :].T, 0.0).astype(BF16)


def _gate_weights(wt):
    tn = 512
    first = MAIN_W // tn
    return pl.pallas_call(
        _wprep_body, grid=(2 * D_MODEL // tn,),
        in_specs=[pl.BlockSpec((tn, D_MODEL), lambda i: (first + i, 0)),
                  pl.BlockSpec((LR_W, D_MODEL), lambda i: ((first + i + 1) * (tn // LR_W), 0))],
        out_specs=[pl.BlockSpec((D_MODEL, tn), lambda i: (0, i)), pl.BlockSpec((D_MODEL, 128), lambda i: (0, 0))],
        out_shape=[jax.ShapeDtypeStruct((D_MODEL, 2 * D_MODEL), BF16), jax.ShapeDtypeStruct((D_MODEL, 128), BF16)],
        compiler_params=_params(1), name="gate_w")(wt, wt)


def _decay_body(h_ref, wlr_ref, w2_ref, b2_ref, trif_ref, trib_ref, bf_ref, bb_ref):
    zlr = _dot(h_ref[...], wlr_ref[...])
    pre = _dot_split(zlr, w2_ref) + b2_ref[...]
    la = (jnp.minimum(pre, 0.0) - jnp.log1p(jnp.exp(-jnp.abs(pre)))) * (1.0 / GLA_TAU)
    hi = la.astype(BF16)
    lo = (la - hi.astype(F32)).astype(BF16)
    bf_ref[...] = _dot(trif_ref[...], hi[:, :GLA_K]) + _dot(trif_ref[...], lo[:, :GLA_K])
    bb_ref[...] = _dot(trib_ref[...], hi[:, GLA_K:]) + _dot(trib_ref[...], lo[:, GLA_K:])


def _decay(hb, wlr, w2, b2):
    tm = 512
    r = jnp.arange(tm)
    same = (r[:, None] // GLA_CHUNK) == (r[None, :] // GLA_CHUNK)
    trif = (same & (r[None, :] <= r[:, None])).astype(BF16)
    trib = (same & (r[None, :] >= r[:, None])).astype(BF16)
    full = lambda shape: pl.BlockSpec(shape, lambda i: (0,) * len(shape))
    out = pl.BlockSpec((tm, GLA_K), lambda i: (i, 0))
    return pl.pallas_call(
        _decay_body, grid=(TOK // tm,),
        in_specs=[pl.BlockSpec((tm, D_MODEL), lambda i: (i, 0)), full((D_MODEL, 128)),
                  full((2, 128, 2 * GLA_K)), full((1, 2 * GLA_K)), full((tm, tm)), full((tm, tm))],
        out_specs=[out, out],
        out_shape=[jax.ShapeDtypeStruct((TOK, GLA_K), F32)] * 2,
        compiler_params=_params(1), name="decay")(hb, wlr, w2, b2, trif, trib)


def _qkprep_body(q_ref, k_ref, cos_ref, sin_ref, qn_ref, kn_ref, qo_ref, ko_ref):
    cos = cos_ref[...]
    sin = sin_ref[...]
    lane = lax.broadcasted_iota(jnp.int32, cos.shape, 1)
    even = (lane % 2) == 0

    def one(x, gain, scale):
        x = x.astype(F32)
        x = x * lax.rsqrt(jnp.mean(x * x, axis=-1, keepdims=True) + RMS_EPS) * gain
        partner = jnp.where(even, pltpu.roll(x, HEAD_DIM - 1, 1), pltpu.roll(x, 1, 1))
        return ((x * cos + partner * sin) * scale).astype(BF16)

    for hh in range(ATT_HEADS):
        sl = slice(hh * HEAD_DIM, (hh + 1) * HEAD_DIM)
        qo_ref[:, sl] = one(q_ref[:, sl], qn_ref[...], LOG2_E * HEAD_DIM ** -0.5)
    for hh in range(ATT_KV_HEADS):
        sl = slice(hh * HEAD_DIM, (hh + 1) * HEAD_DIM)
        ko_ref[:, sl] = one(k_ref[:, sl], kn_ref[...], 1.0)


def _qkprep(zm, cos_full, sin_signed, q_norm, k_norm):
    tm = 256
    nrow = SEQ // tm
    tab = pl.BlockSpec((tm, HEAD_DIM), lambda i: (i % nrow, 0))
    vec = pl.BlockSpec((1, HEAD_DIM), lambda i: (0, 0))
    return pl.pallas_call(
        _qkprep_body, grid=(TOK // tm,),
        in_specs=[pl.BlockSpec((tm, ATT_Q), lambda i: (i, Z_Q // ATT_Q)),
                  pl.BlockSpec((tm, ATT_KV), lambda i: (i, Z_KA // ATT_KV)), tab, tab, vec, vec],
        out_specs=[pl.BlockSpec((tm, ATT_Q), lambda i: (i, 0)), pl.BlockSpec((tm, ATT_KV), lambda i: (i, 0))],
        out_shape=[jax.ShapeDtypeStruct((TOK, ATT_Q), BF16), jax.ShapeDtypeStruct((TOK, ATT_KV), BF16)],
        compiler_params=_params(1), name="qkprep")(zm, zm, cos_full, sin_signed,
                                                   q_norm.reshape(1, -1), k_norm.reshape(1, -1))


ATT_TQ = 256
ATT_TK = 512


def _attn_body(q_ref, k_ref, v_ref, o_ref, qs_ref, s0_ref, s1_ref, p0_ref, p1_ref, a0_ref, a1_ref,
               acc_ref, m_ref, l_ref):
    rows = ATT_GROUP * ATT_TQ
    n_chunks = SEQ // ATT_TK
    groups = ATT_TK // SUBLANES
    for g in range(ATT_GROUP):
        qs_ref[g * ATT_TQ:(g + 1) * ATT_TQ, :] = q_ref[:, g * HEAD_DIM:(g + 1) * HEAD_DIM]
    m_ref[...] = jnp.full(m_ref.shape, -jnp.inf, F32)
    l_ref[...] = jnp.zeros(l_ref.shape, F32)
    acc_ref[...] = jnp.zeros(acc_ref.shape, F32)
    p1_ref[...] = jnp.zeros(p1_ref.shape, BF16)
    a1_ref[...] = jnp.zeros(a1_ref.shape, F32)

    def chunk(c):
        return pl.ds(pl.multiple_of(c * ATT_TK, ATT_TK), ATT_TK)

    s0_ref[...] = _dot_nt(k_ref[0:ATT_TK, :], qs_ref[...])
    lane_blocks = [slice(cb * HEAD_DIM, (cb + 1) * HEAD_DIM) for cb in range(rows // HEAD_DIM)]

    def step(c, s_cur, s_nxt, p_cur, p_prv, a_cur, a_prv):
        c_next = jnp.where(c + 1 < n_chunks, c + 1, 0)
        c_prev = jnp.where(c > 0, c - 1, 0)
        s_nxt[...] = _dot_nt(k_ref[chunk(c_next), :], qs_ref[...])
        acc_ref[...] = a_prv[0:1, :] * acc_ref[...] + _dot_tn(v_ref[chunk(c_prev), :], p_prv[...])
        m_new = []
        for cb in lane_blocks:
            part = jnp.max(s_cur[:, cb].reshape(groups, SUBLANES, HEAD_DIM), axis=0)
            m_new.append(jnp.maximum(m_ref[0:1, cb], jnp.max(part, axis=0, keepdims=True)))
        for cb, mn in zip(lane_blocks, m_new):
            alpha = jnp.exp2(m_ref[0:1, cb] - mn)
            p = jnp.exp2(s_cur[:, cb] - mn)
            l_ref[:, cb] = alpha * l_ref[:, cb] + jnp.sum(p.reshape(groups, SUBLANES, HEAD_DIM), axis=0)
            a_cur[:, cb] = jnp.broadcast_to(alpha, (SUBLANES, HEAD_DIM))
            p_cur[:, cb] = p.astype(BF16)
        for cb, mn in zip(lane_blocks, m_new):
            m_ref[:, cb] = jnp.broadcast_to(mn, (SUBLANES, HEAD_DIM))

    def pair(i, carry):
        step(2 * i, s0_ref, s1_ref, p0_ref, p1_ref, a0_ref, a1_ref)
        step(2 * i + 1, s1_ref, s0_ref, p1_ref, p0_ref, a1_ref, a0_ref)
        return carry

    lax.fori_loop(0, n_chunks // 2, pair, 0)
    acc = a1_ref[0:1, :] * acc_ref[...] + _dot_tn(v_ref[SEQ - ATT_TK:SEQ, :], p1_ref[...])
    out = (acc / jnp.sum(l_ref[...], axis=0, keepdims=True)).T.astype(BF16)
    for g in range(ATT_GROUP):
        o_ref[:, g * HEAD_DIM:(g + 1) * HEAD_DIM] = out[g * ATT_TQ:(g + 1) * ATT_TQ]


def _attention(qr, kr, zm):
    nq = SEQ // ATT_TQ
    gw = ATT_GROUP * HEAD_DIM
    rows = ATT_GROUP * ATT_TQ
    qspec = pl.BlockSpec((ATT_TQ, gw), lambda b, j, i: (b * nq + i, j))
    return pl.pallas_call(
        _attn_body, grid=(BATCH, ATT_KV_HEADS, nq),
        in_specs=[qspec,
                  pl.BlockSpec((SEQ, HEAD_DIM), lambda b, j, i: (b, j)),
                  pl.BlockSpec((SEQ, HEAD_DIM), lambda b, j, i: (b, Z_VA // HEAD_DIM + j))],
        out_specs=qspec,
        out_shape=jax.ShapeDtypeStruct((TOK, ATT_Q), BF16),
        scratch_shapes=[pltpu.VMEM((rows, HEAD_DIM), BF16),
                        pltpu.VMEM((ATT_TK, rows), F32), pltpu.VMEM((ATT_TK, rows), F32),
                        pltpu.VMEM((ATT_TK, rows), BF16), pltpu.VMEM((ATT_TK, rows), BF16),
                        pltpu.VMEM((SUBLANES, rows), F32), pltpu.VMEM((SUBLANES, rows), F32),
                        pltpu.VMEM((HEAD_DIM, rows), F32), pltpu.VMEM((SUBLANES, rows), F32),
                        pltpu.VMEM((SUBLANES, rows), F32)],
        compiler_params=_params(3), name="attn")(qr, kr, zm)


GLA_CB = 4
GLA_RB = GLA_CB * GLA_CHUNK
N_SUB = GLA_CHUNK // GLA_SUB


def _gla_body(q_ref, k_ref, v_ref, bc_ref, o_ref, st_ref, kf_ref, bs_ref, *, rev):
    @pl.when(pl.program_id(1) == 0)
    def _():
        st_ref[...] = jnp.zeros_like(st_ref)

    C, SUB = GLA_CHUNK, GLA_SUB
    rowc = lax.broadcasted_iota(jnp.int32, (C, GLA_DK), 0)
    rows_s = lax.broadcasted_iota(jnp.int32, (SUB, 128), 0)
    lane_s = lax.broadcasted_iota(jnp.int32, (SUB, 128), 1)

    def chunk(ci, carry):
        c = (GLA_CB - 1 - ci) if rev else ci
        r0 = pl.multiple_of(c * C, C)
        for hh in range(GLA_HEADS):
            ksl = slice(hh * GLA_DK, (hh + 1) * GLA_DK)
            vsl = slice(hh * GLA_DV, (hh + 1) * GLA_DV)
            q = q_ref[pl.ds(r0, C), ksl].astype(F32) * (GLA_DK ** -0.5)
            k = k_ref[pl.ds(r0, C), ksl].astype(F32)
            v = v_ref[pl.ds(r0, C), vsl]
            bc = bc_ref[pl.ds(r0, C), ksl]
            kf_ref[hh] = k
            bs_ref[hh] = bc
            st = st_ref[hh]
            blast = bc[0:1] if rev else bc[C - 1:C]
            o_inter = _dot_nt((q * jnp.exp(bc)).astype(BF16), st.astype(BF16))
            kdec = k * jnp.exp(blast - bc)
            st_ref[hh] = st * jnp.exp(blast) + _dot_tn(v, kdec.astype(BF16))

            a_rows = []
            for si in range(N_SUB):
                lo, hi = si * SUB, (si + 1) * SUB
                q_s, b_s = q[lo:hi], bc[lo:hi]
                has_earlier = (si < N_SUB - 1) if rev else (si > 0)
                if has_earlier:
                    ref_row = bc[hi:hi + 1] if rev else bc[lo - 1:lo]
                    earlier = (rowc >= hi) if rev else (rowc < lo)
                    qt = q_s * jnp.exp(b_s - ref_row)
                    kt = k * jnp.exp(jnp.where(earlier, ref_row - bc, -jnp.inf))
                    a = _dot_nt(qt.astype(BF16), kt.astype(BF16))
                else:
                    a = jnp.zeros((SUB, C), F32)
                diag = jnp.zeros((SUB, 128), F32)
                for jl in range(SUB):
                    j = lo + jl
                    d = jnp.minimum(b_s - bs_ref[hh, j:j + 1, :], 0.0)
                    col = jnp.sum(q_s * kf_ref[hh, j:j + 1, :] * jnp.exp(d), axis=-1, keepdims=True)
                    diag = jnp.where(lane_s == j, col, diag)
                keep = (lane_s >= rows_s + lo) if rev else (lane_s <= rows_s + lo)
                diag = jnp.where(keep & (lane_s >= lo) & (lane_s < hi), diag, 0.0)
                a_rows.append(a + diag[:, :C])
            a_full = jnp.concatenate(a_rows, axis=0)
            o_ref[pl.ds(r0, C), vsl] = o_inter + _dot(a_full.astype(BF16), v)
        return carry

    lax.fori_loop(0, GLA_CB, chunk, 0)


def _gla(zm, bcum, rev):
    ncb = SEQ // GLA_RB
    if rev:
        row = lambda b, c: b * ncb + (ncb - 1 - c)
    else:
        row = lambda b, c: b * ncb + c
    return pl.pallas_call(
        functools.partial(_gla_body, rev=rev), grid=(BATCH, ncb),
        in_specs=[pl.BlockSpec((GLA_RB, GLA_K), lambda b, c: (row(b, c), Z_QB // GLA_K)),
                  pl.BlockSpec((GLA_RB, GLA_K), lambda b, c: (row(b, c), Z_KB // GLA_K)),
                  pl.BlockSpec((GLA_RB, GLA_V), lambda b, c: (row(b, c), Z_VB // GLA_V)),
                  pl.BlockSpec((GLA_RB, GLA_K), lambda b, c: (row(b, c), 0))],
        out_specs=pl.BlockSpec((GLA_RB, GLA_V), lambda b, c: (row(b, c), 0)),
        out_shape=jax.ShapeDtypeStruct((TOK, GLA_V), F32),
        scratch_shapes=[pltpu.VMEM((GLA_HEADS, GLA_DV, GLA_DK), F32),
                        pltpu.VMEM((GLA_HEADS, GLA_CHUNK, GLA_DK), F32),
                        pltpu.VMEM((GLA_HEADS, GLA_CHUNK, GLA_DK), F32)],
        compiler_params=_params(2), name="gla_bwd" if rev else "gla_fwd")(zm, zm, zm, bcum)


MIX_TM = 512


def _onorm_body(of_ref, ob_ref, og_ref, gn_ref, o_ref):
    gn = gn_ref[...]
    for hh in range(GLA_HEADS):
        sl = slice(hh * GLA_DV, (hh + 1) * GLA_DV)
        x = of_ref[:, sl] + ob_ref[:, sl]
        g = og_ref[:, sl].astype(F32)
        xn = x * lax.rsqrt(jnp.mean(x * x, axis=-1, keepdims=True) + RMS_EPS) * gn
        o_ref[:, sl] = (xn * (g * _sigmoid(g))).astype(BF16)


def _merge_body(attn_ref, on_ref, ga_ref, gb_ref, wa_ref, wb_ref, m_ref):
    ya = _dot(attn_ref[...], wa_ref[...])
    yb = _dot(on_ref[...], wb_ref[...])
    m_ref[...] = (ga_ref[...].astype(F32) * ya + gb_ref[...].astype(F32) * yb).astype(BF16)


def _outproj_body(m_ref, h_ref, wo_ref, lg_ref, lb_ref, wr_ref, br_ref, h1_ref, idx_ref, gate_ref):
    mix = _dot(m_ref[...], wo_ref[...])
    h1 = _layer_norm(DN_ALPHA * h_ref[...] + mix, lg_ref[...], lb_ref[...])
    h1_ref[...] = h1

    logits = _dot_split(h1, wr_ref) + br_ref[...]
    lane = lax.broadcasted_iota(jnp.int32, logits.shape, 1)
    x = logits
    vals, idxs = [], []
    for _ in range(TOP_K):
        mx = jnp.max(x, axis=-1, keepdims=True)
        ix = jnp.min(jnp.where(x == mx, lane, 128), axis=-1, keepdims=True)
        vals.append(mx)
        idxs.append(ix)
        x = jnp.where(lane == ix, -jnp.inf, x)
    es = [jnp.exp(vv - vals[0]) for vv in vals]
    den = es[0] + es[1] + es[2] + es[3]
    idx_out = jnp.zeros(logits.shape, jnp.int32)
    gate_out = jnp.zeros(logits.shape, F32)
    for kk in range(TOP_K):
        idx_out = jnp.where(lane == kk, idxs[kk], idx_out)
        gate_out = jnp.where(lane == kk, es[kk] / den, gate_out)
    idx_ref[...] = idx_out
    gate_ref[...] = gate_out


def _mixer(attn, o_f, o_b, zm, zg, hf, wa, wb, wo, gn, lg, lb, wr, br):
    tm = MIX_TM
    const = lambda shape: pl.BlockSpec(shape, lambda i: (0,) * len(shape), pipeline_mode=pl.Buffered(1))
    row = lambda w, cb=0: pl.BlockSpec((tm, w), lambda i: (i, cb))
    grid = (TOK // tm,)
    onorm = pl.pallas_call(
        _onorm_body, grid=grid,
        in_specs=[row(GLA_V), row(GLA_V), row(GLA_V, Z_OG // GLA_V), const((1, GLA_DV))],
        out_specs=row(GLA_V), out_shape=jax.ShapeDtypeStruct((TOK, GLA_V), BF16),
        compiler_params=_params(1), name="onorm")(o_f, o_b, zm, gn)
    merged = pl.pallas_call(
        _merge_body, grid=grid,
        in_specs=[row(ATT_Q), row(GLA_V), row(D_MODEL, 0), row(D_MODEL, 1),
                  const((ATT_Q, D_MODEL)), const((GLA_V, D_MODEL))],
        out_specs=row(D_MODEL), out_shape=jax.ShapeDtypeStruct((TOK, D_MODEL), BF16),
        compiler_params=_params(1), name="merge")(attn, onorm, zg, zg, wa, wb)
    return pl.pallas_call(
        _outproj_body, grid=grid,
        in_specs=[row(D_MODEL), row(D_MODEL), const((D_MODEL, D_MODEL)), const((1, D_MODEL)), const((1, D_MODEL)),
                  const((2, D_MODEL, 128)), const((1, 128))],
        out_specs=[row(D_MODEL), row(128), row(128)],
        out_shape=[jax.ShapeDtypeStruct((TOK, D_MODEL), F32),
                   jax.ShapeDtypeStruct((TOK, 128), jnp.int32),
                   jax.ShapeDtypeStruct((TOK, 128), F32)],
        compiler_params=_params(1), name="outproj")(merged, hf, wo, lg, lb, wr, br)


GROUP_SIZES = (512, 256, 128)
GATHER_STEPS = SB_BLOCKS // (SB_ROWS // GROUP_SIZES[0])


def _expert_body(e_ref, nb_ref, nv_ref, tab_hbm, h1_hbm, wg_ref, wl_ref, wd_ref, bg_ref, bl_ref, bd_ref,
                 y_hbm, gbuf, xb_ref, acc_ref, wgb_ref, wlb_ref, wdb_ref, ids, sem_ids, sem_g, sem_s):
    del e_ref
    s = pl.program_id(0)
    j = pl.program_id(1)
    nblk = nb_ref[s]
    slot = s & 1
    has_next = s + 1 < pl.num_programs(0)
    nxt = jnp.minimum(s + 1, N_SB - 1)
    nblk_next = jnp.where(has_next, nb_ref[nxt], 0)
    tiles = MOE_BLOCK // SUBLANES

    def ids_copy(sb, sl):
        return pltpu.make_async_copy(tab_hbm.at[pl.ds(pl.multiple_of(sb * IDS_STRIDE, IDS_STRIDE), IDS_STRIDE)],
                                     ids.at[pl.ds(pl.multiple_of(sl * IDS_STRIDE, IDS_STRIDE), IDS_STRIDE)],
                                     sem_ids.at[sl])

    def gather_block(sl, b):
        id0 = sl * IDS_STRIDE + b * MOE_BLOCK

        def issue(t, c):
            for u in range(SUBLANES):
                tok = ids[id0 + t * SUBLANES + u]
                pltpu.make_async_copy(h1_hbm.at[pl.ds(tok >> 3, 1), pl.ds(tok & 7, 1), :],
                                      gbuf.at[pl.ds(b * tiles + t, 1), pl.ds(u, 1), :], sem_g).start()
            return c

        lax.fori_loop(0, tiles, issue, 0)

    def gather_wait_block(b):
        pltpu.make_async_copy(h1_hbm.at[pl.ds(0, tiles)], gbuf.at[pl.ds(b * tiles, tiles)], sem_g).wait()

    def loop_blocks(n, fn):
        def body(b, c):
            fn(b)
            return c

        lax.fori_loop(0, n, body, 0)

    @pl.when((s == 0) & (j == 0))
    def _():
        ids_copy(0, 0).start()
        ids_copy(0, 0).wait()
        loop_blocks(nblk, lambda b: gather_block(0, b))

    big = GROUP_SIZES[0]
    big_blocks = big // MOE_BLOCK
    n_big = nblk // big_blocks
    in_window = (j >= 1) & (j <= GATHER_STEPS) & has_next
    issued_here = GATHER_STEPS * n_big
    issued_prev = jnp.where(s > 0, GATHER_STEPS * (nb_ref[jnp.maximum(s - 1, 0)] // big_blocks), 0)

    @pl.when(j == 0)
    def _():
        loop_blocks(jnp.maximum(issued_prev, nblk), gather_wait_block)

    @pl.when(nblk > 0)
    def _():
        @pl.when(j == 0)
        def _():
            @pl.when(has_next)
            def _():
                ids_copy(nxt, 1 - slot).start()

            def take_block(b):
                x = gbuf[pl.ds(b * tiles, tiles)].reshape(MOE_BLOCK, D_MODEL)
                xb_ref[pl.ds(pl.multiple_of(b * MOE_BLOCK, MOE_BLOCK), MOE_BLOCK), :] = x.astype(BF16)
                acc_ref[pl.ds(b * tiles, tiles)] = jnp.broadcast_to(bd_ref[0], (tiles, SUBLANES, D_MODEL))

            loop_blocks(nblk, take_block)

        @pl.when((j == 1) & has_next)
        def _():
            ids_copy(nxt, 1 - slot).wait()

        @pl.when(j == GATHER_STEPS + 1)
        def _():
            for b in range(SB_BLOCKS):
                @pl.when((b >= issued_here) & (b < nblk_next))
                def _(b=b):
                    gather_block(1 - slot, b)

        wgb_ref[...] = wg_ref[0].astype(BF16)
        wlb_ref[...] = wl_ref[0].astype(BF16)
        wdb_ref[...] = wd_ref[0].astype(BF16)

        def group(r0, size):
            x = xb_ref[pl.ds(pl.multiple_of(r0, MOE_BLOCK), size), :]
            g = jnp.minimum(_dot(x, wgb_ref[...]) + bg_ref[0], SWIGLU_LIMIT)
            lin = jnp.clip(_dot(x, wlb_ref[...]) + bl_ref[0], -SWIGLU_LIMIT, SWIGLU_LIMIT)
            act = g * _sigmoid(SWIGLU_ALPHA * g) * (lin + 1.0)
            upd = _dot(act.astype(BF16), wdb_ref[...])
            acc_ref[pl.ds(r0 // SUBLANES, size // SUBLANES)] += upd.reshape(size // SUBLANES, SUBLANES, D_MODEL)

        def big_group(i, c):
            group(i * big, big)
            return c

        def big_group_and_gather(i, c):
            group(i * big, big)
            b = (j - 1) * n_big + i
            id0 = (1 - slot) * IDS_STRIDE + b * MOE_BLOCK
            for t in range(tiles):
                for u in range(SUBLANES):
                    tok = ids[id0 + t * SUBLANES + u]
                    pltpu.make_async_copy(h1_hbm.at[pl.ds(tok >> 3, 1), pl.ds(tok & 7, 1), :],
                                          gbuf.at[pl.ds(b * tiles + t, 1), pl.ds(u, 1), :], sem_g).start()
            return c

        @pl.when(in_window)
        def _():
            lax.fori_loop(0, n_big, big_group_and_gather, 0)

        @pl.when(jnp.logical_not(in_window))
        def _():
            lax.fori_loop(0, n_big, big_group, 0)

        done = n_big * big_blocks
        for size in GROUP_SIZES[1:]:
            take = ((nblk - done) // (size // MOE_BLOCK)) > 0

            @pl.when(take)
            def _(done=done, size=size):
                group(done * MOE_BLOCK, size)

            done = done + jnp.where(take, size // MOE_BLOCK, 0)

        @pl.when(j == N_FF_TILES - 1)
        def _():
            nvalid = nv_ref[s]

            dst0 = slot * IDS_STRIDE + SB_ROWS

            def row_copy(t, u, dst):
                return pltpu.make_async_copy(acc_ref.at[pl.ds(t, 1), pl.ds(u, 1), :],
                                             y_hbm.at[pl.ds(dst >> 3, 1), pl.ds(dst & 7, 1), :], sem_s)

            def issue_tile(t, c):
                for u in range(SUBLANES):
                    row_copy(t, u, ids[dst0 + t * SUBLANES + u]).start()
                return c

            def issue_row(r, c):
                row_copy(r >> 3, r & 7, ids[dst0 + r]).start()
                return c

            full_tiles = nvalid // SUBLANES
            lax.fori_loop(0, full_tiles, issue_tile, 0)
            lax.fori_loop(full_tiles * SUBLANES, nvalid, issue_row, 0)

            def wait_block(b, c):
                pltpu.make_async_copy(acc_ref.at[pl.ds(0, tiles)], y_hbm.at[pl.ds(0, tiles)], sem_s).wait()
                return c

            def wait_row(r, c):
                row_copy(0, 0, 0).wait()
                return c

            nfull = nvalid // MOE_BLOCK
            lax.fori_loop(0, nfull, wait_block, 0)
            lax.fori_loop(nfull * MOE_BLOCK, nvalid, wait_row, 0)


def _experts(n_sb, sb_e, sb_nblk, sb_nvalid, tab, h1, w_up, b_up, w_down, b_down):
    def jj(j, nb, s):
        return jnp.where(nb[s] > 0, j, N_FF_TILES - 1)

    grid_spec = pltpu.PrefetchScalarGridSpec(
        num_scalar_prefetch=3, grid=(n_sb, N_FF_TILES),
        in_specs=[pl.BlockSpec(memory_space=pl.ANY),
                  pl.BlockSpec(memory_space=pl.ANY),
                  pl.BlockSpec((1, D_MODEL, FF_TILE), lambda s, j, e, nb, nv: (e[s], 0, jj(j, nb, s))),
                  pl.BlockSpec((1, D_MODEL, FF_TILE), lambda s, j, e, nb, nv: (e[s], 0, N_FF_TILES + jj(j, nb, s))),
                  pl.BlockSpec((1, FF_TILE, D_MODEL), lambda s, j, e, nb, nv: (e[s], jj(j, nb, s), 0)),
                  pl.BlockSpec((1, 1, FF_TILE), lambda s, j, e, nb, nv: (e[s], 0, jj(j, nb, s))),
                  pl.BlockSpec((1, 1, FF_TILE), lambda s, j, e, nb, nv: (e[s], 0, N_FF_TILES + jj(j, nb, s))),
                  pl.BlockSpec((1, 1, D_MODEL), lambda s, j, e, nb, nv: (e[s], 0, 0))],
        out_specs=pl.BlockSpec(memory_space=pl.ANY),
        scratch_shapes=[pltpu.VMEM((SB_ROWS // SUBLANES, SUBLANES, D_MODEL), F32),
                        pltpu.VMEM((SB_ROWS, D_MODEL), BF16),
                        pltpu.VMEM((SB_ROWS // SUBLANES, SUBLANES, D_MODEL), F32),
                        pltpu.VMEM((D_MODEL, FF_TILE), BF16),
                        pltpu.VMEM((D_MODEL, FF_TILE), BF16),
                        pltpu.VMEM((FF_TILE, D_MODEL), BF16),
                        pltpu.SMEM((2 * IDS_STRIDE,), jnp.int32),
                        pltpu.SemaphoreType.DMA((2,)),
                        pltpu.SemaphoreType.DMA(()),
                        pltpu.SemaphoreType.DMA(())])
    y = pl.pallas_call(
        _expert_body, grid_spec=grid_spec,
        out_shape=jax.ShapeDtypeStruct((TOP_K * TOK // SUBLANES, SUBLANES, D_MODEL), F32),
        compiler_params=_params(2), name="experts")(
            sb_e, sb_nblk, sb_nvalid, tab, h1.reshape(TOK // SUBLANES, SUBLANES, D_MODEL), w_up, w_up, w_down,
            b_up.reshape(N_EXPERTS, 1, 2 * D_FF), b_up.reshape(N_EXPERTS, 1, 2 * D_FF),
            b_down.reshape(N_EXPERTS, 1, D_MODEL))
    return y.reshape(TOP_K * TOK, D_MODEL)


def _tail_body(y0_ref, y1_ref, y2_ref, y3_ref, gate_ref, h1_ref, p_ref, wg_ref, wp_ref,
               l2g_ref, l2b_ref, l3g_ref, l3b_ref, o_ref):
    gate = gate_ref[...]
    y = gate[:, 0:1] * y0_ref[...]
    for kk, y_ref in enumerate((y1_ref, y2_ref, y3_ref), start=1):
        y = y + gate[:, kk:kk + 1] * y_ref[...]
    h2 = _layer_norm(DN_ALPHA * h1_ref[...] + y, l2g_ref[...], l2b_ref[...])
    ple = _sigmoid(_dot(h2.astype(BF16), wg_ref[...])) * _dot(p_ref[...].astype(BF16), wp_ref[...])
    o_ref[...] = _layer_norm(DN_ALPHA * h2 + ple, l3g_ref[...], l3b_ref[...])


def _tail(y_slots, gates, h1, p2, wg, wp, l2g, l2b, l3g, l3b):
    tm = 256
    nt = TOK // tm
    const = lambda shape: pl.BlockSpec(shape, lambda i: (0,) * len(shape), pipeline_mode=pl.Buffered(1))
    row = lambda w: pl.BlockSpec((tm, w), lambda i: (i, 0))
    yspec = lambda kk: pl.BlockSpec((tm, D_MODEL), lambda i: (kk * nt + i, 0))
    vec = const((1, D_MODEL))
    return pl.pallas_call(
        _tail_body, grid=(nt,),
        in_specs=[yspec(0), yspec(1), yspec(2), yspec(3), row(128), row(D_MODEL), row(PLE_DIM),
                  const((D_MODEL, D_MODEL)), const((PLE_DIM, D_MODEL)), vec, vec, vec, vec],
        out_specs=row(D_MODEL),
        out_shape=jax.ShapeDtypeStruct((TOK, D_MODEL), F32),
        compiler_params=_params(1), name="tail")(y_slots, y_slots, y_slots, y_slots, gates, h1, p2, wg, wp,
                                                 l2g, l2b, l3g, l3b)


def _routing(top_idx):
    flat_e = top_idx.reshape(-1)
    experts = jnp.arange(N_EXPERTS, dtype=jnp.int32)
    onehot = (flat_e[:, None] == experts[None, :]).astype(jnp.int32)
    csum = jnp.cumsum(onehot, axis=0)
    rank = jnp.sum(onehot * csum, axis=1) - 1
    counts = csum[-1]
    padded = (counts + MOE_BLOCK - 1) // MOE_BLOCK * MOE_BLOCK
    padded_end = jnp.cumsum(padded)
    padded_start = padded_end - padded
    dest = (padded_start[flat_e] + rank).astype(jnp.int32)
    asg = jnp.arange(TOK * TOP_K, dtype=jnp.int32)
    asg_of_row = jnp.zeros((N_ROWS + SB_ROWS,), jnp.int32).at[dest].set(asg)
    tok = (asg_of_row // TOP_K).reshape(-1, MOE_BLOCK)
    dst = ((asg_of_row % TOP_K) * TOK + asg_of_row // TOP_K).reshape(-1, MOE_BLOCK)
    nb = padded // MOE_BLOCK
    n_sb = (nb + SB_BLOCKS - 1) // SB_BLOCKS
    sb_end = jnp.cumsum(n_sb)
    total = sb_end[-1]
    s = jnp.arange(N_SB, dtype=jnp.int32)
    s_eff = jnp.minimum(s, total - 1)
    e = jnp.minimum(jnp.sum((sb_end[None, :] <= s_eff[:, None]).astype(jnp.int32), axis=1), N_EXPERTS - 1)
    local = s_eff - (sb_end[e] - n_sb[e])
    valid = s < total
    sb_nblk = jnp.where(valid, jnp.clip(nb[e] - local * SB_BLOCKS, 0, SB_BLOCKS), 0).astype(jnp.int32)
    sb_blk = padded_start[e] // MOE_BLOCK + local * SB_BLOCKS
    sb_nvalid = jnp.where(valid, jnp.clip(counts[e] - local * SB_ROWS, 0, SB_ROWS), 0).astype(jnp.int32)
    blocks = sb_blk[:, None] + jnp.arange(SB_BLOCKS, dtype=jnp.int32)[None, :]
    tab = jnp.concatenate([tok[blocks].reshape(N_SB, SB_ROWS), dst[blocks].reshape(N_SB, SB_ROWS),
                           jnp.zeros((N_SB, IDS_STRIDE - 2 * SB_ROWS), jnp.int32)], axis=1).astype(jnp.int32).reshape(-1)
    return total.astype(jnp.int32), e.astype(jnp.int32), sb_nblk, sb_nvalid, tab


def _rope_tables():
    rows = SEQ // GRID_W
    row = jnp.repeat(jnp.arange(rows), GRID_W)
    col = jnp.tile(jnp.arange(GRID_W), rows)
    n_pairs = HEAD_DIM // 4
    inv_freq = ROPE_BASE ** (-jnp.arange(n_pairs, dtype=F32) / n_pairs)
    ang = jnp.concatenate([row[:, None] * inv_freq, col[:, None] * inv_freq], -1)
    cos_full = jnp.repeat(jnp.cos(ang), 2, axis=-1)
    sin = jnp.sin(ang)
    sin_signed = jnp.stack([-sin, sin], axis=-1).reshape(SEQ, HEAD_DIM)
    return cos_full, sin_signed


def kernel(x, p, in_ln_g, in_ln_b, w_in, q_norm, k_norm, w_lr_f, b_lr_f, w_lr_b, b_lr_b, gla_norm, w_br_a, w_br_b, w_o, ln1_g, ln1_b, w_router, b_router, w_up, b_up, w_down, b_down, ln2_g, ln2_b, w_ple_gate, w_ple_proj, ln3_g, ln3_b):
    assert x.shape == (BATCH, SEQ, D_MODEL) and w_in.shape[0] == DEPTH == 1
    wt = w_in[0].T
    hf, hb = _ln0(x.reshape(TOK, D_MODEL), in_ln_g, in_ln_b)

    zm = _inproj(hb, wt, MAIN_W // 512, Z_TILE_OF_W_TILE, False, True, "inproj_main")
    w_gate, w_lr = _gate_weights(wt)
    zg = _inproj(hb, w_gate, 2 * D_MODEL // 512, tuple(range(8)), True, False, "inproj_gates")

    w2 = jnp.zeros((128, 2 * GLA_K), F32)
    w2 = w2.at[:GLA_RANK, :GLA_K].set(w_lr_f[0]).at[GLA_RANK:LR_W, GLA_K:].set(w_lr_b[0])
    b2 = jnp.concatenate([b_lr_f[0], b_lr_b[0]]).reshape(1, -1)
    bc_f, bc_b = _decay(hb, w_lr, jnp.stack(_hi_lo(w2)), b2)

    cos_full, sin_signed = _rope_tables()
    qr, kr = _qkprep(zm, cos_full, sin_signed, q_norm[0], k_norm[0])
    attn = _attention(qr, kr, zm)
    o_f = _gla(zm, bc_f, False)
    o_b = _gla(zm, bc_b, True)

    wr = jnp.zeros((D_MODEL, 128), F32).at[:, :N_EXPERTS].set(w_router[0])
    br = jnp.full((1, 128), -jnp.inf, F32).at[0, :N_EXPERTS].set(b_router[0])
    vec = lambda v: v[0].reshape(1, -1)
    h1, idx_pad, gate_pad = _mixer(
        attn, o_f, o_b, zm, zg, hf, w_br_a[0].astype(BF16), w_br_b[0].astype(BF16), w_o[0].astype(BF16),
        vec(gla_norm), vec(ln1_g), vec(ln1_b), jnp.stack(_hi_lo(wr)), br)

    n_sb, sb_e, sb_nblk, sb_nvalid, tab = _routing(idx_pad[:, :TOP_K])
    y_slots = _experts(n_sb, sb_e, sb_nblk, sb_nvalid, tab, h1, w_up[0], b_up[0], w_down[0], b_down[0])

    out = _tail(y_slots, gate_pad, h1, p[0].reshape(TOK, PLE_DIM), w_ple_gate[0].astype(BF16),
                w_ple_proj[0].astype(BF16), vec(ln2_g), vec(ln2_b), vec(ln3_g), vec(ln3_b))
    return out.reshape(BATCH, SEQ, D_MODEL)
```

```python
import functools

import jax
import jax.numpy as jnp
from jax import lax
from jax.experimental import pallas as pl
from jax.experimental.pallas import tpu as pltpu

F32 = jnp.float32
BF16 = jnp.bfloat16

D_MODEL = 2048
BATCH = 2
SEQ = 4096
TOK = BATCH * SEQ
PLE_DIM = 256
GRID_W = 64
ATT_HEADS = 8
ATT_KV_HEADS = 2
ATT_GROUP = ATT_HEADS // ATT_KV_HEADS
HEAD_DIM = 128
ROPE_BASE = 10000.0
GLA_HEADS = 4
GLA_DK = 128
GLA_DV = 256
GLA_RANK = 16
GLA_TAU = 16.0
GLA_CHUNK = 64
GLA_SUB = 16
N_EXPERTS = 32
TOP_K = 4
D_FF = D_MODEL
SWIGLU_LIMIT = 7.0
SWIGLU_ALPHA = 1.702
MOE_BLOCK = 128
ATT_Q = ATT_HEADS * HEAD_DIM
ATT_KV = ATT_KV_HEADS * HEAD_DIM
GLA_K = GLA_HEADS * GLA_DK
GLA_V = GLA_HEADS * GLA_DV
MAIN_W = ATT_Q + 2 * ATT_KV + 2 * GLA_K + 2 * GLA_V
LR_W = 2 * GLA_RANK
DEPTH = 1
DN_ALPHA = (2 * DEPTH) ** 0.25
LN_EPS = 1e-5
RMS_EPS = 1e-6
LOG2_E = 1.4426950408889634

Z_TILE_OF_W_TILE = (0, 1, 6, 7, 8, 2, 3, 4, 5)
Z_Q, Z_VB, Z_OG, Z_KA, Z_VA, Z_QB, Z_KB = 0, 1024, 2048, 3072, 3328, 3584, 4096

VMEM_LIMIT = 56 * 1024 * 1024

N_ROWS = TOK * TOP_K + N_EXPERTS * MOE_BLOCK
N_BLOCKS = N_ROWS // MOE_BLOCK
SB_BLOCKS = 10
SB_ROWS = SB_BLOCKS * MOE_BLOCK
N_SB = (N_BLOCKS + (SB_BLOCKS - 1) * N_EXPERTS) // SB_BLOCKS
SUBLANES = 8
IDS_STRIDE = -(-2 * SB_ROWS // 1024) * 1024
FF_TILE = 256
N_FF_TILES = D_FF // FF_TILE


def _params(n_axes, vmem=None):
    return pltpu.CompilerParams(dimension_semantics=("arbitrary",) * n_axes,
                                vmem_limit_bytes=vmem or VMEM_LIMIT)


def _sigmoid(x):
    return 1.0 / (1.0 + jnp.exp(-x))


def _layer_norm(y, g, b):
    mu = jnp.mean(y, axis=-1, keepdims=True)
    yc = y - mu
    var = jnp.mean(yc * yc, axis=-1, keepdims=True)
    return yc * lax.rsqrt(var + LN_EPS) * g + b


def _dot(a, b):
    return jnp.dot(a, b, preferred_element_type=F32)


def _hi_lo(x):
    hi = x.astype(BF16)
    return hi, (x - hi.astype(F32)).astype(BF16)


def _dot_split(x, w_ref):
    hi, lo = _hi_lo(x)
    return _dot(hi, w_ref[0]) + _dot(hi, w_ref[1]) + _dot(lo, w_ref[0])


def _dot_nt(a, b):
    return lax.dot_general(a, b, (((1,), (1,)), ((), ())), preferred_element_type=F32)


def _dot_tn(a, b):
    return lax.dot_general(a, b, (((0,), (0,)), ((), ())), preferred_element_type=F32)


def _ln0_body(x_ref, g_ref, b_ref, hf_ref, hb_ref):
    y = _layer_norm(x_ref[...], g_ref[...], b_ref[...])
    hf_ref[...] = y
    hb_ref[...] = y.astype(BF16)


def _ln0(x2, g, b):
    tm = 256
    row = pl.BlockSpec((tm, D_MODEL), lambda i: (i, 0))
    vec = pl.BlockSpec((1, D_MODEL), lambda i: (0, 0))
    return pl.pallas_call(
        _ln0_body, grid=(TOK // tm,), in_specs=[row, vec, vec], out_specs=[row, row],
        out_shape=[jax.ShapeDtypeStruct((TOK, D_MODEL), F32), jax.ShapeDtypeStruct((TOK, D_MODEL), BF16)],
        compiler_params=_params(1), name="ln0")(x2, g.reshape(1, -1), b.reshape(1, -1))


def _inproj_body(perm_ref, a_ref, w_ref, o_ref, wb_ref, *, gate, w_transposed):
    del perm_ref

    @pl.when(pl.program_id(1) == 0)
    def _():
        w = w_ref[...].T if w_transposed else w_ref[...]
        wb_ref[...] = w.astype(BF16)

    acc = _dot(a_ref[...], wb_ref[...])
    if gate:
        acc = _sigmoid(acc)
    o_ref[...] = acc.astype(o_ref.dtype)


def _inproj(hb, w, n_tiles, tile_perm, gate, w_transposed, name):
    tm, tn = 1024, 512
    w_spec = (pl.BlockSpec((tn, D_MODEL), lambda n, m, p: (n, 0)) if w_transposed
              else pl.BlockSpec((D_MODEL, tn), lambda n, m, p: (0, n)))
    grid_spec = pltpu.PrefetchScalarGridSpec(
        num_scalar_prefetch=1, grid=(n_tiles, TOK // tm),
        in_specs=[pl.BlockSpec((tm, D_MODEL), lambda n, m, p: (m, 0)), w_spec],
        out_specs=pl.BlockSpec((tm, tn), lambda n, m, p: (m, p[n])),
        scratch_shapes=[pltpu.VMEM((D_MODEL, tn), BF16)])
    return pl.pallas_call(
        functools.partial(_inproj_body, gate=gate, w_transposed=w_transposed), grid_spec=grid_spec,
        out_shape=jax.ShapeDtypeStruct((TOK, n_tiles * tn), BF16),
        compiler_params=_params(2), name=name)(jnp.asarray(tile_perm, jnp.int32), hb, w)


def _wprep_body(a_ref, b_ref, g_ref, lr_ref):
    g_ref[...] = jnp.concatenate([a_ref[LR_W:, :], b_ref[...]], axis=0).T.astype(BF16)

    @pl.when(pl.program_id(0) == 0)
    def _():
        lane = lax.broadcasted_iota(jnp.int32, (D_MODEL, 128), 1)
        lr_ref[...] = jnp.where(lane < LR_W, a_ref[:128, :].T, 0.0).astype(BF16)


def _gate_weights(wt):
    tn = 512
    first = MAIN_W // tn
    return pl.pallas_call(
        _wprep_body, grid=(2 * D_MODEL // tn,),
        in_specs=[pl.BlockSpec((tn, D_MODEL), lambda i: (first + i, 0)),
                  pl.BlockSpec((LR_W, D_MODEL), lambda i: ((first + i + 1) * (tn // LR_W), 0))],
        out_specs=[pl.BlockSpec((D_MODEL, tn), lambda i: (0, i)), pl.BlockSpec((D_MODEL, 128), lambda i: (0, 0))],
        out_shape=[jax.ShapeDtypeStruct((D_MODEL, 2 * D_MODEL), BF16), jax.ShapeDtypeStruct((D_MODEL, 128), BF16)],
        compiler_params=_params(1), name="gate_w")(wt, wt)


def _decay_body(h_ref, wlr_ref, w2_ref, b2_ref, trif_ref, trib_ref, bf_ref, bb_ref):
    zlr = _dot(h_ref[...], wlr_ref[...])
    pre = _dot_split(zlr, w2_ref) + b2_ref[...]
    la = (jnp.minimum(pre, 0.0) - jnp.log1p(jnp.exp(-jnp.abs(pre)))) * (1.0 / GLA_TAU)
    hi = la.astype(BF16)
    lo = (la - hi.astype(F32)).astype(BF16)
    bf_ref[...] = _dot(trif_ref[...], hi[:, :GLA_K]) + _dot(trif_ref[...], lo[:, :GLA_K])
    bb_ref[...] = _dot(trib_ref[...], hi[:, GLA_K:]) + _dot(trib_ref[...], lo[:, GLA_K:])


def _decay(hb, wlr, w2, b2):
    tm = 512
    r = jnp.arange(tm)
    same = (r[:, None] // GLA_CHUNK) == (r[None, :] // GLA_CHUNK)
    trif = (same & (r[None, :] <= r[:, None])).astype(BF16)
    trib = (same & (r[None, :] >= r[:, None])).astype(BF16)
    full = lambda shape: pl.BlockSpec(shape, lambda i: (0,) * len(shape))
    out = pl.BlockSpec((tm, GLA_K), lambda i: (i, 0))
    return pl.pallas_call(
        _decay_body, grid=(TOK // tm,),
        in_specs=[pl.BlockSpec((tm, D_MODEL), lambda i: (i, 0)), full((D_MODEL, 128)),
                  full((2, 128, 2 * GLA_K)), full((1, 2 * GLA_K)), full((tm, tm)), full((tm, tm))],
        out_specs=[out, out],
        out_shape=[jax.ShapeDtypeStruct((TOK, GLA_K), F32)] * 2,
        compiler_params=_params(1), name="decay")(hb, wlr, w2, b2, trif, trib)


def _qkprep_body(q_ref, k_ref, cos_ref, sin_ref, qn_ref, kn_ref, qo_ref, ko_ref):
    cos = cos_ref[...]
    sin = sin_ref[...]
    lane = lax.broadcasted_iota(jnp.int32, cos.shape, 1)
    even = (lane % 2) == 0

    def one(x, gain, scale):
        x = x.astype(F32)
        x = x * lax.rsqrt(jnp.mean(x * x, axis=-1, keepdims=True) + RMS_EPS) * gain
        partner = jnp.where(even, pltpu.roll(x, HEAD_DIM - 1, 1), pltpu.roll(x, 1, 1))
        return ((x * cos + partner * sin) * scale).astype(BF16)

    for hh in range(ATT_HEADS):
        sl = slice(hh * HEAD_DIM, (hh + 1) * HEAD_DIM)
        qo_ref[:, sl] = one(q_ref[:, sl], qn_ref[...], LOG2_E * HEAD_DIM ** -0.5)
    for hh in range(ATT_KV_HEADS):
        sl = slice(hh * HEAD_DIM, (hh + 1) * HEAD_DIM)
        ko_ref[:, sl] = one(k_ref[:, sl], kn_ref[...], 1.0)


def _qkprep(zm, cos_full, sin_signed, q_norm, k_norm):
    tm = 256
    nrow = SEQ // tm
    tab = pl.BlockSpec((tm, HEAD_DIM), lambda i: (i % nrow, 0))
    vec = pl.BlockSpec((1, HEAD_DIM), lambda i: (0, 0))
    return pl.pallas_call(
        _qkprep_body, grid=(TOK // tm,),
        in_specs=[pl.BlockSpec((tm, ATT_Q), lambda i: (i, Z_Q // ATT_Q)),
                  pl.BlockSpec((tm, ATT_KV), lambda i: (i, Z_KA // ATT_KV)), tab, tab, vec, vec],
        out_specs=[pl.BlockSpec((tm, ATT_Q), lambda i: (i, 0)), pl.BlockSpec((tm, ATT_KV), lambda i: (i, 0))],
        out_shape=[jax.ShapeDtypeStruct((TOK, ATT_Q), BF16), jax.ShapeDtypeStruct((TOK, ATT_KV), BF16)],
        compiler_params=_params(1), name="qkprep")(zm, zm, cos_full, sin_signed,
                                                   q_norm.reshape(1, -1), k_norm.reshape(1, -1))


ATT_TQ = 256
ATT_TK = 512


def _attn_body(q_ref, k_ref, v_ref, o_ref, qs_ref, s0_ref, s1_ref, p0_ref, p1_ref, a0_ref, a1_ref,
               acc_ref, m_ref, l_ref):
    rows = ATT_GROUP * ATT_TQ
    n_chunks = SEQ // ATT_TK
    groups = ATT_TK // SUBLANES
    for g in range(ATT_GROUP):
        qs_ref[g * ATT_TQ:(g + 1) * ATT_TQ, :] = q_ref[:, g * HEAD_DIM:(g + 1) * HEAD_DIM]
    m_ref[...] = jnp.full(m_ref.shape, -jnp.inf, F32)
    l_ref[...] = jnp.zeros(l_ref.shape, F32)
    acc_ref[...] = jnp.zeros(acc_ref.shape, F32)
    p1_ref[...] = jnp.zeros(p1_ref.shape, BF16)
    a1_ref[...] = jnp.zeros(a1_ref.shape, F32)

    def chunk(c):
        return pl.ds(pl.multiple_of(c * ATT_TK, ATT_TK), ATT_TK)

    lane_blocks = [slice(cb * HEAD_DIM, (cb + 1) * HEAD_DIM) for cb in range(rows // HEAD_DIM)]

    def scores(kc, s_out):
        qk = _dot_nt(k_ref[kc, :], qs_ref[...])
        for i, cb in enumerate(lane_blocks):
            s_out[i] = qk[:, cb]

    def probs_t(p_in):
        return jnp.concatenate([p_in[i] for i in range(len(lane_blocks))], axis=1)

    scores(slice(0, ATT_TK), s0_ref)

    def step(c, s_cur, s_nxt, p_cur, p_prv, a_cur, a_prv):
        c_next = jnp.where(c + 1 < n_chunks, c + 1, 0)
        c_prev = jnp.where(c > 0, c - 1, 0)
        scores(chunk(c_next), s_nxt)
        acc_ref[...] = a_prv[0:1, :] * acc_ref[...] + _dot_tn(v_ref[chunk(c_prev), :], probs_t(p_prv))
        m_new = []
        for i, cb in enumerate(lane_blocks):
            part = jnp.max(s_cur[i].reshape(groups, SUBLANES, HEAD_DIM), axis=0)
            m_new.append(jnp.maximum(m_ref[0:1, cb], jnp.max(part, axis=0, keepdims=True)))
        for i, (cb, mn) in enumerate(zip(lane_blocks, m_new)):
            alpha = jnp.exp2(m_ref[0:1, cb] - mn)
            p = jnp.exp2(s_cur[i] - mn)
            l_ref[:, cb] = alpha * l_ref[:, cb] + jnp.sum(p.reshape(groups, SUBLANES, HEAD_DIM), axis=0)
            a_cur[:, cb] = jnp.broadcast_to(alpha, (SUBLANES, HEAD_DIM))
            p_cur[i] = p.astype(BF16)
        for cb, mn in zip(lane_blocks, m_new):
            m_ref[:, cb] = jnp.broadcast_to(mn, (SUBLANES, HEAD_DIM))

    def pair(i, carry):
        step(2 * i, s0_ref, s1_ref, p0_ref, p1_ref, a0_ref, a1_ref)
        step(2 * i + 1, s1_ref, s0_ref, p1_ref, p0_ref, a1_ref, a0_ref)
        return carry

    lax.fori_loop(0, n_chunks // 2, pair, 0)
    acc = a1_ref[0:1, :] * acc_ref[...] + _dot_tn(v_ref[SEQ - ATT_TK:SEQ, :], probs_t(p1_ref))
    out = (acc / jnp.sum(l_ref[...], axis=0, keepdims=True)).T.astype(BF16)
    for g in range(ATT_GROUP):
        o_ref[:, g * HEAD_DIM:(g + 1) * HEAD_DIM] = out[g * ATT_TQ:(g + 1) * ATT_TQ]


def _attention(qr, kr, zm):
    nq = SEQ // ATT_TQ
    gw = ATT_GROUP * HEAD_DIM
    rows = ATT_GROUP * ATT_TQ
    qspec = pl.BlockSpec((ATT_TQ, gw), lambda b, j, i: (b * nq + i, j))
    return pl.pallas_call(
        _attn_body, grid=(BATCH, ATT_KV_HEADS, nq),
        in_specs=[qspec,
                  pl.BlockSpec((SEQ, HEAD_DIM), lambda b, j, i: (b, j)),
                  pl.BlockSpec((SEQ, HEAD_DIM), lambda b, j, i: (b, Z_VA // HEAD_DIM + j))],
        out_specs=qspec,
        out_shape=jax.ShapeDtypeStruct((TOK, ATT_Q), BF16),
        scratch_shapes=[pltpu.VMEM((rows, HEAD_DIM), BF16),
                        pltpu.VMEM((rows // HEAD_DIM, ATT_TK, HEAD_DIM), F32),
                        pltpu.VMEM((rows // HEAD_DIM, ATT_TK, HEAD_DIM), F32),
                        pltpu.VMEM((rows // HEAD_DIM, ATT_TK, HEAD_DIM), BF16),
                        pltpu.VMEM((rows // HEAD_DIM, ATT_TK, HEAD_DIM), BF16),
                        pltpu.VMEM((SUBLANES, rows), F32), pltpu.VMEM((SUBLANES, rows), F32),
                        pltpu.VMEM((HEAD_DIM, rows), F32), pltpu.VMEM((SUBLANES, rows), F32),
                        pltpu.VMEM((SUBLANES, rows), F32)],
        compiler_params=_params(3), name="attn")(qr, kr, zm)


GLA_CB = 4
GLA_RB = GLA_CB * GLA_CHUNK
N_SUB = GLA_CHUNK // GLA_SUB


def _gla_body(q_ref, k_ref, v_ref, bc_ref, o_ref, st_ref, kf_ref, bs_ref, *, rev):
    @pl.when(pl.program_id(1) == 0)
    def _():
        st_ref[...] = jnp.zeros_like(st_ref)

    C, SUB = GLA_CHUNK, GLA_SUB
    rowc = lax.broadcasted_iota(jnp.int32, (C, GLA_DK), 0)
    rows_s = lax.broadcasted_iota(jnp.int32, (SUB, 128), 0)
    lane_s = lax.broadcasted_iota(jnp.int32, (SUB, 128), 1)

    def chunk(ci, carry):
        c = (GLA_CB - 1 - ci) if rev else ci
        r0 = pl.multiple_of(c * C, C)
        for hh in range(GLA_HEADS):
            ksl = slice(hh * GLA_DK, (hh + 1) * GLA_DK)
            vsl = slice(hh * GLA_DV, (hh + 1) * GLA_DV)
            q = q_ref[pl.ds(r0, C), ksl].astype(F32) * (GLA_DK ** -0.5)
            k = k_ref[pl.ds(r0, C), ksl].astype(F32)
            v = v_ref[pl.ds(r0, C), vsl]
            bc = bc_ref[pl.ds(r0, C), ksl]
            kf_ref[hh] = k
            bs_ref[hh] = bc
            st = st_ref[hh]
            blast = bc[0:1] if rev else bc[C - 1:C]
            o_inter = _dot_nt((q * jnp.exp(bc)).astype(BF16), st.astype(BF16))
            kdec = k * jnp.exp(blast - bc)
            st_ref[hh] = st * jnp.exp(blast) + _dot_tn(v, kdec.astype(BF16))

            a_rows = []
            for si in range(N_SUB):
                lo, hi = si * SUB, (si + 1) * SUB
                q_s, b_s = q[lo:hi], bc[lo:hi]
                has_earlier = (si < N_SUB - 1) if rev else (si > 0)
                if has_earlier:
                    ref_row = bc[hi:hi + 1] if rev else bc[lo - 1:lo]
                    earlier = (rowc >= hi) if rev else (rowc < lo)
                    qt = q_s * jnp.exp(b_s - ref_row)
                    kt = k * jnp.exp(jnp.where(earlier, ref_row - bc, -jnp.inf))
                    a = _dot_nt(qt.astype(BF16), kt.astype(BF16))
                else:
                    a = jnp.zeros((SUB, C), F32)
                diag = jnp.zeros((SUB, 128), F32)
                for jl in range(SUB):
                    j = lo + jl
                    d = jnp.minimum(b_s - bs_ref[hh, j:j + 1, :], 0.0)
                    col = jnp.sum(q_s * kf_ref[hh, j:j + 1, :] * jnp.exp(d), axis=-1, keepdims=True)
                    diag = jnp.where(lane_s == j, col, diag)
                keep = (lane_s >= rows_s + lo) if rev else (lane_s <= rows_s + lo)
                diag = jnp.where(keep & (lane_s >= lo) & (lane_s < hi), diag, 0.0)
                a_rows.append(a + diag[:, :C])
            a_full = jnp.concatenate(a_rows, axis=0)
            o_ref[pl.ds(r0, C), vsl] = o_inter + _dot(a_full.astype(BF16), v)
        return carry

    lax.fori_loop(0, GLA_CB, chunk, 0)


def _gla(zm, bcum, rev):
    ncb = SEQ // GLA_RB
    if rev:
        row = lambda b, c: b * ncb + (ncb - 1 - c)
    else:
        row = lambda b, c: b * ncb + c
    return pl.pallas_call(
        functools.partial(_gla_body, rev=rev), grid=(BATCH, ncb),
        in_specs=[pl.BlockSpec((GLA_RB, GLA_K), lambda b, c: (row(b, c), Z_QB // GLA_K)),
                  pl.BlockSpec((GLA_RB, GLA_K), lambda b, c: (row(b, c), Z_KB // GLA_K)),
                  pl.BlockSpec((GLA_RB, GLA_V), lambda b, c: (row(b, c), Z_VB // GLA_V)),
                  pl.BlockSpec((GLA_RB, GLA_K), lambda b, c: (row(b, c), 0))],
        out_specs=pl.BlockSpec((GLA_RB, GLA_V), lambda b, c: (row(b, c), 0)),
        out_shape=jax.ShapeDtypeStruct((TOK, GLA_V), F32),
        scratch_shapes=[pltpu.VMEM((GLA_HEADS, GLA_DV, GLA_DK), F32),
                        pltpu.VMEM((GLA_HEADS, GLA_CHUNK, GLA_DK), F32),
                        pltpu.VMEM((GLA_HEADS, GLA_CHUNK, GLA_DK), F32)],
        compiler_params=_params(2), name="gla_bwd" if rev else "gla_fwd")(zm, zm, zm, bcum)


MIX_TM = 512


def _onorm_body(of_ref, ob_ref, og_ref, gn_ref, o_ref):
    gn = gn_ref[...]
    for hh in range(GLA_HEADS):
        sl = slice(hh * GLA_DV, (hh + 1) * GLA_DV)
        x = of_ref[:, sl] + ob_ref[:, sl]
        g = og_ref[:, sl].astype(F32)
        xn = x * lax.rsqrt(jnp.mean(x * x, axis=-1, keepdims=True) + RMS_EPS) * gn
        o_ref[:, sl] = (xn * (g * _sigmoid(g))).astype(BF16)


def _merge_body(attn_ref, on_ref, ga_ref, gb_ref, wa_ref, wb_ref, m_ref):
    ya = _dot(attn_ref[...], wa_ref[...])
    yb = _dot(on_ref[...], wb_ref[...])
    m_ref[...] = (ga_ref[...].astype(F32) * ya + gb_ref[...].astype(F32) * yb).astype(BF16)


def _outproj_body(m_ref, h_ref, wo_ref, lg_ref, lb_ref, wr_ref, br_ref, h1_ref, idx_ref, gate_ref):
    mix = _dot(m_ref[...], wo_ref[...])
    h1 = _layer_norm(DN_ALPHA * h_ref[...] + mix, lg_ref[...], lb_ref[...])
    h1_ref[...] = h1

    logits = _dot_split(h1, wr_ref) + br_ref[...]
    lane = lax.broadcasted_iota(jnp.int32, logits.shape, 1)
    x = logits
    vals, idxs = [], []
    for _ in range(TOP_K):
        mx = jnp.max(x, axis=-1, keepdims=True)
        ix = jnp.min(jnp.where(x == mx, lane, 128), axis=-1, keepdims=True)
        vals.append(mx)
        idxs.append(ix)
        x = jnp.where(lane == ix, -jnp.inf, x)
    es = [jnp.exp(vv - vals[0]) for vv in vals]
    den = es[0] + es[1] + es[2] + es[3]
    idx_out = jnp.zeros(logits.shape, jnp.int32)
    gate_out = jnp.zeros(logits.shape, F32)
    for kk in range(TOP_K):
        idx_out = jnp.where(lane == kk, idxs[kk], idx_out)
        gate_out = jnp.where(lane == kk, es[kk] / den, gate_out)
    idx_ref[...] = idx_out
    gate_ref[...] = gate_out


def _mixer(attn, o_f, o_b, zm, zg, hf, wa, wb, wo, gn, lg, lb, wr, br):
    tm = MIX_TM
    const = lambda shape: pl.BlockSpec(shape, lambda i: (0,) * len(shape), pipeline_mode=pl.Buffered(1))
    row = lambda w, cb=0: pl.BlockSpec((tm, w), lambda i: (i, cb))
    grid = (TOK // tm,)
    onorm = pl.pallas_call(
        _onorm_body, grid=grid,
        in_specs=[row(GLA_V), row(GLA_V), row(GLA_V, Z_OG // GLA_V), const((1, GLA_DV))],
        out_specs=row(GLA_V), out_shape=jax.ShapeDtypeStruct((TOK, GLA_V), BF16),
        compiler_params=_params(1), name="onorm")(o_f, o_b, zm, gn)
    merged = pl.pallas_call(
        _merge_body, grid=grid,
        in_specs=[row(ATT_Q), row(GLA_V), row(D_MODEL, 0), row(D_MODEL, 1),
                  const((ATT_Q, D_MODEL)), const((GLA_V, D_MODEL))],
        out_specs=row(D_MODEL), out_shape=jax.ShapeDtypeStruct((TOK, D_MODEL), BF16),
        compiler_params=_params(1), name="merge")(attn, onorm, zg, zg, wa, wb)
    return pl.pallas_call(
        _outproj_body, grid=grid,
        in_specs=[row(D_MODEL), row(D_MODEL), const((D_MODEL, D_MODEL)), const((1, D_MODEL)), const((1, D_MODEL)),
                  const((2, D_MODEL, 128)), const((1, 128))],
        out_specs=[row(D_MODEL), row(128), row(128)],
        out_shape=[jax.ShapeDtypeStruct((TOK, D_MODEL), F32),
                   jax.ShapeDtypeStruct((TOK, 128), jnp.int32),
                   jax.ShapeDtypeStruct((TOK, 128), F32)],
        compiler_params=_params(1), name="outproj")(merged, hf, wo, lg, lb, wr, br)


GROUP_SIZES = (512, 256, 128)
GATHER_STEPS = SB_BLOCKS // (SB_ROWS // GROUP_SIZES[0])


def _expert_body(e_ref, nb_ref, nv_ref, tab_hbm, h1_hbm, wg_ref, wl_ref, wd_ref, bg_ref, bl_ref, bd_ref,
                 y_hbm, gbuf, xb_ref, acc_ref, wgb_ref, wlb_ref, wdb_ref, ids, sem_ids, sem_g, sem_s):
    del e_ref
    s = pl.program_id(0)
    j = pl.program_id(1)
    nblk = nb_ref[s]
    slot = s & 1
    has_next = s + 1 < pl.num_programs(0)
    nxt = jnp.minimum(s + 1, N_SB - 1)
    nblk_next = jnp.where(has_next, nb_ref[nxt], 0)
    tiles = MOE_BLOCK // SUBLANES

    def ids_copy(sb, sl):
        return pltpu.make_async_copy(tab_hbm.at[pl.ds(pl.multiple_of(sb * IDS_STRIDE, IDS_STRIDE), IDS_STRIDE)],
                                     ids.at[pl.ds(pl.multiple_of(sl * IDS_STRIDE, IDS_STRIDE), IDS_STRIDE)],
                                     sem_ids.at[sl])

    def gather_block(sl, b):
        id0 = sl * IDS_STRIDE + b * MOE_BLOCK

        def issue(t, c):
            for u in range(SUBLANES):
                tok = ids[id0 + t * SUBLANES + u]
                pltpu.make_async_copy(h1_hbm.at[pl.ds(tok >> 3, 1), pl.ds(tok & 7, 1), :],
                                      gbuf.at[pl.ds(b * tiles + t, 1), pl.ds(u, 1), :], sem_g).start()
            return c

        lax.fori_loop(0, tiles, issue, 0)

    def gather_wait_block(b):
        pltpu.make_async_copy(h1_hbm.at[pl.ds(0, tiles)], gbuf.at[pl.ds(b * tiles, tiles)], sem_g).wait()

    def loop_blocks(n, fn):
        def body(b, c):
            fn(b)
            return c

        lax.fori_loop(0, n, body, 0)

    @pl.when((s == 0) & (j == 0))
    def _():
        ids_copy(0, 0).start()
        ids_copy(0, 0).wait()
        loop_blocks(nblk, lambda b: gather_block(0, b))

    big = GROUP_SIZES[0]
    big_blocks = big // MOE_BLOCK
    n_big = nblk // big_blocks
    in_window = (j >= 1) & (j <= GATHER_STEPS) & has_next
    issued_here = GATHER_STEPS * n_big
    issued_prev = jnp.where(s > 0, GATHER_STEPS * (nb_ref[jnp.maximum(s - 1, 0)] // big_blocks), 0)

    @pl.when(j == 0)
    def _():
        loop_blocks(jnp.maximum(issued_prev, nblk), gather_wait_block)

    @pl.when(nblk > 0)
    def _():
        @pl.when(j == 0)
        def _():
            @pl.when(has_next)
            def _():
                ids_copy(nxt, 1 - slot).start()

            def take_block(b):
                x = gbuf[pl.ds(b * tiles, tiles)].reshape(MOE_BLOCK, D_MODEL)
                xb_ref[pl.ds(pl.multiple_of(b * MOE_BLOCK, MOE_BLOCK), MOE_BLOCK), :] = x.astype(BF16)
                acc_ref[pl.ds(b * tiles, tiles)] = jnp.broadcast_to(bd_ref[0], (tiles, SUBLANES, D_MODEL))

            loop_blocks(nblk, take_block)

        @pl.when((j == 1) & has_next)
        def _():
            ids_copy(nxt, 1 - slot).wait()

        @pl.when(j == GATHER_STEPS + 1)
        def _():
            for b in range(SB_BLOCKS):
                @pl.when((b >= issued_here) & (b < nblk_next))
                def _(b=b):
                    gather_block(1 - slot, b)

        wgb_ref[...] = wg_ref[0].astype(BF16)
        wlb_ref[...] = wl_ref[0].astype(BF16)
        wdb_ref[...] = wd_ref[0].astype(BF16)

        def group(r0, size):
            x = xb_ref[pl.ds(pl.multiple_of(r0, MOE_BLOCK), size), :]
            g = jnp.minimum(_dot(x, wgb_ref[...]) + bg_ref[0], SWIGLU_LIMIT)
            lin = jnp.clip(_dot(x, wlb_ref[...]) + bl_ref[0], -SWIGLU_LIMIT, SWIGLU_LIMIT)
            act = g * _sigmoid(SWIGLU_ALPHA * g) * (lin + 1.0)
            upd = _dot(act.astype(BF16), wdb_ref[...])
            acc_ref[pl.ds(r0 // SUBLANES, size // SUBLANES)] += upd.reshape(size // SUBLANES, SUBLANES, D_MODEL)

        def big_group(i, c):
            group(i * big, big)
            return c

        def big_group_and_gather(i, c):
            group(i * big, big)
            b = (j - 1) * n_big + i
            id0 = (1 - slot) * IDS_STRIDE + b * MOE_BLOCK
            for t in range(tiles):
                for u in range(SUBLANES):
                    tok = ids[id0 + t * SUBLANES + u]
                    pltpu.make_async_copy(h1_hbm.at[pl.ds(tok >> 3, 1), pl.ds(tok & 7, 1), :],
                                          gbuf.at[pl.ds(b * tiles + t, 1), pl.ds(u, 1), :], sem_g).start()
            return c

        @pl.when(in_window)
        def _():
            lax.fori_loop(0, n_big, big_group_and_gather, 0)

        @pl.when(jnp.logical_not(in_window))
        def _():
            lax.fori_loop(0, n_big, big_group, 0)

        done = n_big * big_blocks
        for size in GROUP_SIZES[1:]:
            take = ((nblk - done) // (size // MOE_BLOCK)) > 0

            @pl.when(take)
            def _(done=done, size=size):
                group(done * MOE_BLOCK, size)

            done = done + jnp.where(take, size // MOE_BLOCK, 0)

        @pl.when(j == N_FF_TILES - 1)
        def _():
            nvalid = nv_ref[s]

            dst0 = slot * IDS_STRIDE + SB_ROWS

            def row_copy(t, u, dst):
                return pltpu.make_async_copy(acc_ref.at[pl.ds(t, 1), pl.ds(u, 1), :],
                                             y_hbm.at[pl.ds(dst >> 3, 1), pl.ds(dst & 7, 1), :], sem_s)

            def issue_tile(t, c):
                for u in range(SUBLANES):
                    row_copy(t, u, ids[dst0 + t * SUBLANES + u]).start()
                return c

            def issue_row(r, c):
                row_copy(r >> 3, r & 7, ids[dst0 + r]).start()
                return c

            full_tiles = nvalid // SUBLANES
            lax.fori_loop(0, full_tiles, issue_tile, 0)
            lax.fori_loop(full_tiles * SUBLANES, nvalid, issue_row, 0)

            def wait_block(b, c):
                pltpu.make_async_copy(acc_ref.at[pl.ds(0, tiles)], y_hbm.at[pl.ds(0, tiles)], sem_s).wait()
                return c

            def wait_row(r, c):
                row_copy(0, 0, 0).wait()
                return c

            nfull = nvalid // MOE_BLOCK
            lax.fori_loop(0, nfull, wait_block, 0)
            lax.fori_loop(nfull * MOE_BLOCK, nvalid, wait_row, 0)


def _experts(n_sb, sb_e, sb_nblk, sb_nvalid, tab, h1, w_up, b_up, w_down, b_down):
    def jj(j, nb, s):
        return jnp.where(nb[s] > 0, j, N_FF_TILES - 1)

    grid_spec = pltpu.PrefetchScalarGridSpec(
        num_scalar_prefetch=3, grid=(n_sb, N_FF_TILES),
        in_specs=[pl.BlockSpec(memory_space=pl.ANY),
                  pl.BlockSpec(memory_space=pl.ANY),
                  pl.BlockSpec((1, D_MODEL, FF_TILE), lambda s, j, e, nb, nv: (e[s], 0, jj(j, nb, s))),
                  pl.BlockSpec((1, D_MODEL, FF_TILE), lambda s, j, e, nb, nv: (e[s], 0, N_FF_TILES + jj(j, nb, s))),
                  pl.BlockSpec((1, FF_TILE, D_MODEL), lambda s, j, e, nb, nv: (e[s], jj(j, nb, s), 0)),
                  pl.BlockSpec((1, 1, FF_TILE), lambda s, j, e, nb, nv: (e[s], 0, jj(j, nb, s))),
                  pl.BlockSpec((1, 1, FF_TILE), lambda s, j, e, nb, nv: (e[s], 0, N_FF_TILES + jj(j, nb, s))),
                  pl.BlockSpec((1, 1, D_MODEL), lambda s, j, e, nb, nv: (e[s], 0, 0))],
        out_specs=pl.BlockSpec(memory_space=pl.ANY),
        scratch_shapes=[pltpu.VMEM((SB_ROWS // SUBLANES, SUBLANES, D_MODEL), F32),
                        pltpu.VMEM((SB_ROWS, D_MODEL), BF16),
                        pltpu.VMEM((SB_ROWS // SUBLANES, SUBLANES, D_MODEL), F32),
                        pltpu.VMEM((D_MODEL, FF_TILE), BF16),
                        pltpu.VMEM((D_MODEL, FF_TILE), BF16),
                        pltpu.VMEM((FF_TILE, D_MODEL), BF16),
                        pltpu.SMEM((2 * IDS_STRIDE,), jnp.int32),
                        pltpu.SemaphoreType.DMA((2,)),
                        pltpu.SemaphoreType.DMA(()),
                        pltpu.SemaphoreType.DMA(())])
    y = pl.pallas_call(
        _expert_body, grid_spec=grid_spec,
        out_shape=jax.ShapeDtypeStruct((TOP_K * TOK // SUBLANES, SUBLANES, D_MODEL), F32),
        compiler_params=_params(2), name="experts")(
            sb_e, sb_nblk, sb_nvalid, tab, h1.reshape(TOK // SUBLANES, SUBLANES, D_MODEL), w_up, w_up, w_down,
            b_up.reshape(N_EXPERTS, 1, 2 * D_FF), b_up.reshape(N_EXPERTS, 1, 2 * D_FF),
            b_down.reshape(N_EXPERTS, 1, D_MODEL))
    return y.reshape(TOP_K * TOK, D_MODEL)


def _tail_body(y0_ref, y1_ref, y2_ref, y3_ref, gate_ref, h1_ref, p_ref, wg_ref, wp_ref,
               l2g_ref, l2b_ref, l3g_ref, l3b_ref, o_ref):
    gate = gate_ref[...]
    y = gate[:, 0:1] * y0_ref[...]
    for kk, y_ref in enumerate((y1_ref, y2_ref, y3_ref), start=1):
        y = y + gate[:, kk:kk + 1] * y_ref[...]
    h2 = _layer_norm(DN_ALPHA * h1_ref[...] + y, l2g_ref[...], l2b_ref[...])
    ple = _sigmoid(_dot(h2.astype(BF16), wg_ref[...])) * _dot(p_ref[...].astype(BF16), wp_ref[...])
    o_ref[...] = _layer_norm(DN_ALPHA * h2 + ple, l3g_ref[...], l3b_ref[...])


def _tail(y_slots, gates, h1, p2, wg, wp, l2g, l2b, l3g, l3b):
    tm = 256
    nt = TOK // tm
    const = lambda shape: pl.BlockSpec(shape, lambda i: (0,) * len(shape), pipeline_mode=pl.Buffered(1))
    row = lambda w: pl.BlockSpec((tm, w), lambda i: (i, 0))
    yspec = lambda kk: pl.BlockSpec((tm, D_MODEL), lambda i: (kk * nt + i, 0))
    vec = const((1, D_MODEL))
    return pl.pallas_call(
        _tail_body, grid=(nt,),
        in_specs=[yspec(0), yspec(1), yspec(2), yspec(3), row(128), row(D_MODEL), row(PLE_DIM),
                  const((D_MODEL, D_MODEL)), const((PLE_DIM, D_MODEL)), vec, vec, vec, vec],
        out_specs=row(D_MODEL),
        out_shape=jax.ShapeDtypeStruct((TOK, D_MODEL), F32),
        compiler_params=_params(1), name="tail")(y_slots, y_slots, y_slots, y_slots, gates, h1, p2, wg, wp,
                                                 l2g, l2b, l3g, l3b)


def _routing(top_idx):
    flat_e = top_idx.reshape(-1)
    experts = jnp.arange(N_EXPERTS, dtype=jnp.int32)
    onehot = (flat_e[:, None] == experts[None, :]).astype(jnp.int32)
    csum = jnp.cumsum(onehot, axis=0)
    rank = jnp.sum(onehot * csum, axis=1) - 1
    counts = csum[-1]
    padded = (counts + MOE_BLOCK - 1) // MOE_BLOCK * MOE_BLOCK
    padded_end = jnp.cumsum(padded)
    padded_start = padded_end - padded
    dest = (padded_start[flat_e] + rank).astype(jnp.int32)
    asg = jnp.arange(TOK * TOP_K, dtype=jnp.int32)
    asg_of_row = jnp.zeros((N_ROWS + SB_ROWS,), jnp.int32).at[dest].set(asg)
    tok = (asg_of_row // TOP_K).reshape(-1, MOE_BLOCK)
    dst = ((asg_of_row % TOP_K) * TOK + asg_of_row // TOP_K).reshape(-1, MOE_BLOCK)
    nb = padded // MOE_BLOCK
    n_sb = (nb + SB_BLOCKS - 1) // SB_BLOCKS
    sb_end = jnp.cumsum(n_sb)
    total = sb_end[-1]
    s = jnp.arange(N_SB, dtype=jnp.int32)
    s_eff = jnp.minimum(s, total - 1)
    e = jnp.minimum(jnp.sum((sb_end[None, :] <= s_eff[:, None]).astype(jnp.int32), axis=1), N_EXPERTS - 1)
    local = s_eff - (sb_end[e] - n_sb[e])
    valid = s < total
    sb_nblk = jnp.where(valid, jnp.clip(nb[e] - local * SB_BLOCKS, 0, SB_BLOCKS), 0).astype(jnp.int32)
    sb_blk = padded_start[e] // MOE_BLOCK + local * SB_BLOCKS
    sb_nvalid = jnp.where(valid, jnp.clip(counts[e] - local * SB_ROWS, 0, SB_ROWS), 0).astype(jnp.int32)
    blocks = sb_blk[:, None] + jnp.arange(SB_BLOCKS, dtype=jnp.int32)[None, :]
    tab = jnp.concatenate([tok[blocks].reshape(N_SB, SB_ROWS), dst[blocks].reshape(N_SB, SB_ROWS),
                           jnp.zeros((N_SB, IDS_STRIDE - 2 * SB_ROWS), jnp.int32)], axis=1).astype(jnp.int32).reshape(-1)
    return total.astype(jnp.int32), e.astype(jnp.int32), sb_nblk, sb_nvalid, tab


def _rope_tables():
    rows = SEQ // GRID_W
    row = jnp.repeat(jnp.arange(rows), GRID_W)
    col = jnp.tile(jnp.arange(GRID_W), rows)
    n_pairs = HEAD_DIM // 4
    inv_freq = ROPE_BASE ** (-jnp.arange(n_pairs, dtype=F32) / n_pairs)
    ang = jnp.concatenate([row[:, None] * inv_freq, col[:, None] * inv_freq], -1)
    cos_full = jnp.repeat(jnp.cos(ang), 2, axis=-1)
    sin = jnp.sin(ang)
    sin_signed = jnp.stack([-sin, sin], axis=-1).reshape(SEQ, HEAD_DIM)
    return cos_full, sin_signed


def kernel(x, p, in_ln_g, in_ln_b, w_in, q_norm, k_norm, w_lr_f, b_lr_f, w_lr_b, b_lr_b, gla_norm, w_br_a, w_br_b, w_o, ln1_g, ln1_b, w_router, b_router, w_up, b_up, w_down, b_down, ln2_g, ln2_b, w_ple_gate, w_ple_proj, ln3_g, ln3_b):
    assert x.shape == (BATCH, SEQ, D_MODEL) and w_in.shape[0] == DEPTH == 1
    wt = w_in[0].T
    hf, hb = _ln0(x.reshape(TOK, D_MODEL), in_ln_g, in_ln_b)

    zm = _inproj(hb, wt, MAIN_W // 512, Z_TILE_OF_W_TILE, False, True, "inproj_main")
    w_gate, w_lr = _gate_weights(wt)
    zg = _inproj(hb, w_gate, 2 * D_MODEL // 512, tuple(range(8)), True, False, "inproj_gates")

    w2 = jnp.zeros((128, 2 * GLA_K), F32)
    w2 = w2.at[:GLA_RANK, :GLA_K].set(w_lr_f[0]).at[GLA_RANK:LR_W, GLA_K:].set(w_lr_b[0])
    b2 = jnp.concatenate([b_lr_f[0], b_lr_b[0]]).reshape(1, -1)
    bc_f, bc_b = _decay(hb, w_lr, jnp.stack(_hi_lo(w2)), b2)

    cos_full, sin_signed = _rope_tables()
    qr, kr = _qkprep(zm, cos_full, sin_signed, q_norm[0], k_norm[0])
    attn = _attention(qr, kr, zm)
    o_f = _gla(zm, bc_f, False)
    o_b = _gla(zm, bc_b, True)

    wr = jnp.zeros((D_MODEL, 128), F32).at[:, :N_EXPERTS].set(w_router[0])
    br = jnp.full((1, 128), -jnp.inf, F32).at[0, :N_EXPERTS].set(b_router[0])
    vec = lambda v: v[0].reshape(1, -1)
    h1, idx_pad, gate_pad = _mixer(
        attn, o_f, o_b, zm, zg, hf, w_br_a[0].astype(BF16), w_br_b[0].astype(BF16), w_o[0].astype(BF16),
        vec(gla_norm), vec(ln1_g), vec(ln1_b), jnp.stack(_hi_lo(wr)), br)

    n_sb, sb_e, sb_nblk, sb_nvalid, tab = _routing(idx_pad[:, :TOP_K])
    y_slots = _experts(n_sb, sb_e, sb_nblk, sb_nvalid, tab, h1, w_up[0], b_up[0], w_down[0], b_down[0])

    out = _tail(y_slots, gate_pad, h1, p[0].reshape(TOK, PLE_DIM), w_ple_gate[0].astype(BF16),
                w_ple_proj[0].astype(BF16), vec(ln2_g), vec(ln2_b), vec(ln3_g), vec(ln3_b))
    return out.reshape(BATCH, SEQ, D_MODEL)
```

```python
import functools

import jax
import jax.numpy as jnp
from jax import lax
from jax.experimental import pallas as pl
from jax.experimental.pallas import tpu as pltpu

F32 = jnp.float32
BF16 = jnp.bfloat16

D_MODEL = 2048
BATCH = 2
SEQ = 4096
TOK = BATCH * SEQ
PLE_DIM = 256
GRID_W = 64
ATT_HEADS = 8
ATT_KV_HEADS = 2
ATT_GROUP = ATT_HEADS // ATT_KV_HEADS
HEAD_DIM = 128
ROPE_BASE = 10000.0
GLA_HEADS = 4
GLA_DK = 128
GLA_DV = 256
GLA_RANK = 16
GLA_TAU = 16.0
GLA_CHUNK = 64
GLA_SUB = 16
N_EXPERTS = 32
TOP_K = 4
D_FF = D_MODEL
SWIGLU_LIMIT = 7.0
SWIGLU_ALPHA = 1.702
MOE_BLOCK = 128
ATT_Q = ATT_HEADS * HEAD_DIM
ATT_KV = ATT_KV_HEADS * HEAD_DIM
GLA_K = GLA_HEADS * GLA_DK
GLA_V = GLA_HEADS * GLA_DV
MAIN_W = ATT_Q + 2 * ATT_KV + 2 * GLA_K + 2 * GLA_V
LR_W = 2 * GLA_RANK
DEPTH = 1
DN_ALPHA = (2 * DEPTH) ** 0.25
LN_EPS = 1e-5
RMS_EPS = 1e-6
LOG2_E = 1.4426950408889634

Z_TILE_OF_W_TILE = (0, 1, 6, 7, 8, 2, 3, 4, 5)
Z_Q, Z_VB, Z_OG, Z_KA, Z_VA, Z_QB, Z_KB = 0, 1024, 2048, 3072, 3328, 3584, 4096

VMEM_LIMIT = 56 * 1024 * 1024

N_ROWS = TOK * TOP_K + N_EXPERTS * MOE_BLOCK
N_BLOCKS = N_ROWS // MOE_BLOCK
SB_BLOCKS = 10
SB_ROWS = SB_BLOCKS * MOE_BLOCK
N_SB = (N_BLOCKS + (SB_BLOCKS - 1) * N_EXPERTS) // SB_BLOCKS
SUBLANES = 8
IDS_STRIDE = -(-2 * SB_ROWS // 1024) * 1024
FF_TILE = 256
N_FF_TILES = D_FF // FF_TILE


def _params(n_axes, vmem=None):
    return pltpu.CompilerParams(dimension_semantics=("arbitrary",) * n_axes,
                                vmem_limit_bytes=vmem or VMEM_LIMIT)


def _sigmoid(x):
    return 1.0 / (1.0 + jnp.exp(-x))


def _layer_norm(y, g, b):
    mu = jnp.mean(y, axis=-1, keepdims=True)
    yc = y - mu
    var = jnp.mean(yc * yc, axis=-1, keepdims=True)
    return yc * lax.rsqrt(var + LN_EPS) * g + b


def _dot(a, b):
    return jnp.dot(a, b, preferred_element_type=F32)


def _hi_lo(x):
    hi = x.astype(BF16)
    return hi, (x - hi.astype(F32)).astype(BF16)


def _dot_split(x, w_ref):
    hi, lo = _hi_lo(x)
    return _dot(hi, w_ref[0]) + _dot(hi, w_ref[1]) + _dot(lo, w_ref[0])


def _dot_nt(a, b):
    return lax.dot_general(a, b, (((1,), (1,)), ((), ())), preferred_element_type=F32)


def _dot_tn(a, b):
    return lax.dot_general(a, b, (((0,), (0,)), ((), ())), preferred_element_type=F32)


def _ln0_body(x_ref, g_ref, b_ref, hf_ref, hb_ref):
    y = _layer_norm(x_ref[...], g_ref[...], b_ref[...])
    hf_ref[...] = y
    hb_ref[...] = y.astype(BF16)


def _ln0(x2, g, b):
    tm = 256
    row = pl.BlockSpec((tm, D_MODEL), lambda i: (i, 0))
    vec = pl.BlockSpec((1, D_MODEL), lambda i: (0, 0))
    return pl.pallas_call(
        _ln0_body, grid=(TOK // tm,), in_specs=[row, vec, vec], out_specs=[row, row],
        out_shape=[jax.ShapeDtypeStruct((TOK, D_MODEL), F32), jax.ShapeDtypeStruct((TOK, D_MODEL), BF16)],
        compiler_params=_params(1), name="ln0")(x2, g.reshape(1, -1), b.reshape(1, -1))


def _inproj_body(perm_ref, a_ref, w_ref, o_ref, wb_ref, *, gate, w_transposed):
    del perm_ref

    @pl.when(pl.program_id(1) == 0)
    def _():
        w = w_ref[...].T if w_transposed else w_ref[...]
        wb_ref[...] = w.astype(BF16)

    acc = _dot(a_ref[...], wb_ref[...])
    if gate:
        acc = _sigmoid(acc)
    o_ref[...] = acc.astype(o_ref.dtype)


def _inproj(hb, w, n_tiles, tile_perm, gate, w_transposed, name):
    tm, tn = 1024, 512
    w_spec = (pl.BlockSpec((tn, D_MODEL), lambda n, m, p: (n, 0)) if w_transposed
              else pl.BlockSpec((D_MODEL, tn), lambda n, m, p: (0, n)))
    grid_spec = pltpu.PrefetchScalarGridSpec(
        num_scalar_prefetch=1, grid=(n_tiles, TOK // tm),
        in_specs=[pl.BlockSpec((tm, D_MODEL), lambda n, m, p: (m, 0)), w_spec],
        out_specs=pl.BlockSpec((tm, tn), lambda n, m, p: (m, p[n])),
        scratch_shapes=[pltpu.VMEM((D_MODEL, tn), BF16)])
    return pl.pallas_call(
        functools.partial(_inproj_body, gate=gate, w_transposed=w_transposed), grid_spec=grid_spec,
        out_shape=jax.ShapeDtypeStruct((TOK, n_tiles * tn), BF16),
        compiler_params=_params(2), name=name)(jnp.asarray(tile_perm, jnp.int32), hb, w)


def _wprep_body(a_ref, b_ref, g_ref, lr_ref):
    g_ref[...] = jnp.concatenate([a_ref[LR_W:, :], b_ref[...]], axis=0).T.astype(BF16)

    @pl.when(pl.program_id(0) == 0)
    def _():
        lane = lax.broadcasted_iota(jnp.int32, (D_MODEL, 128), 1)
        lr_ref[...] = jnp.where(lane < LR_W, a_ref[:128, :].T, 0.0).astype(BF16)


def _gate_weights(wt):
    tn = 512
    first = MAIN_W // tn
    return pl.pallas_call(
        _wprep_body, grid=(2 * D_MODEL // tn,),
        in_specs=[pl.BlockSpec((tn, D_MODEL), lambda i: (first + i, 0)),
                  pl.BlockSpec((LR_W, D_MODEL), lambda i: ((first + i + 1) * (tn // LR_W), 0))],
        out_specs=[pl.BlockSpec((D_MODEL, tn), lambda i: (0, i)), pl.BlockSpec((D_MODEL, 128), lambda i: (0, 0))],
        out_shape=[jax.ShapeDtypeStruct((D_MODEL, 2 * D_MODEL), BF16), jax.ShapeDtypeStruct((D_MODEL, 128), BF16)],
        compiler_params=_params(1), name="gate_w")(wt, wt)


def _decay_body(h_ref, wlr_ref, w2_ref, b2_ref, trif_ref, trib_ref, bf_ref, bb_ref):
    zlr = _dot(h_ref[...], wlr_ref[...])
    pre = _dot_split(zlr, w2_ref) + b2_ref[...]
    la = (jnp.minimum(pre, 0.0) - jnp.log1p(jnp.exp(-jnp.abs(pre)))) * (1.0 / GLA_TAU)
    hi = la.astype(BF16)
    lo = (la - hi.astype(F32)).astype(BF16)
    bf_ref[...] = _dot(trif_ref[...], hi[:, :GLA_K]) + _dot(trif_ref[...], lo[:, :GLA_K])
    bb_ref[...] = _dot(trib_ref[...], hi[:, GLA_K:]) + _dot(trib_ref[...], lo[:, GLA_K:])


def _decay(hb, wlr, w2, b2):
    tm = 512
    r = jnp.arange(tm)
    same = (r[:, None] // GLA_CHUNK) == (r[None, :] // GLA_CHUNK)
    trif = (same & (r[None, :] <= r[:, None])).astype(BF16)
    trib = (same & (r[None, :] >= r[:, None])).astype(BF16)
    full = lambda shape: pl.BlockSpec(shape, lambda i: (0,) * len(shape))
    out = pl.BlockSpec((tm, GLA_K), lambda i: (i, 0))
    return pl.pallas_call(
        _decay_body, grid=(TOK // tm,),
        in_specs=[pl.BlockSpec((tm, D_MODEL), lambda i: (i, 0)), full((D_MODEL, 128)),
                  full((2, 128, 2 * GLA_K)), full((1, 2 * GLA_K)), full((tm, tm)), full((tm, tm))],
        out_specs=[out, out],
        out_shape=[jax.ShapeDtypeStruct((TOK, GLA_K), F32)] * 2,
        compiler_params=_params(1), name="decay")(hb, wlr, w2, b2, trif, trib)


def _qkprep_body(q_ref, k_ref, cos_ref, sin_ref, qn_ref, kn_ref, qo_ref, ko_ref):
    cos = cos_ref[...]
    sin = sin_ref[...]
    lane = lax.broadcasted_iota(jnp.int32, cos.shape, 1)
    even = (lane % 2) == 0

    def one(x, gain, scale):
        x = x.astype(F32)
        x = x * lax.rsqrt(jnp.mean(x * x, axis=-1, keepdims=True) + RMS_EPS) * gain
        partner = jnp.where(even, pltpu.roll(x, HEAD_DIM - 1, 1), pltpu.roll(x, 1, 1))
        return ((x * cos + partner * sin) * scale).astype(BF16)

    for hh in range(ATT_HEADS):
        sl = slice(hh * HEAD_DIM, (hh + 1) * HEAD_DIM)
        qo_ref[:, sl] = one(q_ref[:, sl], qn_ref[...], LOG2_E * HEAD_DIM ** -0.5)
    for hh in range(ATT_KV_HEADS):
        sl = slice(hh * HEAD_DIM, (hh + 1) * HEAD_DIM)
        ko_ref[:, sl] = one(k_ref[:, sl], kn_ref[...], 1.0)


def _qkprep(zm, cos_full, sin_signed, q_norm, k_norm):
    tm = 256
    nrow = SEQ // tm
    tab = pl.BlockSpec((tm, HEAD_DIM), lambda i: (i % nrow, 0))
    vec = pl.BlockSpec((1, HEAD_DIM), lambda i: (0, 0))
    return pl.pallas_call(
        _qkprep_body, grid=(TOK // tm,),
        in_specs=[pl.BlockSpec((tm, ATT_Q), lambda i: (i, Z_Q // ATT_Q)),
                  pl.BlockSpec((tm, ATT_KV), lambda i: (i, Z_KA // ATT_KV)), tab, tab, vec, vec],
        out_specs=[pl.BlockSpec((tm, ATT_Q), lambda i: (i, 0)), pl.BlockSpec((tm, ATT_KV), lambda i: (i, 0))],
        out_shape=[jax.ShapeDtypeStruct((TOK, ATT_Q), BF16), jax.ShapeDtypeStruct((TOK, ATT_KV), BF16)],
        compiler_params=_params(1), name="qkprep")(zm, zm, cos_full, sin_signed,
                                                   q_norm.reshape(1, -1), k_norm.reshape(1, -1))


ATT_TQ = 256
ATT_TK = 512


def _attn_body(q_ref, k_ref, v_ref, o_ref, qs_ref, s0_ref, s1_ref, p0_ref, p1_ref, a0_ref, a1_ref,
               acc_ref, m_ref, l_ref):
    rows = ATT_GROUP * ATT_TQ
    n_chunks = SEQ // ATT_TK
    groups = ATT_TK // SUBLANES
    for g in range(ATT_GROUP):
        qs_ref[g * ATT_TQ:(g + 1) * ATT_TQ, :] = q_ref[:, g * HEAD_DIM:(g + 1) * HEAD_DIM]
    m_ref[...] = jnp.full(m_ref.shape, -jnp.inf, F32)
    l_ref[...] = jnp.zeros(l_ref.shape, F32)
    acc_ref[...] = jnp.zeros(acc_ref.shape, F32)
    p1_ref[...] = jnp.zeros(p1_ref.shape, BF16)
    a1_ref[...] = jnp.zeros(a1_ref.shape, F32)

    def chunk(c):
        return pl.ds(pl.multiple_of(c * ATT_TK, ATT_TK), ATT_TK)

    lane_blocks = [slice(cb * HEAD_DIM, (cb + 1) * HEAD_DIM) for cb in range(rows // HEAD_DIM)]

    def scores(kc, s_out):
        qk = _dot_nt(k_ref[kc, :], qs_ref[...])
        for i, cb in enumerate(lane_blocks):
            s_out[i] = qk[:, cb]

    def probs_t(p_in):
        return jnp.concatenate([p_in[i] for i in range(len(lane_blocks))], axis=1)

    scores(slice(0, ATT_TK), s0_ref)

    def step(c, s_cur, s_nxt, p_cur, p_prv, a_cur, a_prv):
        c_next = jnp.where(c + 1 < n_chunks, c + 1, 0)
        c_prev = jnp.where(c > 0, c - 1, 0)
        scores(chunk(c_next), s_nxt)
        acc_ref[...] = a_prv[0:1, :] * acc_ref[...] + _dot_tn(v_ref[chunk(c_prev), :], probs_t(p_prv))
        m_new = []
        for i, cb in enumerate(lane_blocks):
            part = jnp.max(s_cur[i].reshape(groups, SUBLANES, HEAD_DIM), axis=0)
            m_new.append(jnp.maximum(m_ref[0:1, cb], jnp.max(part, axis=0, keepdims=True)))
        for i, (cb, mn) in enumerate(zip(lane_blocks, m_new)):
            alpha = jnp.exp2(m_ref[0:1, cb] - mn)
            p = jnp.exp2(s_cur[i] - mn)
            l_ref[:, cb] = alpha * l_ref[:, cb] + jnp.sum(p.reshape(groups, SUBLANES, HEAD_DIM), axis=0)
            a_cur[:, cb] = jnp.broadcast_to(alpha, (SUBLANES, HEAD_DIM))
            p_cur[i] = p.astype(BF16)
        for cb, mn in zip(lane_blocks, m_new):
            m_ref[:, cb] = jnp.broadcast_to(mn, (SUBLANES, HEAD_DIM))

    def pair(i, carry):
        step(2 * i, s0_ref, s1_ref, p0_ref, p1_ref, a0_ref, a1_ref)
        step(2 * i + 1, s1_ref, s0_ref, p1_ref, p0_ref, a1_ref, a0_ref)
        return carry

    lax.fori_loop(0, n_chunks // 2, pair, 0)
    acc = a1_ref[0:1, :] * acc_ref[...] + _dot_tn(v_ref[SEQ - ATT_TK:SEQ, :], probs_t(p1_ref))
    out = (acc / jnp.sum(l_ref[...], axis=0, keepdims=True)).T.astype(BF16)
    for g in range(ATT_GROUP):
        o_ref[:, g * HEAD_DIM:(g + 1) * HEAD_DIM] = out[g * ATT_TQ:(g + 1) * ATT_TQ]


def _attention(qr, kr, zm):
    nq = SEQ // ATT_TQ
    gw = ATT_GROUP * HEAD_DIM
    rows = ATT_GROUP * ATT_TQ
    qspec = pl.BlockSpec((ATT_TQ, gw), lambda b, j, i: (b * nq + i, j))
    return pl.pallas_call(
        _attn_body, grid=(BATCH, ATT_KV_HEADS, nq),
        in_specs=[qspec,
                  pl.BlockSpec((SEQ, HEAD_DIM), lambda b, j, i: (b, j)),
                  pl.BlockSpec((SEQ, HEAD_DIM), lambda b, j, i: (b, Z_VA // HEAD_DIM + j))],
        out_specs=qspec,
        out_shape=jax.ShapeDtypeStruct((TOK, ATT_Q), BF16),
        scratch_shapes=[pltpu.VMEM((rows, HEAD_DIM), BF16),
                        pltpu.VMEM((rows // HEAD_DIM, ATT_TK, HEAD_DIM), F32),
                        pltpu.VMEM((rows // HEAD_DIM, ATT_TK, HEAD_DIM), F32),
                        pltpu.VMEM((rows // HEAD_DIM, ATT_TK, HEAD_DIM), BF16),
                        pltpu.VMEM((rows // HEAD_DIM, ATT_TK, HEAD_DIM), BF16),
                        pltpu.VMEM((SUBLANES, rows), F32), pltpu.VMEM((SUBLANES, rows), F32),
                        pltpu.VMEM((HEAD_DIM, rows), F32), pltpu.VMEM((SUBLANES, rows), F32),
                        pltpu.VMEM((SUBLANES, rows), F32)],
        compiler_params=_params(3), name="attn")(qr, kr, zm)


GLA_CB = 4
GLA_RB = GLA_CB * GLA_CHUNK
N_SUB = GLA_CHUNK // GLA_SUB


def _gla_body(q_ref, k_ref, v_ref, bc_ref, o_ref, st_ref, kf_ref, bs_ref, *, rev):
    @pl.when(pl.program_id(1) == 0)
    def _():
        st_ref[...] = jnp.zeros_like(st_ref)

    C, SUB = GLA_CHUNK, GLA_SUB
    rowc = lax.broadcasted_iota(jnp.int32, (C, GLA_DK), 0)
    rows_s = lax.broadcasted_iota(jnp.int32, (SUB, 128), 0)
    lane_s = lax.broadcasted_iota(jnp.int32, (SUB, 128), 1)

    def chunk(ci, carry):
        c = (GLA_CB - 1 - ci) if rev else ci
        r0 = pl.multiple_of(c * C, C)
        for hh in range(GLA_HEADS):
            ksl = slice(hh * GLA_DK, (hh + 1) * GLA_DK)
            vsl = slice(hh * GLA_DV, (hh + 1) * GLA_DV)
            q = q_ref[pl.ds(r0, C), ksl].astype(F32) * (GLA_DK ** -0.5)
            k = k_ref[pl.ds(r0, C), ksl].astype(F32)
            v = v_ref[pl.ds(r0, C), vsl]
            bc = bc_ref[pl.ds(r0, C), ksl]
            kf_ref[hh] = k
            bs_ref[hh] = bc
            st = st_ref[hh]
            blast = bc[0:1] if rev else bc[C - 1:C]
            o_inter = _dot_nt((q * jnp.exp(bc)).astype(BF16), st.astype(BF16))
            kdec = k * jnp.exp(blast - bc)
            st_ref[hh] = st * jnp.exp(blast) + _dot_tn(v, kdec.astype(BF16))

            a_rows = []
            for si in range(N_SUB):
                lo, hi = si * SUB, (si + 1) * SUB
                q_s, b_s = q[lo:hi], bc[lo:hi]
                has_earlier = (si < N_SUB - 1) if rev else (si > 0)
                if has_earlier:
                    ref_row = bc[hi:hi + 1] if rev else bc[lo - 1:lo]
                    earlier = (rowc >= hi) if rev else (rowc < lo)
                    qt = q_s * jnp.exp(b_s - ref_row)
                    kt = k * jnp.exp(jnp.where(earlier, ref_row - bc, -jnp.inf))
                    a = _dot_nt(qt.astype(BF16), kt.astype(BF16))
                else:
                    a = jnp.zeros((SUB, C), F32)
                diag = jnp.zeros((SUB, 128), F32)
                for jl in range(SUB):
                    j = lo + jl
                    d = jnp.minimum(b_s - bs_ref[hh, j:j + 1, :], 0.0)
                    col = jnp.sum(q_s * kf_ref[hh, j:j + 1, :] * jnp.exp(d), axis=-1, keepdims=True)
                    diag = jnp.where(lane_s == j, col, diag)
                keep = (lane_s >= rows_s + lo) if rev else (lane_s <= rows_s + lo)
                diag = jnp.where(keep & (lane_s >= lo) & (lane_s < hi), diag, 0.0)
                a_rows.append(a + diag[:, :C])
            a_full = jnp.concatenate(a_rows, axis=0)
            o_ref[pl.ds(r0, C), vsl] = o_inter + _dot(a_full.astype(BF16), v)
        return carry

    lax.fori_loop(0, GLA_CB, chunk, 0)


def _gla(zm, bcum, rev):
    ncb = SEQ // GLA_RB
    if rev:
        row = lambda b, c: b * ncb + (ncb - 1 - c)
    else:
        row = lambda b, c: b * ncb + c
    return pl.pallas_call(
        functools.partial(_gla_body, rev=rev), grid=(BATCH, ncb),
        in_specs=[pl.BlockSpec((GLA_RB, GLA_K), lambda b, c: (row(b, c), Z_QB // GLA_K)),
                  pl.BlockSpec((GLA_RB, GLA_K), lambda b, c: (row(b, c), Z_KB // GLA_K)),
                  pl.BlockSpec((GLA_RB, GLA_V), lambda b, c: (row(b, c), Z_VB // GLA_V)),
                  pl.BlockSpec((GLA_RB, GLA_K), lambda b, c: (row(b, c), 0))],
        out_specs=pl.BlockSpec((GLA_RB, GLA_V), lambda b, c: (row(b, c), 0)),
        out_shape=jax.ShapeDtypeStruct((TOK, GLA_V), F32),
        scratch_shapes=[pltpu.VMEM((GLA_HEADS, GLA_DV, GLA_DK), F32),
                        pltpu.VMEM((GLA_HEADS, GLA_CHUNK, GLA_DK), F32),
                        pltpu.VMEM((GLA_HEADS, GLA_CHUNK, GLA_DK), F32)],
        compiler_params=_params(2), name="gla_bwd" if rev else "gla_fwd")(zm, zm, zm, bcum)


MIX_TM = 512


def _onorm_body(of_ref, ob_ref, og_ref, gn_ref, o_ref):
    gn = gn_ref[...]
    for hh in range(GLA_HEADS):
        sl = slice(hh * GLA_DV, (hh + 1) * GLA_DV)
        x = of_ref[:, sl] + ob_ref[:, sl]
        g = og_ref[:, sl].astype(F32)
        xn = x * lax.rsqrt(jnp.mean(x * x, axis=-1, keepdims=True) + RMS_EPS) * gn
        o_ref[:, sl] = (xn * (g * _sigmoid(g))).astype(BF16)


def _merge_body(attn_ref, on_ref, ga_ref, gb_ref, wa_ref, wb_ref, m_ref):
    ya = _dot(attn_ref[...], wa_ref[...])
    yb = _dot(on_ref[...], wb_ref[...])
    m_ref[...] = (ga_ref[...].astype(F32) * ya + gb_ref[...].astype(F32) * yb).astype(BF16)


def _outproj_body(m_ref, h_ref, wo_ref, lg_ref, lb_ref, wr_ref, br_ref, h1_ref, idx_ref, gate_ref):
    mix = _dot(m_ref[...], wo_ref[...])
    h1 = _layer_norm(DN_ALPHA * h_ref[...] + mix, lg_ref[...], lb_ref[...])
    h1_ref[...] = h1

    logits = _dot_split(h1, wr_ref) + br_ref[...]
    lane = lax.broadcasted_iota(jnp.int32, logits.shape, 1)
    x = logits
    vals, idxs = [], []
    for _ in range(TOP_K):
        mx = jnp.max(x, axis=-1, keepdims=True)
        ix = jnp.min(jnp.where(x == mx, lane, 128), axis=-1, keepdims=True)
        vals.append(mx)
        idxs.append(ix)
        x = jnp.where(lane == ix, -jnp.inf, x)
    es = [jnp.exp(vv - vals[0]) for vv in vals]
    den = es[0] + es[1] + es[2] + es[3]
    idx_out = jnp.zeros(logits.shape, jnp.int32)
    gate_out = jnp.zeros(logits.shape, F32)
    for kk in range(TOP_K):
        idx_out = jnp.where(lane == kk, idxs[kk], idx_out)
        gate_out = jnp.where(lane == kk, es[kk] / den, gate_out)
    idx_ref[...] = idx_out
    gate_ref[...] = gate_out


def _mixer(attn, o_f, o_b, zm, zg, hf, wa, wb, wo, gn, lg, lb, wr, br):
    tm = MIX_TM
    const = lambda shape: pl.BlockSpec(shape, lambda i: (0,) * len(shape), pipeline_mode=pl.Buffered(1))
    row = lambda w, cb=0: pl.BlockSpec((tm, w), lambda i: (i, cb))
    grid = (TOK // tm,)
    onorm = pl.pallas_call(
        _onorm_body, grid=grid,
        in_specs=[row(GLA_V), row(GLA_V), row(GLA_V, Z_OG // GLA_V), const((1, GLA_DV))],
        out_specs=row(GLA_V), out_shape=jax.ShapeDtypeStruct((TOK, GLA_V), BF16),
        compiler_params=_params(1), name="onorm")(o_f, o_b, zm, gn)
    merged = pl.pallas_call(
        _merge_body, grid=grid,
        in_specs=[row(ATT_Q), row(GLA_V), row(D_MODEL, 0), row(D_MODEL, 1),
                  const((ATT_Q, D_MODEL)), const((GLA_V, D_MODEL))],
        out_specs=row(D_MODEL), out_shape=jax.ShapeDtypeStruct((TOK, D_MODEL), BF16),
        compiler_params=_params(1), name="merge")(attn, onorm, zg, zg, wa, wb)
    return pl.pallas_call(
        _outproj_body, grid=grid,
        in_specs=[row(D_MODEL), row(D_MODEL), const((D_MODEL, D_MODEL)), const((1, D_MODEL)), const((1, D_MODEL)),
                  const((2, D_MODEL, 128)), const((1, 128))],
        out_specs=[row(D_MODEL), row(128), row(128)],
        out_shape=[jax.ShapeDtypeStruct((TOK, D_MODEL), F32),
                   jax.ShapeDtypeStruct((TOK, 128), jnp.int32),
                   jax.ShapeDtypeStruct((TOK, 128), F32)],
        compiler_params=_params(1), name="outproj")(merged, hf, wo, lg, lb, wr, br)


GROUP_SIZES = (512, 256, 128)
GATHER_STEPS = SB_BLOCKS // (SB_ROWS // GROUP_SIZES[0])


def _expert_body(e_ref, nb_ref, nv_ref, tab_hbm, h1_hbm, wg_ref, wl_ref, wd_ref, bg_ref, bl_ref, bd_ref,
                 y_hbm, gbuf, xb_ref, acc_ref, wgb_ref, wlb_ref, wdb_ref, ids, sem_ids, sem_g, sem_s):
    del e_ref
    s = pl.program_id(0)
    j = pl.program_id(1)
    nblk = nb_ref[s]
    slot = s & 1
    has_next = s + 1 < pl.num_programs(0)
    nxt = jnp.minimum(s + 1, N_SB - 1)
    nblk_next = jnp.where(has_next, nb_ref[nxt], 0)
    tiles = MOE_BLOCK // SUBLANES

    def ids_copy(sb, sl):
        return pltpu.make_async_copy(tab_hbm.at[pl.ds(pl.multiple_of(sb * IDS_STRIDE, IDS_STRIDE), IDS_STRIDE)],
                                     ids.at[pl.ds(pl.multiple_of(sl * IDS_STRIDE, IDS_STRIDE), IDS_STRIDE)],
                                     sem_ids.at[sl])

    def gather_block(sl, b):
        id0 = sl * IDS_STRIDE + b * MOE_BLOCK

        def issue(t, c):
            for u in range(SUBLANES):
                tok = ids[id0 + t * SUBLANES + u]
                pltpu.make_async_copy(h1_hbm.at[pl.ds(tok >> 3, 1), pl.ds(tok & 7, 1), :],
                                      gbuf.at[pl.ds(b * tiles + t, 1), pl.ds(u, 1), :], sem_g).start()
            return c

        lax.fori_loop(0, tiles, issue, 0)

    def gather_wait_block(b):
        pltpu.make_async_copy(h1_hbm.at[pl.ds(0, tiles)], gbuf.at[pl.ds(b * tiles, tiles)], sem_g).wait()

    def loop_blocks(n, fn):
        def body(b, c):
            fn(b)
            return c

        lax.fori_loop(0, n, body, 0)

    @pl.when((s == 0) & (j == 0))
    def _():
        ids_copy(0, 0).start()
        ids_copy(0, 0).wait()
        loop_blocks(nblk, lambda b: gather_block(0, b))

    big = GROUP_SIZES[0]
    big_blocks = big // MOE_BLOCK
    n_big = nblk // big_blocks
    in_window = (j >= 1) & (j <= GATHER_STEPS) & has_next
    issued_here = GATHER_STEPS * n_big
    issued_prev = jnp.where(s > 0, GATHER_STEPS * (nb_ref[jnp.maximum(s - 1, 0)] // big_blocks), 0)

    @pl.when(j == 0)
    def _():
        loop_blocks(jnp.maximum(issued_prev, nblk), gather_wait_block)

    @pl.when(nblk > 0)
    def _():
        @pl.when(j == 0)
        def _():
            @pl.when(has_next)
            def _():
                ids_copy(nxt, 1 - slot).start()

            def take_block(b):
                x = gbuf[pl.ds(b * tiles, tiles)].reshape(MOE_BLOCK, D_MODEL)
                xb_ref[pl.ds(pl.multiple_of(b * MOE_BLOCK, MOE_BLOCK), MOE_BLOCK), :] = x.astype(BF16)
                acc_ref[pl.ds(b * tiles, tiles)] = jnp.broadcast_to(bd_ref[0], (tiles, SUBLANES, D_MODEL))

            loop_blocks(nblk, take_block)

        @pl.when((j == 1) & has_next)
        def _():
            ids_copy(nxt, 1 - slot).wait()

        @pl.when(j == GATHER_STEPS + 1)
        def _():
            for b in range(SB_BLOCKS):
                @pl.when((b >= issued_here) & (b < nblk_next))
                def _(b=b):
                    gather_block(1 - slot, b)

        wgb_ref[...] = wg_ref[0].astype(BF16)
        wlb_ref[...] = wl_ref[0].astype(BF16)
        wdb_ref[...] = wd_ref[0].astype(BF16)

        def group(r0, size):
            x = xb_ref[pl.ds(pl.multiple_of(r0, MOE_BLOCK), size), :]
            g = jnp.minimum(_dot(x, wgb_ref[...]) + bg_ref[0], SWIGLU_LIMIT)
            lin = jnp.clip(_dot(x, wlb_ref[...]) + bl_ref[0], -SWIGLU_LIMIT, SWIGLU_LIMIT)
            act = g * _sigmoid(SWIGLU_ALPHA * g) * (lin + 1.0)
            upd = _dot(act.astype(BF16), wdb_ref[...])
            acc_ref[pl.ds(r0 // SUBLANES, size // SUBLANES)] += upd.reshape(size // SUBLANES, SUBLANES, D_MODEL)

        def big_group(i, c):
            group(i * big, big)
            return c

        def big_group_and_gather(i, c):
            group(i * big, big)
            b = (j - 1) * n_big + i
            id0 = (1 - slot) * IDS_STRIDE + b * MOE_BLOCK
            for t in range(tiles):
                for u in range(SUBLANES):
                    tok = ids[id0 + t * SUBLANES + u]
                    pltpu.make_async_copy(h1_hbm.at[pl.ds(tok >> 3, 1), pl.ds(tok & 7, 1), :],
                                          gbuf.at[pl.ds(b * tiles + t, 1), pl.ds(u, 1), :], sem_g).start()
            return c

        is_last = j == N_FF_TILES - 1
        nvalid = nv_ref[s]
        dst0 = slot * IDS_STRIDE + SB_ROWS

        def row_copy(t, u, dst):
            return pltpu.make_async_copy(acc_ref.at[pl.ds(t, 1), pl.ds(u, 1), :],
                                         y_hbm.at[pl.ds(dst >> 3, 1), pl.ds(dst & 7, 1), :], sem_s)

        @pl.when(in_window)
        def _():
            lax.fori_loop(0, n_big, big_group_and_gather, 0)

        @pl.when(jnp.logical_not(in_window) & jnp.logical_not(is_last))
        def _():
            lax.fori_loop(0, n_big, big_group, 0)

        @pl.when(jnp.logical_not(is_last))
        def _():
            done = n_big * big_blocks
            for size in GROUP_SIZES[1:]:
                take = ((nblk - done) // (size // MOE_BLOCK)) > 0

                @pl.when(take)
                def _(done=done, size=size):
                    group(done * MOE_BLOCK, size)

                done = done + jnp.where(take, size // MOE_BLOCK, 0)

        n_early = jnp.maximum(jnp.minimum(nvalid // MOE_BLOCK, nblk - 1), 0)

        def block_group(i, c):
            group(i * MOE_BLOCK, MOE_BLOCK)
            return c

        def block_group_and_scatter(i, c):
            group(i * MOE_BLOCK, MOE_BLOCK)
            t0 = (i - 1) * tiles
            for t in range(tiles):
                for u in range(SUBLANES):
                    row_copy(t0 + t, u, ids[dst0 + (t0 + t) * SUBLANES + u]).start()
            return c

        @pl.when(is_last)
        def _():
            lax.fori_loop(0, jnp.minimum(nblk, 1), block_group, 0)
            lax.fori_loop(1, 1 + n_early, block_group_and_scatter, 0)
            lax.fori_loop(1 + n_early, nblk, block_group, 0)

        @pl.when(is_last)
        def _():
            def issue_tile(t, c):
                for u in range(SUBLANES):
                    row_copy(t, u, ids[dst0 + t * SUBLANES + u]).start()
                return c

            def issue_row(r, c):
                row_copy(r >> 3, r & 7, ids[dst0 + r]).start()
                return c

            full_tiles = nvalid // SUBLANES
            lax.fori_loop(n_early * tiles, full_tiles, issue_tile, 0)
            lax.fori_loop(full_tiles * SUBLANES, nvalid, issue_row, 0)

            def wait_block(b, c):
                pltpu.make_async_copy(acc_ref.at[pl.ds(0, tiles)], y_hbm.at[pl.ds(0, tiles)], sem_s).wait()
                return c

            def wait_row(r, c):
                row_copy(0, 0, 0).wait()
                return c

            nfull = nvalid // MOE_BLOCK
            lax.fori_loop(0, nfull, wait_block, 0)
            lax.fori_loop(nfull * MOE_BLOCK, nvalid, wait_row, 0)


def _experts(n_sb, sb_e, sb_nblk, sb_nvalid, tab, h1, w_up, b_up, w_down, b_down):
    def jj(j, nb, s):
        return jnp.where(nb[s] > 0, j, N_FF_TILES - 1)

    grid_spec = pltpu.PrefetchScalarGridSpec(
        num_scalar_prefetch=3, grid=(n_sb, N_FF_TILES),
        in_specs=[pl.BlockSpec(memory_space=pl.ANY),
                  pl.BlockSpec(memory_space=pl.ANY),
                  pl.BlockSpec((1, D_MODEL, FF_TILE), lambda s, j, e, nb, nv: (e[s], 0, jj(j, nb, s))),
                  pl.BlockSpec((1, D_MODEL, FF_TILE), lambda s, j, e, nb, nv: (e[s], 0, N_FF_TILES + jj(j, nb, s))),
                  pl.BlockSpec((1, FF_TILE, D_MODEL), lambda s, j, e, nb, nv: (e[s], jj(j, nb, s), 0)),
                  pl.BlockSpec((1, 1, FF_TILE), lambda s, j, e, nb, nv: (e[s], 0, jj(j, nb, s))),
                  pl.BlockSpec((1, 1, FF_TILE), lambda s, j, e, nb, nv: (e[s], 0, N_FF_TILES + jj(j, nb, s))),
                  pl.BlockSpec((1, 1, D_MODEL), lambda s, j, e, nb, nv: (e[s], 0, 0))],
        out_specs=pl.BlockSpec(memory_space=pl.ANY),
        scratch_shapes=[pltpu.VMEM((SB_ROWS // SUBLANES, SUBLANES, D_MODEL), F32),
                        pltpu.VMEM((SB_ROWS, D_MODEL), BF16),
                        pltpu.VMEM((SB_ROWS // SUBLANES, SUBLANES, D_MODEL), F32),
                        pltpu.VMEM((D_MODEL, FF_TILE), BF16),
                        pltpu.VMEM((D_MODEL, FF_TILE), BF16),
                        pltpu.VMEM((FF_TILE, D_MODEL), BF16),
                        pltpu.SMEM((2 * IDS_STRIDE,), jnp.int32),
                        pltpu.SemaphoreType.DMA((2,)),
                        pltpu.SemaphoreType.DMA(()),
                        pltpu.SemaphoreType.DMA(())])
    y = pl.pallas_call(
        _expert_body, grid_spec=grid_spec,
        out_shape=jax.ShapeDtypeStruct((TOP_K * TOK // SUBLANES, SUBLANES, D_MODEL), F32),
        compiler_params=_params(2), name="experts")(
            sb_e, sb_nblk, sb_nvalid, tab, h1.reshape(TOK // SUBLANES, SUBLANES, D_MODEL), w_up, w_up, w_down,
            b_up.reshape(N_EXPERTS, 1, 2 * D_FF), b_up.reshape(N_EXPERTS, 1, 2 * D_FF),
            b_down.reshape(N_EXPERTS, 1, D_MODEL))
    return y.reshape(TOP_K * TOK, D_MODEL)


def _tail_body(y0_ref, y1_ref, y2_ref, y3_ref, gate_ref, h1_ref, p_ref, wg_ref, wp_ref,
               l2g_ref, l2b_ref, l3g_ref, l3b_ref, o_ref):
    gate = gate_ref[...]
    y = gate[:, 0:1] * y0_ref[...]
    for kk, y_ref in enumerate((y1_ref, y2_ref, y3_ref), start=1):
        y = y + gate[:, kk:kk + 1] * y_ref[...]
    h2 = _layer_norm(DN_ALPHA * h1_ref[...] + y, l2g_ref[...], l2b_ref[...])
    ple = _sigmoid(_dot(h2.astype(BF16), wg_ref[...])) * _dot(p_ref[...].astype(BF16), wp_ref[...])
    o_ref[...] = _layer_norm(DN_ALPHA * h2 + ple, l3g_ref[...], l3b_ref[...])


def _tail(y_slots, gates, h1, p2, wg, wp, l2g, l2b, l3g, l3b):
    tm = 256
    nt = TOK // tm
    const = lambda shape: pl.BlockSpec(shape, lambda i: (0,) * len(shape), pipeline_mode=pl.Buffered(1))
    row = lambda w: pl.BlockSpec((tm, w), lambda i: (i, 0))
    yspec = lambda kk: pl.BlockSpec((tm, D_MODEL), lambda i: (kk * nt + i, 0))
    vec = const((1, D_MODEL))
    return pl.pallas_call(
        _tail_body, grid=(nt,),
        in_specs=[yspec(0), yspec(1), yspec(2), yspec(3), row(128), row(D_MODEL), row(PLE_DIM),
                  const((D_MODEL, D_MODEL)), const((PLE_DIM, D_MODEL)), vec, vec, vec, vec],
        out_specs=row(D_MODEL),
        out_shape=jax.ShapeDtypeStruct((TOK, D_MODEL), F32),
        compiler_params=_params(1), name="tail")(y_slots, y_slots, y_slots, y_slots, gates, h1, p2, wg, wp,
                                                 l2g, l2b, l3g, l3b)


def _routing(top_idx):
    flat_e = top_idx.reshape(-1)
    experts = jnp.arange(N_EXPERTS, dtype=jnp.int32)
    onehot = (flat_e[:, None] == experts[None, :]).astype(jnp.int32)
    csum = jnp.cumsum(onehot, axis=0)
    rank = jnp.sum(onehot * csum, axis=1) - 1
    counts = csum[-1]
    padded = (counts + MOE_BLOCK - 1) // MOE_BLOCK * MOE_BLOCK
    padded_end = jnp.cumsum(padded)
    padded_start = padded_end - padded
    dest = (padded_start[flat_e] + rank).astype(jnp.int32)
    asg = jnp.arange(TOK * TOP_K, dtype=jnp.int32)
    asg_of_row = jnp.zeros((N_ROWS + SB_ROWS,), jnp.int32).at[dest].set(asg)
    tok = (asg_of_row // TOP_K).reshape(-1, MOE_BLOCK)
    dst = ((asg_of_row % TOP_K) * TOK + asg_of_row // TOP_K).reshape(-1, MOE_BLOCK)
    nb = padded // MOE_BLOCK
    n_sb = (nb + SB_BLOCKS - 1) // SB_BLOCKS
    sb_end = jnp.cumsum(n_sb)
    total = sb_end[-1]
    s = jnp.arange(N_SB, dtype=jnp.int32)
    s_eff = jnp.minimum(s, total - 1)
    e = jnp.minimum(jnp.sum((sb_end[None, :] <= s_eff[:, None]).astype(jnp.int32), axis=1), N_EXPERTS - 1)
    local = s_eff - (sb_end[e] - n_sb[e])
    valid = s < total
    sb_nblk = jnp.where(valid, jnp.clip(nb[e] - local * SB_BLOCKS, 0, SB_BLOCKS), 0).astype(jnp.int32)
    sb_blk = padded_start[e] // MOE_BLOCK + local * SB_BLOCKS
    sb_nvalid = jnp.where(valid, jnp.clip(counts[e] - local * SB_ROWS, 0, SB_ROWS), 0).astype(jnp.int32)
    blocks = sb_blk[:, None] + jnp.arange(SB_BLOCKS, dtype=jnp.int32)[None, :]
    tab = jnp.concatenate([tok[blocks].reshape(N_SB, SB_ROWS), dst[blocks].reshape(N_SB, SB_ROWS),
                           jnp.zeros((N_SB, IDS_STRIDE - 2 * SB_ROWS), jnp.int32)], axis=1).astype(jnp.int32).reshape(-1)
    return total.astype(jnp.int32), e.astype(jnp.int32), sb_nblk, sb_nvalid, tab


def _rope_tables():
    rows = SEQ // GRID_W
    row = jnp.repeat(jnp.arange(rows), GRID_W)
    col = jnp.tile(jnp.arange(GRID_W), rows)
    n_pairs = HEAD_DIM // 4
    inv_freq = ROPE_BASE ** (-jnp.arange(n_pairs, dtype=F32) / n_pairs)
    ang = jnp.concatenate([row[:, None] * inv_freq, col[:, None] * inv_freq], -1)
    cos_full = jnp.repeat(jnp.cos(ang), 2, axis=-1)
    sin = jnp.sin(ang)
    sin_signed = jnp.stack([-sin, sin], axis=-1).reshape(SEQ, HEAD_DIM)
    return cos_full, sin_signed


def kernel(x, p, in_ln_g, in_ln_b, w_in, q_norm, k_norm, w_lr_f, b_lr_f, w_lr_b, b_lr_b, gla_norm, w_br_a, w_br_b, w_o, ln1_g, ln1_b, w_router, b_router, w_up, b_up, w_down, b_down, ln2_g, ln2_b, w_ple_gate, w_ple_proj, ln3_g, ln3_b):
    assert x.shape == (BATCH, SEQ, D_MODEL) and w_in.shape[0] == DEPTH == 1
    wt = w_in[0].T
    hf, hb = _ln0(x.reshape(TOK, D_MODEL), in_ln_g, in_ln_b)

    zm = _inproj(hb, wt, MAIN_W // 512, Z_TILE_OF_W_TILE, False, True, "inproj_main")
    w_gate, w_lr = _gate_weights(wt)
    zg = _inproj(hb, w_gate, 2 * D_MODEL // 512, tuple(range(8)), True, False, "inproj_gates")

    w2 = jnp.zeros((128, 2 * GLA_K), F32)
    w2 = w2.at[:GLA_RANK, :GLA_K].set(w_lr_f[0]).at[GLA_RANK:LR_W, GLA_K:].set(w_lr_b[0])
    b2 = jnp.concatenate([b_lr_f[0], b_lr_b[0]]).reshape(1, -1)
    bc_f, bc_b = _decay(hb, w_lr, jnp.stack(_hi_lo(w2)), b2)

    cos_full, sin_signed = _rope_tables()
    qr, kr = _qkprep(zm, cos_full, sin_signed, q_norm[0], k_norm[0])
    attn = _attention(qr, kr, zm)
    o_f = _gla(zm, bc_f, False)
    o_b = _gla(zm, bc_b, True)

    wr = jnp.zeros((D_MODEL, 128), F32).at[:, :N_EXPERTS].set(w_router[0])
    br = jnp.full((1, 128), -jnp.inf, F32).at[0, :N_EXPERTS].set(b_router[0])
    vec = lambda v: v[0].reshape(1, -1)
    h1, idx_pad, gate_pad = _mixer(
        attn, o_f, o_b, zm, zg, hf, w_br_a[0].astype(BF16), w_br_b[0].astype(BF16), w_o[0].astype(BF16),
        vec(gla_norm), vec(ln1_g), vec(ln1_b), jnp.stack(_hi_lo(wr)), br)

    n_sb, sb_e, sb_nblk, sb_nvalid, tab = _routing(idx_pad[:, :TOP_K])
    y_slots = _experts(n_sb, sb_e, sb_nblk, sb_nvalid, tab, h1, w_up[0], b_up[0], w_down[0], b_down[0])

    out = _tail(y_slots, gate_pad, h1, p[0].reshape(TOK, PLE_DIM), w_ple_gate[0].astype(BF16),
                w_ple_proj[0].astype(BF16), vec(ln2_g), vec(ln2_b), vec(ln3_g), vec(ln3_b))
    return out.reshape(BATCH, SEQ, D_MODEL)
```

```python
import functools

import jax
import jax.numpy as jnp
from jax import lax
from jax.experimental import pallas as pl
from jax.experimental.pallas import tpu as pltpu

F32 = jnp.float32
BF16 = jnp.bfloat16

D_MODEL = 2048
BATCH = 2
SEQ = 4096
TOK = BATCH * SEQ
PLE_DIM = 256
GRID_W = 64
ATT_HEADS = 8
ATT_KV_HEADS = 2
ATT_GROUP = ATT_HEADS // ATT_KV_HEADS
HEAD_DIM = 128
ROPE_BASE = 10000.0
GLA_HEADS = 4
GLA_DK = 128
GLA_DV = 256
GLA_RANK = 16
GLA_TAU = 16.0
GLA_CHUNK = 64
GLA_SUB = 16
N_EXPERTS = 32
TOP_K = 4
D_FF = D_MODEL
SWIGLU_LIMIT = 7.0
SWIGLU_ALPHA = 1.702
MOE_BLOCK = 128
ATT_Q = ATT_HEADS * HEAD_DIM
ATT_KV = ATT_KV_HEADS * HEAD_DIM
GLA_K = GLA_HEADS * GLA_DK
GLA_V = GLA_HEADS * GLA_DV
MAIN_W = ATT_Q + 2 * ATT_KV + 2 * GLA_K + 2 * GLA_V
LR_W = 2 * GLA_RANK
DEPTH = 1
DN_ALPHA = (2 * DEPTH) ** 0.25
LN_EPS = 1e-5
RMS_EPS = 1e-6
LOG2_E = 1.4426950408889634

Z_TILE_OF_W_TILE = (0, 1, 6, 7, 8, 2, 3, 4, 5)
Z_Q, Z_VB, Z_OG, Z_KA, Z_VA, Z_QB, Z_KB = 0, 1024, 2048, 3072, 3328, 3584, 4096

VMEM_LIMIT = 56 * 1024 * 1024

N_ROWS = TOK * TOP_K + N_EXPERTS * MOE_BLOCK
N_BLOCKS = N_ROWS // MOE_BLOCK
SB_BLOCKS = 10
SB_ROWS = SB_BLOCKS * MOE_BLOCK
N_SB = (N_BLOCKS + (SB_BLOCKS - 1) * N_EXPERTS) // SB_BLOCKS
SUBLANES = 8
IDS_STRIDE = -(-2 * SB_ROWS // 1024) * 1024
FF_TILE = 256
N_FF_TILES = D_FF // FF_TILE


def _params(n_axes, vmem=None):
    return pltpu.CompilerParams(dimension_semantics=("arbitrary",) * n_axes,
                                vmem_limit_bytes=vmem or VMEM_LIMIT)


def _sigmoid(x):
    return 1.0 / (1.0 + jnp.exp(-x))


def _layer_norm(y, g, b):
    mu = jnp.mean(y, axis=-1, keepdims=True)
    yc = y - mu
    var = jnp.mean(yc * yc, axis=-1, keepdims=True)
    return yc * lax.rsqrt(var + LN_EPS) * g + b


def _dot(a, b):
    return jnp.dot(a, b, preferred_element_type=F32)


def _hi_lo(x):
    hi = x.astype(BF16)
    return hi, (x - hi.astype(F32)).astype(BF16)


def _dot_split(x, w_ref):
    hi, lo = _hi_lo(x)
    return _dot(hi, w_ref[0]) + _dot(hi, w_ref[1]) + _dot(lo, w_ref[0])


def _dot_nt(a, b):
    return lax.dot_general(a, b, (((1,), (1,)), ((), ())), preferred_element_type=F32)


def _dot_tn(a, b):
    return lax.dot_general(a, b, (((0,), (0,)), ((), ())), preferred_element_type=F32)


def _ln0_body(x_ref, g_ref, b_ref, hf_ref, hb_ref):
    y = _layer_norm(x_ref[...], g_ref[...], b_ref[...])
    hf_ref[...] = y
    hb_ref[...] = y.astype(BF16)


def _ln0(x2, g, b):
    tm = 256
    row = pl.BlockSpec((tm, D_MODEL), lambda i: (i, 0))
    vec = pl.BlockSpec((1, D_MODEL), lambda i: (0, 0))
    return pl.pallas_call(
        _ln0_body, grid=(TOK // tm,), in_specs=[row, vec, vec], out_specs=[row, row],
        out_shape=[jax.ShapeDtypeStruct((TOK, D_MODEL), F32), jax.ShapeDtypeStruct((TOK, D_MODEL), BF16)],
        compiler_params=_params(1), name="ln0")(x2, g.reshape(1, -1), b.reshape(1, -1))


def _inproj_body(perm_ref, a_ref, w_ref, o_ref, wb_ref, *, gate, w_transposed):
    del perm_ref

    @pl.when(pl.program_id(1) == 0)
    def _():
        w = w_ref[...].T if w_transposed else w_ref[...]
        wb_ref[...] = w.astype(BF16)

    acc = _dot(a_ref[...], wb_ref[...])
    if gate:
        acc = _sigmoid(acc)
    o_ref[...] = acc.astype(o_ref.dtype)


def _inproj(hb, w, n_tiles, tile_perm, gate, w_transposed, name):
    tm, tn = 1024, 512
    w_spec = (pl.BlockSpec((tn, D_MODEL), lambda n, m, p: (n, 0)) if w_transposed
              else pl.BlockSpec((D_MODEL, tn), lambda n, m, p: (0, n)))
    grid_spec = pltpu.PrefetchScalarGridSpec(
        num_scalar_prefetch=1, grid=(n_tiles, TOK // tm),
        in_specs=[pl.BlockSpec((tm, D_MODEL), lambda n, m, p: (m, 0)), w_spec],
        out_specs=pl.BlockSpec((tm, tn), lambda n, m, p: (m, p[n])),
        scratch_shapes=[pltpu.VMEM((D_MODEL, tn), BF16)])
    return pl.pallas_call(
        functools.partial(_inproj_body, gate=gate, w_transposed=w_transposed), grid_spec=grid_spec,
        out_shape=jax.ShapeDtypeStruct((TOK, n_tiles * tn), BF16),
        compiler_params=_params(2), name=name)(jnp.asarray(tile_perm, jnp.int32), hb, w)


def _wprep_body(a_ref, b_ref, g_ref, lr_ref):
    g_ref[...] = jnp.concatenate([a_ref[LR_W:, :], b_ref[...]], axis=0).T.astype(BF16)

    @pl.when(pl.program_id(0) == 0)
    def _():
        lane = lax.broadcasted_iota(jnp.int32, (D_MODEL, 128), 1)
        lr_ref[...] = jnp.where(lane < LR_W, a_ref[:128, :].T, 0.0).astype(BF16)


def _gate_weights(wt):
    tn = 512
    first = MAIN_W // tn
    return pl.pallas_call(
        _wprep_body, grid=(2 * D_MODEL // tn,),
        in_specs=[pl.BlockSpec((tn, D_MODEL), lambda i: (first + i, 0)),
                  pl.BlockSpec((LR_W, D_MODEL), lambda i: ((first + i + 1) * (tn // LR_W), 0))],
        out_specs=[pl.BlockSpec((D_MODEL, tn), lambda i: (0, i)), pl.BlockSpec((D_MODEL, 128), lambda i: (0, 0))],
        out_shape=[jax.ShapeDtypeStruct((D_MODEL, 2 * D_MODEL), BF16), jax.ShapeDtypeStruct((D_MODEL, 128), BF16)],
        compiler_params=_params(1), name="gate_w")(wt, wt)


def _decay_body(h_ref, wlr_ref, w2_ref, b2_ref, trif_ref, trib_ref, bf_ref, bb_ref):
    zlr = _dot(h_ref[...], wlr_ref[...])
    pre = _dot_split(zlr, w2_ref) + b2_ref[...]
    la = (jnp.minimum(pre, 0.0) - jnp.log1p(jnp.exp(-jnp.abs(pre)))) * (1.0 / GLA_TAU)
    hi = la.astype(BF16)
    lo = (la - hi.astype(F32)).astype(BF16)
    bf_ref[...] = _dot(trif_ref[...], hi[:, :GLA_K]) + _dot(trif_ref[...], lo[:, :GLA_K])
    bb_ref[...] = _dot(trib_ref[...], hi[:, GLA_K:]) + _dot(trib_ref[...], lo[:, GLA_K:])


def _decay(hb, wlr, w2, b2):
    tm = 512
    r = jnp.arange(tm)
    same = (r[:, None] // GLA_CHUNK) == (r[None, :] // GLA_CHUNK)
    trif = (same & (r[None, :] <= r[:, None])).astype(BF16)
    trib = (same & (r[None, :] >= r[:, None])).astype(BF16)
    full = lambda shape: pl.BlockSpec(shape, lambda i: (0,) * len(shape))
    out = pl.BlockSpec((tm, GLA_K), lambda i: (i, 0))
    return pl.pallas_call(
        _decay_body, grid=(TOK // tm,),
        in_specs=[pl.BlockSpec((tm, D_MODEL), lambda i: (i, 0)), full((D_MODEL, 128)),
                  full((2, 128, 2 * GLA_K)), full((1, 2 * GLA_K)), full((tm, tm)), full((tm, tm))],
        out_specs=[out, out],
        out_shape=[jax.ShapeDtypeStruct((TOK, GLA_K), F32)] * 2,
        compiler_params=_params(1), name="decay")(hb, wlr, w2, b2, trif, trib)


def _qkprep_body(q_ref, k_ref, cos_ref, sin_ref, qn_ref, kn_ref, qo_ref, ko_ref):
    cos = cos_ref[...]
    sin = sin_ref[...]
    lane = lax.broadcasted_iota(jnp.int32, cos.shape, 1)
    even = (lane % 2) == 0

    def one(x, gain, scale):
        x = x.astype(F32)
        x = x * lax.rsqrt(jnp.mean(x * x, axis=-1, keepdims=True) + RMS_EPS) * gain
        partner = jnp.where(even, pltpu.roll(x, HEAD_DIM - 1, 1), pltpu.roll(x, 1, 1))
        return ((x * cos + partner * sin) * scale).astype(BF16)

    for hh in range(ATT_HEADS):
        sl = slice(hh * HEAD_DIM, (hh + 1) * HEAD_DIM)
        qo_ref[:, sl] = one(q_ref[:, sl], qn_ref[...], LOG2_E * HEAD_DIM ** -0.5)
    for hh in range(ATT_KV_HEADS):
        sl = slice(hh * HEAD_DIM, (hh + 1) * HEAD_DIM)
        ko_ref[:, sl] = one(k_ref[:, sl], kn_ref[...], 1.0)


def _qkprep(zm, cos_full, sin_signed, q_norm, k_norm):
    tm = 256
    nrow = SEQ // tm
    tab = pl.BlockSpec((tm, HEAD_DIM), lambda i: (i % nrow, 0))
    vec = pl.BlockSpec((1, HEAD_DIM), lambda i: (0, 0))
    return pl.pallas_call(
        _qkprep_body, grid=(TOK // tm,),
        in_specs=[pl.BlockSpec((tm, ATT_Q), lambda i: (i, Z_Q // ATT_Q)),
                  pl.BlockSpec((tm, ATT_KV), lambda i: (i, Z_KA // ATT_KV)), tab, tab, vec, vec],
        out_specs=[pl.BlockSpec((tm, ATT_Q), lambda i: (i, 0)), pl.BlockSpec((tm, ATT_KV), lambda i: (i, 0))],
        out_shape=[jax.ShapeDtypeStruct((TOK, ATT_Q), BF16), jax.ShapeDtypeStruct((TOK, ATT_KV), BF16)],
        compiler_params=_params(1), name="qkprep")(zm, zm, cos_full, sin_signed,
                                                   q_norm.reshape(1, -1), k_norm.reshape(1, -1))


ATT_TQ = 256
ATT_TK = 512


def _attn_body(q_ref, k_ref, v_ref, o_ref, qs_ref, s0_ref, s1_ref, p0_ref, p1_ref, a0_ref, a1_ref,
               acc_ref, m_ref, l_ref):
    rows = ATT_GROUP * ATT_TQ
    n_chunks = SEQ // ATT_TK
    groups = ATT_TK // SUBLANES
    for g in range(ATT_GROUP):
        qs_ref[g * ATT_TQ:(g + 1) * ATT_TQ, :] = q_ref[:, g * HEAD_DIM:(g + 1) * HEAD_DIM]
    m_ref[...] = jnp.full(m_ref.shape, -jnp.inf, F32)
    l_ref[...] = jnp.zeros(l_ref.shape, F32)
    acc_ref[...] = jnp.zeros(acc_ref.shape, F32)
    p1_ref[...] = jnp.zeros(p1_ref.shape, BF16)
    a1_ref[...] = jnp.zeros(a1_ref.shape, F32)

    def chunk(c):
        return pl.ds(pl.multiple_of(c * ATT_TK, ATT_TK), ATT_TK)

    lane_blocks = [slice(cb * HEAD_DIM, (cb + 1) * HEAD_DIM) for cb in range(rows // HEAD_DIM)]

    def scores(kc, s_out):
        qk = _dot_nt(k_ref[kc, :], qs_ref[...])
        for i, cb in enumerate(lane_blocks):
            s_out[i] = qk[:, cb]

    def probs_t(p_in):
        return jnp.concatenate([p_in[i] for i in range(len(lane_blocks))], axis=1)

    scores(slice(0, ATT_TK), s0_ref)

    def step(c, s_cur, s_nxt, p_cur, p_prv, a_cur, a_prv):
        c_next = jnp.where(c + 1 < n_chunks, c + 1, 0)
        c_prev = jnp.where(c > 0, c - 1, 0)
        scores(chunk(c_next), s_nxt)
        acc_ref[...] = a_prv[0:1, :] * acc_ref[...] + _dot_tn(v_ref[chunk(c_prev), :], probs_t(p_prv))
        m_new = []
        for i, cb in enumerate(lane_blocks):
            part = jnp.max(s_cur[i].reshape(groups, SUBLANES, HEAD_DIM), axis=0)
            m_new.append(jnp.maximum(m_ref[0:1, cb], jnp.max(part, axis=0, keepdims=True)))
        for i, (cb, mn) in enumerate(zip(lane_blocks, m_new)):
            alpha = jnp.exp2(m_ref[0:1, cb] - mn)
            p = jnp.exp2(s_cur[i] - mn)
            l_ref[:, cb] = alpha * l_ref[:, cb] + jnp.sum(p.reshape(groups, SUBLANES, HEAD_DIM), axis=0)
            a_cur[:, cb] = jnp.broadcast_to(alpha, (SUBLANES, HEAD_DIM))
            p_cur[i] = p.astype(BF16)
        for cb, mn in zip(lane_blocks, m_new):
            m_ref[:, cb] = jnp.broadcast_to(mn, (SUBLANES, HEAD_DIM))

    def pair(i, carry):
        step(2 * i, s0_ref, s1_ref, p0_ref, p1_ref, a0_ref, a1_ref)
        step(2 * i + 1, s1_ref, s0_ref, p1_ref, p0_ref, a1_ref, a0_ref)
        return carry

    lax.fori_loop(0, n_chunks // 2, pair, 0)
    acc = a1_ref[0:1, :] * acc_ref[...] + _dot_tn(v_ref[SEQ - ATT_TK:SEQ, :], probs_t(p1_ref))
    out = (acc / jnp.sum(l_ref[...], axis=0, keepdims=True)).T.astype(BF16)
    for g in range(ATT_GROUP):
        o_ref[:, g * HEAD_DIM:(g + 1) * HEAD_DIM] = out[g * ATT_TQ:(g + 1) * ATT_TQ]


def _attention(qr, kr, zm):
    nq = SEQ // ATT_TQ
    gw = ATT_GROUP * HEAD_DIM
    rows = ATT_GROUP * ATT_TQ
    qspec = pl.BlockSpec((ATT_TQ, gw), lambda b, j, i: (b * nq + i, j))
    return pl.pallas_call(
        _attn_body, grid=(BATCH, ATT_KV_HEADS, nq),
        in_specs=[qspec,
                  pl.BlockSpec((SEQ, HEAD_DIM), lambda b, j, i: (b, j)),
                  pl.BlockSpec((SEQ, HEAD_DIM), lambda b, j, i: (b, Z_VA // HEAD_DIM + j))],
        out_specs=qspec,
        out_shape=jax.ShapeDtypeStruct((TOK, ATT_Q), BF16),
        scratch_shapes=[pltpu.VMEM((rows, HEAD_DIM), BF16),
                        pltpu.VMEM((rows // HEAD_DIM, ATT_TK, HEAD_DIM), F32),
                        pltpu.VMEM((rows // HEAD_DIM, ATT_TK, HEAD_DIM), F32),
                        pltpu.VMEM((rows // HEAD_DIM, ATT_TK, HEAD_DIM), BF16),
                        pltpu.VMEM((rows // HEAD_DIM, ATT_TK, HEAD_DIM), BF16),
                        pltpu.VMEM((SUBLANES, rows), F32), pltpu.VMEM((SUBLANES, rows), F32),
                        pltpu.VMEM((HEAD_DIM, rows), F32), pltpu.VMEM((SUBLANES, rows), F32),
                        pltpu.VMEM((SUBLANES, rows), F32)],
        compiler_params=_params(3), name="attn")(qr, kr, zm)


GLA_CB = 4
GLA_RB = GLA_CB * GLA_CHUNK
N_SUB = GLA_CHUNK // GLA_SUB


def _gla_body(q_ref, k_ref, v_ref, bc_ref, o_ref, st_ref, kf_ref, bs_ref, *, rev):
    @pl.when(pl.program_id(1) == 0)
    def _():
        st_ref[...] = jnp.zeros_like(st_ref)

    C, SUB = GLA_CHUNK, GLA_SUB
    rowc = lax.broadcasted_iota(jnp.int32, (C, GLA_DK), 0)
    rows_s = lax.broadcasted_iota(jnp.int32, (SUB, 128), 0)
    lane_s = lax.broadcasted_iota(jnp.int32, (SUB, 128), 1)

    def chunk(ci, carry):
        c = (GLA_CB - 1 - ci) if rev else ci
        r0 = pl.multiple_of(c * C, C)
        for hh in range(GLA_HEADS):
            ksl = slice(hh * GLA_DK, (hh + 1) * GLA_DK)
            vsl = slice(hh * GLA_DV, (hh + 1) * GLA_DV)
            q = q_ref[pl.ds(r0, C), ksl].astype(F32) * (GLA_DK ** -0.5)
            k = k_ref[pl.ds(r0, C), ksl].astype(F32)
            v = v_ref[pl.ds(r0, C), vsl]
            bc = bc_ref[pl.ds(r0, C), ksl]
            kf_ref[hh] = k
            bs_ref[hh] = bc
            st = st_ref[hh]
            blast = bc[0:1] if rev else bc[C - 1:C]
            o_inter = _dot_nt((q * jnp.exp(bc)).astype(BF16), st.astype(BF16))
            kdec = k * jnp.exp(blast - bc)
            st_ref[hh] = st * jnp.exp(blast) + _dot_tn(v, kdec.astype(BF16))

            a_rows = []
            for si in range(N_SUB):
                lo, hi = si * SUB, (si + 1) * SUB
                q_s, b_s = q[lo:hi], bc[lo:hi]
                has_earlier = (si < N_SUB - 1) if rev else (si > 0)
                if has_earlier:
                    ref_row = bc[hi:hi + 1] if rev else bc[lo - 1:lo]
                    earlier = (rowc >= hi) if rev else (rowc < lo)
                    qt = q_s * jnp.exp(b_s - ref_row)
                    kt = k * jnp.exp(jnp.where(earlier, ref_row - bc, -jnp.inf))
                    a = _dot_nt(qt.astype(BF16), kt.astype(BF16))
                else:
                    a = jnp.zeros((SUB, C), F32)
                diag = jnp.zeros((SUB, 128), F32)
                for jl in range(SUB):
                    j = lo + jl
                    d = jnp.minimum(b_s - bs_ref[hh, j:j + 1, :], 0.0)
                    col = jnp.sum(q_s * kf_ref[hh, j:j + 1, :] * jnp.exp(d), axis=-1, keepdims=True)
                    diag = jnp.where(lane_s == j, col, diag)
                keep = (lane_s >= rows_s + lo) if rev else (lane_s <= rows_s + lo)
                diag = jnp.where(keep & (lane_s >= lo) & (lane_s < hi), diag, 0.0)
                a_rows.append(a + diag[:, :C])
            a_full = jnp.concatenate(a_rows, axis=0)
            o_ref[pl.ds(r0, C), vsl] = o_inter + _dot(a_full.astype(BF16), v)
        return carry

    lax.fori_loop(0, GLA_CB, chunk, 0)


def _gla(zm, bcum, rev):
    ncb = SEQ // GLA_RB
    if rev:
        row = lambda b, c: b * ncb + (ncb - 1 - c)
    else:
        row = lambda b, c: b * ncb + c
    return pl.pallas_call(
        functools.partial(_gla_body, rev=rev), grid=(BATCH, ncb),
        in_specs=[pl.BlockSpec((GLA_RB, GLA_K), lambda b, c: (row(b, c), Z_QB // GLA_K)),
                  pl.BlockSpec((GLA_RB, GLA_K), lambda b, c: (row(b, c), Z_KB // GLA_K)),
                  pl.BlockSpec((GLA_RB, GLA_V), lambda b, c: (row(b, c), Z_VB // GLA_V)),
                  pl.BlockSpec((GLA_RB, GLA_K), lambda b, c: (row(b, c), 0))],
        out_specs=pl.BlockSpec((GLA_RB, GLA_V), lambda b, c: (row(b, c), 0)),
        out_shape=jax.ShapeDtypeStruct((TOK, GLA_V), F32),
        scratch_shapes=[pltpu.VMEM((GLA_HEADS, GLA_DV, GLA_DK), F32),
                        pltpu.VMEM((GLA_HEADS, GLA_CHUNK, GLA_DK), F32),
                        pltpu.VMEM((GLA_HEADS, GLA_CHUNK, GLA_DK), F32)],
        compiler_params=_params(2), name="gla_bwd" if rev else "gla_fwd")(zm, zm, zm, bcum)


MIX_TM = 512


def _onorm_body(of_ref, ob_ref, og_ref, gn_ref, o_ref):
    gn = gn_ref[...]
    for hh in range(GLA_HEADS):
        sl = slice(hh * GLA_DV, (hh + 1) * GLA_DV)
        x = of_ref[:, sl] + ob_ref[:, sl]
        g = og_ref[:, sl].astype(F32)
        xn = x * lax.rsqrt(jnp.mean(x * x, axis=-1, keepdims=True) + RMS_EPS) * gn
        o_ref[:, sl] = (xn * (g * _sigmoid(g))).astype(BF16)


def _merge_body(attn_ref, on_ref, ga_ref, gb_ref, wa_ref, wb_ref, m_ref):
    ya = _dot(attn_ref[...], wa_ref[...])
    yb = _dot(on_ref[...], wb_ref[...])
    m_ref[...] = (ga_ref[...].astype(F32) * ya + gb_ref[...].astype(F32) * yb).astype(BF16)


def _outproj_body(m_ref, h_ref, wo_ref, lg_ref, lb_ref, wr_ref, br_ref, h1_ref, idx_ref, gate_ref):
    mix = _dot(m_ref[...], wo_ref[...])
    h1 = _layer_norm(DN_ALPHA * h_ref[...] + mix, lg_ref[...], lb_ref[...])
    h1_ref[...] = h1

    logits = _dot_split(h1, wr_ref) + br_ref[...]
    lane = lax.broadcasted_iota(jnp.int32, logits.shape, 1)
    x = logits
    vals, idxs = [], []
    for _ in range(TOP_K):
        mx = jnp.max(x, axis=-1, keepdims=True)
        ix = jnp.min(jnp.where(x == mx, lane, 128), axis=-1, keepdims=True)
        vals.append(mx)
        idxs.append(ix)
        x = jnp.where(lane == ix, -jnp.inf, x)
    es = [jnp.exp(vv - vals[0]) for vv in vals]
    den = es[0] + es[1] + es[2] + es[3]
    idx_out = jnp.zeros(logits.shape, jnp.int32)
    gate_out = jnp.zeros(logits.shape, F32)
    for kk in range(TOP_K):
        idx_out = jnp.where(lane == kk, idxs[kk], idx_out)
        gate_out = jnp.where(lane == kk, es[kk] / den, gate_out)
    idx_ref[...] = idx_out
    gate_ref[...] = gate_out


def _mixer(attn, o_f, o_b, zm, zg, hf, wa, wb, wo, gn, lg, lb, wr, br):
    tm = MIX_TM
    const = lambda shape: pl.BlockSpec(shape, lambda i: (0,) * len(shape), pipeline_mode=pl.Buffered(1))
    row = lambda w, cb=0: pl.BlockSpec((tm, w), lambda i: (i, cb))
    grid = (TOK // tm,)
    onorm = pl.pallas_call(
        _onorm_body, grid=grid,
        in_specs=[row(GLA_V), row(GLA_V), row(GLA_V, Z_OG // GLA_V), const((1, GLA_DV))],
        out_specs=row(GLA_V), out_shape=jax.ShapeDtypeStruct((TOK, GLA_V), BF16),
        compiler_params=_params(1), name="onorm")(o_f, o_b, zm, gn)
    merged = pl.pallas_call(
        _merge_body, grid=grid,
        in_specs=[row(ATT_Q), row(GLA_V), row(D_MODEL, 0), row(D_MODEL, 1),
                  const((ATT_Q, D_MODEL)), const((GLA_V, D_MODEL))],
        out_specs=row(D_MODEL), out_shape=jax.ShapeDtypeStruct((TOK, D_MODEL), BF16),
        compiler_params=_params(1), name="merge")(attn, onorm, zg, zg, wa, wb)
    return pl.pallas_call(
        _outproj_body, grid=grid,
        in_specs=[row(D_MODEL), row(D_MODEL), const((D_MODEL, D_MODEL)), const((1, D_MODEL)), const((1, D_MODEL)),
                  const((2, D_MODEL, 128)), const((1, 128))],
        out_specs=[row(D_MODEL), row(128), row(128)],
        out_shape=[jax.ShapeDtypeStruct((TOK, D_MODEL), F32),
                   jax.ShapeDtypeStruct((TOK, 128), jnp.int32),
                   jax.ShapeDtypeStruct((TOK, 128), F32)],
        compiler_params=_params(1), name="outproj")(merged, hf, wo, lg, lb, wr, br)


GROUP_SIZES = (512, 256, 128)
GATHER_STEPS = SB_BLOCKS // (SB_ROWS // GROUP_SIZES[0])


def _expert_body(e_ref, nb_ref, nv_ref, tab_hbm, h1_hbm, wg_ref, wl_ref, wd_ref, bg_ref, bl_ref, bd_ref,
                 y_hbm, gbuf, xb_ref, acc_ref, wgb_ref, wlb_ref, wdb_ref, ids, sem_ids, sem_g, sem_s):
    del e_ref
    s = pl.program_id(0)
    j = pl.program_id(1)
    nblk = nb_ref[s]
    slot = s & 1
    has_next = s + 1 < pl.num_programs(0)
    nxt = jnp.minimum(s + 1, N_SB - 1)
    nblk_next = jnp.where(has_next, nb_ref[nxt], 0)
    tiles = MOE_BLOCK // SUBLANES

    def ids_copy(sb, sl):
        return pltpu.make_async_copy(tab_hbm.at[pl.ds(pl.multiple_of(sb * IDS_STRIDE, IDS_STRIDE), IDS_STRIDE)],
                                     ids.at[pl.ds(pl.multiple_of(sl * IDS_STRIDE, IDS_STRIDE), IDS_STRIDE)],
                                     sem_ids.at[sl])

    def gather_block(sl, b):
        id0 = sl * IDS_STRIDE + b * MOE_BLOCK

        def issue(t, c):
            for u in range(SUBLANES):
                tok = ids[id0 + t * SUBLANES + u]
                pltpu.make_async_copy(h1_hbm.at[pl.ds(tok >> 3, 1), pl.ds(tok & 7, 1), :],
                                      gbuf.at[pl.ds(b * tiles + t, 1), pl.ds(u, 1), :], sem_g).start(priority=u % 2)
            return c

        lax.fori_loop(0, tiles, issue, 0)

    def gather_wait_block(b):
        pltpu.make_async_copy(h1_hbm.at[pl.ds(0, tiles)], gbuf.at[pl.ds(b * tiles, tiles)], sem_g).wait()

    def loop_blocks(n, fn):
        def body(b, c):
            fn(b)
            return c

        lax.fori_loop(0, n, body, 0)

    @pl.when((s == 0) & (j == 0))
    def _():
        ids_copy(0, 0).start()
        ids_copy(0, 0).wait()
        loop_blocks(nblk, lambda b: gather_block(0, b))

    big = GROUP_SIZES[0]
    big_blocks = big // MOE_BLOCK
    n_big = nblk // big_blocks
    in_window = (j >= 1) & (j <= GATHER_STEPS) & has_next
    issued_here = GATHER_STEPS * n_big
    issued_prev = jnp.where(s > 0, GATHER_STEPS * (nb_ref[jnp.maximum(s - 1, 0)] // big_blocks), 0)

    @pl.when(j == 0)
    def _():
        loop_blocks(jnp.maximum(issued_prev, nblk), gather_wait_block)

    @pl.when(nblk > 0)
    def _():
        @pl.when(j == 0)
        def _():
            @pl.when(has_next)
            def _():
                ids_copy(nxt, 1 - slot).start()

            def take_block(b):
                x = gbuf[pl.ds(b * tiles, tiles)].reshape(MOE_BLOCK, D_MODEL)
                xb_ref[pl.ds(pl.multiple_of(b * MOE_BLOCK, MOE_BLOCK), MOE_BLOCK), :] = x.astype(BF16)
                acc_ref[pl.ds(b * tiles, tiles)] = jnp.broadcast_to(bd_ref[0], (tiles, SUBLANES, D_MODEL))

            loop_blocks(nblk, take_block)

        @pl.when((j == 1) & has_next)
        def _():
            ids_copy(nxt, 1 - slot).wait()

        @pl.when(j == GATHER_STEPS + 1)
        def _():
            for b in range(SB_BLOCKS):
                @pl.when((b >= issued_here) & (b < nblk_next))
                def _(b=b):
                    gather_block(1 - slot, b)

        wgb_ref[...] = wg_ref[0].astype(BF16)
        wlb_ref[...] = wl_ref[0].astype(BF16)
        wdb_ref[...] = wd_ref[0].astype(BF16)

        def group(r0, size):
            x = xb_ref[pl.ds(pl.multiple_of(r0, MOE_BLOCK), size), :]
            g = jnp.minimum(_dot(x, wgb_ref[...]) + bg_ref[0], SWIGLU_LIMIT)
            lin = jnp.clip(_dot(x, wlb_ref[...]) + bl_ref[0], -SWIGLU_LIMIT, SWIGLU_LIMIT)
            act = g * _sigmoid(SWIGLU_ALPHA * g) * (lin + 1.0)
            upd = _dot(act.astype(BF16), wdb_ref[...])
            acc_ref[pl.ds(r0 // SUBLANES, size // SUBLANES)] += upd.reshape(size // SUBLANES, SUBLANES, D_MODEL)

        def big_group(i, c):
            group(i * big, big)
            return c

        def big_group_and_gather(i, c):
            group(i * big, big)
            b = (j - 1) * n_big + i
            id0 = (1 - slot) * IDS_STRIDE + b * MOE_BLOCK
            for t in range(tiles):
                for u in range(SUBLANES):
                    tok = ids[id0 + t * SUBLANES + u]
                    pltpu.make_async_copy(h1_hbm.at[pl.ds(tok >> 3, 1), pl.ds(tok & 7, 1), :],
                                          gbuf.at[pl.ds(b * tiles + t, 1), pl.ds(u, 1), :], sem_g).start(priority=u % 2)
            return c

        @pl.when(in_window)
        def _():
            lax.fori_loop(0, n_big, big_group_and_gather, 0)

        @pl.when(jnp.logical_not(in_window))
        def _():
            lax.fori_loop(0, n_big, big_group, 0)

        done = n_big * big_blocks
        for size in GROUP_SIZES[1:]:
            take = ((nblk - done) // (size // MOE_BLOCK)) > 0

            @pl.when(take)
            def _(done=done, size=size):
                group(done * MOE_BLOCK, size)

            done = done + jnp.where(take, size // MOE_BLOCK, 0)

        @pl.when(j == N_FF_TILES - 1)
        def _():
            nvalid = nv_ref[s]

            dst0 = slot * IDS_STRIDE + SB_ROWS

            def row_copy(t, u, dst):
                return pltpu.make_async_copy(acc_ref.at[pl.ds(t, 1), pl.ds(u, 1), :],
                                             y_hbm.at[pl.ds(dst >> 3, 1), pl.ds(dst & 7, 1), :], sem_s)

            def issue_tile(t, c):
                for u in range(SUBLANES):
                    row_copy(t, u, ids[dst0 + t * SUBLANES + u]).start(priority=u % 2)
                return c

            def issue_row(r, c):
                row_copy(r >> 3, r & 7, ids[dst0 + r]).start()
                return c

            full_tiles = nvalid // SUBLANES
            lax.fori_loop(0, full_tiles, issue_tile, 0)
            lax.fori_loop(full_tiles * SUBLANES, nvalid, issue_row, 0)

            def wait_block(b, c):
                pltpu.make_async_copy(acc_ref.at[pl.ds(0, tiles)], y_hbm.at[pl.ds(0, tiles)], sem_s).wait()
                return c

            def wait_row(r, c):
                row_copy(0, 0, 0).wait()
                return c

            nfull = nvalid // MOE_BLOCK
            lax.fori_loop(0, nfull, wait_block, 0)
            lax.fori_loop(nfull * MOE_BLOCK, nvalid, wait_row, 0)


def _experts(n_sb, sb_e, sb_nblk, sb_nvalid, tab, h1, w_up, b_up, w_down, b_down):
    def jj(j, nb, s):
        return jnp.where(nb[s] > 0, j, N_FF_TILES - 1)

    grid_spec = pltpu.PrefetchScalarGridSpec(
        num_scalar_prefetch=3, grid=(n_sb, N_FF_TILES),
        in_specs=[pl.BlockSpec(memory_space=pl.ANY),
                  pl.BlockSpec(memory_space=pl.ANY),
                  pl.BlockSpec((1, D_MODEL, FF_TILE), lambda s, j, e, nb, nv: (e[s], 0, jj(j, nb, s))),
                  pl.BlockSpec((1, D_MODEL, FF_TILE), lambda s, j, e, nb, nv: (e[s], 0, N_FF_TILES + jj(j, nb, s))),
                  pl.BlockSpec((1, FF_TILE, D_MODEL), lambda s, j, e, nb, nv: (e[s], jj(j, nb, s), 0)),
                  pl.BlockSpec((1, 1, FF_TILE), lambda s, j, e, nb, nv: (e[s], 0, jj(j, nb, s))),
                  pl.BlockSpec((1, 1, FF_TILE), lambda s, j, e, nb, nv: (e[s], 0, N_FF_TILES + jj(j, nb, s))),
                  pl.BlockSpec((1, 1, D_MODEL), lambda s, j, e, nb, nv: (e[s], 0, 0))],
        out_specs=pl.BlockSpec(memory_space=pl.ANY),
        scratch_shapes=[pltpu.VMEM((SB_ROWS // SUBLANES, SUBLANES, D_MODEL), F32),
                        pltpu.VMEM((SB_ROWS, D_MODEL), BF16),
                        pltpu.VMEM((SB_ROWS // SUBLANES, SUBLANES, D_MODEL), F32),
                        pltpu.VMEM((D_MODEL, FF_TILE), BF16),
                        pltpu.VMEM((D_MODEL, FF_TILE), BF16),
                        pltpu.VMEM((FF_TILE, D_MODEL), BF16),
                        pltpu.SMEM((2 * IDS_STRIDE,), jnp.int32),
                        pltpu.SemaphoreType.DMA((2,)),
                        pltpu.SemaphoreType.DMA(()),
                        pltpu.SemaphoreType.DMA(())])
    y = pl.pallas_call(
        _expert_body, grid_spec=grid_spec,
        out_shape=jax.ShapeDtypeStruct((TOP_K * TOK // SUBLANES, SUBLANES, D_MODEL), F32),
        compiler_params=_params(2), name="experts")(
            sb_e, sb_nblk, sb_nvalid, tab, h1.reshape(TOK // SUBLANES, SUBLANES, D_MODEL), w_up, w_up, w_down,
            b_up.reshape(N_EXPERTS, 1, 2 * D_FF), b_up.reshape(N_EXPERTS, 1, 2 * D_FF),
            b_down.reshape(N_EXPERTS, 1, D_MODEL))
    return y.reshape(TOP_K * TOK, D_MODEL)


def _tail_body(y0_ref, y1_ref, y2_ref, y3_ref, gate_ref, h1_ref, p_ref, wg_ref, wp_ref,
               l2g_ref, l2b_ref, l3g_ref, l3b_ref, o_ref):
    gate = gate_ref[...]
    y = gate[:, 0:1] * y0_ref[...]
    for kk, y_ref in enumerate((y1_ref, y2_ref, y3_ref), start=1):
        y = y + gate[:, kk:kk + 1] * y_ref[...]
    h2 = _layer_norm(DN_ALPHA * h1_ref[...] + y, l2g_ref[...], l2b_ref[...])
    ple = _sigmoid(_dot(h2.astype(BF16), wg_ref[...])) * _dot(p_ref[...].astype(BF16), wp_ref[...])
    o_ref[...] = _layer_norm(DN_ALPHA * h2 + ple, l3g_ref[...], l3b_ref[...])


def _tail(y_slots, gates, h1, p2, wg, wp, l2g, l2b, l3g, l3b):
    tm = 256
    nt = TOK // tm
    const = lambda shape: pl.BlockSpec(shape, lambda i: (0,) * len(shape), pipeline_mode=pl.Buffered(1))
    row = lambda w: pl.BlockSpec((tm, w), lambda i: (i, 0))
    yspec = lambda kk: pl.BlockSpec((tm, D_MODEL), lambda i: (kk * nt + i, 0))
    vec = const((1, D_MODEL))
    return pl.pallas_call(
        _tail_body, grid=(nt,),
        in_specs=[yspec(0), yspec(1), yspec(2), yspec(3), row(128), row(D_MODEL), row(PLE_DIM),
                  const((D_MODEL, D_MODEL)), const((PLE_DIM, D_MODEL)), vec, vec, vec, vec],
        out_specs=row(D_MODEL),
        out_shape=jax.ShapeDtypeStruct((TOK, D_MODEL), F32),
        compiler_params=_params(1), name="tail")(y_slots, y_slots, y_slots, y_slots, gates, h1, p2, wg, wp,
                                                 l2g, l2b, l3g, l3b)


def _routing(top_idx):
    flat_e = top_idx.reshape(-1)
    experts = jnp.arange(N_EXPERTS, dtype=jnp.int32)
    onehot = (flat_e[:, None] == experts[None, :]).astype(jnp.int32)
    csum = jnp.cumsum(onehot, axis=0)
    rank = jnp.sum(onehot * csum, axis=1) - 1
    counts = csum[-1]
    padded = (counts + MOE_BLOCK - 1) // MOE_BLOCK * MOE_BLOCK
    padded_end = jnp.cumsum(padded)
    padded_start = padded_end - padded
    dest = (padded_start[flat_e] + rank).astype(jnp.int32)
    asg = jnp.arange(TOK * TOP_K, dtype=jnp.int32)
    asg_of_row = jnp.zeros((N_ROWS + SB_ROWS,), jnp.int32).at[dest].set(asg)
    tok = (asg_of_row // TOP_K).reshape(-1, MOE_BLOCK)
    dst = ((asg_of_row % TOP_K) * TOK + asg_of_row // TOP_K).reshape(-1, MOE_BLOCK)
    nb = padded // MOE_BLOCK
    n_sb = (nb + SB_BLOCKS - 1) // SB_BLOCKS
    sb_end = jnp.cumsum(n_sb)
    total = sb_end[-1]
    s = jnp.arange(N_SB, dtype=jnp.int32)
    s_eff = jnp.minimum(s, total - 1)
    e = jnp.minimum(jnp.sum((sb_end[None, :] <= s_eff[:, None]).astype(jnp.int32), axis=1), N_EXPERTS - 1)
    local = s_eff - (sb_end[e] - n_sb[e])
    valid = s < total
    sb_nblk = jnp.where(valid, jnp.clip(nb[e] - local * SB_BLOCKS, 0, SB_BLOCKS), 0).astype(jnp.int32)
    sb_blk = padded_start[e] // MOE_BLOCK + local * SB_BLOCKS
    sb_nvalid = jnp.where(valid, jnp.clip(counts[e] - local * SB_ROWS, 0, SB_ROWS), 0).astype(jnp.int32)
    blocks = sb_blk[:, None] + jnp.arange(SB_BLOCKS, dtype=jnp.int32)[None, :]
    tab = jnp.concatenate([tok[blocks].reshape(N_SB, SB_ROWS), dst[blocks].reshape(N_SB, SB_ROWS),
                           jnp.zeros((N_SB, IDS_STRIDE - 2 * SB_ROWS), jnp.int32)], axis=1).astype(jnp.int32).reshape(-1)
    return total.astype(jnp.int32), e.astype(jnp.int32), sb_nblk, sb_nvalid, tab


def _rope_tables():
    rows = SEQ // GRID_W
    row = jnp.repeat(jnp.arange(rows), GRID_W)
    col = jnp.tile(jnp.arange(GRID_W), rows)
    n_pairs = HEAD_DIM // 4
    inv_freq = ROPE_BASE ** (-jnp.arange(n_pairs, dtype=F32) / n_pairs)
    ang = jnp.concatenate([row[:, None] * inv_freq, col[:, None] * inv_freq], -1)
    cos_full = jnp.repeat(jnp.cos(ang), 2, axis=-1)
    sin = jnp.sin(ang)
    sin_signed = jnp.stack([-sin, sin], axis=-1).reshape(SEQ, HEAD_DIM)
    return cos_full, sin_signed


def kernel(x, p, in_ln_g, in_ln_b, w_in, q_norm, k_norm, w_lr_f, b_lr_f, w_lr_b, b_lr_b, gla_norm, w_br_a, w_br_b, w_o, ln1_g, ln1_b, w_router, b_router, w_up, b_up, w_down, b_down, ln2_g, ln2_b, w_ple_gate, w_ple_proj, ln3_g, ln3_b):
    assert x.shape == (BATCH, SEQ, D_MODEL) and w_in.shape[0] == DEPTH == 1
    wt = w_in[0].T
    hf, hb = _ln0(x.reshape(TOK, D_MODEL), in_ln_g, in_ln_b)

    zm = _inproj(hb, wt, MAIN_W // 512, Z_TILE_OF_W_TILE, False, True, "inproj_main")
    w_gate, w_lr = _gate_weights(wt)
    zg = _inproj(hb, w_gate, 2 * D_MODEL // 512, tuple(range(8)), True, False, "inproj_gates")

    w2 = jnp.zeros((128, 2 * GLA_K), F32)
    w2 = w2.at[:GLA_RANK, :GLA_K].set(w_lr_f[0]).at[GLA_RANK:LR_W, GLA_K:].set(w_lr_b[0])
    b2 = jnp.concatenate([b_lr_f[0], b_lr_b[0]]).reshape(1, -1)
    bc_f, bc_b = _decay(hb, w_lr, jnp.stack(_hi_lo(w2)), b2)

    cos_full, sin_signed = _rope_tables()
    qr, kr = _qkprep(zm, cos_full, sin_signed, q_norm[0], k_norm[0])
    attn = _attention(qr, kr, zm)
    o_f = _gla(zm, bc_f, False)
    o_b = _gla(zm, bc_b, True)

    wr = jnp.zeros((D_MODEL, 128), F32).at[:, :N_EXPERTS].set(w_router[0])
    br = jnp.full((1, 128), -jnp.inf, F32).at[0, :N_EXPERTS].set(b_router[0])
    vec = lambda v: v[0].reshape(1, -1)
    h1, idx_pad, gate_pad = _mixer(
        attn, o_f, o_b, zm, zg, hf, w_br_a[0].astype(BF16), w_br_b[0].astype(BF16), w_o[0].astype(BF16),
        vec(gla_norm), vec(ln1_g), vec(ln1_b), jnp.stack(_hi_lo(wr)), br)

    n_sb, sb_e, sb_nblk, sb_nvalid, tab = _routing(idx_pad[:, :TOP_K])
    y_slots = _experts(n_sb, sb_e, sb_nblk, sb_nvalid, tab, h1, w_up[0], b_up[0], w_down[0], b_down[0])

    out = _tail(y_slots, gate_pad, h1, p[0].reshape(TOK, PLE_DIM), w_ple_gate[0].astype(BF16),
                w_ple_proj[0].astype(BF16), vec(ln2_g), vec(ln2_b), vec(ln3_g), vec(ln3_b))
    return out.reshape(BATCH, SEQ, D_MODEL)
```

```python
import functools

import jax
import jax.numpy as jnp
from jax import lax
from jax.experimental import pallas as pl
from jax.experimental.pallas import tpu as pltpu

F32 = jnp.float32
BF16 = jnp.bfloat16

D_MODEL = 2048
BATCH = 2
SEQ = 4096
TOK = BATCH * SEQ
PLE_DIM = 256
GRID_W = 64
ATT_HEADS = 8
ATT_KV_HEADS = 2
ATT_GROUP = ATT_HEADS // ATT_KV_HEADS
HEAD_DIM = 128
ROPE_BASE = 10000.0
GLA_HEADS = 4
GLA_DK = 128
GLA_DV = 256
GLA_RANK = 16
GLA_TAU = 16.0
GLA_CHUNK = 64
GLA_SUB = 16
N_EXPERTS = 32
TOP_K = 4
D_FF = D_MODEL
SWIGLU_LIMIT = 7.0
SWIGLU_ALPHA = 1.702
MOE_BLOCK = 128
ATT_Q = ATT_HEADS * HEAD_DIM
ATT_KV = ATT_KV_HEADS * HEAD_DIM
GLA_K = GLA_HEADS * GLA_DK
GLA_V = GLA_HEADS * GLA_DV
MAIN_W = ATT_Q + 2 * ATT_KV + 2 * GLA_K + 2 * GLA_V
LR_W = 2 * GLA_RANK
DEPTH = 1
DN_ALPHA = (2 * DEPTH) ** 0.25
LN_EPS = 1e-5
RMS_EPS = 1e-6
LOG2_E = 1.4426950408889634

Z_TILE_OF_W_TILE = (0, 1, 6, 7, 8, 2, 3, 4, 5)
Z_Q, Z_VB, Z_OG, Z_KA, Z_VA, Z_QB, Z_KB = 0, 1024, 2048, 3072, 3328, 3584, 4096

VMEM_LIMIT = 56 * 1024 * 1024

N_ROWS = TOK * TOP_K + N_EXPERTS * MOE_BLOCK
N_BLOCKS = N_ROWS // MOE_BLOCK
SB_BLOCKS = 10
SB_ROWS = SB_BLOCKS * MOE_BLOCK
N_SB = (N_BLOCKS + (SB_BLOCKS - 1) * N_EXPERTS) // SB_BLOCKS
SUBLANES = 8
IDS_STRIDE = -(-2 * SB_ROWS // 1024) * 1024
FF_TILE = 256
N_FF_TILES = D_FF // FF_TILE


def _params(n_axes, vmem=None):
    return pltpu.CompilerParams(dimension_semantics=("arbitrary",) * n_axes,
                                vmem_limit_bytes=vmem or VMEM_LIMIT)


def _sigmoid(x):
    return 1.0 / (1.0 + jnp.exp(-x))


def _layer_norm(y, g, b):
    mu = jnp.mean(y, axis=-1, keepdims=True)
    yc = y - mu
    var = jnp.mean(yc * yc, axis=-1, keepdims=True)
    return yc * lax.rsqrt(var + LN_EPS) * g + b


def _dot(a, b):
    return jnp.dot(a, b, preferred_element_type=F32)


def _hi_lo(x):
    hi = x.astype(BF16)
    return hi, (x - hi.astype(F32)).astype(BF16)


def _dot_split(x, w_ref):
    hi, lo = _hi_lo(x)
    return _dot(hi, w_ref[0]) + _dot(hi, w_ref[1]) + _dot(lo, w_ref[0])


def _dot_nt(a, b):
    return lax.dot_general(a, b, (((1,), (1,)), ((), ())), preferred_element_type=F32)


def _dot_tn(a, b):
    return lax.dot_general(a, b, (((0,), (0,)), ((), ())), preferred_element_type=F32)


def _ln0_body(x_ref, g_ref, b_ref, hf_ref, hb_ref):
    y = _layer_norm(x_ref[...], g_ref[...], b_ref[...])
    hf_ref[...] = y
    hb_ref[...] = y.astype(BF16)


def _ln0(x2, g, b):
    tm = 256
    row = pl.BlockSpec((tm, D_MODEL), lambda i: (i, 0))
    vec = pl.BlockSpec((1, D_MODEL), lambda i: (0, 0))
    return pl.pallas_call(
        _ln0_body, grid=(TOK // tm,), in_specs=[row, vec, vec], out_specs=[row, row],
        out_shape=[jax.ShapeDtypeStruct((TOK, D_MODEL), F32), jax.ShapeDtypeStruct((TOK, D_MODEL), BF16)],
        compiler_params=_params(1), name="ln0")(x2, g.reshape(1, -1), b.reshape(1, -1))


def _inproj_body(perm_ref, a_ref, w_ref, o_ref, wb_ref, *, gate, w_transposed):
    del perm_ref

    @pl.when(pl.program_id(1) == 0)
    def _():
        w = w_ref[...].T if w_transposed else w_ref[...]
        wb_ref[...] = w.astype(BF16)

    acc = _dot(a_ref[...], wb_ref[...])
    if gate:
        acc = _sigmoid(acc)
    o_ref[...] = acc.astype(o_ref.dtype)


def _inproj(hb, w, n_tiles, tile_perm, gate, w_transposed, name):
    tm, tn = 1024, 512
    w_spec = (pl.BlockSpec((tn, D_MODEL), lambda n, m, p: (n, 0)) if w_transposed
              else pl.BlockSpec((D_MODEL, tn), lambda n, m, p: (0, n)))
    grid_spec = pltpu.PrefetchScalarGridSpec(
        num_scalar_prefetch=1, grid=(n_tiles, TOK // tm),
        in_specs=[pl.BlockSpec((tm, D_MODEL), lambda n, m, p: (m, 0)), w_spec],
        out_specs=pl.BlockSpec((tm, tn), lambda n, m, p: (m, p[n])),
        scratch_shapes=[pltpu.VMEM((D_MODEL, tn), BF16)])
    return pl.pallas_call(
        functools.partial(_inproj_body, gate=gate, w_transposed=w_transposed), grid_spec=grid_spec,
        out_shape=jax.ShapeDtypeStruct((TOK, n_tiles * tn), BF16),
        compiler_params=_params(2), name=name)(jnp.asarray(tile_perm, jnp.int32), hb, w)


def _wprep_body(a_ref, b_ref, g_ref, lr_ref):
    g_ref[...] = jnp.concatenate([a_ref[LR_W:, :], b_ref[...]], axis=0).T.astype(BF16)

    @pl.when(pl.program_id(0) == 0)
    def _():
        lane = lax.broadcasted_iota(jnp.int32, (D_MODEL, 128), 1)
        lr_ref[...] = jnp.where(lane < LR_W, a_ref[:128, :].T, 0.0).astype(BF16)


def _gate_weights(wt):
    tn = 512
    first = MAIN_W // tn
    return pl.pallas_call(
        _wprep_body, grid=(2 * D_MODEL // tn,),
        in_specs=[pl.BlockSpec((tn, D_MODEL), lambda i: (first + i, 0)),
                  pl.BlockSpec((LR_W, D_MODEL), lambda i: ((first + i + 1) * (tn // LR_W), 0))],
        out_specs=[pl.BlockSpec((D_MODEL, tn), lambda i: (0, i)), pl.BlockSpec((D_MODEL, 128), lambda i: (0, 0))],
        out_shape=[jax.ShapeDtypeStruct((D_MODEL, 2 * D_MODEL), BF16), jax.ShapeDtypeStruct((D_MODEL, 128), BF16)],
        compiler_params=_params(1), name="gate_w")(wt, wt)


def _decay_body(h_ref, wlr_ref, w2_ref, b2_ref, trif_ref, trib_ref, bf_ref, bb_ref):
    zlr = _dot(h_ref[...], wlr_ref[...])
    pre = _dot_split(zlr, w2_ref) + b2_ref[...]
    la = (jnp.minimum(pre, 0.0) - jnp.log1p(jnp.exp(-jnp.abs(pre)))) * (1.0 / GLA_TAU)
    hi = la.astype(BF16)
    lo = (la - hi.astype(F32)).astype(BF16)
    bf_ref[...] = _dot(trif_ref[...], hi[:, :GLA_K]) + _dot(trif_ref[...], lo[:, :GLA_K])
    bb_ref[...] = _dot(trib_ref[...], hi[:, GLA_K:]) + _dot(trib_ref[...], lo[:, GLA_K:])


def _decay(hb, wlr, w2, b2):
    tm = 512
    r = jnp.arange(tm)
    same = (r[:, None] // GLA_CHUNK) == (r[None, :] // GLA_CHUNK)
    trif = (same & (r[None, :] <= r[:, None])).astype(BF16)
    trib = (same & (r[None, :] >= r[:, None])).astype(BF16)
    full = lambda shape: pl.BlockSpec(shape, lambda i: (0,) * len(shape))
    out = pl.BlockSpec((tm, GLA_K), lambda i: (i, 0))
    return pl.pallas_call(
        _decay_body, grid=(TOK // tm,),
        in_specs=[pl.BlockSpec((tm, D_MODEL), lambda i: (i, 0)), full((D_MODEL, 128)),
                  full((2, 128, 2 * GLA_K)), full((1, 2 * GLA_K)), full((tm, tm)), full((tm, tm))],
        out_specs=[out, out],
        out_shape=[jax.ShapeDtypeStruct((TOK, GLA_K), F32)] * 2,
        compiler_params=_params(1), name="decay")(hb, wlr, w2, b2, trif, trib)


def _qkprep_body(q_ref, k_ref, cos_ref, sin_ref, qn_ref, kn_ref, qo_ref, ko_ref):
    cos = cos_ref[...]
    sin = sin_ref[...]
    lane = lax.broadcasted_iota(jnp.int32, cos.shape, 1)
    even = (lane % 2) == 0

    def one(x, gain, scale):
        x = x.astype(F32)
        x = x * lax.rsqrt(jnp.mean(x * x, axis=-1, keepdims=True) + RMS_EPS) * gain
        partner = jnp.where(even, pltpu.roll(x, HEAD_DIM - 1, 1), pltpu.roll(x, 1, 1))
        return ((x * cos + partner * sin) * scale).astype(BF16)

    for hh in range(ATT_HEADS):
        sl = slice(hh * HEAD_DIM, (hh + 1) * HEAD_DIM)
        qo_ref[:, sl] = one(q_ref[:, sl], qn_ref[...], LOG2_E * HEAD_DIM ** -0.5)
    for hh in range(ATT_KV_HEADS):
        sl = slice(hh * HEAD_DIM, (hh + 1) * HEAD_DIM)
        ko_ref[:, sl] = one(k_ref[:, sl], kn_ref[...], 1.0)


def _qkprep(zm, cos_full, sin_signed, q_norm, k_norm):
    tm = 256
    nrow = SEQ // tm
    tab = pl.BlockSpec((tm, HEAD_DIM), lambda i: (i % nrow, 0))
    vec = pl.BlockSpec((1, HEAD_DIM), lambda i: (0, 0))
    return pl.pallas_call(
        _qkprep_body, grid=(TOK // tm,),
        in_specs=[pl.BlockSpec((tm, ATT_Q), lambda i: (i, Z_Q // ATT_Q)),
                  pl.BlockSpec((tm, ATT_KV), lambda i: (i, Z_KA // ATT_KV)), tab, tab, vec, vec],
        out_specs=[pl.BlockSpec((tm, ATT_Q), lambda i: (i, 0)), pl.BlockSpec((tm, ATT_KV), lambda i: (i, 0))],
        out_shape=[jax.ShapeDtypeStruct((TOK, ATT_Q), BF16), jax.ShapeDtypeStruct((TOK, ATT_KV), BF16)],
        compiler_params=_params(1), name="qkprep")(zm, zm, cos_full, sin_signed,
                                                   q_norm.reshape(1, -1), k_norm.reshape(1, -1))


ATT_TQ = 512
ATT_TK = 512


def _attn_body(q_ref, k_ref, v_ref, o_ref, qs_ref, s0_ref, s1_ref, p0_ref, p1_ref, a0_ref, a1_ref,
               acc_ref, m_ref, l_ref):
    rows = ATT_GROUP * ATT_TQ
    n_chunks = SEQ // ATT_TK
    groups = ATT_TK // SUBLANES
    for g in range(ATT_GROUP):
        qs_ref[g * ATT_TQ:(g + 1) * ATT_TQ, :] = q_ref[:, g * HEAD_DIM:(g + 1) * HEAD_DIM]
    m_ref[...] = jnp.full(m_ref.shape, -jnp.inf, F32)
    l_ref[...] = jnp.zeros(l_ref.shape, F32)
    acc_ref[...] = jnp.zeros(acc_ref.shape, F32)
    p1_ref[...] = jnp.zeros(p1_ref.shape, BF16)
    a1_ref[...] = jnp.zeros(a1_ref.shape, F32)

    def chunk(c):
        return pl.ds(pl.multiple_of(c * ATT_TK, ATT_TK), ATT_TK)

    lane_blocks = [slice(cb * HEAD_DIM, (cb + 1) * HEAD_DIM) for cb in range(rows // HEAD_DIM)]

    def scores(kc, s_out):
        qk = _dot_nt(k_ref[kc, :], qs_ref[...])
        for i, cb in enumerate(lane_blocks):
            s_out[i] = qk[:, cb]

    def probs_t(p_in):
        return jnp.concatenate([p_in[i] for i in range(len(lane_blocks))], axis=1)

    scores(slice(0, ATT_TK), s0_ref)

    def step(c, s_cur, s_nxt, p_cur, p_prv, a_cur, a_prv):
        c_next = jnp.where(c + 1 < n_chunks, c + 1, 0)
        c_prev = jnp.where(c > 0, c - 1, 0)
        scores(chunk(c_next), s_nxt)
        acc_ref[...] = a_prv[0:1, :] * acc_ref[...] + _dot_tn(v_ref[chunk(c_prev), :], probs_t(p_prv))
        m_new = []
        for i, cb in enumerate(lane_blocks):
            part = jnp.max(s_cur[i].reshape(groups, SUBLANES, HEAD_DIM), axis=0)
            m_new.append(jnp.maximum(m_ref[0:1, cb], jnp.max(part, axis=0, keepdims=True)))
        for i, (cb, mn) in enumerate(zip(lane_blocks, m_new)):
            alpha = jnp.exp2(m_ref[0:1, cb] - mn)
            p = jnp.exp2(s_cur[i] - mn)
            l_ref[:, cb] = alpha * l_ref[:, cb] + jnp.sum(p.reshape(groups, SUBLANES, HEAD_DIM), axis=0)
            a_cur[:, cb] = jnp.broadcast_to(alpha, (SUBLANES, HEAD_DIM))
            p_cur[i] = p.astype(BF16)
        for cb, mn in zip(lane_blocks, m_new):
            m_ref[:, cb] = jnp.broadcast_to(mn, (SUBLANES, HEAD_DIM))

    def pair(i, carry):
        step(2 * i, s0_ref, s1_ref, p0_ref, p1_ref, a0_ref, a1_ref)
        step(2 * i + 1, s1_ref, s0_ref, p1_ref, p0_ref, a1_ref, a0_ref)
        return carry

    lax.fori_loop(0, n_chunks // 2, pair, 0)
    acc = a1_ref[0:1, :] * acc_ref[...] + _dot_tn(v_ref[SEQ - ATT_TK:SEQ, :], probs_t(p1_ref))
    out = (acc / jnp.sum(l_ref[...], axis=0, keepdims=True)).T.astype(BF16)
    for g in range(ATT_GROUP):
        o_ref[:, g * HEAD_DIM:(g + 1) * HEAD_DIM] = out[g * ATT_TQ:(g + 1) * ATT_TQ]


def _attention(qr, kr, zm):
    nq = SEQ // ATT_TQ
    gw = ATT_GROUP * HEAD_DIM
    rows = ATT_GROUP * ATT_TQ
    qspec = pl.BlockSpec((ATT_TQ, gw), lambda b, j, i: (b * nq + i, j))
    return pl.pallas_call(
        _attn_body, grid=(BATCH, ATT_KV_HEADS, nq),
        in_specs=[qspec,
                  pl.BlockSpec((SEQ, HEAD_DIM), lambda b, j, i: (b, j)),
                  pl.BlockSpec((SEQ, HEAD_DIM), lambda b, j, i: (b, Z_VA // HEAD_DIM + j))],
        out_specs=qspec,
        out_shape=jax.ShapeDtypeStruct((TOK, ATT_Q), BF16),
        scratch_shapes=[pltpu.VMEM((rows, HEAD_DIM), BF16),
                        pltpu.VMEM((rows // HEAD_DIM, ATT_TK, HEAD_DIM), F32),
                        pltpu.VMEM((rows // HEAD_DIM, ATT_TK, HEAD_DIM), F32),
                        pltpu.VMEM((rows // HEAD_DIM, ATT_TK, HEAD_DIM), BF16),
                        pltpu.VMEM((rows // HEAD_DIM, ATT_TK, HEAD_DIM), BF16),
                        pltpu.VMEM((SUBLANES, rows), F32), pltpu.VMEM((SUBLANES, rows), F32),
                        pltpu.VMEM((HEAD_DIM, rows), F32), pltpu.VMEM((SUBLANES, rows), F32),
                        pltpu.VMEM((SUBLANES, rows), F32)],
        compiler_params=_params(3), name="attn")(qr, kr, zm)


GLA_CB = 4
GLA_RB = GLA_CB * GLA_CHUNK
N_SUB = GLA_CHUNK // GLA_SUB


def _gla_body(q_ref, k_ref, v_ref, bc_ref, o_ref, st_ref, kf_ref, bs_ref, *, rev):
    @pl.when(pl.program_id(1) == 0)
    def _():
        st_ref[...] = jnp.zeros_like(st_ref)

    C, SUB = GLA_CHUNK, GLA_SUB
    rowc = lax.broadcasted_iota(jnp.int32, (C, GLA_DK), 0)
    rows_s = lax.broadcasted_iota(jnp.int32, (SUB, 128), 0)
    lane_s = lax.broadcasted_iota(jnp.int32, (SUB, 128), 1)

    def chunk(ci, carry):
        c = (GLA_CB - 1 - ci) if rev else ci
        r0 = pl.multiple_of(c * C, C)
        for hh in range(GLA_HEADS):
            ksl = slice(hh * GLA_DK, (hh + 1) * GLA_DK)
            vsl = slice(hh * GLA_DV, (hh + 1) * GLA_DV)
            q = q_ref[pl.ds(r0, C), ksl].astype(F32) * (GLA_DK ** -0.5)
            k = k_ref[pl.ds(r0, C), ksl].astype(F32)
            v = v_ref[pl.ds(r0, C), vsl]
            bc = bc_ref[pl.ds(r0, C), ksl]
            kf_ref[hh] = k
            bs_ref[hh] = bc
            st = st_ref[hh]
            blast = bc[0:1] if rev else bc[C - 1:C]
            o_inter = _dot_nt((q * jnp.exp(bc)).astype(BF16), st.astype(BF16))
            kdec = k * jnp.exp(blast - bc)
            st_ref[hh] = st * jnp.exp(blast) + _dot_tn(v, kdec.astype(BF16))

            a_rows = []
            for si in range(N_SUB):
                lo, hi = si * SUB, (si + 1) * SUB
                q_s, b_s = q[lo:hi], bc[lo:hi]
                has_earlier = (si < N_SUB - 1) if rev else (si > 0)
                if has_earlier:
                    ref_row = bc[hi:hi + 1] if rev else bc[lo - 1:lo]
                    earlier = (rowc >= hi) if rev else (rowc < lo)
                    qt = q_s * jnp.exp(b_s - ref_row)
                    kt = k * jnp.exp(jnp.where(earlier, ref_row - bc, -jnp.inf))
                    a = _dot_nt(qt.astype(BF16), kt.astype(BF16))
                else:
                    a = jnp.zeros((SUB, C), F32)
                diag = jnp.zeros((SUB, 128), F32)
                for jl in range(SUB):
                    j = lo + jl
                    d = jnp.minimum(b_s - bs_ref[hh, j:j + 1, :], 0.0)
                    col = jnp.sum(q_s * kf_ref[hh, j:j + 1, :] * jnp.exp(d), axis=-1, keepdims=True)
                    diag = jnp.where(lane_s == j, col, diag)
                keep = (lane_s >= rows_s + lo) if rev else (lane_s <= rows_s + lo)
                diag = jnp.where(keep & (lane_s >= lo) & (lane_s < hi), diag, 0.0)
                a_rows.append(a + diag[:, :C])
            a_full = jnp.concatenate(a_rows, axis=0)
            o_ref[pl.ds(r0, C), vsl] = o_inter + _dot(a_full.astype(BF16), v)
        return carry

    lax.fori_loop(0, GLA_CB, chunk, 0)


def _gla(zm, bcum, rev):
    ncb = SEQ // GLA_RB
    if rev:
        row = lambda b, c: b * ncb + (ncb - 1 - c)
    else:
        row = lambda b, c: b * ncb + c
    return pl.pallas_call(
        functools.partial(_gla_body, rev=rev), grid=(BATCH, ncb),
        in_specs=[pl.BlockSpec((GLA_RB, GLA_K), lambda b, c: (row(b, c), Z_QB // GLA_K)),
                  pl.BlockSpec((GLA_RB, GLA_K), lambda b, c: (row(b, c), Z_KB // GLA_K)),
                  pl.BlockSpec((GLA_RB, GLA_V), lambda b, c: (row(b, c), Z_VB // GLA_V)),
                  pl.BlockSpec((GLA_RB, GLA_K), lambda b, c: (row(b, c), 0))],
        out_specs=pl.BlockSpec((GLA_RB, GLA_V), lambda b, c: (row(b, c), 0)),
        out_shape=jax.ShapeDtypeStruct((TOK, GLA_V), F32),
        scratch_shapes=[pltpu.VMEM((GLA_HEADS, GLA_DV, GLA_DK), F32),
                        pltpu.VMEM((GLA_HEADS, GLA_CHUNK, GLA_DK), F32),
                        pltpu.VMEM((GLA_HEADS, GLA_CHUNK, GLA_DK), F32)],
        compiler_params=_params(2), name="gla_bwd" if rev else "gla_fwd")(zm, zm, zm, bcum)


MIX_TM = 512


def _onorm_body(of_ref, ob_ref, og_ref, gn_ref, o_ref):
    gn = gn_ref[...]
    for hh in range(GLA_HEADS):
        sl = slice(hh * GLA_DV, (hh + 1) * GLA_DV)
        x = of_ref[:, sl] + ob_ref[:, sl]
        g = og_ref[:, sl].astype(F32)
        xn = x * lax.rsqrt(jnp.mean(x * x, axis=-1, keepdims=True) + RMS_EPS) * gn
        o_ref[:, sl] = (xn * (g * _sigmoid(g))).astype(BF16)


def _merge_body(attn_ref, on_ref, ga_ref, gb_ref, wa_ref, wb_ref, m_ref):
    ya = _dot(attn_ref[...], wa_ref[...])
    yb = _dot(on_ref[...], wb_ref[...])
    m_ref[...] = (ga_ref[...].astype(F32) * ya + gb_ref[...].astype(F32) * yb).astype(BF16)


def _outproj_body(m_ref, h_ref, wo_ref, lg_ref, lb_ref, wr_ref, br_ref, h1_ref, idx_ref, gate_ref):
    mix = _dot(m_ref[...], wo_ref[...])
    h1 = _layer_norm(DN_ALPHA * h_ref[...] + mix, lg_ref[...], lb_ref[...])
    h1_ref[...] = h1

    logits = _dot_split(h1, wr_ref) + br_ref[...]
    lane = lax.broadcasted_iota(jnp.int32, logits.shape, 1)
    x = logits
    vals, idxs = [], []
    for _ in range(TOP_K):
        mx = jnp.max(x, axis=-1, keepdims=True)
        ix = jnp.min(jnp.where(x == mx, lane, 128), axis=-1, keepdims=True)
        vals.append(mx)
        idxs.append(ix)
        x = jnp.where(lane == ix, -jnp.inf, x)
    es = [jnp.exp(vv - vals[0]) for vv in vals]
    den = es[0] + es[1] + es[2] + es[3]
    idx_out = jnp.zeros(logits.shape, jnp.int32)
    gate_out = jnp.zeros(logits.shape, F32)
    for kk in range(TOP_K):
        idx_out = jnp.where(lane == kk, idxs[kk], idx_out)
        gate_out = jnp.where(lane == kk, es[kk] / den, gate_out)
    idx_ref[...] = idx_out
    gate_ref[...] = gate_out


def _mixer(attn, o_f, o_b, zm, zg, hf, wa, wb, wo, gn, lg, lb, wr, br):
    tm = MIX_TM
    const = lambda shape: pl.BlockSpec(shape, lambda i: (0,) * len(shape), pipeline_mode=pl.Buffered(1))
    row = lambda w, cb=0: pl.BlockSpec((tm, w), lambda i: (i, cb))
    grid = (TOK // tm,)
    onorm = pl.pallas_call(
        _onorm_body, grid=grid,
        in_specs=[row(GLA_V), row(GLA_V), row(GLA_V, Z_OG // GLA_V), const((1, GLA_DV))],
        out_specs=row(GLA_V), out_shape=jax.ShapeDtypeStruct((TOK, GLA_V), BF16),
        compiler_params=_params(1), name="onorm")(o_f, o_b, zm, gn)
    merged = pl.pallas_call(
        _merge_body, grid=grid,
        in_specs=[row(ATT_Q), row(GLA_V), row(D_MODEL, 0), row(D_MODEL, 1),
                  const((ATT_Q, D_MODEL)), const((GLA_V, D_MODEL))],
        out_specs=row(D_MODEL), out_shape=jax.ShapeDtypeStruct((TOK, D_MODEL), BF16),
        compiler_params=_params(1), name="merge")(attn, onorm, zg, zg, wa, wb)
    return pl.pallas_call(
        _outproj_body, grid=grid,
        in_specs=[row(D_MODEL), row(D_MODEL), const((D_MODEL, D_MODEL)), const((1, D_MODEL)), const((1, D_MODEL)),
                  const((2, D_MODEL, 128)), const((1, 128))],
        out_specs=[row(D_MODEL), row(128), row(128)],
        out_shape=[jax.ShapeDtypeStruct((TOK, D_MODEL), F32),
                   jax.ShapeDtypeStruct((TOK, 128), jnp.int32),
                   jax.ShapeDtypeStruct((TOK, 128), F32)],
        compiler_params=_params(1), name="outproj")(merged, hf, wo, lg, lb, wr, br)


GROUP_SIZES = (512, 256, 128)
GATHER_STEPS = SB_BLOCKS // (SB_ROWS // GROUP_SIZES[0])


def _expert_body(e_ref, nb_ref, nv_ref, tab_hbm, h1_hbm, wg_ref, wl_ref, wd_ref, bg_ref, bl_ref, bd_ref,
                 y_hbm, gbuf, xb_ref, acc_ref, wgb_ref, wlb_ref, wdb_ref, ids, sem_ids, sem_g, sem_s):
    del e_ref
    s = pl.program_id(0)
    j = pl.program_id(1)
    nblk = nb_ref[s]
    slot = s & 1
    has_next = s + 1 < pl.num_programs(0)
    nxt = jnp.minimum(s + 1, N_SB - 1)
    nblk_next = jnp.where(has_next, nb_ref[nxt], 0)
    tiles = MOE_BLOCK // SUBLANES

    def ids_copy(sb, sl):
        return pltpu.make_async_copy(tab_hbm.at[pl.ds(pl.multiple_of(sb * IDS_STRIDE, IDS_STRIDE), IDS_STRIDE)],
                                     ids.at[pl.ds(pl.multiple_of(sl * IDS_STRIDE, IDS_STRIDE), IDS_STRIDE)],
                                     sem_ids.at[sl])

    def gather_block(sl, b):
        id0 = sl * IDS_STRIDE + b * MOE_BLOCK

        def issue(t, c):
            for u in range(SUBLANES):
                tok = ids[id0 + t * SUBLANES + u]
                pltpu.make_async_copy(h1_hbm.at[pl.ds(tok >> 3, 1), pl.ds(tok & 7, 1), :],
                                      gbuf.at[pl.ds(b * tiles + t, 1), pl.ds(u, 1), :], sem_g).start()
            return c

        lax.fori_loop(0, tiles, issue, 0)

    def gather_wait_block(b):
        pltpu.make_async_copy(h1_hbm.at[pl.ds(0, tiles)], gbuf.at[pl.ds(b * tiles, tiles)], sem_g).wait()

    def loop_blocks(n, fn):
        def body(b, c):
            fn(b)
            return c

        lax.fori_loop(0, n, body, 0)

    @pl.when((s == 0) & (j == 0))
    def _():
        ids_copy(0, 0).start()
        ids_copy(0, 0).wait()
        loop_blocks(nblk, lambda b: gather_block(0, b))

    big = GROUP_SIZES[0]
    big_blocks = big // MOE_BLOCK
    n_big = nblk // big_blocks
    in_window = (j >= 1) & (j <= GATHER_STEPS) & has_next
    issued_here = GATHER_STEPS * n_big
    issued_prev = jnp.where(s > 0, GATHER_STEPS * (nb_ref[jnp.maximum(s - 1, 0)] // big_blocks), 0)

    @pl.when(j == 0)
    def _():
        loop_blocks(jnp.maximum(issued_prev, nblk), gather_wait_block)

    @pl.when(nblk > 0)
    def _():
        @pl.when(j == 0)
        def _():
            @pl.when(has_next)
            def _():
                ids_copy(nxt, 1 - slot).start()

            def take_block(b):
                x = gbuf[pl.ds(b * tiles, tiles)].reshape(MOE_BLOCK, D_MODEL)
                xb_ref[pl.ds(pl.multiple_of(b * MOE_BLOCK, MOE_BLOCK), MOE_BLOCK), :] = x.astype(BF16)
                acc_ref[pl.ds(b * tiles, tiles)] = jnp.broadcast_to(bd_ref[0], (tiles, SUBLANES, D_MODEL))

            loop_blocks(nblk, take_block)

        @pl.when((j == 1) & has_next)
        def _():
            ids_copy(nxt, 1 - slot).wait()

        @pl.when(j == GATHER_STEPS + 1)
        def _():
            for b in range(SB_BLOCKS):
                @pl.when((b >= issued_here) & (b < nblk_next))
                def _(b=b):
                    gather_block(1 - slot, b)

        wgb_ref[...] = wg_ref[0].astype(BF16)
        wlb_ref[...] = wl_ref[0].astype(BF16)
        wdb_ref[...] = wd_ref[0].astype(BF16)

        def group(r0, size):
            x = xb_ref[pl.ds(pl.multiple_of(r0, MOE_BLOCK), size), :]
            g = jnp.minimum(_dot(x, wgb_ref[...]) + bg_ref[0], SWIGLU_LIMIT)
            lin = jnp.clip(_dot(x, wlb_ref[...]) + bl_ref[0], -SWIGLU_LIMIT, SWIGLU_LIMIT)
            act = g * _sigmoid(SWIGLU_ALPHA * g) * (lin + 1.0)
            upd = _dot(act.astype(BF16), wdb_ref[...])
            acc_ref[pl.ds(r0 // SUBLANES, size // SUBLANES)] += upd.reshape(size // SUBLANES, SUBLANES, D_MODEL)

        def big_group(i, c):
            group(i * big, big)
            return c

        def big_group_and_gather(i, c):
            group(i * big, big)
            b = (j - 1) * n_big + i
            id0 = (1 - slot) * IDS_STRIDE + b * MOE_BLOCK
            for t in range(tiles):
                for u in range(SUBLANES):
                    tok = ids[id0 + t * SUBLANES + u]
                    pltpu.make_async_copy(h1_hbm.at[pl.ds(tok >> 3, 1), pl.ds(tok & 7, 1), :],
                                          gbuf.at[pl.ds(b * tiles + t, 1), pl.ds(u, 1), :], sem_g).start()
            return c

        @pl.when(in_window)
        def _():
            lax.fori_loop(0, n_big, big_group_and_gather, 0)

        @pl.when(jnp.logical_not(in_window))
        def _():
            lax.fori_loop(0, n_big, big_group, 0)

        done = n_big * big_blocks
        for size in GROUP_SIZES[1:]:
            take = ((nblk - done) // (size // MOE_BLOCK)) > 0

            @pl.when(take)
            def _(done=done, size=size):
                group(done * MOE_BLOCK, size)

            done = done + jnp.where(take, size // MOE_BLOCK, 0)

        @pl.when(j == N_FF_TILES - 1)
        def _():
            nvalid = nv_ref[s]

            dst0 = slot * IDS_STRIDE + SB_ROWS

            def row_copy(t, u, dst):
                return pltpu.make_async_copy(acc_ref.at[pl.ds(t, 1), pl.ds(u, 1), :],
                                             y_hbm.at[pl.ds(dst >> 3, 1), pl.ds(dst & 7, 1), :], sem_s)

            def issue_tile(t, c):
                for u in range(SUBLANES):
                    row_copy(t, u, ids[dst0 + t * SUBLANES + u]).start()
                return c

            def issue_row(r, c):
                row_copy(r >> 3, r & 7, ids[dst0 + r]).start()
                return c

            full_tiles = nvalid // SUBLANES
            lax.fori_loop(0, full_tiles, issue_tile, 0)
            lax.fori_loop(full_tiles * SUBLANES, nvalid, issue_row, 0)

            def wait_block(b, c):
                pltpu.make_async_copy(acc_ref.at[pl.ds(0, tiles)], y_hbm.at[pl.ds(0, tiles)], sem_s).wait()
                return c

            def wait_row(r, c):
                row_copy(0, 0, 0).wait()
                return c

            nfull = nvalid // MOE_BLOCK
            lax.fori_loop(0, nfull, wait_block, 0)
            lax.fori_loop(nfull * MOE_BLOCK, nvalid, wait_row, 0)


def _experts(n_sb, sb_e, sb_nblk, sb_nvalid, tab, h1, w_up, b_up, w_down, b_down):
    def jj(j, nb, s):
        return jnp.where(nb[s] > 0, j, N_FF_TILES - 1)

    grid_spec = pltpu.PrefetchScalarGridSpec(
        num_scalar_prefetch=3, grid=(n_sb, N_FF_TILES),
        in_specs=[pl.BlockSpec(memory_space=pl.ANY),
                  pl.BlockSpec(memory_space=pl.ANY),
                  pl.BlockSpec((1, D_MODEL, FF_TILE), lambda s, j, e, nb, nv: (e[s], 0, jj(j, nb, s))),
                  pl.BlockSpec((1, D_MODEL, FF_TILE), lambda s, j, e, nb, nv: (e[s], 0, N_FF_TILES + jj(j, nb, s))),
                  pl.BlockSpec((1, FF_TILE, D_MODEL), lambda s, j, e, nb, nv: (e[s], jj(j, nb, s), 0)),
                  pl.BlockSpec((1, 1, FF_TILE), lambda s, j, e, nb, nv: (e[s], 0, jj(j, nb, s))),
                  pl.BlockSpec((1, 1, FF_TILE), lambda s, j, e, nb, nv: (e[s], 0, N_FF_TILES + jj(j, nb, s))),
                  pl.BlockSpec((1, 1, D_MODEL), lambda s, j, e, nb, nv: (e[s], 0, 0))],
        out_specs=pl.BlockSpec(memory_space=pl.ANY),
        scratch_shapes=[pltpu.VMEM((SB_ROWS // SUBLANES, SUBLANES, D_MODEL), F32),
                        pltpu.VMEM((SB_ROWS, D_MODEL), BF16),
                        pltpu.VMEM((SB_ROWS // SUBLANES, SUBLANES, D_MODEL), F32),
                        pltpu.VMEM((D_MODEL, FF_TILE), BF16),
                        pltpu.VMEM((D_MODEL, FF_TILE), BF16),
                        pltpu.VMEM((FF_TILE, D_MODEL), BF16),
                        pltpu.SMEM((2 * IDS_STRIDE,), jnp.int32),
                        pltpu.SemaphoreType.DMA((2,)),
                        pltpu.SemaphoreType.DMA(()),
                        pltpu.SemaphoreType.DMA(())])
    y = pl.pallas_call(
        _expert_body, grid_spec=grid_spec,
        out_shape=jax.ShapeDtypeStruct((TOP_K * TOK // SUBLANES, SUBLANES, D_MODEL), F32),
        compiler_params=_params(2), name="experts")(
            sb_e, sb_nblk, sb_nvalid, tab, h1.reshape(TOK // SUBLANES, SUBLANES, D_MODEL), w_up, w_up, w_down,
            b_up.reshape(N_EXPERTS, 1, 2 * D_FF), b_up.reshape(N_EXPERTS, 1, 2 * D_FF),
            b_down.reshape(N_EXPERTS, 1, D_MODEL))
    return y.reshape(TOP_K * TOK, D_MODEL)


def _tail_body(y0_ref, y1_ref, y2_ref, y3_ref, gate_ref, h1_ref, p_ref, wg_ref, wp_ref,
               l2g_ref, l2b_ref, l3g_ref, l3b_ref, o_ref):
    gate = gate_ref[...]
    y = gate[:, 0:1] * y0_ref[...]
    for kk, y_ref in enumerate((y1_ref, y2_ref, y3_ref), start=1):
        y = y + gate[:, kk:kk + 1] * y_ref[...]
    h2 = _layer_norm(DN_ALPHA * h1_ref[...] + y, l2g_ref[...], l2b_ref[...])
    ple = _sigmoid(_dot(h2.astype(BF16), wg_ref[...])) * _dot(p_ref[...].astype(BF16), wp_ref[...])
    o_ref[...] = _layer_norm(DN_ALPHA * h2 + ple, l3g_ref[...], l3b_ref[...])


def _tail(y_slots, gates, h1, p2, wg, wp, l2g, l2b, l3g, l3b):
    tm = 256
    nt = TOK // tm
    const = lambda shape: pl.BlockSpec(shape, lambda i: (0,) * len(shape), pipeline_mode=pl.Buffered(1))
    row = lambda w: pl.BlockSpec((tm, w), lambda i: (i, 0))
    yspec = lambda kk: pl.BlockSpec((tm, D_MODEL), lambda i: (kk * nt + i, 0))
    vec = const((1, D_MODEL))
    return pl.pallas_call(
        _tail_body, grid=(nt,),
        in_specs=[yspec(0), yspec(1), yspec(2), yspec(3), row(128), row(D_MODEL), row(PLE_DIM),
                  const((D_MODEL, D_MODEL)), const((PLE_DIM, D_MODEL)), vec, vec, vec, vec],
        out_specs=row(D_MODEL),
        out_shape=jax.ShapeDtypeStruct((TOK, D_MODEL), F32),
        compiler_params=_params(1), name="tail")(y_slots, y_slots, y_slots, y_slots, gates, h1, p2, wg, wp,
                                                 l2g, l2b, l3g, l3b)


def _routing(top_idx):
    flat_e = top_idx.reshape(-1)
    experts = jnp.arange(N_EXPERTS, dtype=jnp.int32)
    onehot = (flat_e[:, None] == experts[None, :]).astype(jnp.int32)
    csum = jnp.cumsum(onehot, axis=0)
    rank = jnp.sum(onehot * csum, axis=1) - 1
    counts = csum[-1]
    padded = (counts + MOE_BLOCK - 1) // MOE_BLOCK * MOE_BLOCK
    padded_end = jnp.cumsum(padded)
    padded_start = padded_end - padded
    dest = (padded_start[flat_e] + rank).astype(jnp.int32)
    asg = jnp.arange(TOK * TOP_K, dtype=jnp.int32)
    asg_of_row = jnp.zeros((N_ROWS + SB_ROWS,), jnp.int32).at[dest].set(asg)
    tok = (asg_of_row // TOP_K).reshape(-1, MOE_BLOCK)
    dst = ((asg_of_row % TOP_K) * TOK + asg_of_row // TOP_K).reshape(-1, MOE_BLOCK)
    nb = padded // MOE_BLOCK
    n_sb = (nb + SB_BLOCKS - 1) // SB_BLOCKS
    sb_end = jnp.cumsum(n_sb)
    total = sb_end[-1]
    s = jnp.arange(N_SB, dtype=jnp.int32)
    s_eff = jnp.minimum(s, total - 1)
    e = jnp.minimum(jnp.sum((sb_end[None, :] <= s_eff[:, None]).astype(jnp.int32), axis=1), N_EXPERTS - 1)
    local = s_eff - (sb_end[e] - n_sb[e])
    valid = s < total
    sb_nblk = jnp.where(valid, jnp.clip(nb[e] - local * SB_BLOCKS, 0, SB_BLOCKS), 0).astype(jnp.int32)
    sb_blk = padded_start[e] // MOE_BLOCK + local * SB_BLOCKS
    sb_nvalid = jnp.where(valid, jnp.clip(counts[e] - local * SB_ROWS, 0, SB_ROWS), 0).astype(jnp.int32)
    blocks = sb_blk[:, None] + jnp.arange(SB_BLOCKS, dtype=jnp.int32)[None, :]
    tab = jnp.concatenate([tok[blocks].reshape(N_SB, SB_ROWS), dst[blocks].reshape(N_SB, SB_ROWS),
                           jnp.zeros((N_SB, IDS_STRIDE - 2 * SB_ROWS), jnp.int32)], axis=1).astype(jnp.int32).reshape(-1)
    return total.astype(jnp.int32), e.astype(jnp.int32), sb_nblk, sb_nvalid, tab


def _rope_tables():
    rows = SEQ // GRID_W
    row = jnp.repeat(jnp.arange(rows), GRID_W)
    col = jnp.tile(jnp.arange(GRID_W), rows)
    n_pairs = HEAD_DIM // 4
    inv_freq = ROPE_BASE ** (-jnp.arange(n_pairs, dtype=F32) / n_pairs)
    ang = jnp.concatenate([row[:, None] * inv_freq, col[:, None] * inv_freq], -1)
    cos_full = jnp.repeat(jnp.cos(ang), 2, axis=-1)
    sin = jnp.sin(ang)
    sin_signed = jnp.stack([-sin, sin], axis=-1).reshape(SEQ, HEAD_DIM)
    return cos_full, sin_signed


def kernel(x, p, in_ln_g, in_ln_b, w_in, q_norm, k_norm, w_lr_f, b_lr_f, w_lr_b, b_lr_b, gla_norm, w_br_a, w_br_b, w_o, ln1_g, ln1_b, w_router, b_router, w_up, b_up, w_down, b_down, ln2_g, ln2_b, w_ple_gate, w_ple_proj, ln3_g, ln3_b):
    assert x.shape == (BATCH, SEQ, D_MODEL) and w_in.shape[0] == DEPTH == 1
    wt = w_in[0].T
    hf, hb = _ln0(x.reshape(TOK, D_MODEL), in_ln_g, in_ln_b)

    zm = _inproj(hb, wt, MAIN_W // 512, Z_TILE_OF_W_TILE, False, True, "inproj_main")
    w_gate, w_lr = _gate_weights(wt)
    zg = _inproj(hb, w_gate, 2 * D_MODEL // 512, tuple(range(8)), True, False, "inproj_gates")

    w2 = jnp.zeros((128, 2 * GLA_K), F32)
    w2 = w2.at[:GLA_RANK, :GLA_K].set(w_lr_f[0]).at[GLA_RANK:LR_W, GLA_K:].set(w_lr_b[0])
    b2 = jnp.concatenate([b_lr_f[0], b_lr_b[0]]).reshape(1, -1)
    bc_f, bc_b = _decay(hb, w_lr, jnp.stack(_hi_lo(w2)), b2)

    cos_full, sin_signed = _rope_tables()
    qr, kr = _qkprep(zm, cos_full, sin_signed, q_norm[0], k_norm[0])
    attn = _attention(qr, kr, zm)
    o_f = _gla(zm, bc_f, False)
    o_b = _gla(zm, bc_b, True)

    wr = jnp.zeros((D_MODEL, 128), F32).at[:, :N_EXPERTS].set(w_router[0])
    br = jnp.full((1, 128), -jnp.inf, F32).at[0, :N_EXPERTS].set(b_router[0])
    vec = lambda v: v[0].reshape(1, -1)
    h1, idx_pad, gate_pad = _mixer(
        attn, o_f, o_b, zm, zg, hf, w_br_a[0].astype(BF16), w_br_b[0].astype(BF16), w_o[0].astype(BF16),
        vec(gla_norm), vec(ln1_g), vec(ln1_b), jnp.stack(_hi_lo(wr)), br)

    n_sb, sb_e, sb_nblk, sb_nvalid, tab = _routing(idx_pad[:, :TOP_K])
    y_slots = _experts(n_sb, sb_e, sb_nblk, sb_nvalid, tab, h1, w_up[0], b_up[0], w_down[0], b_down[0])

    out = _tail(y_slots, gate_pad, h1, p[0].reshape(TOK, PLE_DIM), w_ple_gate[0].astype(BF16),
                w_ple_proj[0].astype(BF16), vec(ln2_g), vec(ln2_b), vec(ln3_g), vec(ln3_b))
    return out.reshape(BATCH, SEQ, D_MODEL)
```

```python
import functools

import jax
import jax.numpy as jnp
from jax import lax
from jax.experimental import pallas as pl
from jax.experimental.pallas import tpu as pltpu

F32 = jnp.float32
BF16 = jnp.bfloat16

D_MODEL = 2048
BATCH = 2
SEQ = 4096
TOK = BATCH * SEQ
PLE_DIM = 256
GRID_W = 64
ATT_HEADS = 8
ATT_KV_HEADS = 2
ATT_GROUP = ATT_HEADS // ATT_KV_HEADS
HEAD_DIM = 128
ROPE_BASE = 10000.0
GLA_HEADS = 4
GLA_DK = 128
GLA_DV = 256
GLA_RANK = 16
GLA_TAU = 16.0
GLA_CHUNK = 64
GLA_SUB = 16
N_EXPERTS = 32
TOP_K = 4
D_FF = D_MODEL
SWIGLU_LIMIT = 7.0
SWIGLU_ALPHA = 1.702
MOE_BLOCK = 128
ATT_Q = ATT_HEADS * HEAD_DIM
ATT_KV = ATT_KV_HEADS * HEAD_DIM
GLA_K = GLA_HEADS * GLA_DK
GLA_V = GLA_HEADS * GLA_DV
MAIN_W = ATT_Q + 2 * ATT_KV + 2 * GLA_K + 2 * GLA_V
LR_W = 2 * GLA_RANK
DEPTH = 1
DN_ALPHA = (2 * DEPTH) ** 0.25
LN_EPS = 1e-5
RMS_EPS = 1e-6
LOG2_E = 1.4426950408889634

Z_TILE_OF_W_TILE = (0, 1, 6, 7, 8, 2, 3, 4, 5)
Z_Q, Z_VB, Z_OG, Z_KA, Z_VA, Z_QB, Z_KB = 0, 1024, 2048, 3072, 3328, 3584, 4096

VMEM_LIMIT = 56 * 1024 * 1024

N_ROWS = TOK * TOP_K + N_EXPERTS * MOE_BLOCK
N_BLOCKS = N_ROWS // MOE_BLOCK
SB_BLOCKS = 10
SB_ROWS = SB_BLOCKS * MOE_BLOCK
N_SB = (N_BLOCKS + (SB_BLOCKS - 1) * N_EXPERTS) // SB_BLOCKS
SUBLANES = 8
IDS_STRIDE = -(-2 * SB_ROWS // 1024) * 1024
FF_TILE = 256
N_FF_TILES = D_FF // FF_TILE


def _params(n_axes, vmem=None):
    return pltpu.CompilerParams(dimension_semantics=("arbitrary",) * n_axes,
                                vmem_limit_bytes=vmem or VMEM_LIMIT)


def _sigmoid(x):
    return 1.0 / (1.0 + jnp.exp(-x))


def _layer_norm(y, g, b):
    mu = jnp.mean(y, axis=-1, keepdims=True)
    yc = y - mu
    var = jnp.mean(yc * yc, axis=-1, keepdims=True)
    return yc * lax.rsqrt(var + LN_EPS) * g + b


def _dot(a, b):
    return jnp.dot(a, b, preferred_element_type=F32)


def _hi_lo(x):
    hi = x.astype(BF16)
    return hi, (x - hi.astype(F32)).astype(BF16)


def _dot_split(x, w_ref):
    hi, lo = _hi_lo(x)
    return _dot(hi, w_ref[0]) + _dot(hi, w_ref[1]) + _dot(lo, w_ref[0])


def _dot_nt(a, b):
    return lax.dot_general(a, b, (((1,), (1,)), ((), ())), preferred_element_type=F32)


def _dot_tn(a, b):
    return lax.dot_general(a, b, (((0,), (0,)), ((), ())), preferred_element_type=F32)


def _ln0_body(x_ref, g_ref, b_ref, hb_ref):
    hb_ref[...] = _layer_norm(x_ref[...], g_ref[...], b_ref[...]).astype(BF16)


def _ln0(x2, g, b):
    tm = 256
    row = pl.BlockSpec((tm, D_MODEL), lambda i: (i, 0))
    vec = pl.BlockSpec((1, D_MODEL), lambda i: (0, 0))
    return pl.pallas_call(
        _ln0_body, grid=(TOK // tm,), in_specs=[row, vec, vec], out_specs=row,
        out_shape=jax.ShapeDtypeStruct((TOK, D_MODEL), BF16),
        compiler_params=_params(1), name="ln0")(x2, g, b)


def _inproj_body(perm_ref, a_ref, w_ref, o_ref, wb_ref, *, gate, w_transposed):
    del perm_ref

    @pl.when(pl.program_id(1) == 0)
    def _():
        w = w_ref[...].T if w_transposed else w_ref[...]
        wb_ref[...] = w.astype(BF16)

    acc = _dot(a_ref[...], wb_ref[...])
    if gate:
        acc = _sigmoid(acc)
    o_ref[...] = acc.astype(o_ref.dtype)


def _inproj(hb, w, n_tiles, tile_perm, gate, w_transposed, name):
    tm, tn = 1024, 512
    w_spec = (pl.BlockSpec((tn, D_MODEL), lambda n, m, p: (n, 0)) if w_transposed
              else pl.BlockSpec((D_MODEL, tn), lambda n, m, p: (0, n)))
    grid_spec = pltpu.PrefetchScalarGridSpec(
        num_scalar_prefetch=1, grid=(n_tiles, TOK // tm),
        in_specs=[pl.BlockSpec((tm, D_MODEL), lambda n, m, p: (m, 0)), w_spec],
        out_specs=pl.BlockSpec((tm, tn), lambda n, m, p: (m, p[n])),
        scratch_shapes=[pltpu.VMEM((D_MODEL, tn), BF16)])
    return pl.pallas_call(
        functools.partial(_inproj_body, gate=gate, w_transposed=w_transposed), grid_spec=grid_spec,
        out_shape=jax.ShapeDtypeStruct((TOK, n_tiles * tn), BF16),
        compiler_params=_params(2), name=name)(jnp.asarray(tile_perm, jnp.int32), hb, w)


def _wprep_body(a_ref, b_ref, g_ref, lr_ref):
    g_ref[...] = jnp.concatenate([a_ref[LR_W:, :], b_ref[...]], axis=0).T.astype(BF16)

    @pl.when(pl.program_id(0) == 0)
    def _():
        lane = lax.broadcasted_iota(jnp.int32, (D_MODEL, 128), 1)
        lr_ref[...] = jnp.where(lane < LR_W, a_ref[:128, :].T, 0.0).astype(BF16)


def _gate_weights(wt):
    tn = 512
    first = MAIN_W // tn
    return pl.pallas_call(
        _wprep_body, grid=(2 * D_MODEL // tn,),
        in_specs=[pl.BlockSpec((tn, D_MODEL), lambda i: (first + i, 0)),
                  pl.BlockSpec((LR_W, D_MODEL), lambda i: ((first + i + 1) * (tn // LR_W), 0))],
        out_specs=[pl.BlockSpec((D_MODEL, tn), lambda i: (0, i)), pl.BlockSpec((D_MODEL, 128), lambda i: (0, 0))],
        out_shape=[jax.ShapeDtypeStruct((D_MODEL, 2 * D_MODEL), BF16), jax.ShapeDtypeStruct((D_MODEL, 128), BF16)],
        compiler_params=_params(1), name="gate_w")(wt, wt)


def _decay_body(h_ref, wlr_ref, w2_ref, b2_ref, trif_ref, trib_ref, bf_ref, bb_ref):
    zlr = _dot(h_ref[...], wlr_ref[...])
    pre = _dot_split(zlr, w2_ref) + b2_ref[...]
    la = (jnp.minimum(pre, 0.0) - jnp.log1p(jnp.exp(-jnp.abs(pre)))) * (1.0 / GLA_TAU)
    hi = la.astype(BF16)
    lo = (la - hi.astype(F32)).astype(BF16)
    bf_ref[...] = _dot(trif_ref[...], hi[:, :GLA_K]) + _dot(trif_ref[...], lo[:, :GLA_K])
    bb_ref[...] = _dot(trib_ref[...], hi[:, GLA_K:]) + _dot(trib_ref[...], lo[:, GLA_K:])


def _decay(hb, wlr, w2, b2):
    tm = 512
    r = jnp.arange(tm)
    same = (r[:, None] // GLA_CHUNK) == (r[None, :] // GLA_CHUNK)
    trif = (same & (r[None, :] <= r[:, None])).astype(BF16)
    trib = (same & (r[None, :] >= r[:, None])).astype(BF16)
    full = lambda shape: pl.BlockSpec(shape, lambda i: (0,) * len(shape))
    out = pl.BlockSpec((tm, GLA_K), lambda i: (i, 0))
    return pl.pallas_call(
        _decay_body, grid=(TOK // tm,),
        in_specs=[pl.BlockSpec((tm, D_MODEL), lambda i: (i, 0)), full((D_MODEL, 128)),
                  full((2, 128, 2 * GLA_K)), full((1, 2 * GLA_K)), full((tm, tm)), full((tm, tm))],
        out_specs=[out, out],
        out_shape=[jax.ShapeDtypeStruct((TOK, GLA_K), F32)] * 2,
        compiler_params=_params(1), name="decay")(hb, wlr, w2, b2, trif, trib)


def _qkprep_body(q_ref, k_ref, cos_ref, sin_ref, qn_ref, kn_ref, qo_ref, ko_ref):
    cos = cos_ref[...]
    sin = sin_ref[...]
    lane = lax.broadcasted_iota(jnp.int32, cos.shape, 1)
    even = (lane % 2) == 0

    def one(x, gain, scale):
        x = x.astype(F32)
        x = x * lax.rsqrt(jnp.mean(x * x, axis=-1, keepdims=True) + RMS_EPS) * gain
        partner = jnp.where(even, pltpu.roll(x, HEAD_DIM - 1, 1), pltpu.roll(x, 1, 1))
        return ((x * cos + partner * sin) * scale).astype(BF16)

    for hh in range(ATT_HEADS):
        sl = slice(hh * HEAD_DIM, (hh + 1) * HEAD_DIM)
        qo_ref[:, sl] = one(q_ref[:, sl], qn_ref[...], LOG2_E * HEAD_DIM ** -0.5)
    for hh in range(ATT_KV_HEADS):
        sl = slice(hh * HEAD_DIM, (hh + 1) * HEAD_DIM)
        ko_ref[:, sl] = one(k_ref[:, sl], kn_ref[...], 1.0)


def _qkprep(zm, cos_full, sin_signed, q_norm, k_norm):
    tm = 256
    nrow = SEQ // tm
    tab = pl.BlockSpec((tm, HEAD_DIM), lambda i: (i % nrow, 0))
    vec = pl.BlockSpec((1, HEAD_DIM), lambda i: (0, 0))
    return pl.pallas_call(
        _qkprep_body, grid=(TOK // tm,),
        in_specs=[pl.BlockSpec((tm, ATT_Q), lambda i: (i, Z_Q // ATT_Q)),
                  pl.BlockSpec((tm, ATT_KV), lambda i: (i, Z_KA // ATT_KV)), tab, tab, vec, vec],
        out_specs=[pl.BlockSpec((tm, ATT_Q), lambda i: (i, 0)), pl.BlockSpec((tm, ATT_KV), lambda i: (i, 0))],
        out_shape=[jax.ShapeDtypeStruct((TOK, ATT_Q), BF16), jax.ShapeDtypeStruct((TOK, ATT_KV), BF16)],
        compiler_params=_params(1), name="qkprep")(zm, zm, cos_full, sin_signed,
                                                   q_norm.reshape(1, -1), k_norm.reshape(1, -1))


ATT_TQ = 512
ATT_TK = 512


def _attn_body(q_ref, k_ref, v_ref, o_ref, qs_ref, s0_ref, s1_ref, p0_ref, p1_ref, a0_ref, a1_ref,
               acc_ref, m_ref, l_ref):
    rows = ATT_GROUP * ATT_TQ
    n_chunks = SEQ // ATT_TK
    groups = ATT_TK // SUBLANES
    for g in range(ATT_GROUP):
        qs_ref[g * ATT_TQ:(g + 1) * ATT_TQ, :] = q_ref[:, g * HEAD_DIM:(g + 1) * HEAD_DIM]
    m_ref[...] = jnp.full(m_ref.shape, -jnp.inf, F32)
    l_ref[...] = jnp.zeros(l_ref.shape, F32)
    acc_ref[...] = jnp.zeros(acc_ref.shape, F32)
    p1_ref[...] = jnp.zeros(p1_ref.shape, BF16)
    a1_ref[...] = jnp.zeros(a1_ref.shape, F32)

    def chunk(c):
        return pl.ds(pl.multiple_of(c * ATT_TK, ATT_TK), ATT_TK)

    lane_blocks = [slice(cb * HEAD_DIM, (cb + 1) * HEAD_DIM) for cb in range(rows // HEAD_DIM)]

    def scores(kc, s_out):
        qk = _dot_nt(k_ref[kc, :], qs_ref[...])
        for i, cb in enumerate(lane_blocks):
            s_out[i] = qk[:, cb]

    def probs_t(p_in):
        return jnp.concatenate([p_in[i] for i in range(len(lane_blocks))], axis=1)

    scores(slice(0, ATT_TK), s0_ref)

    def step(c, s_cur, s_nxt, p_cur, p_prv, a_cur, a_prv):
        c_next = jnp.where(c + 1 < n_chunks, c + 1, 0)
        c_prev = jnp.where(c > 0, c - 1, 0)
        scores(chunk(c_next), s_nxt)
        acc_ref[...] = a_prv[0:1, :] * acc_ref[...] + _dot_tn(v_ref[chunk(c_prev), :], probs_t(p_prv))
        m_new = []
        for i, cb in enumerate(lane_blocks):
            part = jnp.max(s_cur[i].reshape(groups, SUBLANES, HEAD_DIM), axis=0)
            m_new.append(jnp.maximum(m_ref[0:1, cb], jnp.max(part, axis=0, keepdims=True)))
        for i, (cb, mn) in enumerate(zip(lane_blocks, m_new)):
            alpha = jnp.exp2(m_ref[0:1, cb] - mn)
            p = jnp.exp2(s_cur[i] - mn)
            l_ref[:, cb] = alpha * l_ref[:, cb] + jnp.sum(p.reshape(groups, SUBLANES, HEAD_DIM), axis=0)
            a_cur[:, cb] = jnp.broadcast_to(alpha, (SUBLANES, HEAD_DIM))
            p_cur[i] = p.astype(BF16)
        for cb, mn in zip(lane_blocks, m_new):
            m_ref[:, cb] = jnp.broadcast_to(mn, (SUBLANES, HEAD_DIM))

    def pair(i, carry):
        step(2 * i, s0_ref, s1_ref, p0_ref, p1_ref, a0_ref, a1_ref)
        step(2 * i + 1, s1_ref, s0_ref, p1_ref, p0_ref, a1_ref, a0_ref)
        return carry

    lax.fori_loop(0, n_chunks // 2, pair, 0)
    acc = a1_ref[0:1, :] * acc_ref[...] + _dot_tn(v_ref[SEQ - ATT_TK:SEQ, :], probs_t(p1_ref))
    out = (acc / jnp.sum(l_ref[...], axis=0, keepdims=True)).T.astype(BF16)
    for g in range(ATT_GROUP):
        o_ref[:, g * HEAD_DIM:(g + 1) * HEAD_DIM] = out[g * ATT_TQ:(g + 1) * ATT_TQ]


def _attention(qr, kr, zm):
    nq = SEQ // ATT_TQ
    gw = ATT_GROUP * HEAD_DIM
    rows = ATT_GROUP * ATT_TQ
    qspec = pl.BlockSpec((ATT_TQ, gw), lambda b, j, i: (b * nq + i, j))
    return pl.pallas_call(
        _attn_body, grid=(BATCH, ATT_KV_HEADS, nq),
        in_specs=[qspec,
                  pl.BlockSpec((SEQ, HEAD_DIM), lambda b, j, i: (b, j)),
                  pl.BlockSpec((SEQ, HEAD_DIM), lambda b, j, i: (b, Z_VA // HEAD_DIM + j))],
        out_specs=qspec,
        out_shape=jax.ShapeDtypeStruct((TOK, ATT_Q), BF16),
        scratch_shapes=[pltpu.VMEM((rows, HEAD_DIM), BF16),
                        pltpu.VMEM((rows // HEAD_DIM, ATT_TK, HEAD_DIM), F32),
                        pltpu.VMEM((rows // HEAD_DIM, ATT_TK, HEAD_DIM), F32),
                        pltpu.VMEM((rows // HEAD_DIM, ATT_TK, HEAD_DIM), BF16),
                        pltpu.VMEM((rows // HEAD_DIM, ATT_TK, HEAD_DIM), BF16),
                        pltpu.VMEM((SUBLANES, rows), F32), pltpu.VMEM((SUBLANES, rows), F32),
                        pltpu.VMEM((HEAD_DIM, rows), F32), pltpu.VMEM((SUBLANES, rows), F32),
                        pltpu.VMEM((SUBLANES, rows), F32)],
        compiler_params=_params(3), name="attn")(qr, kr, zm)


GLA_CB = 4
GLA_RB = GLA_CB * GLA_CHUNK
N_SUB = GLA_CHUNK // GLA_SUB


def _gla_body(q_ref, k_ref, v_ref, bc_ref, o_ref, st_ref, kf_ref, bs_ref, *, rev):
    @pl.when(pl.program_id(1) == 0)
    def _():
        st_ref[...] = jnp.zeros_like(st_ref)

    C, SUB = GLA_CHUNK, GLA_SUB
    rowc = lax.broadcasted_iota(jnp.int32, (C, GLA_DK), 0)
    rows_s = lax.broadcasted_iota(jnp.int32, (SUB, 128), 0)
    lane_s = lax.broadcasted_iota(jnp.int32, (SUB, 128), 1)

    def chunk(ci, carry):
        c = (GLA_CB - 1 - ci) if rev else ci
        r0 = pl.multiple_of(c * C, C)
        for hh in range(GLA_HEADS):
            ksl = slice(hh * GLA_DK, (hh + 1) * GLA_DK)
            vsl = slice(hh * GLA_DV, (hh + 1) * GLA_DV)
            q = q_ref[pl.ds(r0, C), ksl].astype(F32) * (GLA_DK ** -0.5)
            k = k_ref[pl.ds(r0, C), ksl].astype(F32)
            v = v_ref[pl.ds(r0, C), vsl]
            bc = bc_ref[pl.ds(r0, C), ksl]
            kf_ref[hh] = k
            bs_ref[hh] = bc
            st = st_ref[hh]
            blast = bc[0:1] if rev else bc[C - 1:C]
            o_inter = _dot_nt((q * jnp.exp(bc)).astype(BF16), st.astype(BF16))
            kdec = k * jnp.exp(blast - bc)
            st_ref[hh] = st * jnp.exp(blast) + _dot_tn(v, kdec.astype(BF16))

            a_rows = []
            for si in range(N_SUB):
                lo, hi = si * SUB, (si + 1) * SUB
                q_s, b_s = q[lo:hi], bc[lo:hi]
                has_earlier = (si < N_SUB - 1) if rev else (si > 0)
                if has_earlier:
                    ref_row = bc[hi:hi + 1] if rev else bc[lo - 1:lo]
                    earlier = (rowc >= hi) if rev else (rowc < lo)
                    qt = q_s * jnp.exp(b_s - ref_row)
                    kt = k * jnp.exp(jnp.where(earlier, ref_row - bc, -jnp.inf))
                    a = _dot_nt(qt.astype(BF16), kt.astype(BF16))
                else:
                    a = jnp.zeros((SUB, C), F32)
                diag = jnp.zeros((SUB, 128), F32)
                for jl in range(SUB):
                    j = lo + jl
                    d = jnp.minimum(b_s - bs_ref[hh, j:j + 1, :], 0.0)
                    col = jnp.sum(q_s * kf_ref[hh, j:j + 1, :] * jnp.exp(d), axis=-1, keepdims=True)
                    diag = jnp.where(lane_s == j, col, diag)
                keep = (lane_s >= rows_s + lo) if rev else (lane_s <= rows_s + lo)
                diag = jnp.where(keep & (lane_s >= lo) & (lane_s < hi), diag, 0.0)
                a_rows.append(a + diag[:, :C])
            a_full = jnp.concatenate(a_rows, axis=0)
            o_ref[pl.ds(r0, C), vsl] = o_inter + _dot(a_full.astype(BF16), v)
        return carry

    lax.fori_loop(0, GLA_CB, chunk, 0)


def _gla(zm, bcum, rev):
    ncb = SEQ // GLA_RB
    if rev:
        row = lambda b, c: b * ncb + (ncb - 1 - c)
    else:
        row = lambda b, c: b * ncb + c
    return pl.pallas_call(
        functools.partial(_gla_body, rev=rev), grid=(BATCH, ncb),
        in_specs=[pl.BlockSpec((GLA_RB, GLA_K), lambda b, c: (row(b, c), Z_QB // GLA_K)),
                  pl.BlockSpec((GLA_RB, GLA_K), lambda b, c: (row(b, c), Z_KB // GLA_K)),
                  pl.BlockSpec((GLA_RB, GLA_V), lambda b, c: (row(b, c), Z_VB // GLA_V)),
                  pl.BlockSpec((GLA_RB, GLA_K), lambda b, c: (row(b, c), 0))],
        out_specs=pl.BlockSpec((GLA_RB, GLA_V), lambda b, c: (row(b, c), 0)),
        out_shape=jax.ShapeDtypeStruct((TOK, GLA_V), F32),
        scratch_shapes=[pltpu.VMEM((GLA_HEADS, GLA_DV, GLA_DK), F32),
                        pltpu.VMEM((GLA_HEADS, GLA_CHUNK, GLA_DK), F32),
                        pltpu.VMEM((GLA_HEADS, GLA_CHUNK, GLA_DK), F32)],
        compiler_params=_params(2), name="gla_bwd" if rev else "gla_fwd")(zm, zm, zm, bcum)


MIX_TM = 512


def _merge_body(attn_ref, of_ref, ob_ref, og_ref, ga_ref, gb_ref, gn_ref, wa_ref, wb_ref, m_ref, on_ref):
    gn = gn_ref[...]
    for hh in range(GLA_HEADS):
        sl = slice(hh * GLA_DV, (hh + 1) * GLA_DV)
        x = of_ref[:, sl] + ob_ref[:, sl]
        g = og_ref[:, sl].astype(F32)
        xn = x * lax.rsqrt(jnp.mean(x * x, axis=-1, keepdims=True) + RMS_EPS) * gn
        on_ref[:, sl] = (xn * (g * _sigmoid(g))).astype(BF16)
    ya = _dot(attn_ref[...], wa_ref[...])
    yb = _dot(on_ref[...], wb_ref[...])
    m_ref[...] = (ga_ref[...].astype(F32) * ya + gb_ref[...].astype(F32) * yb).astype(BF16)


def _outproj_body(m_ref, x_ref, ig_ref, ib_ref, wo_ref, lg_ref, lb_ref, wr_ref, br_ref, h1_ref, idx_ref, gate_ref):
    mix = _dot(m_ref[...], wo_ref[...])
    h0 = _layer_norm(x_ref[...], ig_ref[...], ib_ref[...])
    h1 = _layer_norm(DN_ALPHA * h0 + mix, lg_ref[...], lb_ref[...])
    h1_ref[...] = h1

    logits = _dot_split(h1, wr_ref) + br_ref[...]
    lane = lax.broadcasted_iota(jnp.int32, logits.shape, 1)
    x = logits
    vals, idxs = [], []
    for _ in range(TOP_K):
        mx = jnp.max(x, axis=-1, keepdims=True)
        ix = jnp.min(jnp.where(x == mx, lane, 128), axis=-1, keepdims=True)
        vals.append(mx)
        idxs.append(ix)
        x = jnp.where(lane == ix, -jnp.inf, x)
    es = [jnp.exp(vv - vals[0]) for vv in vals]
    den = es[0] + es[1] + es[2] + es[3]
    idx_out = jnp.zeros(logits.shape, jnp.int32)
    gate_out = jnp.zeros(logits.shape, F32)
    for kk in range(TOP_K):
        idx_out = jnp.where(lane == kk, idxs[kk], idx_out)
        gate_out = jnp.where(lane == kk, es[kk] / den, gate_out)
    idx_ref[...] = idx_out
    gate_ref[...] = gate_out


def _mixer(attn, o_f, o_b, zm, zg, x2, in_g, in_b, wa, wb, wo, gn, lg, lb, wr, br):
    tm = MIX_TM
    const = lambda shape: pl.BlockSpec(shape, lambda i: (0,) * len(shape), pipeline_mode=pl.Buffered(1))
    row = lambda w, cb=0: pl.BlockSpec((tm, w), lambda i: (i, cb))
    vec = const((1, D_MODEL))
    grid = (TOK // tm,)
    merged = pl.pallas_call(
        _merge_body, grid=grid,
        in_specs=[row(ATT_Q), row(GLA_V), row(GLA_V), row(GLA_V, Z_OG // GLA_V), row(D_MODEL, 0), row(D_MODEL, 1),
                  const((1, GLA_DV)), const((ATT_Q, D_MODEL)), const((GLA_V, D_MODEL))],
        out_specs=row(D_MODEL), out_shape=jax.ShapeDtypeStruct((TOK, D_MODEL), BF16),
        scratch_shapes=[pltpu.VMEM((tm, GLA_V), BF16)],
        compiler_params=_params(1), name="merge")(attn, o_f, o_b, zm, zg, zg, gn, wa, wb)
    return pl.pallas_call(
        _outproj_body, grid=grid,
        in_specs=[row(D_MODEL), row(D_MODEL), vec, vec, const((D_MODEL, D_MODEL)), vec, vec,
                  const((2, D_MODEL, 128)), const((1, 128))],
        out_specs=[row(D_MODEL), row(128), row(128)],
        out_shape=[jax.ShapeDtypeStruct((TOK, D_MODEL), F32),
                   jax.ShapeDtypeStruct((TOK, 128), jnp.int32),
                   jax.ShapeDtypeStruct((TOK, 128), F32)],
        compiler_params=_params(1), name="outproj")(merged, x2, in_g, in_b, wo, lg, lb, wr, br)


GROUP_SIZES = (512, 256, 128)
GATHER_STEPS = SB_BLOCKS // (SB_ROWS // GROUP_SIZES[0])


def _expert_body(e_ref, nb_ref, nv_ref, tab_hbm, h1_hbm, wg_ref, wl_ref, wd_ref, bg_ref, bl_ref, bd_ref,
                 y_hbm, gbuf, xb_ref, acc_ref, wgb_ref, wlb_ref, wdb_ref, ids, sem_ids, sem_g, sem_s):
    del e_ref
    s = pl.program_id(0)
    j = pl.program_id(1)
    nblk = nb_ref[s]
    slot = s & 1
    has_next = s + 1 < pl.num_programs(0)
    nxt = jnp.minimum(s + 1, N_SB - 1)
    nblk_next = jnp.where(has_next, nb_ref[nxt], 0)
    tiles = MOE_BLOCK // SUBLANES

    def ids_copy(sb, sl):
        return pltpu.make_async_copy(tab_hbm.at[pl.ds(pl.multiple_of(sb * IDS_STRIDE, IDS_STRIDE), IDS_STRIDE)],
                                     ids.at[pl.ds(pl.multiple_of(sl * IDS_STRIDE, IDS_STRIDE), IDS_STRIDE)],
                                     sem_ids.at[sl])

    def gather_block(sl, b):
        id0 = sl * IDS_STRIDE + b * MOE_BLOCK

        def issue(t, c):
            for u in range(SUBLANES):
                tok = ids[id0 + t * SUBLANES + u]
                pltpu.make_async_copy(h1_hbm.at[pl.ds(tok >> 3, 1), pl.ds(tok & 7, 1), :],
                                      gbuf.at[pl.ds(b * tiles + t, 1), pl.ds(u, 1), :], sem_g).start()
            return c

        lax.fori_loop(0, tiles, issue, 0)

    def gather_wait_block(b):
        pltpu.make_async_copy(h1_hbm.at[pl.ds(0, tiles)], gbuf.at[pl.ds(b * tiles, tiles)], sem_g).wait()

    def loop_blocks(n, fn):
        def body(b, c):
            fn(b)
            return c

        lax.fori_loop(0, n, body, 0)

    @pl.when((s == 0) & (j == 0))
    def _():
        ids_copy(0, 0).start()
        ids_copy(0, 0).wait()
        loop_blocks(nblk, lambda b: gather_block(0, b))

    big = GROUP_SIZES[0]
    big_blocks = big // MOE_BLOCK
    n_big = nblk // big_blocks
    in_window = (j >= 1) & (j <= GATHER_STEPS) & has_next
    issued_here = GATHER_STEPS * n_big
    issued_prev = jnp.where(s > 0, GATHER_STEPS * (nb_ref[jnp.maximum(s - 1, 0)] // big_blocks), 0)

    @pl.when(j == 0)
    def _():
        loop_blocks(jnp.maximum(issued_prev, nblk), gather_wait_block)

    @pl.when(nblk > 0)
    def _():
        @pl.when(j == 0)
        def _():
            @pl.when(has_next)
            def _():
                ids_copy(nxt, 1 - slot).start()

            def take_block(b):
                x = gbuf[pl.ds(b * tiles, tiles)].reshape(MOE_BLOCK, D_MODEL)
                xb_ref[pl.ds(pl.multiple_of(b * MOE_BLOCK, MOE_BLOCK), MOE_BLOCK), :] = x.astype(BF16)
                acc_ref[pl.ds(b * tiles, tiles)] = jnp.broadcast_to(bd_ref[0], (tiles, SUBLANES, D_MODEL))

            loop_blocks(nblk, take_block)

        @pl.when((j == 1) & has_next)
        def _():
            ids_copy(nxt, 1 - slot).wait()

        @pl.when(j == GATHER_STEPS + 1)
        def _():
            for b in range(SB_BLOCKS):
                @pl.when((b >= issued_here) & (b < nblk_next))
                def _(b=b):
                    gather_block(1 - slot, b)

        wgb_ref[...] = wg_ref[0].astype(BF16)
        wlb_ref[...] = wl_ref[0].astype(BF16)
        wdb_ref[...] = wd_ref[0].astype(BF16)

        def group(r0, size):
            x = xb_ref[pl.ds(pl.multiple_of(r0, MOE_BLOCK), size), :]
            g = jnp.minimum(_dot(x, wgb_ref[...]) + bg_ref[0], SWIGLU_LIMIT)
            lin = jnp.clip(_dot(x, wlb_ref[...]) + bl_ref[0], -SWIGLU_LIMIT, SWIGLU_LIMIT)
            act = g * _sigmoid(SWIGLU_ALPHA * g) * (lin + 1.0)
            upd = _dot(act.astype(BF16), wdb_ref[...])
            acc_ref[pl.ds(r0 // SUBLANES, size // SUBLANES)] += upd.reshape(size // SUBLANES, SUBLANES, D_MODEL)

        def big_group(i, c):
            group(i * big, big)
            return c

        def big_group_and_gather(i, c):
            group(i * big, big)
            b = (j - 1) * n_big + i
            id0 = (1 - slot) * IDS_STRIDE + b * MOE_BLOCK
            for t in range(tiles):
                for u in range(SUBLANES):
                    tok = ids[id0 + t * SUBLANES + u]
                    pltpu.make_async_copy(h1_hbm.at[pl.ds(tok >> 3, 1), pl.ds(tok & 7, 1), :],
                                          gbuf.at[pl.ds(b * tiles + t, 1), pl.ds(u, 1), :], sem_g).start()
            return c

        @pl.when(in_window)
        def _():
            lax.fori_loop(0, n_big, big_group_and_gather, 0)

        @pl.when(jnp.logical_not(in_window))
        def _():
            lax.fori_loop(0, n_big, big_group, 0)

        done = n_big * big_blocks
        for size in GROUP_SIZES[1:]:
            take = ((nblk - done) // (size // MOE_BLOCK)) > 0

            @pl.when(take)
            def _(done=done, size=size):
                group(done * MOE_BLOCK, size)

            done = done + jnp.where(take, size // MOE_BLOCK, 0)

        @pl.when(j == N_FF_TILES - 1)
        def _():
            nvalid = nv_ref[s]

            dst0 = slot * IDS_STRIDE + SB_ROWS

            def row_copy(t, u, dst):
                return pltpu.make_async_copy(acc_ref.at[pl.ds(t, 1), pl.ds(u, 1), :],
                                             y_hbm.at[pl.ds(dst >> 3, 1), pl.ds(dst & 7, 1), :], sem_s)

            def issue_tile(t, c):
                for u in range(SUBLANES):
                    row_copy(t, u, ids[dst0 + t * SUBLANES + u]).start()
                return c

            def issue_row(r, c):
                row_copy(r >> 3, r & 7, ids[dst0 + r]).start()
                return c

            full_tiles = nvalid // SUBLANES
            lax.fori_loop(0, full_tiles, issue_tile, 0)
            lax.fori_loop(full_tiles * SUBLANES, nvalid, issue_row, 0)

            def wait_block(b, c):
                pltpu.make_async_copy(acc_ref.at[pl.ds(0, tiles)], y_hbm.at[pl.ds(0, tiles)], sem_s).wait()
                return c

            def wait_row(r, c):
                row_copy(0, 0, 0).wait()
                return c

            nfull = nvalid // MOE_BLOCK
            lax.fori_loop(0, nfull, wait_block, 0)
            lax.fori_loop(nfull * MOE_BLOCK, nvalid, wait_row, 0)


def _experts(n_sb, sb_e, sb_nblk, sb_nvalid, tab, h1, w_up, b_up, w_down, b_down):
    def jj(j, nb, s):
        return jnp.where(nb[s] > 0, j, N_FF_TILES - 1)

    grid_spec = pltpu.PrefetchScalarGridSpec(
        num_scalar_prefetch=3, grid=(n_sb, N_FF_TILES),
        in_specs=[pl.BlockSpec(memory_space=pl.ANY),
                  pl.BlockSpec(memory_space=pl.ANY),
                  pl.BlockSpec((1, D_MODEL, FF_TILE), lambda s, j, e, nb, nv: (e[s], 0, jj(j, nb, s))),
                  pl.BlockSpec((1, D_MODEL, FF_TILE), lambda s, j, e, nb, nv: (e[s], 0, N_FF_TILES + jj(j, nb, s))),
                  pl.BlockSpec((1, FF_TILE, D_MODEL), lambda s, j, e, nb, nv: (e[s], jj(j, nb, s), 0)),
                  pl.BlockSpec((1, 1, FF_TILE), lambda s, j, e, nb, nv: (e[s], 0, jj(j, nb, s))),
                  pl.BlockSpec((1, 1, FF_TILE), lambda s, j, e, nb, nv: (e[s], 0, N_FF_TILES + jj(j, nb, s))),
                  pl.BlockSpec((1, 1, D_MODEL), lambda s, j, e, nb, nv: (e[s], 0, 0))],
        out_specs=pl.BlockSpec(memory_space=pl.ANY),
        scratch_shapes=[pltpu.VMEM((SB_ROWS // SUBLANES, SUBLANES, D_MODEL), F32),
                        pltpu.VMEM((SB_ROWS, D_MODEL), BF16),
                        pltpu.VMEM((SB_ROWS // SUBLANES, SUBLANES, D_MODEL), F32),
                        pltpu.VMEM((D_MODEL, FF_TILE), BF16),
                        pltpu.VMEM((D_MODEL, FF_TILE), BF16),
                        pltpu.VMEM((FF_TILE, D_MODEL), BF16),
                        pltpu.SMEM((2 * IDS_STRIDE,), jnp.int32),
                        pltpu.SemaphoreType.DMA((2,)),
                        pltpu.SemaphoreType.DMA(()),
                        pltpu.SemaphoreType.DMA(())])
    y = pl.pallas_call(
        _expert_body, grid_spec=grid_spec,
        out_shape=jax.ShapeDtypeStruct((TOP_K * TOK // SUBLANES, SUBLANES, D_MODEL), F32),
        compiler_params=_params(2), name="experts")(
            sb_e, sb_nblk, sb_nvalid, tab, h1.reshape(TOK // SUBLANES, SUBLANES, D_MODEL), w_up, w_up, w_down,
            b_up.reshape(N_EXPERTS, 1, 2 * D_FF), b_up.reshape(N_EXPERTS, 1, 2 * D_FF),
            b_down.reshape(N_EXPERTS, 1, D_MODEL))
    return y.reshape(TOP_K * TOK, D_MODEL)


def _tail_body(y0_ref, y1_ref, y2_ref, y3_ref, gate_ref, h1_ref, p_ref, wg_ref, wp_ref,
               l2g_ref, l2b_ref, l3g_ref, l3b_ref, o_ref):
    gate = gate_ref[...]
    y = gate[:, 0:1] * y0_ref[...]
    for kk, y_ref in enumerate((y1_ref, y2_ref, y3_ref), start=1):
        y = y + gate[:, kk:kk + 1] * y_ref[...]
    h2 = _layer_norm(DN_ALPHA * h1_ref[...] + y, l2g_ref[...], l2b_ref[...])
    ple = _sigmoid(_dot(h2.astype(BF16), wg_ref[...])) * _dot(p_ref[...].astype(BF16), wp_ref[...])
    o_ref[...] = _layer_norm(DN_ALPHA * h2 + ple, l3g_ref[...], l3b_ref[...])


def _tail(y_slots, gates, h1, p2, wg, wp, l2g, l2b, l3g, l3b):
    tm = 256
    nt = TOK // tm
    const = lambda shape: pl.BlockSpec(shape, lambda i: (0,) * len(shape), pipeline_mode=pl.Buffered(1))
    row = lambda w: pl.BlockSpec((tm, w), lambda i: (i, 0))
    yspec = lambda kk: pl.BlockSpec((tm, D_MODEL), lambda i: (kk * nt + i, 0))
    vec = const((1, D_MODEL))
    return pl.pallas_call(
        _tail_body, grid=(nt,),
        in_specs=[yspec(0), yspec(1), yspec(2), yspec(3), row(128), row(D_MODEL), row(PLE_DIM),
                  const((D_MODEL, D_MODEL)), const((PLE_DIM, D_MODEL)), vec, vec, vec, vec],
        out_specs=row(D_MODEL),
        out_shape=jax.ShapeDtypeStruct((TOK, D_MODEL), F32),
        compiler_params=_params(1), name="tail")(y_slots, y_slots, y_slots, y_slots, gates, h1, p2, wg, wp,
                                                 l2g, l2b, l3g, l3b)


def _routing(top_idx):
    flat_e = top_idx.reshape(-1)
    experts = jnp.arange(N_EXPERTS, dtype=jnp.int32)
    onehot = (flat_e[:, None] == experts[None, :]).astype(jnp.int32)
    csum = jnp.cumsum(onehot, axis=0)
    rank = jnp.sum(onehot * csum, axis=1) - 1
    counts = csum[-1]
    padded = (counts + MOE_BLOCK - 1) // MOE_BLOCK * MOE_BLOCK
    padded_end = jnp.cumsum(padded)
    padded_start = padded_end - padded
    dest = (padded_start[flat_e] + rank).astype(jnp.int32)
    asg = jnp.arange(TOK * TOP_K, dtype=jnp.int32)
    asg_of_row = jnp.zeros((N_ROWS + SB_ROWS,), jnp.int32).at[dest].set(asg)
    tok = (asg_of_row // TOP_K).reshape(-1, MOE_BLOCK)
    dst = ((asg_of_row % TOP_K) * TOK + asg_of_row // TOP_K).reshape(-1, MOE_BLOCK)
    nb = padded // MOE_BLOCK
    n_sb = (nb + SB_BLOCKS - 1) // SB_BLOCKS
    sb_end = jnp.cumsum(n_sb)
    total = sb_end[-1]
    s = jnp.arange(N_SB, dtype=jnp.int32)
    s_eff = jnp.minimum(s, total - 1)
    e = jnp.minimum(jnp.sum((sb_end[None, :] <= s_eff[:, None]).astype(jnp.int32), axis=1), N_EXPERTS - 1)
    local = s_eff - (sb_end[e] - n_sb[e])
    valid = s < total
    sb_nblk = jnp.where(valid, jnp.clip(nb[e] - local * SB_BLOCKS, 0, SB_BLOCKS), 0).astype(jnp.int32)
    sb_blk = padded_start[e] // MOE_BLOCK + local * SB_BLOCKS
    sb_nvalid = jnp.where(valid, jnp.clip(counts[e] - local * SB_ROWS, 0, SB_ROWS), 0).astype(jnp.int32)
    blocks = sb_blk[:, None] + jnp.arange(SB_BLOCKS, dtype=jnp.int32)[None, :]
    tab = jnp.concatenate([tok[blocks].reshape(N_SB, SB_ROWS), dst[blocks].reshape(N_SB, SB_ROWS),
                           jnp.zeros((N_SB, IDS_STRIDE - 2 * SB_ROWS), jnp.int32)], axis=1).astype(jnp.int32).reshape(-1)
    return total.astype(jnp.int32), e.astype(jnp.int32), sb_nblk, sb_nvalid, tab


def _rope_tables():
    rows = SEQ // GRID_W
    row = jnp.repeat(jnp.arange(rows), GRID_W)
    col = jnp.tile(jnp.arange(GRID_W), rows)
    n_pairs = HEAD_DIM // 4
    inv_freq = ROPE_BASE ** (-jnp.arange(n_pairs, dtype=F32) / n_pairs)
    ang = jnp.concatenate([row[:, None] * inv_freq, col[:, None] * inv_freq], -1)
    cos_full = jnp.repeat(jnp.cos(ang), 2, axis=-1)
    sin = jnp.sin(ang)
    sin_signed = jnp.stack([-sin, sin], axis=-1).reshape(SEQ, HEAD_DIM)
    return cos_full, sin_signed


def kernel(x, p, in_ln_g, in_ln_b, w_in, q_norm, k_norm, w_lr_f, b_lr_f, w_lr_b, b_lr_b, gla_norm, w_br_a, w_br_b, w_o, ln1_g, ln1_b, w_router, b_router, w_up, b_up, w_down, b_down, ln2_g, ln2_b, w_ple_gate, w_ple_proj, ln3_g, ln3_b):
    assert x.shape == (BATCH, SEQ, D_MODEL) and w_in.shape[0] == DEPTH == 1
    wt = w_in[0].T
    x2 = x.reshape(TOK, D_MODEL)
    in_g, in_b = in_ln_g.reshape(1, -1), in_ln_b.reshape(1, -1)
    hb = _ln0(x2, in_g, in_b)

    zm = _inproj(hb, wt, MAIN_W // 512, Z_TILE_OF_W_TILE, False, True, "inproj_main")
    w_gate, w_lr = _gate_weights(wt)
    zg = _inproj(hb, w_gate, 2 * D_MODEL // 512, tuple(range(8)), True, False, "inproj_gates")

    w2 = jnp.zeros((128, 2 * GLA_K), F32)
    w2 = w2.at[:GLA_RANK, :GLA_K].set(w_lr_f[0]).at[GLA_RANK:LR_W, GLA_K:].set(w_lr_b[0])
    b2 = jnp.concatenate([b_lr_f[0], b_lr_b[0]]).reshape(1, -1)
    bc_f, bc_b = _decay(hb, w_lr, jnp.stack(_hi_lo(w2)), b2)

    cos_full, sin_signed = _rope_tables()
    qr, kr = _qkprep(zm, cos_full, sin_signed, q_norm[0], k_norm[0])
    attn = _attention(qr, kr, zm)
    o_f = _gla(zm, bc_f, False)
    o_b = _gla(zm, bc_b, True)

    wr = jnp.zeros((D_MODEL, 128), F32).at[:, :N_EXPERTS].set(w_router[0])
    br = jnp.full((1, 128), -jnp.inf, F32).at[0, :N_EXPERTS].set(b_router[0])
    vec = lambda v: v[0].reshape(1, -1)
    h1, idx_pad, gate_pad = _mixer(
        attn, o_f, o_b, zm, zg, x2, in_g, in_b, w_br_a[0].astype(BF16), w_br_b[0].astype(BF16), w_o[0].astype(BF16),
        vec(gla_norm), vec(ln1_g), vec(ln1_b), jnp.stack(_hi_lo(wr)), br)

    n_sb, sb_e, sb_nblk, sb_nvalid, tab = _routing(idx_pad[:, :TOP_K])
    y_slots = _experts(n_sb, sb_e, sb_nblk, sb_nvalid, tab, h1, w_up[0], b_up[0], w_down[0], b_down[0])

    out = _tail(y_slots, gate_pad, h1, p[0].reshape(TOK, PLE_DIM), w_ple_gate[0].astype(BF16),
                w_ple_proj[0].astype(BF16), vec(ln2_g), vec(ln2_b), vec(ln3_g), vec(ln3_b))
    return out.reshape(BATCH, SEQ, D_MODEL)
```

```python
import functools

import jax
import jax.numpy as jnp
from jax import lax
from jax.experimental import pallas as pl
from jax.experimental.pallas import tpu as pltpu

F32 = jnp.float32
BF16 = jnp.bfloat16

D_MODEL = 2048
BATCH = 2
SEQ = 4096
TOK = BATCH * SEQ
PLE_DIM = 256
GRID_W = 64
ATT_HEADS = 8
ATT_KV_HEADS = 2
ATT_GROUP = ATT_HEADS // ATT_KV_HEADS
HEAD_DIM = 128
ROPE_BASE = 10000.0
GLA_HEADS = 4
GLA_DK = 128
GLA_DV = 256
GLA_RANK = 16
GLA_TAU = 16.0
GLA_CHUNK = 64
GLA_SUB = 16
N_EXPERTS = 32
TOP_K = 4
D_FF = D_MODEL
SWIGLU_LIMIT = 7.0
SWIGLU_ALPHA = 1.702
MOE_BLOCK = 128
ATT_Q = ATT_HEADS * HEAD_DIM
ATT_KV = ATT_KV_HEADS * HEAD_DIM
GLA_K = GLA_HEADS * GLA_DK
GLA_V = GLA_HEADS * GLA_DV
MAIN_W = ATT_Q + 2 * ATT_KV + 2 * GLA_K + 2 * GLA_V
LR_W = 2 * GLA_RANK
DEPTH = 1
DN_ALPHA = (2 * DEPTH) ** 0.25
LN_EPS = 1e-5
RMS_EPS = 1e-6
LOG2_E = 1.4426950408889634

Z_TILE_OF_W_TILE = (0, 1, 6, 7, 8, 2, 3, 4, 5)
Z_Q, Z_VB, Z_OG, Z_KA, Z_VA, Z_QB, Z_KB = 0, 1024, 2048, 3072, 3328, 3584, 4096

VMEM_LIMIT = 56 * 1024 * 1024

N_ROWS = TOK * TOP_K + N_EXPERTS * MOE_BLOCK
N_BLOCKS = N_ROWS // MOE_BLOCK
SB_BLOCKS = 10
SB_ROWS = SB_BLOCKS * MOE_BLOCK
N_SB = (N_BLOCKS + (SB_BLOCKS - 1) * N_EXPERTS) // SB_BLOCKS
SUBLANES = 8
IDS_STRIDE = -(-2 * SB_ROWS // 1024) * 1024
FF_TILE = 256
N_FF_TILES = D_FF // FF_TILE


def _params(n_axes, vmem=None):
    return pltpu.CompilerParams(dimension_semantics=("arbitrary",) * n_axes,
                                vmem_limit_bytes=vmem or VMEM_LIMIT)


def _sigmoid(x):
    return 1.0 / (1.0 + jnp.exp(-x))


def _layer_norm(y, g, b):
    mu = jnp.mean(y, axis=-1, keepdims=True)
    yc = y - mu
    var = jnp.mean(yc * yc, axis=-1, keepdims=True)
    return yc * lax.rsqrt(var + LN_EPS) * g + b


def _dot(a, b):
    return jnp.dot(a, b, preferred_element_type=F32)


def _hi_lo(x):
    hi = x.astype(BF16)
    return hi, (x - hi.astype(F32)).astype(BF16)


def _dot_split(x, w_ref):
    hi, lo = _hi_lo(x)
    return _dot(hi, w_ref[0]) + _dot(hi, w_ref[1]) + _dot(lo, w_ref[0])


def _dot_nt(a, b):
    return lax.dot_general(a, b, (((1,), (1,)), ((), ())), preferred_element_type=F32)


def _dot_tn(a, b):
    return lax.dot_general(a, b, (((0,), (0,)), ((), ())), preferred_element_type=F32)


def _ln0_body(x_ref, g_ref, b_ref, hb_ref):
    hb_ref[...] = _layer_norm(x_ref[...], g_ref[...], b_ref[...]).astype(BF16)


def _ln0(x2, g, b):
    tm = 256
    row = pl.BlockSpec((tm, D_MODEL), lambda i: (i, 0))
    vec = pl.BlockSpec((1, D_MODEL), lambda i: (0, 0))
    return pl.pallas_call(
        _ln0_body, grid=(TOK // tm,), in_specs=[row, vec, vec], out_specs=row,
        out_shape=jax.ShapeDtypeStruct((TOK, D_MODEL), BF16),
        compiler_params=_params(1), name="ln0")(x2, g, b)


def _inproj_body(perm_ref, a_ref, w_ref, o_ref, wb_ref, *, gate, w_transposed):
    del perm_ref

    @pl.when(pl.program_id(1) == 0)
    def _():
        w = w_ref[...].T if w_transposed else w_ref[...]
        wb_ref[...] = w.astype(BF16)

    acc = _dot(a_ref[...], wb_ref[...])
    if gate:
        acc = _sigmoid(acc)
    o_ref[...] = acc.astype(o_ref.dtype)


def _inproj(hb, w, n_tiles, tile_perm, gate, w_transposed, name):
    tm, tn = 1024, 512
    w_spec = (pl.BlockSpec((tn, D_MODEL), lambda n, m, p: (n, 0)) if w_transposed
              else pl.BlockSpec((D_MODEL, tn), lambda n, m, p: (0, n)))
    grid_spec = pltpu.PrefetchScalarGridSpec(
        num_scalar_prefetch=1, grid=(n_tiles, TOK // tm),
        in_specs=[pl.BlockSpec((tm, D_MODEL), lambda n, m, p: (m, 0)), w_spec],
        out_specs=pl.BlockSpec((tm, tn), lambda n, m, p: (m, p[n])),
        scratch_shapes=[pltpu.VMEM((D_MODEL, tn), BF16)])
    return pl.pallas_call(
        functools.partial(_inproj_body, gate=gate, w_transposed=w_transposed), grid_spec=grid_spec,
        out_shape=jax.ShapeDtypeStruct((TOK, n_tiles * tn), BF16),
        compiler_params=_params(2), name=name)(jnp.asarray(tile_perm, jnp.int32), hb, w)


def _wprep_body(a_ref, b_ref, g_ref, lr_ref):
    g_ref[...] = jnp.concatenate([a_ref[LR_W:, :], b_ref[...]], axis=0).T.astype(BF16)

    @pl.when(pl.program_id(0) == 0)
    def _():
        lane = lax.broadcasted_iota(jnp.int32, (D_MODEL, 128), 1)
        lr_ref[...] = jnp.where(lane < LR_W, a_ref[:128, :].T, 0.0).astype(BF16)


def _gate_weights(wt):
    tn = 512
    first = MAIN_W // tn
    return pl.pallas_call(
        _wprep_body, grid=(2 * D_MODEL // tn,),
        in_specs=[pl.BlockSpec((tn, D_MODEL), lambda i: (first + i, 0)),
                  pl.BlockSpec((LR_W, D_MODEL), lambda i: ((first + i + 1) * (tn // LR_W), 0))],
        out_specs=[pl.BlockSpec((D_MODEL, tn), lambda i: (0, i)), pl.BlockSpec((D_MODEL, 128), lambda i: (0, 0))],
        out_shape=[jax.ShapeDtypeStruct((D_MODEL, 2 * D_MODEL), BF16), jax.ShapeDtypeStruct((D_MODEL, 128), BF16)],
        compiler_params=_params(1), name="gate_w")(wt, wt)


def _decay_body(h_ref, wlr_ref, w2_ref, b2_ref, trif_ref, trib_ref, bf_ref, bb_ref):
    zlr = _dot(h_ref[...], wlr_ref[...])
    pre = _dot_split(zlr, w2_ref) + b2_ref[...]
    la = (jnp.minimum(pre, 0.0) - jnp.log1p(jnp.exp(-jnp.abs(pre)))) * (1.0 / GLA_TAU)
    hi = la.astype(BF16)
    lo = (la - hi.astype(F32)).astype(BF16)
    bf_ref[...] = _dot(trif_ref[...], hi[:, :GLA_K]) + _dot(trif_ref[...], lo[:, :GLA_K])
    bb_ref[...] = _dot(trib_ref[...], hi[:, GLA_K:]) + _dot(trib_ref[...], lo[:, GLA_K:])


def _decay(hb, wlr, w2, b2):
    tm = 512
    r = jnp.arange(tm)
    same = (r[:, None] // GLA_CHUNK) == (r[None, :] // GLA_CHUNK)
    trif = (same & (r[None, :] <= r[:, None])).astype(BF16)
    trib = (same & (r[None, :] >= r[:, None])).astype(BF16)
    full = lambda shape: pl.BlockSpec(shape, lambda i: (0,) * len(shape))
    out = pl.BlockSpec((tm, GLA_K), lambda i: (i, 0))
    return pl.pallas_call(
        _decay_body, grid=(TOK // tm,),
        in_specs=[pl.BlockSpec((tm, D_MODEL), lambda i: (i, 0)), full((D_MODEL, 128)),
                  full((2, 128, 2 * GLA_K)), full((1, 2 * GLA_K)), full((tm, tm)), full((tm, tm))],
        out_specs=[out, out],
        out_shape=[jax.ShapeDtypeStruct((TOK, GLA_K), F32)] * 2,
        compiler_params=_params(1), name="decay")(hb, wlr, w2, b2, trif, trib)


def _qkprep_body(q_ref, k_ref, cos_ref, sin_ref, qn_ref, kn_ref, qo_ref, ko_ref):
    cos = cos_ref[...]
    sin = sin_ref[...]
    lane = lax.broadcasted_iota(jnp.int32, cos.shape, 1)
    even = (lane % 2) == 0

    def one(x, gain, scale):
        x = x.astype(F32)
        x = x * lax.rsqrt(jnp.mean(x * x, axis=-1, keepdims=True) + RMS_EPS) * gain
        partner = jnp.where(even, pltpu.roll(x, HEAD_DIM - 1, 1), pltpu.roll(x, 1, 1))
        return ((x * cos + partner * sin) * scale).astype(BF16)

    for hh in range(ATT_HEADS):
        sl = slice(hh * HEAD_DIM, (hh + 1) * HEAD_DIM)
        qo_ref[:, sl] = one(q_ref[:, sl], qn_ref[...], LOG2_E * HEAD_DIM ** -0.5)
    for hh in range(ATT_KV_HEADS):
        sl = slice(hh * HEAD_DIM, (hh + 1) * HEAD_DIM)
        ko_ref[:, sl] = one(k_ref[:, sl], kn_ref[...], 1.0)


def _qkprep(zm, cos_full, sin_signed, q_norm, k_norm):
    tm = 256
    nrow = SEQ // tm
    tab = pl.BlockSpec((tm, HEAD_DIM), lambda i: (i % nrow, 0))
    vec = pl.BlockSpec((1, HEAD_DIM), lambda i: (0, 0))
    return pl.pallas_call(
        _qkprep_body, grid=(TOK // tm,),
        in_specs=[pl.BlockSpec((tm, ATT_Q), lambda i: (i, Z_Q // ATT_Q)),
                  pl.BlockSpec((tm, ATT_KV), lambda i: (i, Z_KA // ATT_KV)), tab, tab, vec, vec],
        out_specs=[pl.BlockSpec((tm, ATT_Q), lambda i: (i, 0)), pl.BlockSpec((tm, ATT_KV), lambda i: (i, 0))],
        out_shape=[jax.ShapeDtypeStruct((TOK, ATT_Q), BF16), jax.ShapeDtypeStruct((TOK, ATT_KV), BF16)],
        compiler_params=_params(1), name="qkprep")(zm, zm, cos_full, sin_signed,
                                                   q_norm.reshape(1, -1), k_norm.reshape(1, -1))


ATT_TQ = 512
ATT_TK = 512


def _attn_body(q_ref, k_ref, v_ref, o_ref, qs_ref, s0_ref, s1_ref, p0_ref, p1_ref, a0_ref, a1_ref,
               acc_ref, m_ref, l_ref):
    rows = ATT_GROUP * ATT_TQ
    n_chunks = SEQ // ATT_TK
    groups = ATT_TK // SUBLANES
    for g in range(ATT_GROUP):
        qs_ref[g * ATT_TQ:(g + 1) * ATT_TQ, :] = q_ref[:, g * HEAD_DIM:(g + 1) * HEAD_DIM]
    m_ref[...] = jnp.full(m_ref.shape, -jnp.inf, F32)
    l_ref[...] = jnp.zeros(l_ref.shape, F32)
    acc_ref[...] = jnp.zeros(acc_ref.shape, F32)
    p1_ref[...] = jnp.zeros(p1_ref.shape, BF16)
    a1_ref[...] = jnp.zeros(a1_ref.shape, F32)

    def chunk(c):
        return pl.ds(pl.multiple_of(c * ATT_TK, ATT_TK), ATT_TK)

    lane_blocks = [slice(cb * HEAD_DIM, (cb + 1) * HEAD_DIM) for cb in range(rows // HEAD_DIM)]

    def scores(kc, s_out):
        qk = _dot_nt(k_ref[kc, :], qs_ref[...])
        for i, cb in enumerate(lane_blocks):
            s_out[i] = qk[:, cb]

    def probs_t(p_in):
        return jnp.concatenate([p_in[i] for i in range(len(lane_blocks))], axis=1)

    scores(slice(0, ATT_TK), s0_ref)

    def step(c, s_cur, s_nxt, p_cur, p_prv, a_cur, a_prv):
        c_next = jnp.where(c + 1 < n_chunks, c + 1, 0)
        c_prev = jnp.where(c > 0, c - 1, 0)
        scores(chunk(c_next), s_nxt)
        acc_ref[...] = a_prv[0:1, :] * acc_ref[...] + _dot_tn(v_ref[chunk(c_prev), :], probs_t(p_prv))
        m_new = []
        for i, cb in enumerate(lane_blocks):
            part = jnp.max(s_cur[i].reshape(groups, SUBLANES, HEAD_DIM), axis=0)
            m_new.append(jnp.maximum(m_ref[0:1, cb], jnp.max(part, axis=0, keepdims=True)))
        for i, (cb, mn) in enumerate(zip(lane_blocks, m_new)):
            alpha = jnp.exp2(m_ref[0:1, cb] - mn)
            p = jnp.exp2(s_cur[i] - mn)
            l_ref[:, cb] = alpha * l_ref[:, cb] + jnp.sum(p.reshape(groups, SUBLANES, HEAD_DIM), axis=0)
            a_cur[:, cb] = jnp.broadcast_to(alpha, (SUBLANES, HEAD_DIM))
            p_cur[i] = p.astype(BF16)
        for cb, mn in zip(lane_blocks, m_new):
            m_ref[:, cb] = jnp.broadcast_to(mn, (SUBLANES, HEAD_DIM))

    def pair(i, carry):
        step(2 * i, s0_ref, s1_ref, p0_ref, p1_ref, a0_ref, a1_ref)
        step(2 * i + 1, s1_ref, s0_ref, p1_ref, p0_ref, a1_ref, a0_ref)
        return carry

    lax.fori_loop(0, n_chunks // 2, pair, 0)
    acc = a1_ref[0:1, :] * acc_ref[...] + _dot_tn(v_ref[SEQ - ATT_TK:SEQ, :], probs_t(p1_ref))
    out = (acc / jnp.sum(l_ref[...], axis=0, keepdims=True)).T.astype(BF16)
    for g in range(ATT_GROUP):
        o_ref[:, g * HEAD_DIM:(g + 1) * HEAD_DIM] = out[g * ATT_TQ:(g + 1) * ATT_TQ]


def _attention(qr, kr, zm):
    nq = SEQ // ATT_TQ
    gw = ATT_GROUP * HEAD_DIM
    rows = ATT_GROUP * ATT_TQ
    qspec = pl.BlockSpec((ATT_TQ, gw), lambda b, j, i: (b * nq + i, j))
    return pl.pallas_call(
        _attn_body, grid=(BATCH, ATT_KV_HEADS, nq),
        in_specs=[qspec,
                  pl.BlockSpec((SEQ, HEAD_DIM), lambda b, j, i: (b, j)),
                  pl.BlockSpec((SEQ, HEAD_DIM), lambda b, j, i: (b, Z_VA // HEAD_DIM + j))],
        out_specs=qspec,
        out_shape=jax.ShapeDtypeStruct((TOK, ATT_Q), BF16),
        scratch_shapes=[pltpu.VMEM((rows, HEAD_DIM), BF16),
                        pltpu.VMEM((rows // HEAD_DIM, ATT_TK, HEAD_DIM), F32),
                        pltpu.VMEM((rows // HEAD_DIM, ATT_TK, HEAD_DIM), F32),
                        pltpu.VMEM((rows // HEAD_DIM, ATT_TK, HEAD_DIM), BF16),
                        pltpu.VMEM((rows // HEAD_DIM, ATT_TK, HEAD_DIM), BF16),
                        pltpu.VMEM((SUBLANES, rows), F32), pltpu.VMEM((SUBLANES, rows), F32),
                        pltpu.VMEM((HEAD_DIM, rows), F32), pltpu.VMEM((SUBLANES, rows), F32),
                        pltpu.VMEM((SUBLANES, rows), F32)],
        compiler_params=_params(3), name="attn")(qr, kr, zm)


GLA_CB = 4
GLA_RB = GLA_CB * GLA_CHUNK
N_SUB = GLA_CHUNK // GLA_SUB


def _gla_body(q_ref, k_ref, v_ref, bc_ref, o_ref, st_ref, kf_ref, bs_ref, *, rev):
    @pl.when(pl.program_id(1) == 0)
    def _():
        st_ref[...] = jnp.zeros_like(st_ref)

    C, SUB = GLA_CHUNK, GLA_SUB
    rowc = lax.broadcasted_iota(jnp.int32, (C, GLA_DK), 0)
    rows_s = lax.broadcasted_iota(jnp.int32, (SUB, 128), 0)
    lane_s = lax.broadcasted_iota(jnp.int32, (SUB, 128), 1)

    def chunk(ci, carry):
        c = (GLA_CB - 1 - ci) if rev else ci
        r0 = pl.multiple_of(c * C, C)
        for hh in range(GLA_HEADS):
            ksl = slice(hh * GLA_DK, (hh + 1) * GLA_DK)
            vsl = slice(hh * GLA_DV, (hh + 1) * GLA_DV)
            q = q_ref[pl.ds(r0, C), ksl].astype(F32) * (GLA_DK ** -0.5)
            k = k_ref[pl.ds(r0, C), ksl].astype(F32)
            v = v_ref[pl.ds(r0, C), vsl]
            bc = bc_ref[pl.ds(r0, C), ksl]
            kf_ref[hh] = k
            bs_ref[hh] = bc
            st = st_ref[hh]
            blast = bc[0:1] if rev else bc[C - 1:C]
            o_inter = _dot_nt((q * jnp.exp(bc)).astype(BF16), st.astype(BF16))
            kdec = k * jnp.exp(blast - bc)
            st_ref[hh] = st * jnp.exp(blast) + _dot_tn(v, kdec.astype(BF16))

            a_rows = []
            for si in range(N_SUB):
                lo, hi = si * SUB, (si + 1) * SUB
                q_s, b_s = q[lo:hi], bc[lo:hi]
                has_earlier = (si < N_SUB - 1) if rev else (si > 0)
                if has_earlier:
                    ref_row = bc[hi:hi + 1] if rev else bc[lo - 1:lo]
                    earlier = (rowc >= hi) if rev else (rowc < lo)
                    qt = q_s * jnp.exp(b_s - ref_row)
                    kt = k * jnp.exp(jnp.where(earlier, ref_row - bc, -jnp.inf))
                    a = _dot_nt(qt.astype(BF16), kt.astype(BF16))
                else:
                    a = jnp.zeros((SUB, C), F32)
                diag = jnp.zeros((SUB, 128), F32)
                for jl in range(SUB):
                    j = lo + jl
                    d = jnp.minimum(b_s - bs_ref[hh, j:j + 1, :], 0.0)
                    col = jnp.sum(q_s * kf_ref[hh, j:j + 1, :] * jnp.exp(d), axis=-1, keepdims=True)
                    diag = jnp.where(lane_s == j, col, diag)
                keep = (lane_s >= rows_s + lo) if rev else (lane_s <= rows_s + lo)
                diag = jnp.where(keep & (lane_s >= lo) & (lane_s < hi), diag, 0.0)
                a_rows.append(a + diag[:, :C])
            a_full = jnp.concatenate(a_rows, axis=0)
            o_ref[pl.ds(r0, C), vsl] = o_inter + _dot(a_full.astype(BF16), v)
        return carry

    lax.fori_loop(0, GLA_CB, chunk, 0)


def _gla(zm, bcum, rev):
    ncb = SEQ // GLA_RB
    if rev:
        row = lambda b, c: b * ncb + (ncb - 1 - c)
    else:
        row = lambda b, c: b * ncb + c
    return pl.pallas_call(
        functools.partial(_gla_body, rev=rev), grid=(BATCH, ncb),
        in_specs=[pl.BlockSpec((GLA_RB, GLA_K), lambda b, c: (row(b, c), Z_QB // GLA_K)),
                  pl.BlockSpec((GLA_RB, GLA_K), lambda b, c: (row(b, c), Z_KB // GLA_K)),
                  pl.BlockSpec((GLA_RB, GLA_V), lambda b, c: (row(b, c), Z_VB // GLA_V)),
                  pl.BlockSpec((GLA_RB, GLA_K), lambda b, c: (row(b, c), 0))],
        out_specs=pl.BlockSpec((GLA_RB, GLA_V), lambda b, c: (row(b, c), 0)),
        out_shape=jax.ShapeDtypeStruct((TOK, GLA_V), F32),
        scratch_shapes=[pltpu.VMEM((GLA_HEADS, GLA_DV, GLA_DK), F32),
                        pltpu.VMEM((GLA_HEADS, GLA_CHUNK, GLA_DK), F32),
                        pltpu.VMEM((GLA_HEADS, GLA_CHUNK, GLA_DK), F32)],
        compiler_params=_params(2), name="gla_bwd" if rev else "gla_fwd")(zm, zm, zm, bcum)


MIX_TM = 512


def _merge_body(attn_ref, of_ref, ob_ref, og_ref, ga_ref, gb_ref, gn_ref, wa_ref, wb_ref, m_ref, on_ref):
    gn = gn_ref[...]
    for hh in range(GLA_HEADS):
        sl = slice(hh * GLA_DV, (hh + 1) * GLA_DV)
        x = of_ref[:, sl] + ob_ref[:, sl]
        g = og_ref[:, sl].astype(F32)
        xn = x * lax.rsqrt(jnp.mean(x * x, axis=-1, keepdims=True) + RMS_EPS) * gn
        on_ref[:, sl] = (xn * (g * _sigmoid(g))).astype(BF16)
    ya = _dot(attn_ref[...], wa_ref[...])
    yb = _dot(on_ref[...], wb_ref[...])
    m_ref[...] = (ga_ref[...].astype(F32) * ya + gb_ref[...].astype(F32) * yb).astype(BF16)


def _outproj_body(m_ref, x_ref, ig_ref, ib_ref, wo_ref, lg_ref, lb_ref, wr_ref, br_ref, h1_ref, idx_ref, gate_ref):
    mix = _dot(m_ref[...], wo_ref[...])
    h0 = _layer_norm(x_ref[...], ig_ref[...], ib_ref[...])
    h1 = _layer_norm(DN_ALPHA * h0 + mix, lg_ref[...], lb_ref[...])
    h1_ref[...] = h1

    logits = _dot_split(h1, wr_ref) + br_ref[...]
    lane = lax.broadcasted_iota(jnp.int32, logits.shape, 1)
    x = logits
    vals, idxs = [], []
    for _ in range(TOP_K):
        mx = jnp.max(x, axis=-1, keepdims=True)
        ix = jnp.min(jnp.where(x == mx, lane, 128), axis=-1, keepdims=True)
        vals.append(mx)
        idxs.append(ix)
        x = jnp.where(lane == ix, -jnp.inf, x)
    es = [jnp.exp(vv - vals[0]) for vv in vals]
    den = es[0] + es[1] + es[2] + es[3]
    idx_out = jnp.zeros(logits.shape, jnp.int32)
    gate_out = jnp.zeros(logits.shape, F32)
    for kk in range(TOP_K):
        idx_out = jnp.where(lane == kk, idxs[kk], idx_out)
        gate_out = jnp.where(lane == kk, es[kk] / den, gate_out)
    idx_ref[...] = idx_out
    gate_ref[...] = gate_out


def _mixer(attn, o_f, o_b, zm, zg, x2, in_g, in_b, wa, wb, wo, gn, lg, lb, wr, br):
    tm = MIX_TM
    const = lambda shape: pl.BlockSpec(shape, lambda i: (0,) * len(shape), pipeline_mode=pl.Buffered(1))
    row = lambda w, cb=0: pl.BlockSpec((tm, w), lambda i: (i, cb))
    vec = const((1, D_MODEL))
    grid = (TOK // tm,)
    merged = pl.pallas_call(
        _merge_body, grid=grid,
        in_specs=[row(ATT_Q), row(GLA_V), row(GLA_V), row(GLA_V, Z_OG // GLA_V), row(D_MODEL, 0), row(D_MODEL, 1),
                  const((1, GLA_DV)), const((ATT_Q, D_MODEL)), const((GLA_V, D_MODEL))],
        out_specs=row(D_MODEL), out_shape=jax.ShapeDtypeStruct((TOK, D_MODEL), BF16),
        scratch_shapes=[pltpu.VMEM((tm, GLA_V), BF16)],
        compiler_params=_params(1), name="merge")(attn, o_f, o_b, zm, zg, zg, gn, wa, wb)
    return pl.pallas_call(
        _outproj_body, grid=grid,
        in_specs=[row(D_MODEL), row(D_MODEL), vec, vec, const((D_MODEL, D_MODEL)), vec, vec,
                  const((2, D_MODEL, 128)), const((1, 128))],
        out_specs=[row(D_MODEL), row(128), row(128)],
        out_shape=[jax.ShapeDtypeStruct((TOK, D_MODEL), F32),
                   jax.ShapeDtypeStruct((TOK, 128), jnp.int32),
                   jax.ShapeDtypeStruct((TOK, 128), F32)],
        compiler_params=_params(1), name="outproj")(merged, x2, in_g, in_b, wo, lg, lb, wr, br)


GROUP_SIZES = (512, 256, 128)
GATHER_STEPS = SB_BLOCKS // (SB_ROWS // GROUP_SIZES[0])


def _expert_body(e_ref, nb_ref, nv_ref, tab_hbm, h1_hbm, wg_ref, wl_ref, wd_ref, bg_ref, bl_ref, bd_ref,
                 y_hbm, gbuf, xb_ref, acc_ref, wgb_ref, wlb_ref, wdb_ref, ids, sem_ids, sem_g, sem_s):
    del e_ref
    s = pl.program_id(0)
    j = pl.program_id(1)
    nblk = nb_ref[s]
    slot = s & 1
    has_next = s + 1 < pl.num_programs(0)
    nxt = jnp.minimum(s + 1, N_SB - 1)
    nblk_next = jnp.where(has_next, nb_ref[nxt], 0)
    tiles = MOE_BLOCK // SUBLANES

    def ids_copy(sb, sl):
        return pltpu.make_async_copy(tab_hbm.at[pl.ds(pl.multiple_of(sb * IDS_STRIDE, IDS_STRIDE), IDS_STRIDE)],
                                     ids.at[pl.ds(pl.multiple_of(sl * IDS_STRIDE, IDS_STRIDE), IDS_STRIDE)],
                                     sem_ids.at[sl])

    def gather_block(sl, b):
        id0 = sl * IDS_STRIDE + b * MOE_BLOCK

        def issue(t, c):
            for u in range(SUBLANES):
                tok = ids[id0 + t * SUBLANES + u]
                pltpu.make_async_copy(h1_hbm.at[pl.ds(tok >> 3, 1), pl.ds(tok & 7, 1), :],
                                      gbuf.at[pl.ds(b * tiles + t, 1), pl.ds(u, 1), :], sem_g).start()
            return c

        lax.fori_loop(0, tiles, issue, 0)

    def gather_wait_block(b):
        pltpu.make_async_copy(h1_hbm.at[pl.ds(0, tiles)], gbuf.at[pl.ds(b * tiles, tiles)], sem_g).wait()

    def loop_blocks(n, fn):
        def body(b, c):
            fn(b)
            return c

        lax.fori_loop(0, n, body, 0)

    @pl.when((s == 0) & (j == 0))
    def _():
        ids_copy(0, 0).start()
        ids_copy(0, 0).wait()
        loop_blocks(nblk, lambda b: gather_block(0, b))

    big = GROUP_SIZES[0]
    big_blocks = big // MOE_BLOCK
    n_big = nblk // big_blocks
    in_window = (j >= 1) & (j <= GATHER_STEPS) & has_next
    issued_here = GATHER_STEPS * n_big
    issued_prev = jnp.where(s > 0, GATHER_STEPS * (nb_ref[jnp.maximum(s - 1, 0)] // big_blocks), 0)

    @pl.when(j == 0)
    def _():
        loop_blocks(jnp.maximum(issued_prev, nblk), gather_wait_block)

    @pl.when(nblk > 0)
    def _():
        @pl.when(j == 0)
        def _():
            @pl.when(has_next)
            def _():
                ids_copy(nxt, 1 - slot).start()

            def take_block(b):
                x = gbuf[pl.ds(b * tiles, tiles)].reshape(MOE_BLOCK, D_MODEL)
                xb_ref[pl.ds(pl.multiple_of(b * MOE_BLOCK, MOE_BLOCK), MOE_BLOCK), :] = x.astype(BF16)
                acc_ref[pl.ds(b * tiles, tiles)] = jnp.broadcast_to(bd_ref[0], (tiles, SUBLANES, D_MODEL))

            loop_blocks(nblk, take_block)

        @pl.when((j == 1) & has_next)
        def _():
            ids_copy(nxt, 1 - slot).wait()

        @pl.when(j == GATHER_STEPS + 1)
        def _():
            for b in range(SB_BLOCKS):
                @pl.when((b >= issued_here) & (b < nblk_next))
                def _(b=b):
                    gather_block(1 - slot, b)

        wgb_ref[...] = wg_ref[0].astype(BF16)
        wlb_ref[...] = wl_ref[0].astype(BF16)
        wdb_ref[...] = wd_ref[0].astype(BF16)

        def group(r0, size):
            x = xb_ref[pl.ds(pl.multiple_of(r0, MOE_BLOCK), size), :]
            g = jnp.minimum(_dot(x, wgb_ref[...]) + bg_ref[0], SWIGLU_LIMIT)
            lin = jnp.clip(_dot(x, wlb_ref[...]) + bl_ref[0], -SWIGLU_LIMIT, SWIGLU_LIMIT)
            act = g * _sigmoid(SWIGLU_ALPHA * g) * (lin + 1.0)
            upd = _dot(act.astype(BF16), wdb_ref[...])
            acc_ref[pl.ds(r0 // SUBLANES, size // SUBLANES)] += upd.reshape(size // SUBLANES, SUBLANES, D_MODEL)

        def big_group(i, c):
            group(i * big, big)
            return c

        def big_group_and_gather(i, c):
            group(i * big, big)
            b = (j - 1) * n_big + i
            id0 = (1 - slot) * IDS_STRIDE + b * MOE_BLOCK
            for t in range(tiles):
                for u in range(SUBLANES):
                    tok = ids[id0 + t * SUBLANES + u]
                    pltpu.make_async_copy(h1_hbm.at[pl.ds(tok >> 3, 1), pl.ds(tok & 7, 1), :],
                                          gbuf.at[pl.ds(b * tiles + t, 1), pl.ds(u, 1), :], sem_g).start()
            return c

        @pl.when(in_window)
        def _():
            lax.fori_loop(0, n_big, big_group_and_gather, 0)

        @pl.when(jnp.logical_not(in_window))
        def _():
            lax.fori_loop(0, n_big, big_group, 0)

        done = n_big * big_blocks
        for size in GROUP_SIZES[1:]:
            take = ((nblk - done) // (size // MOE_BLOCK)) > 0

            @pl.when(take)
            def _(done=done, size=size):
                group(done * MOE_BLOCK, size)

            done = done + jnp.where(take, size // MOE_BLOCK, 0)

        @pl.when(j == N_FF_TILES - 1)
        def _():
            nvalid = nv_ref[s]

            dst0 = slot * IDS_STRIDE + SB_ROWS

            def row_copy(t, u, dst):
                return pltpu.make_async_copy(acc_ref.at[pl.ds(t, 1), pl.ds(u, 1), :],
                                             y_hbm.at[pl.ds(dst >> 3, 1), pl.ds(dst & 7, 1), :], sem_s)

            def issue_tile(t, c):
                for u in range(SUBLANES):
                    row_copy(t, u, ids[dst0 + t * SUBLANES + u]).start()
                return c

            def issue_row(r, c):
                row_copy(r >> 3, r & 7, ids[dst0 + r]).start()
                return c

            full_tiles = nvalid // SUBLANES
            lax.fori_loop(0, full_tiles, issue_tile, 0)
            lax.fori_loop(full_tiles * SUBLANES, nvalid, issue_row, 0)

            def wait_block(b, c):
                pltpu.make_async_copy(acc_ref.at[pl.ds(0, tiles)], y_hbm.at[pl.ds(0, tiles)], sem_s).wait()
                return c

            def wait_row(r, c):
                row_copy(0, 0, 0).wait()
                return c

            nfull = nvalid // MOE_BLOCK
            lax.fori_loop(0, nfull, wait_block, 0)
            lax.fori_loop(nfull * MOE_BLOCK, nvalid, wait_row, 0)


def _experts(n_sb, sb_e, sb_nblk, sb_nvalid, tab, h1, w_up, b_up, w_down, b_down):
    def jj(j, nb, s):
        return jnp.where(nb[s] > 0, j, N_FF_TILES - 1)

    grid_spec = pltpu.PrefetchScalarGridSpec(
        num_scalar_prefetch=3, grid=(N_SB, N_FF_TILES),
        in_specs=[pl.BlockSpec(memory_space=pl.ANY),
                  pl.BlockSpec(memory_space=pl.ANY),
                  pl.BlockSpec((1, D_MODEL, FF_TILE), lambda s, j, e, nb, nv: (e[s], 0, jj(j, nb, s))),
                  pl.BlockSpec((1, D_MODEL, FF_TILE), lambda s, j, e, nb, nv: (e[s], 0, N_FF_TILES + jj(j, nb, s))),
                  pl.BlockSpec((1, FF_TILE, D_MODEL), lambda s, j, e, nb, nv: (e[s], jj(j, nb, s), 0)),
                  pl.BlockSpec((1, 1, FF_TILE), lambda s, j, e, nb, nv: (e[s], 0, jj(j, nb, s))),
                  pl.BlockSpec((1, 1, FF_TILE), lambda s, j, e, nb, nv: (e[s], 0, N_FF_TILES + jj(j, nb, s))),
                  pl.BlockSpec((1, 1, D_MODEL), lambda s, j, e, nb, nv: (e[s], 0, 0))],
        out_specs=pl.BlockSpec(memory_space=pl.ANY),
        scratch_shapes=[pltpu.VMEM((SB_ROWS // SUBLANES, SUBLANES, D_MODEL), F32),
                        pltpu.VMEM((SB_ROWS, D_MODEL), BF16),
                        pltpu.VMEM((SB_ROWS // SUBLANES, SUBLANES, D_MODEL), F32),
                        pltpu.VMEM((D_MODEL, FF_TILE), BF16),
                        pltpu.VMEM((D_MODEL, FF_TILE), BF16),
                        pltpu.VMEM((FF_TILE, D_MODEL), BF16),
                        pltpu.SMEM((2 * IDS_STRIDE,), jnp.int32),
                        pltpu.SemaphoreType.DMA((2,)),
                        pltpu.SemaphoreType.DMA(()),
                        pltpu.SemaphoreType.DMA(())])
    y = pl.pallas_call(
        _expert_body, grid_spec=grid_spec,
        out_shape=jax.ShapeDtypeStruct((TOP_K * TOK // SUBLANES, SUBLANES, D_MODEL), F32),
        compiler_params=_params(2), name="experts")(
            sb_e, sb_nblk, sb_nvalid, tab, h1.reshape(TOK // SUBLANES, SUBLANES, D_MODEL), w_up, w_up, w_down,
            b_up.reshape(N_EXPERTS, 1, 2 * D_FF), b_up.reshape(N_EXPERTS, 1, 2 * D_FF),
            b_down.reshape(N_EXPERTS, 1, D_MODEL))
    return y.reshape(TOP_K * TOK, D_MODEL)


def _tail_body(y0_ref, y1_ref, y2_ref, y3_ref, gate_ref, h1_ref, p_ref, wg_ref, wp_ref,
               l2g_ref, l2b_ref, l3g_ref, l3b_ref, o_ref):
    gate = gate_ref[...]
    y = gate[:, 0:1] * y0_ref[...]
    for kk, y_ref in enumerate((y1_ref, y2_ref, y3_ref), start=1):
        y = y + gate[:, kk:kk + 1] * y_ref[...]
    h2 = _layer_norm(DN_ALPHA * h1_ref[...] + y, l2g_ref[...], l2b_ref[...])
    ple = _sigmoid(_dot(h2.astype(BF16), wg_ref[...])) * _dot(p_ref[...].astype(BF16), wp_ref[...])
    o_ref[...] = _layer_norm(DN_ALPHA * h2 + ple, l3g_ref[...], l3b_ref[...])


def _tail(y_slots, gates, h1, p2, wg, wp, l2g, l2b, l3g, l3b):
    tm = 256
    nt = TOK // tm
    const = lambda shape: pl.BlockSpec(shape, lambda i: (0,) * len(shape), pipeline_mode=pl.Buffered(1))
    row = lambda w: pl.BlockSpec((tm, w), lambda i: (i, 0))
    yspec = lambda kk: pl.BlockSpec((tm, D_MODEL), lambda i: (kk * nt + i, 0))
    vec = const((1, D_MODEL))
    return pl.pallas_call(
        _tail_body, grid=(nt,),
        in_specs=[yspec(0), yspec(1), yspec(2), yspec(3), row(128), row(D_MODEL), row(PLE_DIM),
                  const((D_MODEL, D_MODEL)), const((PLE_DIM, D_MODEL)), vec, vec, vec, vec],
        out_specs=row(D_MODEL),
        out_shape=jax.ShapeDtypeStruct((TOK, D_MODEL), F32),
        compiler_params=_params(1), name="tail")(y_slots, y_slots, y_slots, y_slots, gates, h1, p2, wg, wp,
                                                 l2g, l2b, l3g, l3b)


def _routing(top_idx):
    flat_e = top_idx.reshape(-1)
    experts = jnp.arange(N_EXPERTS, dtype=jnp.int32)
    onehot = (flat_e[:, None] == experts[None, :]).astype(jnp.int32)
    csum = jnp.cumsum(onehot, axis=0)
    rank = jnp.sum(onehot * csum, axis=1) - 1
    counts = csum[-1]
    padded = (counts + MOE_BLOCK - 1) // MOE_BLOCK * MOE_BLOCK
    padded_end = jnp.cumsum(padded)
    padded_start = padded_end - padded
    dest = (padded_start[flat_e] + rank).astype(jnp.int32)
    asg = jnp.arange(TOK * TOP_K, dtype=jnp.int32)
    asg_of_row = jnp.zeros((N_ROWS + SB_ROWS,), jnp.int32).at[dest].set(asg)
    tok = (asg_of_row // TOP_K).reshape(-1, MOE_BLOCK)
    dst = ((asg_of_row % TOP_K) * TOK + asg_of_row // TOP_K).reshape(-1, MOE_BLOCK)
    nb = padded // MOE_BLOCK
    n_sb = (nb + SB_BLOCKS - 1) // SB_BLOCKS
    sb_end = jnp.cumsum(n_sb)
    total = sb_end[-1]
    s = jnp.arange(N_SB, dtype=jnp.int32)
    s_eff = jnp.minimum(s, total - 1)
    e = jnp.minimum(jnp.sum((sb_end[None, :] <= s_eff[:, None]).astype(jnp.int32), axis=1), N_EXPERTS - 1)
    local = s_eff - (sb_end[e] - n_sb[e])
    valid = s < total
    sb_nblk = jnp.where(valid, jnp.clip(nb[e] - local * SB_BLOCKS, 0, SB_BLOCKS), 0).astype(jnp.int32)
    sb_blk = padded_start[e] // MOE_BLOCK + local * SB_BLOCKS
    sb_nvalid = jnp.where(valid, jnp.clip(counts[e] - local * SB_ROWS, 0, SB_ROWS), 0).astype(jnp.int32)
    blocks = sb_blk[:, None] + jnp.arange(SB_BLOCKS, dtype=jnp.int32)[None, :]
    tab = jnp.concatenate([tok[blocks].reshape(N_SB, SB_ROWS), dst[blocks].reshape(N_SB, SB_ROWS),
                           jnp.zeros((N_SB, IDS_STRIDE - 2 * SB_ROWS), jnp.int32)], axis=1).astype(jnp.int32).reshape(-1)
    return total.astype(jnp.int32), e.astype(jnp.int32), sb_nblk, sb_nvalid, tab


def _rope_tables():
    rows = SEQ // GRID_W
    row = jnp.repeat(jnp.arange(rows), GRID_W)
    col = jnp.tile(jnp.arange(GRID_W), rows)
    n_pairs = HEAD_DIM // 4
    inv_freq = ROPE_BASE ** (-jnp.arange(n_pairs, dtype=F32) / n_pairs)
    ang = jnp.concatenate([row[:, None] * inv_freq, col[:, None] * inv_freq], -1)
    cos_full = jnp.repeat(jnp.cos(ang), 2, axis=-1)
    sin = jnp.sin(ang)
    sin_signed = jnp.stack([-sin, sin], axis=-1).reshape(SEQ, HEAD_DIM)
    return cos_full, sin_signed


def kernel(x, p, in_ln_g, in_ln_b, w_in, q_norm, k_norm, w_lr_f, b_lr_f, w_lr_b, b_lr_b, gla_norm, w_br_a, w_br_b, w_o, ln1_g, ln1_b, w_router, b_router, w_up, b_up, w_down, b_down, ln2_g, ln2_b, w_ple_gate, w_ple_proj, ln3_g, ln3_b):
    assert x.shape == (BATCH, SEQ, D_MODEL) and w_in.shape[0] == DEPTH == 1
    wt = w_in[0].T
    x2 = x.reshape(TOK, D_MODEL)
    in_g, in_b = in_ln_g.reshape(1, -1), in_ln_b.reshape(1, -1)
    hb = _ln0(x2, in_g, in_b)

    zm = _inproj(hb, wt, MAIN_W // 512, Z_TILE_OF_W_TILE, False, True, "inproj_main")
    w_gate, w_lr = _gate_weights(wt)
    zg = _inproj(hb, w_gate, 2 * D_MODEL // 512, tuple(range(8)), True, False, "inproj_gates")

    w2 = jnp.zeros((128, 2 * GLA_K), F32)
    w2 = w2.at[:GLA_RANK, :GLA_K].set(w_lr_f[0]).at[GLA_RANK:LR_W, GLA_K:].set(w_lr_b[0])
    b2 = jnp.concatenate([b_lr_f[0], b_lr_b[0]]).reshape(1, -1)
    bc_f, bc_b = _decay(hb, w_lr, jnp.stack(_hi_lo(w2)), b2)

    cos_full, sin_signed = _rope_tables()
    qr, kr = _qkprep(zm, cos_full, sin_signed, q_norm[0], k_norm[0])
    attn = _attention(qr, kr, zm)
    o_f = _gla(zm, bc_f, False)
    o_b = _gla(zm, bc_b, True)

    wr = jnp.zeros((D_MODEL, 128), F32).at[:, :N_EXPERTS].set(w_router[0])
    br = jnp.full((1, 128), -jnp.inf, F32).at[0, :N_EXPERTS].set(b_router[0])
    vec = lambda v: v[0].reshape(1, -1)
    h1, idx_pad, gate_pad = _mixer(
        attn, o_f, o_b, zm, zg, x2, in_g, in_b, w_br_a[0].astype(BF16), w_br_b[0].astype(BF16), w_o[0].astype(BF16),
        vec(gla_norm), vec(ln1_g), vec(ln1_b), jnp.stack(_hi_lo(wr)), br)

    n_sb, sb_e, sb_nblk, sb_nvalid, tab = _routing(idx_pad[:, :TOP_K])
    y_slots = _experts(n_sb, sb_e, sb_nblk, sb_nvalid, tab, h1, w_up[0], b_up[0], w_down[0], b_down[0])

    out = _tail(y_slots, gate_pad, h1, p[0].reshape(TOK, PLE_DIM), w_ple_gate[0].astype(BF16),
                w_ple_proj[0].astype(BF16), vec(ln2_g), vec(ln2_b), vec(ln3_g), vec(ln3_b))
    return out.reshape(BATCH, SEQ, D_MODEL)
```

```python
import functools

import jax
import jax.numpy as jnp
from jax import lax
from jax.experimental import pallas as pl
from jax.experimental.pallas import tpu as pltpu

F32 = jnp.float32
BF16 = jnp.bfloat16

D_MODEL = 2048
BATCH = 2
SEQ = 4096
TOK = BATCH * SEQ
PLE_DIM = 256
GRID_W = 64
ATT_HEADS = 8
ATT_KV_HEADS = 2
ATT_GROUP = ATT_HEADS // ATT_KV_HEADS
HEAD_DIM = 128
ROPE_BASE = 10000.0
GLA_HEADS = 4
GLA_DK = 128
GLA_DV = 256
GLA_RANK = 16
GLA_TAU = 16.0
GLA_CHUNK = 64
GLA_SUB = 16
N_EXPERTS = 32
TOP_K = 4
D_FF = D_MODEL
SWIGLU_LIMIT = 7.0
SWIGLU_ALPHA = 1.702
MOE_BLOCK = 128
ATT_Q = ATT_HEADS * HEAD_DIM
ATT_KV = ATT_KV_HEADS * HEAD_DIM
GLA_K = GLA_HEADS * GLA_DK
GLA_V = GLA_HEADS * GLA_DV
MAIN_W = ATT_Q + 2 * ATT_KV + 2 * GLA_K + 2 * GLA_V
LR_W = 2 * GLA_RANK
DEPTH = 1
DN_ALPHA = (2 * DEPTH) ** 0.25
LN_EPS = 1e-5
RMS_EPS = 1e-6
LOG2_E = 1.4426950408889634

Z_TILE_OF_W_TILE = (0, 1, 6, 7, 8, 2, 3, 4, 5)
Z_Q, Z_VB, Z_OG, Z_KA, Z_VA, Z_QB, Z_KB = 0, 1024, 2048, 3072, 3328, 3584, 4096

VMEM_LIMIT = 56 * 1024 * 1024

N_ROWS = TOK * TOP_K + N_EXPERTS * MOE_BLOCK
N_BLOCKS = N_ROWS // MOE_BLOCK
SB_BLOCKS = 10
SB_ROWS = SB_BLOCKS * MOE_BLOCK
N_SB = (N_BLOCKS + (SB_BLOCKS - 1) * N_EXPERTS) // SB_BLOCKS
SUBLANES = 8
IDS_STRIDE = -(-2 * SB_ROWS // 1024) * 1024
FF_TILE = 256
N_FF_TILES = D_FF // FF_TILE


def _params(n_axes, vmem=None):
    return pltpu.CompilerParams(dimension_semantics=("arbitrary",) * n_axes,
                                vmem_limit_bytes=vmem or VMEM_LIMIT)


def _sigmoid(x):
    return 1.0 / (1.0 + jnp.exp(-x))


def _layer_norm(y, g, b):
    mu = jnp.mean(y, axis=-1, keepdims=True)
    yc = y - mu
    var = jnp.mean(yc * yc, axis=-1, keepdims=True)
    return yc * lax.rsqrt(var + LN_EPS) * g + b


def _dot(a, b):
    return jnp.dot(a, b, preferred_element_type=F32)


def _hi_lo(x):
    hi = x.astype(BF16)
    return hi, (x - hi.astype(F32)).astype(BF16)


def _dot_split(x, w_ref):
    hi, lo = _hi_lo(x)
    return _dot(hi, w_ref[0]) + _dot(hi, w_ref[1]) + _dot(lo, w_ref[0])


def _dot_nt(a, b):
    return lax.dot_general(a, b, (((1,), (1,)), ((), ())), preferred_element_type=F32)


def _dot_tn(a, b):
    return lax.dot_general(a, b, (((0,), (0,)), ((), ())), preferred_element_type=F32)


def _ln0_body(x_ref, g_ref, b_ref, hb_ref):
    hb_ref[...] = _layer_norm(x_ref[...], g_ref[...], b_ref[...]).astype(BF16)


def _ln0(x2, g, b):
    tm = 256
    row = pl.BlockSpec((tm, D_MODEL), lambda i: (i, 0))
    vec = pl.BlockSpec((1, D_MODEL), lambda i: (0, 0))
    return pl.pallas_call(
        _ln0_body, grid=(TOK // tm,), in_specs=[row, vec, vec], out_specs=row,
        out_shape=jax.ShapeDtypeStruct((TOK, D_MODEL), BF16),
        compiler_params=_params(1), name="ln0")(x2, g, b)


def _inproj_body(perm_ref, a_ref, w_ref, o_ref, wb_ref, *, gate, w_transposed):
    del perm_ref

    @pl.when(pl.program_id(1) == 0)
    def _():
        w = w_ref[...].T if w_transposed else w_ref[...]
        wb_ref[...] = w.astype(BF16)

    acc = _dot(a_ref[...], wb_ref[...])
    if gate:
        acc = _sigmoid(acc)
    o_ref[...] = acc.astype(o_ref.dtype)


def _inproj(hb, w, n_tiles, tile_perm, gate, w_transposed, name):
    tm, tn = 1024, 512
    w_spec = (pl.BlockSpec((tn, D_MODEL), lambda n, m, p: (n, 0)) if w_transposed
              else pl.BlockSpec((D_MODEL, tn), lambda n, m, p: (0, n)))
    grid_spec = pltpu.PrefetchScalarGridSpec(
        num_scalar_prefetch=1, grid=(n_tiles, TOK // tm),
        in_specs=[pl.BlockSpec((tm, D_MODEL), lambda n, m, p: (m, 0)), w_spec],
        out_specs=pl.BlockSpec((tm, tn), lambda n, m, p: (m, p[n])),
        scratch_shapes=[pltpu.VMEM((D_MODEL, tn), BF16)])
    return pl.pallas_call(
        functools.partial(_inproj_body, gate=gate, w_transposed=w_transposed), grid_spec=grid_spec,
        out_shape=jax.ShapeDtypeStruct((TOK, n_tiles * tn), BF16),
        compiler_params=_params(2), name=name)(jnp.asarray(tile_perm, jnp.int32), hb, w)


def _wprep_body(a_ref, b_ref, g_ref, lr_ref):
    g_ref[...] = jnp.concatenate([a_ref[LR_W:, :], b_ref[...]], axis=0).T.astype(BF16)

    @pl.when(pl.program_id(0) == 0)
    def _():
        lane = lax.broadcasted_iota(jnp.int32, (D_MODEL, 128), 1)
        lr_ref[...] = jnp.where(lane < LR_W, a_ref[:128, :].T, 0.0).astype(BF16)


def _gate_weights(wt):
    tn = 512
    first = MAIN_W // tn
    return pl.pallas_call(
        _wprep_body, grid=(2 * D_MODEL // tn,),
        in_specs=[pl.BlockSpec((tn, D_MODEL), lambda i: (first + i, 0)),
                  pl.BlockSpec((LR_W, D_MODEL), lambda i: ((first + i + 1) * (tn // LR_W), 0))],
        out_specs=[pl.BlockSpec((D_MODEL, tn), lambda i: (0, i)), pl.BlockSpec((D_MODEL, 128), lambda i: (0, 0))],
        out_shape=[jax.ShapeDtypeStruct((D_MODEL, 2 * D_MODEL), BF16), jax.ShapeDtypeStruct((D_MODEL, 128), BF16)],
        compiler_params=_params(1), name="gate_w")(wt, wt)


def _decay_body(h_ref, wlr_ref, w2_ref, b2_ref, trif_ref, trib_ref, bf_ref, bb_ref):
    zlr = _dot(h_ref[...], wlr_ref[...])
    pre = _dot_split(zlr, w2_ref) + b2_ref[...]
    la = (jnp.minimum(pre, 0.0) - jnp.log1p(jnp.exp(-jnp.abs(pre)))) * (1.0 / GLA_TAU)
    hi = la.astype(BF16)
    lo = (la - hi.astype(F32)).astype(BF16)
    bf_ref[...] = _dot(trif_ref[...], hi[:, :GLA_K]) + _dot(trif_ref[...], lo[:, :GLA_K])
    bb_ref[...] = _dot(trib_ref[...], hi[:, GLA_K:]) + _dot(trib_ref[...], lo[:, GLA_K:])


def _decay(hb, wlr, w2, b2):
    tm = 512
    r = jnp.arange(tm)
    same = (r[:, None] // GLA_CHUNK) == (r[None, :] // GLA_CHUNK)
    trif = (same & (r[None, :] <= r[:, None])).astype(BF16)
    trib = (same & (r[None, :] >= r[:, None])).astype(BF16)
    full = lambda shape: pl.BlockSpec(shape, lambda i: (0,) * len(shape))
    out = pl.BlockSpec((tm, GLA_K), lambda i: (i, 0))
    return pl.pallas_call(
        _decay_body, grid=(TOK // tm,),
        in_specs=[pl.BlockSpec((tm, D_MODEL), lambda i: (i, 0)), full((D_MODEL, 128)),
                  full((2, 128, 2 * GLA_K)), full((1, 2 * GLA_K)), full((tm, tm)), full((tm, tm))],
        out_specs=[out, out],
        out_shape=[jax.ShapeDtypeStruct((TOK, GLA_K), F32)] * 2,
        compiler_params=_params(1), name="decay")(hb, wlr, w2, b2, trif, trib)


def _qkprep_body(q_ref, k_ref, cos_ref, sin_ref, qn_ref, kn_ref, qo_ref, ko_ref):
    cos = cos_ref[...]
    sin = sin_ref[...]
    lane = lax.broadcasted_iota(jnp.int32, cos.shape, 1)
    even = (lane % 2) == 0

    def one(x, gain, scale):
        x = x.astype(F32)
        x = x * lax.rsqrt(jnp.mean(x * x, axis=-1, keepdims=True) + RMS_EPS) * gain
        partner = jnp.where(even, pltpu.roll(x, HEAD_DIM - 1, 1), pltpu.roll(x, 1, 1))
        return ((x * cos + partner * sin) * scale).astype(BF16)

    for hh in range(ATT_HEADS):
        sl = slice(hh * HEAD_DIM, (hh + 1) * HEAD_DIM)
        qo_ref[:, sl] = one(q_ref[:, sl], qn_ref[...], LOG2_E * HEAD_DIM ** -0.5)
    for hh in range(ATT_KV_HEADS):
        sl = slice(hh * HEAD_DIM, (hh + 1) * HEAD_DIM)
        ko_ref[:, sl] = one(k_ref[:, sl], kn_ref[...], 1.0)


def _qkprep(zm, cos_full, sin_signed, q_norm, k_norm):
    tm = 256
    nrow = SEQ // tm
    tab = pl.BlockSpec((tm, HEAD_DIM), lambda i: (i % nrow, 0))
    vec = pl.BlockSpec((1, HEAD_DIM), lambda i: (0, 0))
    return pl.pallas_call(
        _qkprep_body, grid=(TOK // tm,),
        in_specs=[pl.BlockSpec((tm, ATT_Q), lambda i: (i, Z_Q // ATT_Q)),
                  pl.BlockSpec((tm, ATT_KV), lambda i: (i, Z_KA // ATT_KV)), tab, tab, vec, vec],
        out_specs=[pl.BlockSpec((tm, ATT_Q), lambda i: (i, 0)), pl.BlockSpec((tm, ATT_KV), lambda i: (i, 0))],
        out_shape=[jax.ShapeDtypeStruct((TOK, ATT_Q), BF16), jax.ShapeDtypeStruct((TOK, ATT_KV), BF16)],
        compiler_params=_params(1), name="qkprep")(zm, zm, cos_full, sin_signed,
                                                   q_norm.reshape(1, -1), k_norm.reshape(1, -1))


ATT_TQ = 512
ATT_TK = 512


def _attn_body(q_ref, k_ref, v_ref, o_ref, qs_ref, s0_ref, s1_ref, p0_ref, p1_ref, a0_ref, a1_ref,
               acc_ref, m_ref, l_ref):
    rows = ATT_GROUP * ATT_TQ
    n_chunks = SEQ // ATT_TK
    groups = ATT_TK // SUBLANES
    for g in range(ATT_GROUP):
        qs_ref[g * ATT_TQ:(g + 1) * ATT_TQ, :] = q_ref[:, g * HEAD_DIM:(g + 1) * HEAD_DIM]
    m_ref[...] = jnp.full(m_ref.shape, -jnp.inf, F32)
    l_ref[...] = jnp.zeros(l_ref.shape, F32)
    acc_ref[...] = jnp.zeros(acc_ref.shape, F32)
    p1_ref[...] = jnp.zeros(p1_ref.shape, BF16)
    a1_ref[...] = jnp.zeros(a1_ref.shape, F32)

    def chunk(c):
        return pl.ds(pl.multiple_of(c * ATT_TK, ATT_TK), ATT_TK)

    lane_blocks = [slice(cb * HEAD_DIM, (cb + 1) * HEAD_DIM) for cb in range(rows // HEAD_DIM)]

    def scores(kc, s_out):
        qk = _dot_nt(k_ref[kc, :], qs_ref[...])
        for i, cb in enumerate(lane_blocks):
            s_out[i] = qk[:, cb]

    def probs_t(p_in):
        return jnp.concatenate([p_in[i] for i in range(len(lane_blocks))], axis=1)

    scores(slice(0, ATT_TK), s0_ref)

    def step(c, s_cur, s_nxt, p_cur, p_prv, a_cur, a_prv):
        c_next = jnp.where(c + 1 < n_chunks, c + 1, 0)
        c_prev = jnp.where(c > 0, c - 1, 0)
        scores(chunk(c_next), s_nxt)
        acc_ref[...] = a_prv[0:1, :] * acc_ref[...] + _dot_tn(v_ref[chunk(c_prev), :], probs_t(p_prv))
        m_new = []
        for i, cb in enumerate(lane_blocks):
            part = jnp.max(s_cur[i].reshape(groups, SUBLANES, HEAD_DIM), axis=0)
            m_new.append(jnp.maximum(m_ref[0:1, cb], jnp.max(part, axis=0, keepdims=True)))
        for i, (cb, mn) in enumerate(zip(lane_blocks, m_new)):
            alpha = jnp.exp2(m_ref[0:1, cb] - mn)
            p = jnp.exp2(s_cur[i] - mn)
            l_ref[:, cb] = alpha * l_ref[:, cb] + jnp.sum(p.reshape(groups, SUBLANES, HEAD_DIM), axis=0)
            a_cur[:, cb] = jnp.broadcast_to(alpha, (SUBLANES, HEAD_DIM))
            p_cur[i] = p.astype(BF16)
        for cb, mn in zip(lane_blocks, m_new):
            m_ref[:, cb] = jnp.broadcast_to(mn, (SUBLANES, HEAD_DIM))

    def pair(i, carry):
        step(2 * i, s0_ref, s1_ref, p0_ref, p1_ref, a0_ref, a1_ref)
        step(2 * i + 1, s1_ref, s0_ref, p1_ref, p0_ref, a1_ref, a0_ref)
        return carry

    lax.fori_loop(0, n_chunks // 2, pair, 0)
    acc = a1_ref[0:1, :] * acc_ref[...] + _dot_tn(v_ref[SEQ - ATT_TK:SEQ, :], probs_t(p1_ref))
    out = (acc / jnp.sum(l_ref[...], axis=0, keepdims=True)).T.astype(BF16)
    for g in range(ATT_GROUP):
        o_ref[:, g * HEAD_DIM:(g + 1) * HEAD_DIM] = out[g * ATT_TQ:(g + 1) * ATT_TQ]


def _attention(qr, kr, zm):
    nq = SEQ // ATT_TQ
    gw = ATT_GROUP * HEAD_DIM
    rows = ATT_GROUP * ATT_TQ
    qspec = pl.BlockSpec((ATT_TQ, gw), lambda b, j, i: (b * nq + i, j))
    return pl.pallas_call(
        _attn_body, grid=(BATCH, ATT_KV_HEADS, nq),
        in_specs=[qspec,
                  pl.BlockSpec((SEQ, HEAD_DIM), lambda b, j, i: (b, j)),
                  pl.BlockSpec((SEQ, HEAD_DIM), lambda b, j, i: (b, Z_VA // HEAD_DIM + j))],
        out_specs=qspec,
        out_shape=jax.ShapeDtypeStruct((TOK, ATT_Q), BF16),
        scratch_shapes=[pltpu.VMEM((rows, HEAD_DIM), BF16),
                        pltpu.VMEM((rows // HEAD_DIM, ATT_TK, HEAD_DIM), F32),
                        pltpu.VMEM((rows // HEAD_DIM, ATT_TK, HEAD_DIM), F32),
                        pltpu.VMEM((rows // HEAD_DIM, ATT_TK, HEAD_DIM), BF16),
                        pltpu.VMEM((rows // HEAD_DIM, ATT_TK, HEAD_DIM), BF16),
                        pltpu.VMEM((SUBLANES, rows), F32), pltpu.VMEM((SUBLANES, rows), F32),
                        pltpu.VMEM((HEAD_DIM, rows), F32), pltpu.VMEM((SUBLANES, rows), F32),
                        pltpu.VMEM((SUBLANES, rows), F32)],
        compiler_params=_params(3), name="attn")(qr, kr, zm)


GLA_CB = 8
GLA_RB = GLA_CB * GLA_CHUNK
N_SUB = GLA_CHUNK // GLA_SUB


def _gla_body(q_ref, k_ref, v_ref, bc_ref, o_ref, st_ref, kf_ref, bs_ref, *, rev):
    @pl.when(pl.program_id(1) == 0)
    def _():
        st_ref[...] = jnp.zeros_like(st_ref)

    C, SUB = GLA_CHUNK, GLA_SUB
    rowc = lax.broadcasted_iota(jnp.int32, (C, GLA_DK), 0)
    rows_s = lax.broadcasted_iota(jnp.int32, (SUB, 128), 0)
    lane_s = lax.broadcasted_iota(jnp.int32, (SUB, 128), 1)

    def chunk(ci, carry):
        c = (GLA_CB - 1 - ci) if rev else ci
        r0 = pl.multiple_of(c * C, C)
        for hh in range(GLA_HEADS):
            ksl = slice(hh * GLA_DK, (hh + 1) * GLA_DK)
            vsl = slice(hh * GLA_DV, (hh + 1) * GLA_DV)
            q = q_ref[pl.ds(r0, C), ksl].astype(F32) * (GLA_DK ** -0.5)
            k = k_ref[pl.ds(r0, C), ksl].astype(F32)
            v = v_ref[pl.ds(r0, C), vsl]
            bc = bc_ref[pl.ds(r0, C), ksl]
            kf_ref[hh] = k
            bs_ref[hh] = bc
            st = st_ref[hh]
            blast = bc[0:1] if rev else bc[C - 1:C]
            o_inter = _dot_nt((q * jnp.exp(bc)).astype(BF16), st.astype(BF16))
            kdec = k * jnp.exp(blast - bc)
            st_ref[hh] = st * jnp.exp(blast) + _dot_tn(v, kdec.astype(BF16))

            a_rows = []
            for si in range(N_SUB):
                lo, hi = si * SUB, (si + 1) * SUB
                q_s, b_s = q[lo:hi], bc[lo:hi]
                has_earlier = (si < N_SUB - 1) if rev else (si > 0)
                if has_earlier:
                    ref_row = bc[hi:hi + 1] if rev else bc[lo - 1:lo]
                    earlier = (rowc >= hi) if rev else (rowc < lo)
                    qt = q_s * jnp.exp(b_s - ref_row)
                    kt = k * jnp.exp(jnp.where(earlier, ref_row - bc, -jnp.inf))
                    a = _dot_nt(qt.astype(BF16), kt.astype(BF16))
                else:
                    a = jnp.zeros((SUB, C), F32)
                diag = jnp.zeros((SUB, 128), F32)
                for jl in range(SUB):
                    j = lo + jl
                    d = jnp.minimum(b_s - bs_ref[hh, j:j + 1, :], 0.0)
                    col = jnp.sum(q_s * kf_ref[hh, j:j + 1, :] * jnp.exp(d), axis=-1, keepdims=True)
                    diag = jnp.where(lane_s == j, col, diag)
                keep = (lane_s >= rows_s + lo) if rev else (lane_s <= rows_s + lo)
                diag = jnp.where(keep & (lane_s >= lo) & (lane_s < hi), diag, 0.0)
                a_rows.append(a + diag[:, :C])
            a_full = jnp.concatenate(a_rows, axis=0)
            o_ref[pl.ds(r0, C), vsl] = o_inter + _dot(a_full.astype(BF16), v)
        return carry

    lax.fori_loop(0, GLA_CB, chunk, 0)


def _gla(zm, bcum, rev):
    ncb = SEQ // GLA_RB
    if rev:
        row = lambda b, c: b * ncb + (ncb - 1 - c)
    else:
        row = lambda b, c: b * ncb + c
    return pl.pallas_call(
        functools.partial(_gla_body, rev=rev), grid=(BATCH, ncb),
        in_specs=[pl.BlockSpec((GLA_RB, GLA_K), lambda b, c: (row(b, c), Z_QB // GLA_K)),
                  pl.BlockSpec((GLA_RB, GLA_K), lambda b, c: (row(b, c), Z_KB // GLA_K)),
                  pl.BlockSpec((GLA_RB, GLA_V), lambda b, c: (row(b, c), Z_VB // GLA_V)),
                  pl.BlockSpec((GLA_RB, GLA_K), lambda b, c: (row(b, c), 0))],
        out_specs=pl.BlockSpec((GLA_RB, GLA_V), lambda b, c: (row(b, c), 0)),
        out_shape=jax.ShapeDtypeStruct((TOK, GLA_V), F32),
        scratch_shapes=[pltpu.VMEM((GLA_HEADS, GLA_DV, GLA_DK), F32),
                        pltpu.VMEM((GLA_HEADS, GLA_CHUNK, GLA_DK), F32),
                        pltpu.VMEM((GLA_HEADS, GLA_CHUNK, GLA_DK), F32)],
        compiler_params=_params(2), name="gla_bwd" if rev else "gla_fwd")(zm, zm, zm, bcum)


MIX_TM = 512


def _merge_body(attn_ref, of_ref, ob_ref, og_ref, ga_ref, gb_ref, gn_ref, wa_ref, wb_ref, m_ref, on_ref):
    gn = gn_ref[...]
    for hh in range(GLA_HEADS):
        sl = slice(hh * GLA_DV, (hh + 1) * GLA_DV)
        x = of_ref[:, sl] + ob_ref[:, sl]
        g = og_ref[:, sl].astype(F32)
        xn = x * lax.rsqrt(jnp.mean(x * x, axis=-1, keepdims=True) + RMS_EPS) * gn
        on_ref[:, sl] = (xn * (g * _sigmoid(g))).astype(BF16)
    ya = _dot(attn_ref[...], wa_ref[...])
    yb = _dot(on_ref[...], wb_ref[...])
    m_ref[...] = (ga_ref[...].astype(F32) * ya + gb_ref[...].astype(F32) * yb).astype(BF16)


def _outproj_body(m_ref, x_ref, ig_ref, ib_ref, wo_ref, lg_ref, lb_ref, wr_ref, br_ref, h1_ref, idx_ref, gate_ref):
    mix = _dot(m_ref[...], wo_ref[...])
    h0 = _layer_norm(x_ref[...], ig_ref[...], ib_ref[...])
    h1 = _layer_norm(DN_ALPHA * h0 + mix, lg_ref[...], lb_ref[...])
    h1_ref[...] = h1

    logits = _dot_split(h1, wr_ref) + br_ref[...]
    lane = lax.broadcasted_iota(jnp.int32, logits.shape, 1)
    x = logits
    vals, idxs = [], []
    for _ in range(TOP_K):
        mx = jnp.max(x, axis=-1, keepdims=True)
        ix = jnp.min(jnp.where(x == mx, lane, 128), axis=-1, keepdims=True)
        vals.append(mx)
        idxs.append(ix)
        x = jnp.where(lane == ix, -jnp.inf, x)
    es = [jnp.exp(vv - vals[0]) for vv in vals]
    den = es[0] + es[1] + es[2] + es[3]
    idx_out = jnp.zeros(logits.shape, jnp.int32)
    gate_out = jnp.zeros(logits.shape, F32)
    for kk in range(TOP_K):
        idx_out = jnp.where(lane == kk, idxs[kk], idx_out)
        gate_out = jnp.where(lane == kk, es[kk] / den, gate_out)
    idx_ref[...] = idx_out
    gate_ref[...] = gate_out


def _mixer(attn, o_f, o_b, zm, zg, x2, in_g, in_b, wa, wb, wo, gn, lg, lb, wr, br):
    tm = MIX_TM
    const = lambda shape: pl.BlockSpec(shape, lambda i: (0,) * len(shape), pipeline_mode=pl.Buffered(1))
    row = lambda w, cb=0: pl.BlockSpec((tm, w), lambda i: (i, cb))
    vec = const((1, D_MODEL))
    grid = (TOK // tm,)
    merged = pl.pallas_call(
        _merge_body, grid=grid,
        in_specs=[row(ATT_Q), row(GLA_V), row(GLA_V), row(GLA_V, Z_OG // GLA_V), row(D_MODEL, 0), row(D_MODEL, 1),
                  const((1, GLA_DV)), const((ATT_Q, D_MODEL)), const((GLA_V, D_MODEL))],
        out_specs=row(D_MODEL), out_shape=jax.ShapeDtypeStruct((TOK, D_MODEL), BF16),
        scratch_shapes=[pltpu.VMEM((tm, GLA_V), BF16)],
        compiler_params=_params(1), name="merge")(attn, o_f, o_b, zm, zg, zg, gn, wa, wb)
    return pl.pallas_call(
        _outproj_body, grid=grid,
        in_specs=[row(D_MODEL), row(D_MODEL), vec, vec, const((D_MODEL, D_MODEL)), vec, vec,
                  const((2, D_MODEL, 128)), const((1, 128))],
        out_specs=[row(D_MODEL), row(128), row(128)],
        out_shape=[jax.ShapeDtypeStruct((TOK, D_MODEL), F32),
                   jax.ShapeDtypeStruct((TOK, 128), jnp.int32),
                   jax.ShapeDtypeStruct((TOK, 128), F32)],
        compiler_params=_params(1), name="outproj")(merged, x2, in_g, in_b, wo, lg, lb, wr, br)


GROUP_SIZES = (512, 256, 128)
GATHER_STEPS = SB_BLOCKS // (SB_ROWS // GROUP_SIZES[0])


def _expert_body(e_ref, nb_ref, nv_ref, tab_hbm, h1_hbm, wg_ref, wl_ref, wd_ref, bg_ref, bl_ref, bd_ref,
                 y_hbm, gbuf, xb_ref, acc_ref, wgb_ref, wlb_ref, wdb_ref, ids, sem_ids, sem_g, sem_s):
    del e_ref
    s = pl.program_id(0)
    j = pl.program_id(1)
    nblk = nb_ref[s]
    slot = s & 1
    has_next = s + 1 < pl.num_programs(0)
    nxt = jnp.minimum(s + 1, N_SB - 1)
    nblk_next = jnp.where(has_next, nb_ref[nxt], 0)
    tiles = MOE_BLOCK // SUBLANES

    def ids_copy(sb, sl):
        return pltpu.make_async_copy(tab_hbm.at[pl.ds(pl.multiple_of(sb * IDS_STRIDE, IDS_STRIDE), IDS_STRIDE)],
                                     ids.at[pl.ds(pl.multiple_of(sl * IDS_STRIDE, IDS_STRIDE), IDS_STRIDE)],
                                     sem_ids.at[sl])

    def gather_block(sl, b):
        id0 = sl * IDS_STRIDE + b * MOE_BLOCK

        def issue(t, c):
            for u in range(SUBLANES):
                tok = ids[id0 + t * SUBLANES + u]
                pltpu.make_async_copy(h1_hbm.at[pl.ds(tok >> 3, 1), pl.ds(tok & 7, 1), :],
                                      gbuf.at[pl.ds(b * tiles + t, 1), pl.ds(u, 1), :], sem_g).start()
            return c

        lax.fori_loop(0, tiles, issue, 0)

    def gather_wait_block(b):
        pltpu.make_async_copy(h1_hbm.at[pl.ds(0, tiles)], gbuf.at[pl.ds(b * tiles, tiles)], sem_g).wait()

    def loop_blocks(n, fn):
        def body(b, c):
            fn(b)
            return c

        lax.fori_loop(0, n, body, 0)

    @pl.when((s == 0) & (j == 0))
    def _():
        ids_copy(0, 0).start()
        ids_copy(0, 0).wait()
        loop_blocks(nblk, lambda b: gather_block(0, b))

    big = GROUP_SIZES[0]
    big_blocks = big // MOE_BLOCK
    n_big = nblk // big_blocks
    in_window = (j >= 1) & (j <= GATHER_STEPS) & has_next
    issued_here = GATHER_STEPS * n_big
    issued_prev = jnp.where(s > 0, GATHER_STEPS * (nb_ref[jnp.maximum(s - 1, 0)] // big_blocks), 0)

    @pl.when(j == 0)
    def _():
        loop_blocks(jnp.maximum(issued_prev, nblk), gather_wait_block)

    @pl.when(nblk > 0)
    def _():
        @pl.when(j == 0)
        def _():
            @pl.when(has_next)
            def _():
                ids_copy(nxt, 1 - slot).start()

            def take_block(b):
                x = gbuf[pl.ds(b * tiles, tiles)].reshape(MOE_BLOCK, D_MODEL)
                xb_ref[pl.ds(pl.multiple_of(b * MOE_BLOCK, MOE_BLOCK), MOE_BLOCK), :] = x.astype(BF16)
                acc_ref[pl.ds(b * tiles, tiles)] = jnp.broadcast_to(bd_ref[0], (tiles, SUBLANES, D_MODEL))

            loop_blocks(nblk, take_block)

        @pl.when((j == 1) & has_next)
        def _():
            ids_copy(nxt, 1 - slot).wait()

        @pl.when(j == GATHER_STEPS + 1)
        def _():
            for b in range(SB_BLOCKS):
                @pl.when((b >= issued_here) & (b < nblk_next))
                def _(b=b):
                    gather_block(1 - slot, b)

        wgb_ref[...] = wg_ref[0].astype(BF16)
        wlb_ref[...] = wl_ref[0].astype(BF16)
        wdb_ref[...] = wd_ref[0].astype(BF16)

        def group(r0, size):
            x = xb_ref[pl.ds(pl.multiple_of(r0, MOE_BLOCK), size), :]
            g = jnp.minimum(_dot(x, wgb_ref[...]) + bg_ref[0], SWIGLU_LIMIT)
            lin = jnp.clip(_dot(x, wlb_ref[...]) + bl_ref[0], -SWIGLU_LIMIT, SWIGLU_LIMIT)
            act = g * _sigmoid(SWIGLU_ALPHA * g) * (lin + 1.0)
            upd = _dot(act.astype(BF16), wdb_ref[...])
            acc_ref[pl.ds(r0 // SUBLANES, size // SUBLANES)] += upd.reshape(size // SUBLANES, SUBLANES, D_MODEL)

        def big_group(i, c):
            group(i * big, big)
            return c

        def big_group_and_gather(i, c):
            group(i * big, big)
            b = (j - 1) * n_big + i
            id0 = (1 - slot) * IDS_STRIDE + b * MOE_BLOCK
            for t in range(tiles):
                for u in range(SUBLANES):
                    tok = ids[id0 + t * SUBLANES + u]
                    pltpu.make_async_copy(h1_hbm.at[pl.ds(tok >> 3, 1), pl.ds(tok & 7, 1), :],
                                          gbuf.at[pl.ds(b * tiles + t, 1), pl.ds(u, 1), :], sem_g).start()
            return c

        @pl.when(in_window)
        def _():
            lax.fori_loop(0, n_big, big_group_and_gather, 0)

        @pl.when(jnp.logical_not(in_window))
        def _():
            lax.fori_loop(0, n_big, big_group, 0)

        done = n_big * big_blocks
        for size in GROUP_SIZES[1:]:
            take = ((nblk - done) // (size // MOE_BLOCK)) > 0

            @pl.when(take)
            def _(done=done, size=size):
                group(done * MOE_BLOCK, size)

            done = done + jnp.where(take, size // MOE_BLOCK, 0)

        @pl.when(j == N_FF_TILES - 1)
        def _():
            nvalid = nv_ref[s]

            dst0 = slot * IDS_STRIDE + SB_ROWS

            def row_copy(t, u, dst):
                return pltpu.make_async_copy(acc_ref.at[pl.ds(t, 1), pl.ds(u, 1), :],
                                             y_hbm.at[pl.ds(dst >> 3, 1), pl.ds(dst & 7, 1), :], sem_s)

            def issue_tile(t, c):
                for u in range(SUBLANES):
                    row_copy(t, u, ids[dst0 + t * SUBLANES + u]).start()
                return c

            def issue_row(r, c):
                row_copy(r >> 3, r & 7, ids[dst0 + r]).start()
                return c

            full_tiles = nvalid // SUBLANES
            lax.fori_loop(0, full_tiles, issue_tile, 0)
            lax.fori_loop(full_tiles * SUBLANES, nvalid, issue_row, 0)

            def wait_block(b, c):
                pltpu.make_async_copy(acc_ref.at[pl.ds(0, tiles)], y_hbm.at[pl.ds(0, tiles)], sem_s).wait()
                return c

            def wait_row(r, c):
                row_copy(0, 0, 0).wait()
                return c

            nfull = nvalid // MOE_BLOCK
            lax.fori_loop(0, nfull, wait_block, 0)
            lax.fori_loop(nfull * MOE_BLOCK, nvalid, wait_row, 0)


def _experts(n_sb, sb_e, sb_nblk, sb_nvalid, tab, h1, w_up, b_up, w_down, b_down):
    def jj(j, nb, s):
        return jnp.where(nb[s] > 0, j, N_FF_TILES - 1)

    grid_spec = pltpu.PrefetchScalarGridSpec(
        num_scalar_prefetch=3, grid=(n_sb, N_FF_TILES),
        in_specs=[pl.BlockSpec(memory_space=pl.ANY),
                  pl.BlockSpec(memory_space=pl.ANY),
                  pl.BlockSpec((1, D_MODEL, FF_TILE), lambda s, j, e, nb, nv: (e[s], 0, jj(j, nb, s))),
                  pl.BlockSpec((1, D_MODEL, FF_TILE), lambda s, j, e, nb, nv: (e[s], 0, N_FF_TILES + jj(j, nb, s))),
                  pl.BlockSpec((1, FF_TILE, D_MODEL), lambda s, j, e, nb, nv: (e[s], jj(j, nb, s), 0)),
                  pl.BlockSpec((1, 1, FF_TILE), lambda s, j, e, nb, nv: (e[s], 0, jj(j, nb, s))),
                  pl.BlockSpec((1, 1, FF_TILE), lambda s, j, e, nb, nv: (e[s], 0, N_FF_TILES + jj(j, nb, s))),
                  pl.BlockSpec((1, 1, D_MODEL), lambda s, j, e, nb, nv: (e[s], 0, 0))],
        out_specs=pl.BlockSpec(memory_space=pl.ANY),
        scratch_shapes=[pltpu.VMEM((SB_ROWS // SUBLANES, SUBLANES, D_MODEL), F32),
                        pltpu.VMEM((SB_ROWS, D_MODEL), BF16),
                        pltpu.VMEM((SB_ROWS // SUBLANES, SUBLANES, D_MODEL), F32),
                        pltpu.VMEM((D_MODEL, FF_TILE), BF16),
                        pltpu.VMEM((D_MODEL, FF_TILE), BF16),
                        pltpu.VMEM((FF_TILE, D_MODEL), BF16),
                        pltpu.SMEM((2 * IDS_STRIDE,), jnp.int32),
                        pltpu.SemaphoreType.DMA((2,)),
                        pltpu.SemaphoreType.DMA(()),
                        pltpu.SemaphoreType.DMA(())])
    y = pl.pallas_call(
        _expert_body, grid_spec=grid_spec,
        out_shape=jax.ShapeDtypeStruct((TOP_K * TOK // SUBLANES, SUBLANES, D_MODEL), F32),
        compiler_params=_params(2), name="experts")(
            sb_e, sb_nblk, sb_nvalid, tab, h1.reshape(TOK // SUBLANES, SUBLANES, D_MODEL), w_up, w_up, w_down,
            b_up.reshape(N_EXPERTS, 1, 2 * D_FF), b_up.reshape(N_EXPERTS, 1, 2 * D_FF),
            b_down.reshape(N_EXPERTS, 1, D_MODEL))
    return y.reshape(TOP_K * TOK, D_MODEL)


def _tail_body(y0_ref, y1_ref, y2_ref, y3_ref, gate_ref, h1_ref, p_ref, wg_ref, wp_ref,
               l2g_ref, l2b_ref, l3g_ref, l3b_ref, o_ref):
    gate = gate_ref[...]
    y = gate[:, 0:1] * y0_ref[...]
    for kk, y_ref in enumerate((y1_ref, y2_ref, y3_ref), start=1):
        y = y + gate[:, kk:kk + 1] * y_ref[...]
    h2 = _layer_norm(DN_ALPHA * h1_ref[...] + y, l2g_ref[...], l2b_ref[...])
    ple = _sigmoid(_dot(h2.astype(BF16), wg_ref[...])) * _dot(p_ref[...].astype(BF16), wp_ref[...])
    o_ref[...] = _layer_norm(DN_ALPHA * h2 + ple, l3g_ref[...], l3b_ref[...])


def _tail(y_slots, gates, h1, p2, wg, wp, l2g, l2b, l3g, l3b):
    tm = 256
    nt = TOK // tm
    const = lambda shape: pl.BlockSpec(shape, lambda i: (0,) * len(shape), pipeline_mode=pl.Buffered(1))
    row = lambda w: pl.BlockSpec((tm, w), lambda i: (i, 0))
    yspec = lambda kk: pl.BlockSpec((tm, D_MODEL), lambda i: (kk * nt + i, 0))
    vec = const((1, D_MODEL))
    return pl.pallas_call(
        _tail_body, grid=(nt,),
        in_specs=[yspec(0), yspec(1), yspec(2), yspec(3), row(128), row(D_MODEL), row(PLE_DIM),
                  const((D_MODEL, D_MODEL)), const((PLE_DIM, D_MODEL)), vec, vec, vec, vec],
        out_specs=row(D_MODEL),
        out_shape=jax.ShapeDtypeStruct((TOK, D_MODEL), F32),
        compiler_params=_params(1), name="tail")(y_slots, y_slots, y_slots, y_slots, gates, h1, p2, wg, wp,
                                                 l2g, l2b, l3g, l3b)


def _routing(top_idx):
    flat_e = top_idx.reshape(-1)
    experts = jnp.arange(N_EXPERTS, dtype=jnp.int32)
    onehot = (flat_e[:, None] == experts[None, :]).astype(jnp.int32)
    csum = jnp.cumsum(onehot, axis=0)
    rank = jnp.sum(onehot * csum, axis=1) - 1
    counts = csum[-1]
    padded = (counts + MOE_BLOCK - 1) // MOE_BLOCK * MOE_BLOCK
    padded_end = jnp.cumsum(padded)
    padded_start = padded_end - padded
    dest = (padded_start[flat_e] + rank).astype(jnp.int32)
    asg = jnp.arange(TOK * TOP_K, dtype=jnp.int32)
    asg_of_row = jnp.zeros((N_ROWS + SB_ROWS,), jnp.int32).at[dest].set(asg)
    tok = (asg_of_row // TOP_K).reshape(-1, MOE_BLOCK)
    dst = ((asg_of_row % TOP_K) * TOK + asg_of_row // TOP_K).reshape(-1, MOE_BLOCK)
    nb = padded // MOE_BLOCK
    n_sb = (nb + SB_BLOCKS - 1) // SB_BLOCKS
    sb_end = jnp.cumsum(n_sb)
    total = sb_end[-1]
    s = jnp.arange(N_SB, dtype=jnp.int32)
    s_eff = jnp.minimum(s, total - 1)
    e = jnp.minimum(jnp.sum((sb_end[None, :] <= s_eff[:, None]).astype(jnp.int32), axis=1), N_EXPERTS - 1)
    local = s_eff - (sb_end[e] - n_sb[e])
    valid = s < total
    sb_nblk = jnp.where(valid, jnp.clip(nb[e] - local * SB_BLOCKS, 0, SB_BLOCKS), 0).astype(jnp.int32)
    sb_blk = padded_start[e] // MOE_BLOCK + local * SB_BLOCKS
    sb_nvalid = jnp.where(valid, jnp.clip(counts[e] - local * SB_ROWS, 0, SB_ROWS), 0).astype(jnp.int32)
    blocks = sb_blk[:, None] + jnp.arange(SB_BLOCKS, dtype=jnp.int32)[None, :]
    tab = jnp.concatenate([tok[blocks].reshape(N_SB, SB_ROWS), dst[blocks].reshape(N_SB, SB_ROWS),
                           jnp.zeros((N_SB, IDS_STRIDE - 2 * SB_ROWS), jnp.int32)], axis=1).astype(jnp.int32).reshape(-1)
    return total.astype(jnp.int32), e.astype(jnp.int32), sb_nblk, sb_nvalid, tab


def _rope_tables():
    rows = SEQ // GRID_W
    row = jnp.repeat(jnp.arange(rows), GRID_W)
    col = jnp.tile(jnp.arange(GRID_W), rows)
    n_pairs = HEAD_DIM // 4
    inv_freq = ROPE_BASE ** (-jnp.arange(n_pairs, dtype=F32) / n_pairs)
    ang = jnp.concatenate([row[:, None] * inv_freq, col[:, None] * inv_freq], -1)
    cos_full = jnp.repeat(jnp.cos(ang), 2, axis=-1)
    sin = jnp.sin(ang)
    sin_signed = jnp.stack([-sin, sin], axis=-1).reshape(SEQ, HEAD_DIM)
    return cos_full, sin_signed


def kernel(x, p, in_ln_g, in_ln_b, w_in, q_norm, k_norm, w_lr_f, b_lr_f, w_lr_b, b_lr_b, gla_norm, w_br_a, w_br_b, w_o, ln1_g, ln1_b, w_router, b_router, w_up, b_up, w_down, b_down, ln2_g, ln2_b, w_ple_gate, w_ple_proj, ln3_g, ln3_b):
    assert x.shape == (BATCH, SEQ, D_MODEL) and w_in.shape[0] == DEPTH == 1
    wt = w_in[0].T
    x2 = x.reshape(TOK, D_MODEL)
    in_g, in_b = in_ln_g.reshape(1, -1), in_ln_b.reshape(1, -1)
    hb = _ln0(x2, in_g, in_b)

    zm = _inproj(hb, wt, MAIN_W // 512, Z_TILE_OF_W_TILE, False, True, "inproj_main")
    w_gate, w_lr = _gate_weights(wt)
    zg = _inproj(hb, w_gate, 2 * D_MODEL // 512, tuple(range(8)), True, False, "inproj_gates")

    w2 = jnp.zeros((128, 2 * GLA_K), F32)
    w2 = w2.at[:GLA_RANK, :GLA_K].set(w_lr_f[0]).at[GLA_RANK:LR_W, GLA_K:].set(w_lr_b[0])
    b2 = jnp.concatenate([b_lr_f[0], b_lr_b[0]]).reshape(1, -1)
    bc_f, bc_b = _decay(hb, w_lr, jnp.stack(_hi_lo(w2)), b2)

    cos_full, sin_signed = _rope_tables()
    qr, kr = _qkprep(zm, cos_full, sin_signed, q_norm[0], k_norm[0])
    attn = _attention(qr, kr, zm)
    o_f = _gla(zm, bc_f, False)
    o_b = _gla(zm, bc_b, True)

    wr = jnp.zeros((D_MODEL, 128), F32).at[:, :N_EXPERTS].set(w_router[0])
    br = jnp.full((1, 128), -jnp.inf, F32).at[0, :N_EXPERTS].set(b_router[0])
    vec = lambda v: v[0].reshape(1, -1)
    h1, idx_pad, gate_pad = _mixer(
        attn, o_f, o_b, zm, zg, x2, in_g, in_b, w_br_a[0].astype(BF16), w_br_b[0].astype(BF16), w_o[0].astype(BF16),
        vec(gla_norm), vec(ln1_g), vec(ln1_b), jnp.stack(_hi_lo(wr)), br)

    n_sb, sb_e, sb_nblk, sb_nvalid, tab = _routing(idx_pad[:, :TOP_K])
    y_slots = _experts(n_sb, sb_e, sb_nblk, sb_nvalid, tab, h1, w_up[0], b_up[0], w_down[0], b_down[0])

    out = _tail(y_slots, gate_pad, h1, p[0].reshape(TOK, PLE_DIM), w_ple_gate[0].astype(BF16),
                w_ple_proj[0].astype(BF16), vec(ln2_g), vec(ln2_b), vec(ln3_g), vec(ln3_b))
    return out.reshape(BATCH, SEQ, D_MODEL)
```
